```python
import math
import jax
import jax.numpy as jnp
from jax import lax
import numpy as np

D_MODEL = 1024
BATCH = 16
SEQ = 4096
DEPTH = 4

PLE_DIM = 256
D_FF = 2816
NORM_EPS = 1e-6
NEG_INF = -1e30
NUM_BUCKETS = 32
MAX_DISTANCE = 128
A_HEADS = 4
A_HEAD_DIM = 64
A_QBLOCK = 128
B_HEADS = 8
B_GROUPS = 2
B_REP = B_HEADS // B_GROUPS
B_HEAD_DIM = 64
B_CMP_LEN = 32
B_CMP_STRIDE = 16
B_CMP_HIDDEN = 256
B_SEL_BLOCK = 64
B_SEL_TOPK = 16
B_WINDOW = 512
B_QBLOCK = 64
B_SEL_FORCE = 1e6
C_HEADS = 8
C_GROUPS = 2
C_REP = C_HEADS // C_GROUPS
C_HEAD_DIM = 64
C_WINDOW = 128
C_QBLOCK = 128

A_WIDTH = A_HEADS * 2 * A_HEAD_DIM
B_WIDTH = B_HEADS * B_HEAD_DIM
B_KV = B_GROUPS * B_HEAD_DIM
C_WIDTH = C_HEADS * C_HEAD_DIM
C_KV = C_GROUPS * C_HEAD_DIM
N_BIAS_HEADS = A_HEADS + B_HEADS + C_HEADS
IN_SPLITS = (A_WIDTH, A_WIDTH, A_WIDTH,
             B_WIDTH, B_KV, B_KV, B_KV, B_KV, B_KV, B_KV, 3 * B_HEADS,
             C_WIDTH, C_KV, C_KV,
             D_MODEL, D_MODEL, D_MODEL)
D_IN = 3 * A_WIDTH + B_WIDTH + 6 * B_KV + 3 * B_HEADS + C_WIDTH + 2 * C_KV + 3 * D_MODEL

kernel_name = "hybrid_diff_nsa_swa_macaron_trunk"


def rms_norm(x, g):
    xf = x.astype(jnp.float32)
    y = xf * lax.rsqrt(jnp.mean(xf * xf, axis=-1, keepdims=True) + NORM_EPS)
    return (y * g.astype(jnp.float32)).astype(x.dtype)


def swiglu(x, wi, wo):
    gate, up = jnp.split(x @ wi, 2, axis=-1)
    return (jax.nn.silu(gate) * up) @ wo


def split_cols(z, sizes):
    outs, start = [], 0
    for s in sizes:
        outs.append(z[..., start:start + s])
        start += s
    return outs


def rel_bucket(dist):
    n = jnp.maximum(dist, 0)
    max_exact = NUM_BUCKETS // 2
    nf = jnp.maximum(n, 1).astype(jnp.float32)
    large = max_exact + (jnp.log(nf / max_exact) / math.log(MAX_DISTANCE / max_exact)
                         * (NUM_BUCKETS - max_exact)).astype(jnp.int32)
    large = jnp.minimum(large, NUM_BUCKETS - 1)
    return jnp.where(n < max_exact, n, large)


def head_bias(table, dist):
    return jnp.moveaxis(table[rel_bucket(dist)].astype(jnp.float32), -1, 0)


def grouped_head_bias(table, dist, groups, rep):
    b = head_bias(table, dist)
    return b.reshape((groups, rep) + dist.shape)


def masked_softmax(s, mask):
    return jax.nn.softmax(jnp.where(mask, s, NEG_INF), axis=-1) * mask


def diff_attention(q, k, v, bias_table, lam, lam_init, subln_g):
    bsz, seq = q.shape[:2]
    nblk = seq // A_QBLOCK
    scale = A_HEAD_DIM ** -0.5
    qb = q.reshape(bsz, nblk, A_QBLOCK, A_HEADS, 2, A_HEAD_DIM).swapaxes(0, 1)
    kpos = jnp.arange(seq)

    def block(args):
        i, qi = args
        qpos = i * A_QBLOCK + jnp.arange(A_QBLOCK)
        dist = qpos[:, None] - kpos[None, :]
        s = jnp.einsum('bqhcd,bkhcd->bchqk', qi, k, preferred_element_type=jnp.float32) * scale
        s = s + head_bias(bias_table, dist)[None, None]
        pr = masked_softmax(s, dist >= 0)
        attn = pr[:, 0] - lam * pr[:, 1]
        return jnp.einsum('bhqk,bkhe->bqhe', attn.astype(v.dtype), v)

    o = lax.map(block, (jnp.arange(nblk), qb))
    o = o.swapaxes(0, 1).reshape(bsz, seq, A_HEADS, 2 * A_HEAD_DIM)
    o = rms_norm(o, subln_g) * (1.0 - lam_init)
    return o.reshape(bsz, seq, A_WIDTH)


def nsa_attention(q, kc, vc, ks, vs, kw, vw, gates, bias_table, cmp_pos, cmp_w1, cmp_w2):
    bsz, seq = q.shape[:2]
    scale = B_HEAD_DIM ** -0.5
    n_chunk = seq // B_CMP_STRIDE
    n_cmp = n_chunk - 1
    n_sel = seq // B_SEL_BLOCK
    top_k = min(B_SEL_TOPK, n_sel)
    nblk = seq // B_QBLOCK

    def compress(t, pos, w1, w2):
        c = t.reshape(bsz, n_chunk, B_CMP_STRIDE, B_GROUPS, B_HEAD_DIM)
        blk = jnp.concatenate([c[:, :-1], c[:, 1:]], axis=2) + pos[None, None, :, None, :]
        blk = blk.transpose(0, 1, 3, 2, 4).reshape(bsz, n_cmp, B_GROUPS, B_CMP_LEN * B_HEAD_DIM)
        return jax.nn.gelu(blk @ w1) @ w2

    k_cmp = compress(kc, cmp_pos[0], cmp_w1[0], cmp_w2[0])
    v_cmp = compress(vc, cmp_pos[1], cmp_w1[1], cmp_w2[1])
    cmp_start = jnp.arange(n_cmp) * B_CMP_STRIDE
    cmp_end = cmp_start + B_CMP_LEN - 1
    sel_start = jnp.arange(n_sel) * B_SEL_BLOCK
    overlap = ((cmp_start[:, None] < sel_start[None, :] + B_SEL_BLOCK)
               & (cmp_start[:, None] + B_CMP_LEN > sel_start[None, :])).astype(jnp.float32)
    ks_blk = ks.reshape(bsz, n_sel, B_SEL_BLOCK, B_GROUPS, B_HEAD_DIM).transpose(0, 3, 1, 2, 4)
    vs_blk = vs.reshape(bsz, n_sel, B_SEL_BLOCK, B_GROUPS, B_HEAD_DIM).transpose(0, 3, 1, 2, 4)
    kw_pad = jnp.pad(kw, ((0, 0), (B_WINDOW, 0), (0, 0), (0, 0)))
    vw_pad = jnp.pad(vw, ((0, 0), (B_WINDOW, 0), (0, 0), (0, 0)))
    tbg = bias_table.reshape(NUM_BUCKETS, B_GROUPS, B_REP).transpose(1, 0, 2)
    b_idx = jnp.arange(bsz)[:, None, None, None]
    g_idx = jnp.arange(B_GROUPS)[None, :, None, None]
    blk_id = jnp.arange(n_sel)
    qb = q.reshape(bsz, nblk, B_QBLOCK, B_GROUPS, B_REP, B_HEAD_DIM).swapaxes(0, 1)
    gb = gates.reshape(bsz, nblk, B_QBLOCK, B_GROUPS, B_REP, 3).swapaxes(0, 1)

    def block(args):
        i, qi, gi = args
        qpos = i * B_QBLOCK + jnp.arange(B_QBLOCK)
        dist_c = qpos[:, None] - cmp_end[None, :]
        s_c = jnp.einsum('bqgrd,bngd->bgrqn', qi, k_cmp, preferred_element_type=jnp.float32) * scale
        s_c = s_c + grouped_head_bias(bias_table, dist_c, B_GROUPS, B_REP)
        p_c = masked_softmax(s_c, dist_c >= 0)
        o_c = jnp.einsum('bgrqn,bngd->bqgrd', p_c.astype(v_cmp.dtype), v_cmp)
        imp = jnp.einsum('bgrqn,nm->bgqm', p_c, overlap)
        cur = qpos // B_SEL_BLOCK
        forced = (blk_id[None, :] == 0) | (blk_id[None, :] == cur[:, None]) | (blk_id[None, :] == cur[:, None] - 1)
        future = blk_id[None, :] > cur[:, None]
        imp = jnp.where(forced, B_SEL_FORCE, jnp.where(future, -B_SEL_FORCE, imp))
        _, idx = lax.top_k(imp, top_k)
        n_keys = top_k * B_SEL_BLOCK
        k_sel = ks_blk[b_idx, g_idx, idx].reshape(bsz, B_GROUPS, B_QBLOCK, n_keys, B_HEAD_DIM)
        v_sel = vs_blk[b_idx, g_idx, idx].reshape(bsz, B_GROUPS, B_QBLOCK, n_keys, B_HEAD_DIM)
        kpos_s = (idx[..., None] * B_SEL_BLOCK + jnp.arange(B_SEL_BLOCK)).reshape(bsz, B_GROUPS, B_QBLOCK, n_keys)
        dist_s = qpos[None, None, :, None] - kpos_s
        bias_s = tbg[g_idx, rel_bucket(dist_s)].astype(jnp.float32).transpose(0, 1, 4, 2, 3)
        s_s = jnp.einsum('bqgrd,bgqld->bgrql', qi, k_sel, preferred_element_type=jnp.float32) * scale + bias_s
        p_s = masked_softmax(s_s, (dist_s >= 0)[:, :, None])
        o_s = jnp.einsum('bgrql,bgqld->bqgrd', p_s.astype(v_sel.dtype), v_sel)
        start = i * B_QBLOCK
        k_win = lax.dynamic_slice_in_dim(kw_pad, start, B_WINDOW + B_QBLOCK, axis=1)
        v_win = lax.dynamic_slice_in_dim(vw_pad, start, B_WINDOW + B_QBLOCK, axis=1)
        kpos_w = start - B_WINDOW + jnp.arange(B_WINDOW + B_QBLOCK)
        dist_w = qpos[:, None] - kpos_w[None, :]
        mask_w = (dist_w >= 0) & (dist_w < B_WINDOW) & (kpos_w >= 0)[None, :]
        s_w = jnp.einsum('bqgrd,bkgd->bgrqk', qi, k_win, preferred_element_type=jnp.float32) * scale
        s_w = s_w + grouped_head_bias(bias_table, dist_w, B_GROUPS, B_REP)
        p_w = masked_softmax(s_w, mask_w)
        o_w = jnp.einsum('bgrqk,bkgd->bqgrd', p_w.astype(v_win.dtype), v_win)
        return gi[..., 0:1] * o_c + gi[..., 1:2] * o_s + gi[..., 2:3] * o_w

    o = lax.map(block, (jnp.arange(nblk), qb, gb))
    return o.swapaxes(0, 1).reshape(bsz, seq, B_WIDTH)


def swa_sink_attention(q, k, v, bias_table, sinks):
    bsz, seq = q.shape[:2]
    scale = C_HEAD_DIM ** -0.5
    nblk = seq // C_QBLOCK
    k_pad = jnp.pad(k, ((0, 0), (C_WINDOW, 0), (0, 0), (0, 0)))
    v_pad = jnp.pad(v, ((0, 0), (C_WINDOW, 0), (0, 0), (0, 0)))
    qb = q.reshape(bsz, nblk, C_QBLOCK, C_GROUPS, C_REP, C_HEAD_DIM).swapaxes(0, 1)
    sink = jnp.broadcast_to(sinks.astype(jnp.float32).reshape(C_GROUPS, C_REP, 1, 1),
                            (bsz, C_GROUPS, C_REP, C_QBLOCK, 1))

    def block(args):
        i, qi = args
        start = i * C_QBLOCK
        qpos = start + jnp.arange(C_QBLOCK)
        kb = lax.dynamic_slice_in_dim(k_pad, start, C_WINDOW + C_QBLOCK, axis=1)
        vb = lax.dynamic_slice_in_dim(v_pad, start, C_WINDOW + C_QBLOCK, axis=1)
        kpos = start - C_WINDOW + jnp.arange(C_WINDOW + C_QBLOCK)
        dist = qpos[:, None] - kpos[None, :]
        mask = (dist >= 0) & (dist < C_WINDOW) & (kpos >= 0)[None, :]
        s = jnp.einsum('bqgrd,bkgd->bgrqk', qi, kb, preferred_element_type=jnp.float32) * scale
        s = jnp.where(mask, s + grouped_head_bias(bias_table, dist, C_GROUPS, C_REP), NEG_INF)
        pr = jax.nn.softmax(jnp.concatenate([s, sink], axis=-1), axis=-1)[..., :-1]
        return jnp.einsum('bgrqk,bkgd->bqgrd', pr.astype(vb.dtype), vb)

    o = lax.map(block, (jnp.arange(nblk), qb))
    return o.swapaxes(0, 1).reshape(bsz, seq, C_WIDTH)


def setup_inputs(seed: int = 0) -> dict:
    key = jax.random.key(seed)
    ks = jax.random.split(key, 20)

    def nrm(k, shape, scale):
        return jax.random.normal(k, shape, jnp.float32) * scale

    return {
        'x': nrm(ks[0], (BATCH, SEQ, D_MODEL), 1.0),
        'p': nrm(ks[1], (DEPTH, BATCH, SEQ, PLE_DIM), 1.0),
        'norm_g': 1.0 + nrm(ks[2], (DEPTH, 4, D_MODEL), 0.05),
        'ffn1_wi': nrm(ks[3], (DEPTH, D_MODEL, 2 * D_FF), D_MODEL ** -0.5),
        'ffn1_wo': nrm(ks[4], (DEPTH, D_FF, D_MODEL), D_FF ** -0.5),
        'w_in': nrm(ks[5], (DEPTH, D_MODEL, D_IN), D_MODEL ** -0.5),
        'diff_lambda': nrm(ks[6], (DEPTH, 4, A_HEAD_DIM), 0.1),
        'diff_subln': 1.0 + nrm(ks[7], (DEPTH, 2 * A_HEAD_DIM), 0.05),
        'nsa_cmp_pos': nrm(ks[8], (DEPTH, 2, B_CMP_LEN, B_HEAD_DIM), 0.1),
        'nsa_cmp_w1': nrm(ks[9], (DEPTH, 2, B_CMP_LEN * B_HEAD_DIM, B_CMP_HIDDEN), (B_CMP_LEN * B_HEAD_DIM) ** -0.5),
        'nsa_cmp_w2': nrm(ks[10], (DEPTH, 2, B_CMP_HIDDEN, B_HEAD_DIM), B_CMP_HIDDEN ** -0.5),
        'swa_sinks': nrm(ks[11], (DEPTH, C_HEADS), 0.5),
        'w_branch': nrm(ks[12], (DEPTH, 3, A_WIDTH, D_MODEL), A_WIDTH ** -0.5),
        'w_out': nrm(ks[13], (DEPTH, D_MODEL, D_MODEL), D_MODEL ** -0.5),
        'ffn2_wi': nrm(ks[14], (DEPTH, D_MODEL, 2 * D_FF), D_MODEL ** -0.5),
        'ffn2_wo': nrm(ks[15], (DEPTH, D_FF, D_MODEL), D_FF ** -0.5),
        'w_ple': nrm(ks[16], (DEPTH, PLE_DIM, D_MODEL), PLE_DIM ** -0.5),
        'w_ple_gate': nrm(ks[17], (DEPTH, D_MODEL, D_MODEL), D_MODEL ** -0.5),
        'rel_bias': nrm(ks[18], (NUM_BUCKETS, N_BIAS_HEADS), 0.5),
        'final_norm': 1.0 + nrm(ks[19], (D_MODEL,), 0.05),
    }


def reference(x, p, norm_g, ffn1_wi, ffn1_wo, w_in, diff_lambda, diff_subln, nsa_cmp_pos,
              nsa_cmp_w1, nsa_cmp_w2, swa_sinks, w_branch, w_out, ffn2_wi, ffn2_wo,
              w_ple, w_ple_gate, rel_bias, final_norm):
    bsz, seq, _ = x.shape
    bias_a = rel_bias[:, :A_HEADS]
    bias_b = rel_bias[:, A_HEADS:A_HEADS + B_HEADS]
    bias_c = rel_bias[:, A_HEADS + B_HEADS:]
    h = x
    for i in range(DEPTH):
        h = h + 0.5 * swiglu(rms_norm(h, norm_g[i, 0]), ffn1_wi[i], ffn1_wo[i])
        n = rms_norm(h, norm_g[i, 1])
        (aq, ak, av, bq, bkc, bvc, bks, bvs, bkw, bvw, bgate,
         cq, ck, cv, ga, gb, gc) = split_cols(n @ w_in[i], IN_SPLITS)
        lam_init = 0.8 - 0.6 * math.exp(-0.3 * i)
        lp = diff_lambda[i].astype(jnp.float32)
        lam = jnp.exp(jnp.sum(lp[0] * lp[1])) - jnp.exp(jnp.sum(lp[2] * lp[3])) + lam_init
        ya = diff_attention(aq.reshape(bsz, seq, A_HEADS, 2, A_HEAD_DIM),
                            ak.reshape(bsz, seq, A_HEADS, 2, A_HEAD_DIM),
                            av.reshape(bsz, seq, A_HEADS, 2 * A_HEAD_DIM),
                            bias_a, lam, lam_init, diff_subln[i])
        kvb = lambda t: t.reshape(bsz, seq, B_GROUPS, B_HEAD_DIM)
        yb = nsa_attention(bq.reshape(bsz, seq, B_GROUPS, B_REP, B_HEAD_DIM),
                           kvb(bkc), kvb(bvc), kvb(bks), kvb(bvs), kvb(bkw), kvb(bvw),
                           jax.nn.sigmoid(bgate.reshape(bsz, seq, B_GROUPS, B_REP, 3)),
                           bias_b, nsa_cmp_pos[i], nsa_cmp_w1[i], nsa_cmp_w2[i])
        yc = swa_sink_attention(cq.reshape(bsz, seq, C_GROUPS, C_REP, C_HEAD_DIM),
                                ck.reshape(bsz, seq, C_GROUPS, C_HEAD_DIM),
                                cv.reshape(bsz, seq, C_GROUPS, C_HEAD_DIM),
                                bias_c, swa_sinks[i])
        merged = (jax.nn.sigmoid(ga) * (ya @ w_branch[i, 0])
                  + jax.nn.sigmoid(gb) * (yb @ w_branch[i, 1])
                  + jax.nn.sigmoid(gc) * (yc @ w_branch[i, 2]))
        h = h + merged @ w_out[i]
        h = h + 0.5 * swiglu(rms_norm(h, norm_g[i, 2]), ffn2_wi[i], ffn2_wo[i])
        h = h + jax.nn.sigmoid(rms_norm(h, norm_g[i, 3]) @ w_ple_gate[i]) * (p[i] @ w_ple[i])
    return rms_norm(h, final_norm)
```

```python
import functools
import math

import numpy as np
import jax
import jax.numpy as jnp
from jax import lax
from jax.experimental import pallas as pl
from jax.experimental.pallas import tpu as pltpu

F32 = jnp.float32
BF16 = jnp.bfloat16

D_MODEL = 1024
PLE_DIM = 256
D_FF = 2816
NORM_EPS = 1e-6
NEG = -1e30
NUM_BUCKETS = 32
MAX_DISTANCE = 128
A_HEADS = 4
A_HEAD_DIM = 64
B_HEADS = 8
B_GROUPS = 2
B_REP = B_HEADS // B_GROUPS
B_HEAD_DIM = 64
B_CMP_LEN = 32
B_CMP_STRIDE = 16
B_CMP_HIDDEN = 256
B_SEL_BLOCK = 64
B_SEL_TOPK = 16
B_WINDOW = 512
B_SEL_FORCE = 1e6
C_HEADS = 8
C_GROUPS = 2
C_REP = C_HEADS // C_GROUPS
C_HEAD_DIM = 64
C_WINDOW = 128
A_WIDTH = A_HEADS * 2 * A_HEAD_DIM
B_WIDTH = B_HEADS * B_HEAD_DIM
B_KV = B_GROUPS * B_HEAD_DIM
C_WIDTH = C_HEADS * C_HEAD_DIM
C_KV = C_GROUPS * C_HEAD_DIM
N_BIAS_HEADS = A_HEADS + B_HEADS + C_HEADS

LANES = 128
VMEM_LIMIT = 56 * 1024 * 1024
UNSEL = -float(2 ** 30)
FFN_CHUNKS = ((0, 1536), (1536, 1280))


def _bucket_thresholds():
    n = np.arange(4 * MAX_DISTANCE)
    max_exact = NUM_BUCKETS // 2
    nf = np.maximum(n, 1).astype(np.float32)
    large = max_exact + (np.log(nf / max_exact) / math.log(MAX_DISTANCE / max_exact)
                         * (NUM_BUCKETS - max_exact)).astype(np.int32)
    bucket = np.where(n < max_exact, n, np.minimum(large, NUM_BUCKETS - 1))
    out = []
    for b in range(1, NUM_BUCKETS):
        hit = np.nonzero(bucket == b)[0]
        if hit.size:
            out.append((b, int(hit[0])))
    return tuple(out)


BUCKET_LO = _bucket_thresholds()


def _bias_of_dist(dist, tbl_ref, head):
    out = jnp.full(dist.shape, tbl_ref[0, head], F32)
    for b, lo in BUCKET_LO:
        out = jnp.where(dist >= lo, tbl_ref[b, head], out)
    return out


def _rms(x, g):
    return x * lax.rsqrt(jnp.mean(x * x, axis=-1, keepdims=True) + NORM_EPS) * g


def _dot(a, b):
    return jnp.dot(a, b, preferred_element_type=F32)


def _dot_nt(a, b):
    return lax.dot_general(a, b, (((1,), (1,)), ((), ())), preferred_element_type=F32)


def _online_update(carry, s, v):
    m, l, acc = carry
    m_new = jnp.maximum(m, jnp.max(s, axis=-1, keepdims=True))
    alpha = jnp.exp(m - m_new)
    p = jnp.exp(s - m_new)
    l = alpha * l + jnp.sum(p, axis=-1, keepdims=True)
    acc = alpha * acc + _dot(p.astype(BF16), v)
    return m_new, l, acc


def _softmax_init(rows, width):
    return (jnp.full((rows, 1), NEG, F32), jnp.zeros((rows, 1), F32), jnp.zeros((rows, width), F32))


def _params(*sem):
    return pltpu.CompilerParams(dimension_semantics=sem, vmem_limit_bytes=VMEM_LIMIT)


def _resident(shape):
    zeros = (0,) * len(shape)
    return pl.BlockSpec(shape, lambda *_: zeros, pipeline_mode=pl.Buffered(1))


def _smem():
    return pl.BlockSpec(memory_space=pltpu.SMEM)


def _ffn_kernel(h_ref, g_ref, wi_ref, wo_ref, o_ref):
    x = h_ref[...]
    n = _rms(x, g_ref[...]).astype(BF16)
    acc = x
    for lo, width in FFN_CHUNKS:
        gate = _dot(n, wi_ref[:, lo:lo + width])
        up = _dot(n, wi_ref[:, D_FF + lo:D_FF + lo + width])
        act = (gate * jax.nn.sigmoid(gate) * up).astype(BF16)
        acc = acc + 0.5 * _dot(act, wo_ref[lo:lo + width, :])
    o_ref[...] = acc


def _ffn(h, g, wi, wo, tm):
    t = h.shape[0]
    tile = pl.BlockSpec((tm, D_MODEL), lambda i: (i, 0))
    return pl.pallas_call(
        _ffn_kernel,
        name="ffn",
        grid=(t // tm,),
        in_specs=[tile, _resident((1, D_MODEL)), _resident(wi.shape), _resident(wo.shape)],
        out_specs=tile,
        out_shape=jax.ShapeDtypeStruct(h.shape, F32),
        compiler_params=_params("arbitrary"),
    )(h, g.reshape(1, D_MODEL), wi, wo)


IN_A = 3 * A_WIDTH
IN_B = B_WIDTH + 4 * B_KV
IN_C = C_WIDTH + 2 * C_KV
IN_G = 3 * D_MODEL
IN_SECTIONS = (IN_A, IN_B, B_KV, B_KV, IN_C, IN_G, LANES)


def _inproj_kernel(h_ref, g_ref, w_ref, za_ref, zb_ref, zkc_ref, zvc_ref, zc_ref, gates_ref, bg_ref):
    n = _rms(h_ref[...], g_ref[...]).astype(BF16)
    lo = 0
    for ref, width in zip((za_ref, zb_ref, zkc_ref, zvc_ref, zc_ref, gates_ref, bg_ref), IN_SECTIONS):
        ref[...] = _dot(n, w_ref[:, lo:lo + width]).astype(ref.dtype)
        lo += width


def _inproj(h, g, w, tm):
    t = h.shape[0]
    dtypes = (BF16, BF16, BF16, BF16, BF16, F32, F32)
    return pl.pallas_call(
        _inproj_kernel,
        name="inproj",
        grid=(t // tm,),
        in_specs=[pl.BlockSpec((tm, D_MODEL), lambda i: (i, 0)), _resident((1, D_MODEL)), _resident(w.shape)],
        out_specs=[pl.BlockSpec((tm, width), lambda i: (i, 0)) for width in IN_SECTIONS],
        out_shape=[jax.ShapeDtypeStruct((t, width), dt) for width, dt in zip(IN_SECTIONS, dtypes)],
        compiler_params=_params("arbitrary"),
    )(h, g.reshape(1, D_MODEL), w)


def _post_kernel(h_ref, ya_ref, yb_ref, yc_ref, gates_ref, wb_ref, wo_ref, o_ref):
    merged = jnp.zeros(h_ref.shape, F32)
    for m, y_ref in enumerate((ya_ref, yb_ref, yc_ref)):
        gate = jax.nn.sigmoid(gates_ref[:, m * D_MODEL:(m + 1) * D_MODEL])
        merged = merged + gate * _dot(y_ref[...], wb_ref[m])
    o_ref[...] = h_ref[...] + _dot(merged.astype(BF16), wo_ref[...])


def _post(h, ya, yb, yc, gates, wb, wo, tm):
    t = h.shape[0]
    tile = pl.BlockSpec((tm, D_MODEL), lambda i: (i, 0))
    ytile = pl.BlockSpec((tm, A_WIDTH), lambda i: (i, 0))
    return pl.pallas_call(
        _post_kernel,
        name="post",
        grid=(t // tm,),
        in_specs=[tile, ytile, ytile, ytile, pl.BlockSpec((tm, IN_G), lambda i: (i, 0)),
                  _resident(wb.shape), _resident(wo.shape)],
        out_specs=tile,
        out_shape=jax.ShapeDtypeStruct(h.shape, F32),
        compiler_params=_params("arbitrary"),
    )(h, ya, yb, yc, gates, wb, wo)


def _ple_kernel(h_ref, p_ref, g_ref, wg_ref, we_ref, gf_ref, o_ref, *, final):
    x = h_ref[...]
    gate = jax.nn.sigmoid(_dot(_rms(x, g_ref[...]).astype(BF16), wg_ref[...]))
    x = x + gate * _dot(p_ref[...].astype(BF16), we_ref[...])
    if final:
        x = _rms(x, gf_ref[...])
    o_ref[...] = x


def _ple(h, p, g, wg, we, gf, tm, final):
    t = h.shape[0]
    tile = pl.BlockSpec((tm, D_MODEL), lambda i: (i, 0))
    return pl.pallas_call(
        functools.partial(_ple_kernel, final=final),
        name="ple",
        grid=(t // tm,),
        in_specs=[tile, pl.BlockSpec((tm, PLE_DIM), lambda i: (i, 0)), _resident((1, D_MODEL)),
                  _resident(wg.shape), _resident(we.shape), _resident((1, D_MODEL))],
        out_specs=tile,
        out_shape=jax.ShapeDtypeStruct(h.shape, F32),
        compiler_params=_params("arbitrary"),
    )(h, p, g.reshape(1, D_MODEL), wg, we, gf.reshape(1, D_MODEL))


def _diff_kernel(tbl_ref, lam_ref, subg_ref, q_ref, k_ref, v_ref, o_ref, bias_ref, *, tq, lam_init):
    head = pl.program_id(1)
    qi = pl.program_id(2)

    @pl.when(qi == 0)
    def _build_bias():
        row = lax.broadcasted_iota(jnp.int32, (tq, tq), 0)
        col = lax.broadcasted_iota(jnp.int32, (tq, tq), 1)
        for kind in (0, 1):
            dist = kind * tq + row - col
            tile = jnp.where(dist >= 0, _bias_of_dist(dist, tbl_ref, head), NEG)
            bias_ref[kind, 0:tq, :] = tile
            bias_ref[kind, tq:2 * tq, :] = tile
        bias_ref[2] = jnp.full((2 * tq, tq), tbl_ref[NUM_BUCKETS - 1, head], F32)

    q = q_ref[...]
    lane = lax.broadcasted_iota(jnp.int32, q.shape, 1)
    zero = jnp.zeros_like(q)
    qs = jnp.concatenate([jnp.where(lane < A_HEAD_DIM, q, zero), jnp.where(lane >= A_HEAD_DIM, q, zero)], axis=0)

    def body(j, carry):
        start = pl.multiple_of(j * tq, tq)
        s = _dot_nt(qs, k_ref[pl.ds(start, tq), :]) + bias_ref[jnp.minimum(qi - j, 2)]
        return _online_update(carry, s, v_ref[pl.ds(start, tq), :])

    _, l, acc = lax.fori_loop(0, qi + 1, body, _softmax_init(2 * tq, 2 * A_HEAD_DIM))
    out = acc / l
    lp = lam_ref[...]
    lam = (jnp.exp(jnp.sum(lp[0:1] * lp[1:2], axis=-1, keepdims=True))
           - jnp.exp(jnp.sum(lp[2:3] * lp[3:4], axis=-1, keepdims=True)) + lam_init)
    o = out[0:tq] - lam * out[tq:2 * tq]
    o_ref[...] = (_rms(o, subg_ref[...]) * (1.0 - lam_init)).astype(o_ref.dtype)


def _diff_attention(za, tbl, lam_p, subg, bsz, seq, lam_init, tq):
    nq = seq // tq
    width = 2 * A_HEAD_DIM
    return pl.pallas_call(
        functools.partial(_diff_kernel, tq=tq, lam_init=lam_init),
        name="diff_attn",
        grid=(bsz, A_HEADS, nq),
        in_specs=[_smem(), _resident(lam_p.shape), _resident((1, width)),
                  pl.BlockSpec((tq, width), lambda b, h, i: (b * nq + i, h)),
                  pl.BlockSpec((seq, width), lambda b, h, i: (b, A_HEADS + h)),
                  pl.BlockSpec((seq, width), lambda b, h, i: (b, 2 * A_HEADS + h))],
        out_specs=pl.BlockSpec((tq, width), lambda b, h, i: (b * nq + i, h)),
        out_shape=jax.ShapeDtypeStruct((bsz * seq, A_WIDTH), BF16),
        scratch_shapes=[pltpu.VMEM((3, 2 * tq, tq), F32)],
        compiler_params=_params("arbitrary", "arbitrary", "arbitrary"),
    )(tbl, lam_p, subg.reshape(1, width), za, za, za)


def _compress_kernel(xk_ref, xv_ref, pos_ref, w1_ref, w2_ref, kc_ref, vc_ref):
    rows = xk_ref.shape[0]
    for t, (x_ref, o_ref) in enumerate(((xk_ref, kc_ref), (xv_ref, vc_ref))):
        x = x_ref[...].astype(F32)
        first = _dot((x + pos_ref[t, 0]).astype(BF16), w1_ref[t, 0])
        second = _dot((x + pos_ref[t, 1]).astype(BF16), w1_ref[t, 1])
        hidden = first + pltpu.roll(second, rows - 1, 0)
        o_ref[...] = _dot(jax.nn.gelu(hidden).astype(BF16), w2_ref[t]).astype(o_ref.dtype)


def _compress(xk, xv, pos, w1, w2):
    bsz, rows, width = xk.shape
    xspec = pl.BlockSpec((None, rows, width), lambda b: (b, 0, 0))
    ospec = pl.BlockSpec((None, rows, B_KV), lambda b: (b, 0, 0))
    return pl.pallas_call(
        _compress_kernel,
        name="nsa_compress",
        grid=(bsz,),
        in_specs=[xspec, xspec, _resident(pos.shape), _resident(w1.shape), _resident(w2.shape)],
        out_specs=[ospec, ospec],
        out_shape=[jax.ShapeDtypeStruct((bsz, rows, B_KV), BF16)] * 2,
        compiler_params=_params("arbitrary"),
    )(xk, xv, pos, w1, w2)


def _stack_heads(q, group, rep):
    tq = q.shape[0]
    upper = lax.broadcasted_iota(jnp.int32, (tq, LANES), 1) >= LANES // 2
    keep = upper if group == 1 else ~upper
    parts = []
    for r in range(rep):
        head = group * rep + r
        slab = q[:, (head // 2) * LANES:(head // 2 + 1) * LANES]
        if head % 2 != group:
            slab = pltpu.roll(slab, LANES // 2, 1)
        parts.append(jnp.where(keep, slab, jnp.zeros_like(slab)))
    return jnp.concatenate(parts, axis=0)


def _store_heads(o_ref, outs, group, rep):
    tq = outs[0].shape[0]
    upper = lax.broadcasted_iota(jnp.int32, (tq, LANES), 1) >= LANES // 2
    for r in range(0, rep, 2):
        even, odd = outs[r], outs[r + 1]
        if group == 1:
            even = pltpu.roll(even, LANES // 2, 1)
        else:
            odd = pltpu.roll(odd, LANES // 2, 1)
        slab = (group * rep + r) // 2
        o_ref[:, slab * LANES:(slab + 1) * LANES] = jnp.where(upper, odd, even).astype(o_ref.dtype)


def _build_band_bias(bias_ref, tbl_ref, head0, groups, rep, tq, window):
    row = lax.broadcasted_iota(jnp.int32, (tq, tq), 0)
    col = lax.broadcasted_iota(jnp.int32, (tq, tq), 1)
    for g in range(groups):
        for r in range(rep):
            head = head0 + g * rep + r
            rows = slice(r * tq, (r + 1) * tq)
            far = tbl_ref[NUM_BUCKETS - 1, head]
            for kind in (0, 1):
                dist = kind * tq + row - col
                if kind == 1 and window == tq:
                    visible = (dist >= 0) & (dist < window)
                else:
                    visible = dist >= 0
                bias_ref[g, kind, rows, :] = jnp.where(visible, _bias_of_dist(dist, tbl_ref, head), NEG)
            bias_ref[g, 2, rows, :] = jnp.full((tq, tq), far, F32)
            bias_ref[g, 3, rows, :] = jnp.where(row < col, far, NEG)


def _nsa_kernel(tbl_ref, q_ref, ks_ref, vs_ref, kw_ref, vw_ref, kc_ref, vc_ref, bg_ref, ovt_ref, o_ref,
                kaug_ref, bias_ref, *, tq, seq):
    first_batch = pl.program_id(0) == 0
    qi = pl.program_id(1)
    q0 = qi * tq
    n_cmp_pad = seq // B_CMP_STRIDE
    n_sel = seq // B_SEL_BLOCK
    top_k = min(B_SEL_TOPK, n_sel)
    rows = B_REP * tq

    @pl.when(first_batch & (qi == 0))
    def _build_bias():
        _build_band_bias(bias_ref, tbl_ref, A_HEADS, B_GROUPS, B_REP, tq, B_WINDOW)

    @pl.when(qi == 0)
    def _build_kaug():
        kaug_ref[:, 0:LANES] = ks_ref[...]
        key_blk = lax.broadcasted_iota(jnp.int32, (seq, LANES), 0) // B_SEL_BLOCK
        lane = lax.broadcasted_iota(jnp.int32, (seq, LANES), 1)
        kaug_ref[:, LANES:2 * LANES] = jnp.where(key_blk == lane, 1.0, 0.0).astype(BF16)

    q = q_ref[...]
    gates = jax.nn.sigmoid(bg_ref[...])

    qpos_c = q0 + lax.broadcasted_iota(jnp.int32, (tq, n_cmp_pad), 0)
    ncol = lax.broadcasted_iota(jnp.int32, (tq, n_cmp_pad), 1)
    dist_c = qpos_c - (ncol * B_CMP_STRIDE + B_CMP_LEN - 1)
    vis_c = (dist_c >= 0) & (ncol < n_cmp_pad - 1)
    blk = lax.broadcasted_iota(jnp.int32, (LANES, tq), 0)
    cur = (q0 + lax.broadcasted_iota(jnp.int32, (LANES, tq), 1)) // B_SEL_BLOCK
    forced = (blk == 0) | (blk == cur) | (blk == cur - 1)
    future = blk > cur

    for g in range(B_GROUPS):
        qg = _stack_heads(q, g, B_REP)
        s_c = _dot_nt(qg, kc_ref[...])
        probs = []
        for r in range(B_REP):
            s = s_c[r * tq:(r + 1) * tq] + _bias_of_dist(dist_c, tbl_ref, A_HEADS + g * B_REP + r)
            s = jnp.where(vis_c, s, NEG)
            e = jnp.where(vis_c, jnp.exp(s - jnp.max(s, axis=-1, keepdims=True)), 0.0)
            l = jnp.sum(e, axis=-1, keepdims=True)
            probs.append(jnp.where(l > 0.0, e / jnp.where(l > 0.0, l, 1.0), 0.0))
        p_c = jnp.concatenate(probs, axis=0)
        o_c = _dot(p_c.astype(BF16), vc_ref[...])
        p_sum = probs[0]
        for r in range(1, B_REP):
            p_sum = p_sum + probs[r]
        imp = _dot_nt(ovt_ref[...], p_sum.astype(BF16))
        imp = jnp.where(forced, B_SEL_FORCE, jnp.where(future, -B_SEL_FORCE, imp))
        rank = jnp.zeros((LANES, tq), F32)
        for other in range(n_sel):
            val = imp[other:other + 1, :]
            ahead = (val > imp) | ((val == imp) & (blk > other))
            rank = rank + jnp.where(ahead, 1.0, 0.0)
        sel_bias = jnp.where(rank < float(top_k), 0.0, UNSEL).T.astype(BF16)
        q_aug = jnp.concatenate([qg, jnp.concatenate([sel_bias] * B_REP, axis=0)], axis=1)

        def sel_body(j, carry):
            start = pl.multiple_of(j * tq, tq)
            s = _dot_nt(q_aug, kaug_ref[pl.ds(start, tq), :]) + bias_ref[g, jnp.minimum(qi - j, 2)]
            return _online_update(carry, s, vs_ref[pl.ds(start, tq), :])

        _, l_s, acc_s = lax.fori_loop(0, qi + 1, sel_body, _softmax_init(rows, LANES))
        o_s = acc_s / l_s
        carry = _softmax_init(rows, LANES)
        n_back = B_WINDOW // tq
        for back in range(n_back, -1, -1):
            j = qi - back
            start = pl.multiple_of(jnp.maximum(j, 0) * tq, tq)
            kind = 3 if back == n_back else min(back, 2)
            s = _dot_nt(qg, kw_ref[pl.ds(start, tq), :]) + bias_ref[g, kind]
            if back > 0:
                s = s + jnp.where(j >= 0, 0.0, NEG)
            carry = _online_update(carry, s, vw_ref[pl.ds(start, tq), :])
        o_w = carry[2] / carry[1]
        outs = []
        for r in range(B_REP):
            rr = slice(r * tq, (r + 1) * tq)
            c0 = (g * B_REP + r) * 3
            outs.append(gates[:, c0:c0 + 1] * o_c[rr] + gates[:, c0 + 1:c0 + 2] * o_s[rr]
                        + gates[:, c0 + 2:c0 + 3] * o_w[rr])
        _store_heads(o_ref, outs, g, B_REP)


def _nsa_attention(zb, kc, vc, bgate, tbl, ovt, bsz, seq, tq):
    nq = seq // tq
    n_cmp_pad = seq // B_CMP_STRIDE
    kv = lambda col: pl.BlockSpec((seq, LANES), lambda b, i: (b, col))
    cmp_spec = pl.BlockSpec((None, n_cmp_pad, B_KV), lambda b, i: (b, 0, 0))
    qcols = B_WIDTH // LANES
    return pl.pallas_call(
        functools.partial(_nsa_kernel, tq=tq, seq=seq),
        name="nsa_attn",
        grid=(bsz, nq),
        in_specs=[_smem(),
                  pl.BlockSpec((tq, B_WIDTH), lambda b, i: (b * nq + i, 0)),
                  kv(qcols), kv(qcols + 1), kv(qcols + 2), kv(qcols + 3),
                  cmp_spec, cmp_spec,
                  pl.BlockSpec((tq, LANES), lambda b, i: (b * nq + i, 0)),
                  _resident(ovt.shape)],
        out_specs=pl.BlockSpec((tq, B_WIDTH), lambda b, i: (b * nq + i, 0)),
        out_shape=jax.ShapeDtypeStruct((bsz * seq, B_WIDTH), BF16),
        scratch_shapes=[pltpu.VMEM((seq, 2 * LANES), BF16),
                        pltpu.VMEM((B_GROUPS, 4, B_REP * tq, tq), F32)],
        compiler_params=_params("arbitrary", "arbitrary"),
    )(tbl, zb, zb, zb, zb, zb, kc, vc, bgate, ovt)


def _swa_kernel(tbl_ref, sink_ref, q_ref, kp_ref, kd_ref, vp_ref, vd_ref, o_ref, bias_ref, *, tq):
    qi = pl.program_id(1)

    @pl.when((pl.program_id(0) == 0) & (qi == 0))
    def _build_bias():
        _build_band_bias(bias_ref, tbl_ref, A_HEADS + B_HEADS, C_GROUPS, C_REP, tq, C_WINDOW)

    q = q_ref[...]
    rows = C_REP * tq
    prev_bias = jnp.where(qi > 0, 0.0, NEG)
    for g in range(C_GROUPS):
        qg = _stack_heads(q, g, C_REP)
        carry = _softmax_init(rows, LANES)
        carry = _online_update(carry, _dot_nt(qg, kp_ref[...]) + bias_ref[g, 1] + prev_bias, vp_ref[...])
        m, l, acc = _online_update(carry, _dot_nt(qg, kd_ref[...]) + bias_ref[g, 0], vd_ref[...])
        outs = []
        for r in range(C_REP):
            rr = slice(r * tq, (r + 1) * tq)
            sink = sink_ref[g * C_REP + r]
            m_all = jnp.maximum(m[rr], sink)
            scale = jnp.exp(m[rr] - m_all)
            outs.append(acc[rr] * (scale / (l[rr] * scale + jnp.exp(sink - m_all))))
        _store_heads(o_ref, outs, g, C_REP)


def _swa_attention(zc, tbl, sinks, bsz, seq, tq):
    nq = seq // tq
    qcols = C_WIDTH // LANES
    prev = lambda col: pl.BlockSpec((tq, LANES), lambda b, i: (b * nq + jnp.maximum(i - 1, 0), col))
    diag = lambda col: pl.BlockSpec((tq, LANES), lambda b, i: (b * nq + i, col))
    return pl.pallas_call(
        functools.partial(_swa_kernel, tq=tq),
        name="swa_attn",
        grid=(bsz, nq),
        in_specs=[_smem(), _smem(),
                  pl.BlockSpec((tq, C_WIDTH), lambda b, i: (b * nq + i, 0)),
                  prev(qcols), diag(qcols), prev(qcols + 1), diag(qcols + 1)],
        out_specs=pl.BlockSpec((tq, C_WIDTH), lambda b, i: (b * nq + i, 0)),
        out_shape=jax.ShapeDtypeStruct((bsz * seq, C_WIDTH), BF16),
        scratch_shapes=[pltpu.VMEM((C_GROUPS, 4, C_REP * tq, tq), F32)],
        compiler_params=_params("arbitrary", "arbitrary"),
    )(tbl, sinks, zc, zc, zc, zc, zc)


def _prep_w_in(w_in):
    sizes = (A_WIDTH, A_WIDTH, A_WIDTH, B_WIDTH, B_KV, B_KV, B_KV, B_KV, B_KV, B_KV, 3 * B_HEADS,
             C_WIDTH, C_KV, C_KV, D_MODEL, D_MODEL, D_MODEL)
    offs = np.concatenate([[0], np.cumsum(sizes)])
    (aq, ak, av, bq, bkc, bvc, bks, bvs, bkw, bvw, bgate, cq, ck, cv, ga, gb, gc) = [
        w_in[..., offs[i]:offs[i + 1]] for i in range(len(sizes))]
    scale = A_HEAD_DIM ** -0.5
    bgate = jnp.pad(bgate, ((0, 0), (0, 0), (0, LANES - 3 * B_HEADS)))
    cols = [aq * scale, ak, av, bq * scale, bks, bvs, bkw, bvw, bkc, bvc, cq * scale, ck, cv, ga, gb, gc, bgate]
    return jnp.concatenate(cols, axis=-1).astype(BF16)


def _prep_compress(cmp_pos, cmp_w1, cmp_w2):
    depth = cmp_pos.shape[0]
    half = B_CMP_STRIDE
    pos = cmp_pos.reshape(depth, 2, 2, half, 1, B_HEAD_DIM)
    pos = jnp.broadcast_to(pos, (depth, 2, 2, half, B_GROUPS, B_HEAD_DIM)).reshape(depth, 2, 2, 1, half * B_KV)
    w1 = cmp_w1.reshape(depth, 2, 2, half, 1, B_HEAD_DIM, 1, B_CMP_HIDDEN)
    eye = jnp.eye(B_GROUPS, dtype=cmp_w1.dtype).reshape(1, 1, 1, 1, B_GROUPS, 1, B_GROUPS, 1)
    w1 = (w1 * eye).reshape(depth, 2, 2, half * B_KV, B_GROUPS * B_CMP_HIDDEN)
    w2 = cmp_w2.reshape(depth, 2, 1, B_CMP_HIDDEN, 1, B_HEAD_DIM)
    eye2 = jnp.eye(B_GROUPS, dtype=cmp_w2.dtype).reshape(1, 1, B_GROUPS, 1, B_GROUPS, 1)
    w2 = (w2 * eye2).reshape(depth, 2, B_GROUPS * B_CMP_HIDDEN, B_KV)
    return pos.astype(F32), w1.astype(BF16), w2.astype(BF16)


def _overlap_t(seq):
    n_cmp_pad = seq // B_CMP_STRIDE
    start = np.arange(n_cmp_pad) * B_CMP_STRIDE
    sel = np.arange(LANES) * B_SEL_BLOCK
    ov = (start[None, :] < sel[:, None] + B_SEL_BLOCK) & (start[None, :] + B_CMP_LEN > sel[:, None])
    ov &= (np.arange(n_cmp_pad) < n_cmp_pad - 1)[None, :] & (np.arange(LANES) < seq // B_SEL_BLOCK)[:, None]
    return jnp.asarray(ov.astype(np.float32), dtype=BF16)


def kernel(x, p, norm_g, ffn1_wi, ffn1_wo, w_in, diff_lambda, diff_subln, nsa_cmp_pos, nsa_cmp_w1,
           nsa_cmp_w2, swa_sinks, w_branch, w_out, ffn2_wi, ffn2_wo, w_ple, w_ple_gate, rel_bias, final_norm):
    bsz, seq, _ = x.shape
    depth = norm_g.shape[0]
    tokens = bsz * seq
    assert seq % 1024 == 0 and seq // B_SEL_BLOCK <= LANES
    tm, tm_post, tq_a, tq = 512, 256, 256, 128

    wi1, wo1 = ffn1_wi.astype(BF16), ffn1_wo.astype(BF16)
    wi2, wo2 = ffn2_wi.astype(BF16), ffn2_wo.astype(BF16)
    w_in_r = _prep_w_in(w_in)
    cmp_pos, cmp_w1, cmp_w2 = _prep_compress(nsa_cmp_pos, nsa_cmp_w1, nsa_cmp_w2)
    wb, wo = w_branch.astype(BF16), w_out.astype(BF16)
    wpg, wpe = w_ple_gate.astype(BF16), w_ple.astype(BF16)
    tbl = rel_bias.astype(F32)
    ovt = _overlap_t(seq)
    chunk_rows = seq // B_CMP_STRIDE

    h = x.reshape(tokens, D_MODEL)
    for i in range(depth):
        h = _ffn(h, norm_g[i, 0], wi1[i], wo1[i], tm)
        za, zb, zkc, zvc, zc, gates, bgate = _inproj(h, norm_g[i, 1], w_in_r[i], tm)
        lam_init = 0.8 - 0.6 * math.exp(-0.3 * i)
        ya = _diff_attention(za, tbl, diff_lambda[i].astype(F32), diff_subln[i], bsz, seq, lam_init, tq_a)
        kc, vc = _compress(zkc.reshape(bsz, chunk_rows, B_CMP_STRIDE * B_KV),
                           zvc.reshape(bsz, chunk_rows, B_CMP_STRIDE * B_KV), cmp_pos[i], cmp_w1[i], cmp_w2[i])
        yb = _nsa_attention(zb, kc, vc, bgate, tbl, ovt, bsz, seq, tq)
        yc = _swa_attention(zc, tbl, swa_sinks[i].astype(F32), bsz, seq, tq)
        h = _post(h, ya, yb, yc, gates, wb[i], wo[i], tm_post)
        h = _ffn(h, norm_g[i, 2], wi2[i], wo2[i], tm)
        h = _ple(h, p[i].reshape(tokens, PLE_DIM), norm_g[i, 3], wpg[i], wpe[i], final_norm, tm,
                 final=(i == depth - 1))
    return h.reshape(bsz, seq, D_MODEL)
```

```python
import functools
import math

import numpy as np
import jax
import jax.numpy as jnp
from jax import lax
from jax.experimental import pallas as pl
from jax.experimental.pallas import tpu as pltpu

F32 = jnp.float32
BF16 = jnp.bfloat16

D_MODEL = 1024
PLE_DIM = 256
D_FF = 2816
NORM_EPS = 1e-6
NEG = -1e30
NUM_BUCKETS = 32
MAX_DISTANCE = 128
A_HEADS = 4
A_HEAD_DIM = 64
B_HEADS = 8
B_GROUPS = 2
B_REP = B_HEADS // B_GROUPS
B_HEAD_DIM = 64
B_CMP_LEN = 32
B_CMP_STRIDE = 16
B_CMP_HIDDEN = 256
B_SEL_BLOCK = 64
B_SEL_TOPK = 16
B_WINDOW = 512
B_SEL_FORCE = 1e6
C_HEADS = 8
C_GROUPS = 2
C_REP = C_HEADS // C_GROUPS
C_HEAD_DIM = 64
C_WINDOW = 128
A_WIDTH = A_HEADS * 2 * A_HEAD_DIM
B_WIDTH = B_HEADS * B_HEAD_DIM
B_KV = B_GROUPS * B_HEAD_DIM
C_WIDTH = C_HEADS * C_HEAD_DIM
C_KV = C_GROUPS * C_HEAD_DIM
N_BIAS_HEADS = A_HEADS + B_HEADS + C_HEADS

LANES = 128
HEAD_DIM = 64
VMEM_LIMIT = 56 * 1024 * 1024
UNSEL = -float(2 ** 30)
FFN_CHUNKS = ((0, 1536), (1536, 1280))
CMP_BAND_BACK = 16


def _bucket_thresholds():
    n = np.arange(4 * MAX_DISTANCE)
    max_exact = NUM_BUCKETS // 2
    nf = np.maximum(n, 1).astype(np.float32)
    large = max_exact + (np.log(nf / max_exact) / math.log(MAX_DISTANCE / max_exact)
                         * (NUM_BUCKETS - max_exact)).astype(np.int32)
    bucket = np.where(n < max_exact, n, np.minimum(large, NUM_BUCKETS - 1))
    out = []
    for b in range(1, NUM_BUCKETS):
        hit = np.nonzero(bucket == b)[0]
        if hit.size:
            out.append((b, int(hit[0])))
    return tuple(out)


BUCKET_LO = _bucket_thresholds()
FAR_DIST = BUCKET_LO[-1][1]


def _bias_of_dist(dist, tbl_ref, head):
    out = jnp.full(dist.shape, tbl_ref[0, head], F32)
    for b, lo in BUCKET_LO:
        out = jnp.where(dist >= lo, tbl_ref[b, head], out)
    return out


def _rms(x, g):
    return x * lax.rsqrt(jnp.mean(x * x, axis=-1, keepdims=True) + NORM_EPS) * g


def _dot(a, b):
    return jnp.dot(a, b, preferred_element_type=F32)


def _online_update(carry, s, vts):
    m, l, acc = carry
    m_new = jnp.maximum(m, jnp.max(s, axis=0, keepdims=True))
    alpha = jnp.exp(m - m_new)
    p = jnp.exp(s - m_new)
    l = alpha * l + jnp.sum(p, axis=0, keepdims=True)
    p = p.astype(BF16)
    chunk = s.shape[0] // len(vts)
    acc = alpha * acc
    for c, vt in enumerate(vts):
        acc = acc + _dot(vt, p[c * chunk:(c + 1) * chunk])
    return m_new, l, acc


def _softmax_init(width, queries):
    return (jnp.full((1, queries), NEG, F32), jnp.zeros((1, queries), F32), jnp.zeros((width, queries), F32))


def _transpose_bf16(x):
    return x.astype(F32).T.astype(BF16)


def _params(*sem):
    return pltpu.CompilerParams(dimension_semantics=sem, vmem_limit_bytes=VMEM_LIMIT)


def _resident(shape):
    zeros = (0,) * len(shape)
    return pl.BlockSpec(shape, lambda *_: zeros, pipeline_mode=pl.Buffered(1))


def _smem():
    return pl.BlockSpec(memory_space=pltpu.SMEM)


def _ffn_kernel(h_ref, g_ref, wi_ref, wo_ref, o_ref):
    x = h_ref[...]
    n = _rms(x, g_ref[...]).astype(BF16)
    acc = x
    for lo, width in FFN_CHUNKS:
        gate = _dot(n, wi_ref[:, lo:lo + width])
        up = _dot(n, wi_ref[:, D_FF + lo:D_FF + lo + width])
        act = (gate * jax.nn.sigmoid(gate) * up).astype(BF16)
        acc = acc + 0.5 * _dot(act, wo_ref[lo:lo + width, :])
    o_ref[...] = acc


def _ffn(h, g, wi, wo, tm):
    t = h.shape[0]
    tile = pl.BlockSpec((tm, D_MODEL), lambda i: (i, 0))
    return pl.pallas_call(
        _ffn_kernel,
        name="ffn",
        grid=(t // tm,),
        in_specs=[tile, _resident((1, D_MODEL)), _resident(wi.shape), _resident(wo.shape)],
        out_specs=tile,
        out_shape=jax.ShapeDtypeStruct(h.shape, F32),
        compiler_params=_params("arbitrary"),
    )(h, g.reshape(1, D_MODEL), wi, wo)


IN_A = 3 * A_WIDTH
IN_B = B_WIDTH + 4 * B_KV
IN_C = C_WIDTH + 2 * C_KV
IN_G = 3 * D_MODEL
IN_SECTIONS = (IN_A, IN_B, B_KV, B_KV, IN_C, IN_G, LANES)


def _inproj_kernel(h_ref, g_ref, w_ref, za_ref, zb_ref, zkc_ref, zvc_ref, zc_ref, gates_ref, bg_ref):
    n = _rms(h_ref[...], g_ref[...]).astype(BF16)
    lo = 0
    for ref, width in zip((za_ref, zb_ref, zkc_ref, zvc_ref, zc_ref, gates_ref, bg_ref), IN_SECTIONS):
        ref[...] = _dot(n, w_ref[:, lo:lo + width]).astype(ref.dtype)
        lo += width


def _inproj(h, g, w, tm):
    t = h.shape[0]
    dtypes = (BF16, BF16, BF16, BF16, BF16, F32, F32)
    return pl.pallas_call(
        _inproj_kernel,
        name="inproj",
        grid=(t // tm,),
        in_specs=[pl.BlockSpec((tm, D_MODEL), lambda i: (i, 0)), _resident((1, D_MODEL)), _resident(w.shape)],
        out_specs=[pl.BlockSpec((tm, width), lambda i: (i, 0)) for width in IN_SECTIONS],
        out_shape=[jax.ShapeDtypeStruct((t, width), dt) for width, dt in zip(IN_SECTIONS, dtypes)],
        compiler_params=_params("arbitrary"),
    )(h, g.reshape(1, D_MODEL), w)


def _post_kernel(h_ref, ya_ref, yb_ref, yc_ref, gates_ref, wb_ref, wo_ref, o_ref):
    merged = jnp.zeros(h_ref.shape, F32)
    for m, y_ref in enumerate((ya_ref, yb_ref, yc_ref)):
        gate = jax.nn.sigmoid(gates_ref[:, m * D_MODEL:(m + 1) * D_MODEL])
        merged = merged + gate * _dot(y_ref[...], wb_ref[m])
    o_ref[...] = h_ref[...] + _dot(merged.astype(BF16), wo_ref[...])


def _post(h, ya, yb, yc, gates, wb, wo, tm):
    t = h.shape[0]
    tile = pl.BlockSpec((tm, D_MODEL), lambda i: (i, 0))
    ytile = pl.BlockSpec((tm, A_WIDTH), lambda i: (i, 0))
    return pl.pallas_call(
        _post_kernel,
        name="post",
        grid=(t // tm,),
        in_specs=[tile, ytile, ytile, ytile, pl.BlockSpec((tm, IN_G), lambda i: (i, 0)),
                  _resident(wb.shape), _resident(wo.shape)],
        out_specs=tile,
        out_shape=jax.ShapeDtypeStruct(h.shape, F32),
        compiler_params=_params("arbitrary"),
    )(h, ya, yb, yc, gates, wb, wo)


def _ple_kernel(h_ref, p_ref, g_ref, wg_ref, we_ref, gf_ref, o_ref, *, final):
    x = h_ref[...]
    gate = jax.nn.sigmoid(_dot(_rms(x, g_ref[...]).astype(BF16), wg_ref[...]))
    x = x + gate * _dot(p_ref[...].astype(BF16), we_ref[...])
    if final:
        x = _rms(x, gf_ref[...])
    o_ref[...] = x


def _ple(h, p, g, wg, we, gf, tm, final):
    t = h.shape[0]
    tile = pl.BlockSpec((tm, D_MODEL), lambda i: (i, 0))
    return pl.pallas_call(
        functools.partial(_ple_kernel, final=final),
        name="ple",
        grid=(t // tm,),
        in_specs=[tile, pl.BlockSpec((tm, PLE_DIM), lambda i: (i, 0)), _resident((1, D_MODEL)),
                  _resident(wg.shape), _resident(we.shape), _resident((1, D_MODEL))],
        out_specs=tile,
        out_shape=jax.ShapeDtypeStruct(h.shape, F32),
        compiler_params=_params("arbitrary"),
    )(h, p, g.reshape(1, D_MODEL), wg, we, gf.reshape(1, D_MODEL))


def _diff_kernel(tbl_ref, lam_ref, subg_ref, q_ref, k_ref, v_ref, o_ref, kaug_ref, vt_ref, near_ref,
                 *, tq, lam_init):
    first_call = (pl.program_id(0) == 0) & (pl.program_id(1) == 0)
    head = pl.program_id(1)
    qi = pl.program_id(2)
    width = 2 * A_HEAD_DIM

    @pl.when(first_call & (qi == 0))
    def _static_setup():
        lane = lax.broadcasted_iota(jnp.int32, (tq, 2 * LANES), 1)
        kaug_ref[0:tq, :] = jnp.where(lane == LANES, 1.0, 0.0).astype(BF16)
        kaug_ref[tq:, LANES:2 * LANES] = jnp.zeros((kaug_ref.shape[0] - tq, LANES), BF16)
        vt_ref[0] = jnp.zeros(vt_ref.shape[1:], BF16)

    @pl.when(qi == 0)
    def _per_head_setup():
        kaug_ref[tq:, 0:LANES] = k_ref[...]
        for c in range(vt_ref.shape[0] - 1):
            vt_ref[c + 1] = _transpose_bf16(v_ref[c * tq:(c + 1) * tq, :])
        key = lax.broadcasted_iota(jnp.int32, (2 * tq, tq), 0)
        qry = lax.broadcasted_iota(jnp.int32, (2 * tq, tq), 1)
        dist = tq + qry - key
        far = tbl_ref[NUM_BUCKETS - 1, head]
        tile = jnp.where(dist >= 0, _bias_of_dist(dist, tbl_ref, head) - far, NEG)
        near_ref[:, 0:tq] = tile
        near_ref[:, tq:2 * tq] = tile

    qt = q_ref[...].astype(F32).T
    zero = jnp.zeros((A_HEAD_DIM, tq), F32)
    qs = jnp.concatenate([jnp.concatenate([qt[0:A_HEAD_DIM], zero], axis=0),
                          jnp.concatenate([zero, qt[A_HEAD_DIM:width]], axis=0)], axis=1).astype(BF16)
    flag_row = lax.broadcasted_iota(jnp.int32, (LANES, 2 * tq), 0) == 0
    qs = jnp.concatenate([qs, jnp.where(flag_row, UNSEL, 0.0).astype(BF16)], axis=0)

    def group(first_tile, carry, bias):
        start = pl.multiple_of(first_tile * tq, tq)
        s = _dot(kaug_ref[pl.ds(start, 2 * tq), :], qs)
        if bias is not None:
            s = s + bias
        return _online_update(carry, s, [vt_ref[first_tile], vt_ref[first_tile + 1]])

    carry = group(qi, _softmax_init(width, 2 * tq), near_ref[...])
    _, l, acc = lax.fori_loop(1, (qi + 2) // 2, lambda gi, c: group(qi - 2 * gi, c, None), carry)
    out = acc / l
    lp = lam_ref[...]
    lam = (jnp.exp(jnp.sum(lp[0:1] * lp[1:2], axis=-1, keepdims=True))
           - jnp.exp(jnp.sum(lp[2:3] * lp[3:4], axis=-1, keepdims=True)) + lam_init)
    o = out[:, 0:tq] - lam * out[:, tq:2 * tq]
    o = o * lax.rsqrt(jnp.mean(o * o, axis=0, keepdims=True) + NORM_EPS) * (1.0 - lam_init)
    o_ref[...] = (o.T * subg_ref[...]).astype(o_ref.dtype)


def _diff_attention(za, tbl, lam_p, subg, bsz, seq, lam_init, tq):
    nq = seq // tq
    width = 2 * A_HEAD_DIM
    return pl.pallas_call(
        functools.partial(_diff_kernel, tq=tq, lam_init=lam_init),
        name="diff_attn",
        grid=(bsz, A_HEADS, nq),
        in_specs=[_smem(), _resident(lam_p.shape), _resident((1, width)),
                  pl.BlockSpec((tq, width), lambda b, h, i: (b * nq + i, h)),
                  pl.BlockSpec((seq, width), lambda b, h, i: (b, A_HEADS + h)),
                  pl.BlockSpec((seq, width), lambda b, h, i: (b, 2 * A_HEADS + h))],
        out_specs=pl.BlockSpec((tq, width), lambda b, h, i: (b * nq + i, h)),
        out_shape=jax.ShapeDtypeStruct((bsz * seq, A_WIDTH), BF16),
        scratch_shapes=[pltpu.VMEM((tq + seq, 2 * LANES), BF16), pltpu.VMEM((nq + 1, width, tq), BF16),
                        pltpu.VMEM((2 * tq, 2 * tq), F32)],
        compiler_params=_params("arbitrary", "arbitrary", "arbitrary"),
    )(tbl, lam_p, subg.reshape(1, width), za, za, za)


def _compress_kernel(xk_ref, xv_ref, pos_ref, w1_ref, w2_ref, kc_ref, vc_ref):
    rows = xk_ref.shape[0]
    for t, (x_ref, o_ref) in enumerate(((xk_ref, kc_ref), (xv_ref, vc_ref))):
        x = x_ref[...].astype(F32)
        first = _dot((x + pos_ref[t, 0]).astype(BF16), w1_ref[t, 0])
        second = _dot((x + pos_ref[t, 1]).astype(BF16), w1_ref[t, 1])
        hidden = first + pltpu.roll(second, rows - 1, 0)
        o_ref[...] = _dot(jax.nn.gelu(hidden).astype(BF16), w2_ref[t]).astype(o_ref.dtype)


def _compress(xk, xv, pos, w1, w2):
    bsz, rows, width = xk.shape
    xspec = pl.BlockSpec((None, rows, width), lambda b: (b, 0, 0))
    ospec = pl.BlockSpec((None, rows, B_KV), lambda b: (b, 0, 0))
    return pl.pallas_call(
        _compress_kernel,
        name="nsa_compress",
        grid=(bsz,),
        in_specs=[xspec, xspec, _resident(pos.shape), _resident(w1.shape), _resident(w2.shape)],
        out_specs=[ospec, ospec],
        out_shape=[jax.ShapeDtypeStruct((bsz, rows, B_KV), BF16)] * 2,
        compiler_params=_params("arbitrary"),
    )(xk, xv, pos, w1, w2)


def _group_queries_t(q, group, rep):
    tq = q.shape[0]
    zero = jnp.zeros((HEAD_DIM, tq), F32)
    cols = []
    for r in range(rep):
        head = group * rep + r
        slab = q[:, (head // 2) * LANES:(head // 2 + 1) * LANES].astype(F32).T
        part = slab[(head % 2) * HEAD_DIM:(head % 2 + 1) * HEAD_DIM]
        cols.append(jnp.concatenate([part, zero] if group == 0 else [zero, part], axis=0))
    return jnp.concatenate(cols, axis=1).astype(BF16)


def _build_near_bias(near_ref, tbl_ref, head0, groups, rep, tq, window_is_tile, minus_far):
    key = lax.broadcasted_iota(jnp.int32, (2 * tq, tq), 0)
    qry = lax.broadcasted_iota(jnp.int32, (2 * tq, tq), 1)
    dist = tq + qry - key
    visible = (dist >= 0) & (dist < tq) if window_is_tile else dist >= 0
    for g in range(groups):
        for r in range(rep):
            head = head0 + g * rep + r
            shift = tbl_ref[NUM_BUCKETS - 1, head] if minus_far else 0.0
            near_ref[g, :, r * tq:(r + 1) * tq] = jnp.where(visible, _bias_of_dist(dist, tbl_ref, head) - shift, NEG)


SEL_GROUP = 4
SEL_PAD = SEL_GROUP - 1
WIN_PAD = 4


def _nsa_kernel(tbl_ref, q_ref, ks_ref, vs_ref, kw_ref, vw_ref, kc_ref, vc_ref, bg_ref, ovt_ref, o_ref,
                kaug_ref, kwp_ref, vst_ref, vwt_ref, vct_ref, near_ref, edge_ref, band_ref, sc_ref, *, tq, seq):
    first_batch = pl.program_id(0) == 0
    qi = pl.program_id(1)
    n_cmp_pad = seq // B_CMP_STRIDE
    n_sel = seq // B_SEL_BLOCK
    top_k = min(B_SEL_TOPK, n_sel)
    nq = seq // tq
    cols = B_REP * tq
    cmp_per_tile = tq // B_CMP_STRIDE
    band = CMP_BAND_BACK + cmp_per_tile

    @pl.when(first_batch & (qi == 0))
    def _static_setup():
        _build_near_bias(near_ref, tbl_ref, A_HEADS, B_GROUPS, B_REP, tq, False, True)
        key = lax.broadcasted_iota(jnp.int32, (tq, cols), 0)
        qry = lax.broadcasted_iota(jnp.int32, (tq, cols), 1) % tq
        edge_ref[...] = jnp.where(qry < key, 0.0, NEG)
        blk = lax.broadcasted_iota(jnp.int32, (band, tq), 0) - CMP_BAND_BACK
        dist = lax.broadcasted_iota(jnp.int32, (band, tq), 1) - (blk * B_CMP_STRIDE + B_CMP_LEN - 1)
        for g in range(B_GROUPS):
            for r in range(B_REP):
                head = A_HEADS + g * B_REP + r
                far = tbl_ref[NUM_BUCKETS - 1, head]
                band_ref[g, :, r * tq:(r + 1) * tq] = jnp.where(dist >= 0, _bias_of_dist(dist, tbl_ref, head) - far, NEG)
        sc_ref[0:CMP_BAND_BACK, :] = jnp.zeros((CMP_BAND_BACK, cols), F32)
        pad = SEL_PAD * tq
        row = lax.broadcasted_iota(jnp.int32, (pad + seq, LANES), 0)
        lane = lax.broadcasted_iota(jnp.int32, (pad + seq, LANES), 1)
        blk_id = jnp.where(row < pad, LANES - 1, (row - pad) // B_SEL_BLOCK)
        kaug_ref[:, LANES:2 * LANES] = jnp.where(blk_id == lane, 1.0, 0.0).astype(BF16)
        kaug_ref[0:pad, 0:LANES] = jnp.zeros((pad, LANES), BF16)
        kwp_ref[0:WIN_PAD * tq, :] = jnp.zeros((WIN_PAD * tq, LANES), BF16)
        for c in range(SEL_PAD):
            vst_ref[c] = jnp.zeros(vst_ref.shape[1:], BF16)
        for c in range(WIN_PAD):
            vwt_ref[c] = jnp.zeros(vwt_ref.shape[1:], BF16)

    @pl.when(qi == 0)
    def _per_batch_setup():
        kaug_ref[SEL_PAD * tq:, 0:LANES] = ks_ref[...]
        kwp_ref[WIN_PAD * tq:, :] = kw_ref[...]
        for c in range(nq):
            vst_ref[SEL_PAD + c] = _transpose_bf16(vs_ref[c * tq:(c + 1) * tq, :])
            vwt_ref[WIN_PAD + c] = _transpose_bf16(vw_ref[c * tq:(c + 1) * tq, :])
        for c in range(n_cmp_pad // LANES):
            vct_ref[:, c * LANES:(c + 1) * LANES] = _transpose_bf16(vc_ref[c * LANES:(c + 1) * LANES, :])

    q = q_ref[...]
    gates_t = jax.nn.sigmoid(bg_ref[...]).T
    first_cmp = qi * cmp_per_tile
    cmp_row = lax.broadcasted_iota(jnp.int32, (n_cmp_pad, cols), 0)
    cmp_visible = cmp_row < first_cmp + cmp_per_tile
    blk = lax.broadcasted_iota(jnp.int32, (LANES, tq), 0)
    cur = (qi * tq + lax.broadcasted_iota(jnp.int32, (LANES, tq), 1)) // B_SEL_BLOCK
    forced = (blk == 0) | (blk == cur) | (blk == cur - 1)
    future = blk > cur

    heads_out = []
    for g in range(B_GROUPS):
        half = slice(g * HEAD_DIM, (g + 1) * HEAD_DIM)
        qg = _group_queries_t(q, g, B_REP)
        sc_ref[CMP_BAND_BACK:, :] = _dot(kc_ref[...], qg)
        band_rows = pl.ds(pl.multiple_of(first_cmp, 8), band)
        sc_ref[band_rows, :] = sc_ref[band_rows, :] + band_ref[g]
        s = jnp.where(cmp_visible, sc_ref[CMP_BAND_BACK:, :], NEG)
        e = jnp.where(s > 0.5 * NEG, jnp.exp(s - jnp.max(s, axis=0, keepdims=True)), 0.0)
        l = jnp.sum(e, axis=0, keepdims=True)
        p_c = e * jnp.where(l > 0.0, 1.0 / jnp.where(l > 0.0, l, 1.0), 0.0)
        o_c = _dot(vct_ref[half, :], p_c.astype(BF16))
        p_sum = p_c[:, 0:tq]
        for r in range(1, B_REP):
            p_sum = p_sum + p_c[:, r * tq:(r + 1) * tq]
        imp = _dot(ovt_ref[...], p_sum.astype(BF16))
        imp = jnp.where(forced, B_SEL_FORCE, jnp.where(future, -B_SEL_FORCE, imp))
        rank = jnp.zeros((LANES, tq), F32)
        for other in range(n_sel):
            val = imp[other:other + 1, :]
            ahead = (val > imp) | ((val == imp) & (blk > other))
            rank = rank + jnp.where(ahead, 1.0, 0.0)
        sel_bias = jnp.where(rank < float(top_k), 0.0, UNSEL).astype(BF16)
        q_aug = jnp.concatenate([qg, jnp.concatenate([sel_bias] * B_REP, axis=1)], axis=0)

        def sel_group(first_tile, carry, with_near):
            start = pl.multiple_of(first_tile * tq, tq)
            s = _dot(kaug_ref[pl.ds(start, SEL_GROUP * tq), :], q_aug)
            if with_near:
                far_rows = (SEL_GROUP - 2) * tq
                s = jnp.concatenate([s[0:far_rows], s[far_rows:] + near_ref[g]], axis=0)
            return _online_update(carry, s, [vst_ref[first_tile + t, half, :] for t in range(SEL_GROUP)])

        carry = sel_group(qi, _softmax_init(HEAD_DIM, cols), True)
        _, l_s, acc_s = lax.fori_loop(1, (qi + SEL_GROUP) // SEL_GROUP,
                                      lambda gi, c: sel_group(qi - SEL_GROUP * gi, c, False), carry)
        o_s = acc_s / l_s
        start = pl.multiple_of(qi * tq, tq)
        s_w = _dot(kwp_ref[pl.ds(start, (WIN_PAD + 1) * tq), :], qg)
        slabs = []
        for t in range(WIN_PAD + 1):
            slab = s_w[t * tq:(t + 1) * tq]
            if t == 0:
                slab = slab + edge_ref[...]
            if t >= WIN_PAD - 1:
                slab = slab + near_ref[g, (t - WIN_PAD + 1) * tq:(t - WIN_PAD + 2) * tq, :]
            if t < WIN_PAD:
                slab = slab + jnp.where(qi >= WIN_PAD - t, 0.0, NEG)
            slabs.append(slab)
        _, l_w, acc_w = _online_update(_softmax_init(HEAD_DIM, cols), jnp.concatenate(slabs, axis=0),
                                       [vwt_ref[qi + t, half, :] for t in range(WIN_PAD + 1)])
        o_w = acc_w / l_w
        for r in range(B_REP):
            cc = slice(r * tq, (r + 1) * tq)
            c0 = (g * B_REP + r) * 3
            heads_out.append(gates_t[c0:c0 + 1] * o_c[:, cc] + gates_t[c0 + 1:c0 + 2] * o_s[:, cc]
                             + gates_t[c0 + 2:c0 + 3] * o_w[:, cc])
    o_ref[...] = jnp.concatenate(heads_out, axis=0).T.astype(o_ref.dtype)


def _nsa_attention(zb, kc, vc, bgate, tbl, ovt, bsz, seq, tq):
    nq = seq // tq
    n_cmp_pad = seq // B_CMP_STRIDE
    cols = B_REP * tq
    kv = lambda col: pl.BlockSpec((seq, LANES), lambda b, i: (b, col))
    cmp_spec = pl.BlockSpec((None, n_cmp_pad, B_KV), lambda b, i: (b, 0, 0))
    qcols = B_WIDTH // LANES
    return pl.pallas_call(
        functools.partial(_nsa_kernel, tq=tq, seq=seq),
        name="nsa_attn",
        grid=(bsz, nq),
        in_specs=[_smem(),
                  pl.BlockSpec((tq, B_WIDTH), lambda b, i: (b * nq + i, 0)),
                  kv(qcols), kv(qcols + 1), kv(qcols + 2), kv(qcols + 3),
                  cmp_spec, cmp_spec,
                  pl.BlockSpec((tq, LANES), lambda b, i: (b * nq + i, 0)),
                  _resident(ovt.shape)],
        out_specs=pl.BlockSpec((tq, B_WIDTH), lambda b, i: (b * nq + i, 0)),
        out_shape=jax.ShapeDtypeStruct((bsz * seq, B_WIDTH), BF16),
        scratch_shapes=[pltpu.VMEM((SEL_PAD * tq + seq, 2 * LANES), BF16),
                        pltpu.VMEM((WIN_PAD * tq + seq, LANES), BF16),
                        pltpu.VMEM((SEL_PAD + nq, B_KV, tq), BF16),
                        pltpu.VMEM((WIN_PAD + nq, B_KV, tq), BF16),
                        pltpu.VMEM((B_KV, n_cmp_pad), BF16),
                        pltpu.VMEM((B_GROUPS, 2 * tq, cols), F32),
                        pltpu.VMEM((tq, cols), F32),
                        pltpu.VMEM((B_GROUPS, CMP_BAND_BACK + tq // B_CMP_STRIDE, cols), F32),
                        pltpu.VMEM((CMP_BAND_BACK + n_cmp_pad, cols), F32)],
        compiler_params=_params("arbitrary", "arbitrary"),
    )(tbl, zb, zb, zb, zb, zb, kc, vc, bgate, ovt)


def _swa_kernel(tbl_ref, sink_ref, q_ref, kp_ref, kd_ref, vp_ref, vd_ref, o_ref, bias_ref, *, tq):
    qi = pl.program_id(1)

    @pl.when((pl.program_id(0) == 0) & (qi == 0))
    def _build_tables():
        _build_near_bias(bias_ref, tbl_ref, A_HEADS + B_HEADS, C_GROUPS, C_REP, tq, True, False)

    q = q_ref[...]
    cols = C_REP * tq
    prev_bias = jnp.where(qi > 0, 0.0, NEG)
    vp_t = _transpose_bf16(vp_ref[...])
    vd_t = _transpose_bf16(vd_ref[...])
    heads_out = []
    for g in range(C_GROUPS):
        half = slice(g * HEAD_DIM, (g + 1) * HEAD_DIM)
        qg = _group_queries_t(q, g, C_REP)
        s = jnp.concatenate([_dot(kp_ref[...], qg) + bias_ref[g, 0:tq, :] + prev_bias,
                             _dot(kd_ref[...], qg) + bias_ref[g, tq:2 * tq, :]], axis=0)
        m, l, acc = _online_update(_softmax_init(HEAD_DIM, cols), s, [vp_t[half], vd_t[half]])
        for r in range(C_REP):
            cc = slice(r * tq, (r + 1) * tq)
            sink = sink_ref[g * C_REP + r]
            m_all = jnp.maximum(m[:, cc], sink)
            scale = jnp.exp(m[:, cc] - m_all)
            heads_out.append(acc[:, cc] * (scale / (l[:, cc] * scale + jnp.exp(sink - m_all))))
    o_ref[...] = jnp.concatenate(heads_out, axis=0).T.astype(o_ref.dtype)


def _swa_attention(zc, tbl, sinks, bsz, seq, tq):
    nq = seq // tq
    qcols = C_WIDTH // LANES
    prev = lambda col: pl.BlockSpec((tq, LANES), lambda b, i: (b * nq + jnp.maximum(i - 1, 0), col))
    diag = lambda col: pl.BlockSpec((tq, LANES), lambda b, i: (b * nq + i, col))
    return pl.pallas_call(
        functools.partial(_swa_kernel, tq=tq),
        name="swa_attn",
        grid=(bsz, nq),
        in_specs=[_smem(), _smem(),
                  pl.BlockSpec((tq, C_WIDTH), lambda b, i: (b * nq + i, 0)),
                  prev(qcols), diag(qcols), prev(qcols + 1), diag(qcols + 1)],
        out_specs=pl.BlockSpec((tq, C_WIDTH), lambda b, i: (b * nq + i, 0)),
        out_shape=jax.ShapeDtypeStruct((bsz * seq, C_WIDTH), BF16),
        scratch_shapes=[pltpu.VMEM((C_GROUPS, 2 * tq, C_REP * tq), F32)],
        compiler_params=_params("arbitrary", "arbitrary"),
    )(tbl, sinks, zc, zc, zc, zc, zc)


def _prep_w_in(w_in):
    sizes = (A_WIDTH, A_WIDTH, A_WIDTH, B_WIDTH, B_KV, B_KV, B_KV, B_KV, B_KV, B_KV, 3 * B_HEADS,
             C_WIDTH, C_KV, C_KV, D_MODEL, D_MODEL, D_MODEL)
    offs = np.concatenate([[0], np.cumsum(sizes)])
    (aq, ak, av, bq, bkc, bvc, bks, bvs, bkw, bvw, bgate, cq, ck, cv, ga, gb, gc) = [
        w_in[..., offs[i]:offs[i + 1]] for i in range(len(sizes))]
    scale = HEAD_DIM ** -0.5
    bgate = jnp.pad(bgate, ((0, 0), (0, 0), (0, LANES - 3 * B_HEADS)))
    cols = [aq * scale, ak, av, bq * scale, bks, bvs, bkw, bvw, bkc, bvc, cq * scale, ck, cv, ga, gb, gc, bgate]
    return jnp.concatenate(cols, axis=-1).astype(BF16)


def _prep_compress(cmp_pos, cmp_w1, cmp_w2):
    depth = cmp_pos.shape[0]
    half = B_CMP_STRIDE
    pos = cmp_pos.reshape(depth, 2, 2, half, 1, B_HEAD_DIM)
    pos = jnp.broadcast_to(pos, (depth, 2, 2, half, B_GROUPS, B_HEAD_DIM)).reshape(depth, 2, 2, 1, half * B_KV)
    w1 = cmp_w1.reshape(depth, 2, 2, half, 1, B_HEAD_DIM, 1, B_CMP_HIDDEN)
    eye = jnp.eye(B_GROUPS, dtype=cmp_w1.dtype).reshape(1, 1, 1, 1, B_GROUPS, 1, B_GROUPS, 1)
    w1 = (w1 * eye).reshape(depth, 2, 2, half * B_KV, B_GROUPS * B_CMP_HIDDEN)
    w2 = cmp_w2.reshape(depth, 2, 1, B_CMP_HIDDEN, 1, B_HEAD_DIM)
    eye2 = jnp.eye(B_GROUPS, dtype=cmp_w2.dtype).reshape(1, 1, B_GROUPS, 1, B_GROUPS, 1)
    w2 = (w2 * eye2).reshape(depth, 2, B_GROUPS * B_CMP_HIDDEN, B_KV)
    return pos.astype(F32), w1.astype(BF16), w2.astype(BF16)


def _overlap_t(seq):
    n_cmp_pad = seq // B_CMP_STRIDE
    start = np.arange(n_cmp_pad) * B_CMP_STRIDE
    sel = np.arange(LANES) * B_SEL_BLOCK
    ov = (start[None, :] < sel[:, None] + B_SEL_BLOCK) & (start[None, :] + B_CMP_LEN > sel[:, None])
    ov &= (np.arange(n_cmp_pad) < n_cmp_pad - 1)[None, :] & (np.arange(LANES) < seq // B_SEL_BLOCK)[:, None]
    return jnp.asarray(ov.astype(np.float32), dtype=BF16)


def kernel(x, p, norm_g, ffn1_wi, ffn1_wo, w_in, diff_lambda, diff_subln, nsa_cmp_pos, nsa_cmp_w1,
           nsa_cmp_w2, swa_sinks, w_branch, w_out, ffn2_wi, ffn2_wo, w_ple, w_ple_gate, rel_bias, final_norm):
    bsz, seq, _ = x.shape
    depth = norm_g.shape[0]
    tokens = bsz * seq
    assert seq % 1024 == 0 and seq // B_SEL_BLOCK <= LANES
    tm, tm_post, tq_a, tq = 512, 256, 256, 128
    assert FAR_DIST + B_CMP_LEN - 1 <= CMP_BAND_BACK * B_CMP_STRIDE and FAR_DIST <= tq
    assert WIN_PAD * tq == B_WINDOW and tq == C_WINDOW and seq // B_SEL_BLOCK < LANES

    wi1, wo1 = ffn1_wi.astype(BF16), ffn1_wo.astype(BF16)
    wi2, wo2 = ffn2_wi.astype(BF16), ffn2_wo.astype(BF16)
    w_in_r = _prep_w_in(w_in)
    cmp_pos, cmp_w1, cmp_w2 = _prep_compress(nsa_cmp_pos, nsa_cmp_w1, nsa_cmp_w2)
    wb, wo = w_branch.astype(BF16), w_out.astype(BF16)
    wpg, wpe = w_ple_gate.astype(BF16), w_ple.astype(BF16)
    tbl = rel_bias.astype(F32)
    ovt = _overlap_t(seq)
    chunk_rows = seq // B_CMP_STRIDE

    h = x.reshape(tokens, D_MODEL)
    for i in range(depth):
        h = _ffn(h, norm_g[i, 0], wi1[i], wo1[i], tm)
        za, zb, zkc, zvc, zc, gates, bgate = _inproj(h, norm_g[i, 1], w_in_r[i], tm)
        lam_init = 0.8 - 0.6 * math.exp(-0.3 * i)
        ya = _diff_attention(za, tbl, diff_lambda[i].astype(F32), diff_subln[i], bsz, seq, lam_init, tq_a)
        kc, vc = _compress(zkc.reshape(bsz, chunk_rows, B_CMP_STRIDE * B_KV),
                           zvc.reshape(bsz, chunk_rows, B_CMP_STRIDE * B_KV), cmp_pos[i], cmp_w1[i], cmp_w2[i])
        yb = _nsa_attention(zb, kc, vc, bgate, tbl, ovt, bsz, seq, tq)
        yc = _swa_attention(zc, tbl, swa_sinks[i].astype(F32), bsz, seq, tq)
        h = _post(h, ya, yb, yc, gates, wb[i], wo[i], tm_post)
        h = _ffn(h, norm_g[i, 2], wi2[i], wo2[i], tm)
        h = _ple(h, p[i].reshape(tokens, PLE_DIM), norm_g[i, 3], wpg[i], wpe[i], final_norm, tm,
                 final=(i == depth - 1))
    return h.reshape(bsz, seq, D_MODEL)
```

```python
import functools
import math

import numpy as np
import jax
import jax.numpy as jnp
from jax import lax
from jax.experimental import pallas as pl
from jax.experimental.pallas import tpu as pltpu

F32 = jnp.float32
BF16 = jnp.bfloat16

D_MODEL = 1024
PLE_DIM = 256
D_FF = 2816
NORM_EPS = 1e-6
NEG = -1e30
NUM_BUCKETS = 32
MAX_DISTANCE = 128
A_HEADS = 4
A_HEAD_DIM = 64
B_HEADS = 8
B_GROUPS = 2
B_REP = B_HEADS // B_GROUPS
B_HEAD_DIM = 64
B_CMP_LEN = 32
B_CMP_STRIDE = 16
B_CMP_HIDDEN = 256
B_SEL_BLOCK = 64
B_SEL_TOPK = 16
B_WINDOW = 512
B_SEL_FORCE = 1e6
C_HEADS = 8
C_GROUPS = 2
C_REP = C_HEADS // C_GROUPS
C_HEAD_DIM = 64
C_WINDOW = 128
A_WIDTH = A_HEADS * 2 * A_HEAD_DIM
B_WIDTH = B_HEADS * B_HEAD_DIM
B_KV = B_GROUPS * B_HEAD_DIM
C_WIDTH = C_HEADS * C_HEAD_DIM
C_KV = C_GROUPS * C_HEAD_DIM
N_BIAS_HEADS = A_HEADS + B_HEADS + C_HEADS

LANES = 128
HEAD_DIM = 64
VMEM_LIMIT = 56 * 1024 * 1024
UNSEL = -float(2 ** 30)
FFN_CHUNKS = ((0, 1536), (1536, 1280))
CMP_BAND_BACK = 16


def _bucket_thresholds():
    n = np.arange(4 * MAX_DISTANCE)
    max_exact = NUM_BUCKETS // 2
    nf = np.maximum(n, 1).astype(np.float32)
    large = max_exact + (np.log(nf / max_exact) / math.log(MAX_DISTANCE / max_exact)
                         * (NUM_BUCKETS - max_exact)).astype(np.int32)
    bucket = np.where(n < max_exact, n, np.minimum(large, NUM_BUCKETS - 1))
    out = []
    for b in range(1, NUM_BUCKETS):
        hit = np.nonzero(bucket == b)[0]
        if hit.size:
            out.append((b, int(hit[0])))
    return tuple(out)


BUCKET_LO = _bucket_thresholds()
FAR_DIST = BUCKET_LO[-1][1]


def _bias_of_dist(dist, tbl_ref, head):
    out = jnp.full(dist.shape, tbl_ref[0, head], F32)
    for b, lo in BUCKET_LO:
        out = jnp.where(dist >= lo, tbl_ref[b, head], out)
    return out


def _rms(x, g):
    return x * lax.rsqrt(jnp.mean(x * x, axis=-1, keepdims=True) + NORM_EPS) * g


def _dot(a, b):
    return jnp.dot(a, b, preferred_element_type=F32)


def _online_update(carry, s, vts):
    m, l, acc = carry
    m_new = jnp.maximum(m, jnp.max(s, axis=0, keepdims=True))
    alpha = jnp.exp(m - m_new)
    p = jnp.exp(s - m_new)
    l = alpha * l + jnp.sum(p, axis=0, keepdims=True)
    p = p.astype(BF16)
    chunk = s.shape[0] // len(vts)
    acc = alpha * acc
    for c, vt in enumerate(vts):
        acc = acc + _dot(vt, p[c * chunk:(c + 1) * chunk])
    return m_new, l, acc


def _softmax_init(width, queries):
    return (jnp.full((1, queries), NEG, F32), jnp.zeros((1, queries), F32), jnp.zeros((width, queries), F32))


def _transpose_bf16(x):
    return x.astype(F32).T.astype(BF16)


def _params(*sem):
    return pltpu.CompilerParams(dimension_semantics=sem, vmem_limit_bytes=VMEM_LIMIT)


def _resident(shape):
    zeros = (0,) * len(shape)
    return pl.BlockSpec(shape, lambda *_: zeros, pipeline_mode=pl.Buffered(1))


def _smem():
    return pl.BlockSpec(memory_space=pltpu.SMEM)


def _ffn_kernel(h_ref, g_ref, wi_ref, wo_ref, o_ref):
    x = h_ref[...]
    n = _rms(x, g_ref[...]).astype(BF16)
    acc = x
    for lo, width in FFN_CHUNKS:
        gate = _dot(n, wi_ref[:, lo:lo + width])
        up = _dot(n, wi_ref[:, D_FF + lo:D_FF + lo + width])
        act = (gate * jax.nn.sigmoid(gate) * up).astype(BF16)
        acc = acc + 0.5 * _dot(act, wo_ref[lo:lo + width, :])
    o_ref[...] = acc


def _ffn(h, g, wi, wo, tm):
    t = h.shape[0]
    tile = pl.BlockSpec((tm, D_MODEL), lambda i: (i, 0))
    return pl.pallas_call(
        _ffn_kernel,
        name="ffn",
        grid=(t // tm,),
        in_specs=[tile, _resident((1, D_MODEL)), _resident(wi.shape), _resident(wo.shape)],
        out_specs=tile,
        out_shape=jax.ShapeDtypeStruct(h.shape, F32),
        compiler_params=_params("arbitrary"),
    )(h, g.reshape(1, D_MODEL), wi, wo)


IN_A = 3 * A_WIDTH
IN_B = B_WIDTH + 4 * B_KV
IN_C = C_WIDTH + 2 * C_KV
IN_G = 3 * D_MODEL
IN_SECTIONS = (IN_A, IN_B, B_KV, B_KV, IN_C, IN_G, LANES)


def _inproj_kernel(h_ref, g_ref, w_ref, za_ref, zb_ref, zkc_ref, zvc_ref, zc_ref, gates_ref, bg_ref):
    n = _rms(h_ref[...], g_ref[...]).astype(BF16)
    lo = 0
    for ref, width in zip((za_ref, zb_ref, zkc_ref, zvc_ref, zc_ref, gates_ref, bg_ref), IN_SECTIONS):
        ref[...] = _dot(n, w_ref[:, lo:lo + width]).astype(ref.dtype)
        lo += width


def _inproj(h, g, w, tm):
    t = h.shape[0]
    dtypes = (BF16, BF16, BF16, BF16, BF16, F32, F32)
    return pl.pallas_call(
        _inproj_kernel,
        name="inproj",
        grid=(t // tm,),
        in_specs=[pl.BlockSpec((tm, D_MODEL), lambda i: (i, 0)), _resident((1, D_MODEL)), _resident(w.shape)],
        out_specs=[pl.BlockSpec((tm, width), lambda i: (i, 0)) for width in IN_SECTIONS],
        out_shape=[jax.ShapeDtypeStruct((t, width), dt) for width, dt in zip(IN_SECTIONS, dtypes)],
        compiler_params=_params("arbitrary"),
    )(h, g.reshape(1, D_MODEL), w)


def _post_kernel(h_ref, ya_ref, yb_ref, yc_ref, gates_ref, wb_ref, wo_ref, o_ref):
    merged = jnp.zeros(h_ref.shape, F32)
    for m, y_ref in enumerate((ya_ref, yb_ref, yc_ref)):
        gate = jax.nn.sigmoid(gates_ref[:, m * D_MODEL:(m + 1) * D_MODEL])
        merged = merged + gate * _dot(y_ref[...], wb_ref[m])
    o_ref[...] = h_ref[...] + _dot(merged.astype(BF16), wo_ref[...])


def _post(h, ya, yb, yc, gates, wb, wo, tm):
    t = h.shape[0]
    tile = pl.BlockSpec((tm, D_MODEL), lambda i: (i, 0))
    ytile = pl.BlockSpec((tm, A_WIDTH), lambda i: (i, 0))
    return pl.pallas_call(
        _post_kernel,
        name="post",
        grid=(t // tm,),
        in_specs=[tile, ytile, ytile, ytile, pl.BlockSpec((tm, IN_G), lambda i: (i, 0)),
                  _resident(wb.shape), _resident(wo.shape)],
        out_specs=tile,
        out_shape=jax.ShapeDtypeStruct(h.shape, F32),
        compiler_params=_params("arbitrary"),
    )(h, ya, yb, yc, gates, wb, wo)


def _ple_kernel(h_ref, p_ref, g_ref, wg_ref, we_ref, gf_ref, o_ref, *, final):
    x = h_ref[...]
    gate = jax.nn.sigmoid(_dot(_rms(x, g_ref[...]).astype(BF16), wg_ref[...]))
    x = x + gate * _dot(p_ref[...].astype(BF16), we_ref[...])
    if final:
        x = _rms(x, gf_ref[...])
    o_ref[...] = x


def _ple(h, p, g, wg, we, gf, tm, final):
    t = h.shape[0]
    tile = pl.BlockSpec((tm, D_MODEL), lambda i: (i, 0))
    return pl.pallas_call(
        functools.partial(_ple_kernel, final=final),
        name="ple",
        grid=(t // tm,),
        in_specs=[tile, pl.BlockSpec((tm, PLE_DIM), lambda i: (i, 0)), _resident((1, D_MODEL)),
                  _resident(wg.shape), _resident(we.shape), _resident((1, D_MODEL))],
        out_specs=tile,
        out_shape=jax.ShapeDtypeStruct(h.shape, F32),
        compiler_params=_params("arbitrary"),
    )(h, p, g.reshape(1, D_MODEL), wg, we, gf.reshape(1, D_MODEL))


def _diff_kernel(tbl_ref, lam_ref, subg_ref, q_ref, k_ref, v_ref, o_ref, kaug_ref, vt_ref, near_ref,
                 *, tq, lam_init):
    first_call = (pl.program_id(0) == 0) & (pl.program_id(1) == 0)
    qi = pl.program_id(2)
    width = 2 * A_HEAD_DIM
    n_heads = kaug_ref.shape[0]
    head0 = pl.program_id(1) * n_heads

    @pl.when(first_call & (qi == 0))
    def _static_setup():
        lane = lax.broadcasted_iota(jnp.int32, (tq, 2 * LANES), 1)
        for h in range(n_heads):
            kaug_ref[h, 0:tq, :] = jnp.where(lane == LANES, 1.0, 0.0).astype(BF16)
            kaug_ref[h, tq:, LANES:2 * LANES] = jnp.zeros((kaug_ref.shape[1] - tq, LANES), BF16)
            vt_ref[h, 0] = jnp.zeros(vt_ref.shape[2:], BF16)

    @pl.when(qi == 0)
    def _per_batch_setup():
        key = lax.broadcasted_iota(jnp.int32, (2 * tq, tq), 0)
        qry = lax.broadcasted_iota(jnp.int32, (2 * tq, tq), 1)
        dist = tq + qry - key
        for h in range(n_heads):
            cols = slice(h * width, (h + 1) * width)
            kaug_ref[h, tq:, 0:LANES] = k_ref[:, cols]
            for c in range(vt_ref.shape[1] - 1):
                vt_ref[h, c + 1] = _transpose_bf16(v_ref[c * tq:(c + 1) * tq, cols])
            far = tbl_ref[NUM_BUCKETS - 1, head0 + h]
            near_ref[h] = jnp.where(dist >= 0, _bias_of_dist(dist, tbl_ref, head0 + h) - far, NEG)

    zero = jnp.zeros((A_HEAD_DIM, tq), F32)
    flag_row = lax.broadcasted_iota(jnp.int32, (LANES, 2 * tq), 0) == 0
    flags = jnp.where(flag_row, UNSEL, 0.0)
    queries = []
    for h in range(n_heads):
        qt = q_ref[:, h * width:(h + 1) * width].astype(F32).T
        both = jnp.concatenate([jnp.concatenate([qt[0:A_HEAD_DIM], zero], axis=0),
                                jnp.concatenate([zero, qt[A_HEAD_DIM:width]], axis=0)], axis=1)
        queries.append(jnp.concatenate([both, flags], axis=0).astype(BF16))

    def logits_of(first_tile):
        start = pl.multiple_of(first_tile * tq, tq)
        return tuple(_dot(kaug_ref[h, pl.ds(start, 2 * tq), :], queries[h]) for h in range(n_heads))

    def update(first_tile, logits, carries):
        return tuple(_online_update(carries[h], logits[h], [vt_ref[h, first_tile], vt_ref[h, first_tile + 1]])
                     for h in range(n_heads))

    def body(gi, carries):
        first_tile = qi - 2 * gi
        return update(first_tile, logits_of(first_tile), carries)

    nearest = tuple(s + jnp.concatenate([near_ref[h], near_ref[h]], axis=1) for h, s in enumerate(logits_of(qi)))
    carries = update(qi, nearest, tuple(_softmax_init(width, 2 * tq) for _ in range(n_heads)))
    carries = lax.fori_loop(1, (qi + 2) // 2, body, carries)
    lp = lam_ref[...]
    lam = (jnp.exp(jnp.sum(lp[0:1] * lp[1:2], axis=-1, keepdims=True))
           - jnp.exp(jnp.sum(lp[2:3] * lp[3:4], axis=-1, keepdims=True)) + lam_init)
    for h in range(n_heads):
        _, l, acc = carries[h]
        out = acc / l
        o = out[:, 0:tq] - lam * out[:, tq:2 * tq]
        o = o * lax.rsqrt(jnp.mean(o * o, axis=0, keepdims=True) + NORM_EPS) * (1.0 - lam_init)
        o_ref[:, h * width:(h + 1) * width] = (o.T * subg_ref[...]).astype(o_ref.dtype)


def _diff_attention(za, tbl, lam_p, subg, bsz, seq, lam_init, tq, heads_per_step):
    nq = seq // tq
    width = 2 * A_HEAD_DIM
    steps = A_HEADS // heads_per_step
    wide = heads_per_step * width
    return pl.pallas_call(
        functools.partial(_diff_kernel, tq=tq, lam_init=lam_init),
        name="diff_attn",
        grid=(bsz, steps, nq),
        in_specs=[_smem(), _resident(lam_p.shape), _resident((1, width)),
                  pl.BlockSpec((tq, wide), lambda b, h, i: (b * nq + i, h)),
                  pl.BlockSpec((seq, wide), lambda b, h, i: (b, steps + h)),
                  pl.BlockSpec((seq, wide), lambda b, h, i: (b, 2 * steps + h))],
        out_specs=pl.BlockSpec((tq, wide), lambda b, h, i: (b * nq + i, h)),
        out_shape=jax.ShapeDtypeStruct((bsz * seq, A_WIDTH), BF16),
        scratch_shapes=[pltpu.VMEM((heads_per_step, tq + seq, 2 * LANES), BF16),
                        pltpu.VMEM((heads_per_step, nq + 1, width, tq), BF16),
                        pltpu.VMEM((heads_per_step, 2 * tq, tq), F32)],
        compiler_params=_params("arbitrary", "arbitrary", "arbitrary"),
    )(tbl, lam_p, subg.reshape(1, width), za, za, za)


def _compress_kernel(xk_ref, xv_ref, pos_ref, w1_ref, w2_ref, kc_ref, vc_ref):
    rows = xk_ref.shape[0]
    for t, (x_ref, o_ref) in enumerate(((xk_ref, kc_ref), (xv_ref, vc_ref))):
        x = x_ref[...].astype(F32)
        first = _dot((x + pos_ref[t, 0]).astype(BF16), w1_ref[t, 0])
        second = _dot((x + pos_ref[t, 1]).astype(BF16), w1_ref[t, 1])
        hidden = first + pltpu.roll(second, rows - 1, 0)
        o_ref[...] = _dot(jax.nn.gelu(hidden).astype(BF16), w2_ref[t]).astype(o_ref.dtype)


def _compress(xk, xv, pos, w1, w2):
    bsz, rows, width = xk.shape
    xspec = pl.BlockSpec((None, rows, width), lambda b: (b, 0, 0))
    ospec = pl.BlockSpec((None, rows, B_KV), lambda b: (b, 0, 0))
    return pl.pallas_call(
        _compress_kernel,
        name="nsa_compress",
        grid=(bsz,),
        in_specs=[xspec, xspec, _resident(pos.shape), _resident(w1.shape), _resident(w2.shape)],
        out_specs=[ospec, ospec],
        out_shape=[jax.ShapeDtypeStruct((bsz, rows, B_KV), BF16)] * 2,
        compiler_params=_params("arbitrary"),
    )(xk, xv, pos, w1, w2)


def _group_queries_t(q, group, rep):
    tq = q.shape[0]
    zero = jnp.zeros((HEAD_DIM, tq), F32)
    cols = []
    for r in range(rep):
        head = group * rep + r
        slab = q[:, (head // 2) * LANES:(head // 2 + 1) * LANES].astype(F32).T
        part = slab[(head % 2) * HEAD_DIM:(head % 2 + 1) * HEAD_DIM]
        cols.append(jnp.concatenate([part, zero] if group == 0 else [zero, part], axis=0))
    return jnp.concatenate(cols, axis=1).astype(BF16)


def _build_near_bias(near_ref, tbl_ref, head0, groups, rep, tq, window_is_tile, minus_far):
    key = lax.broadcasted_iota(jnp.int32, (2 * tq, tq), 0)
    qry = lax.broadcasted_iota(jnp.int32, (2 * tq, tq), 1)
    dist = tq + qry - key
    visible = (dist >= 0) & (dist < tq) if window_is_tile else dist >= 0
    for g in range(groups):
        for r in range(rep):
            head = head0 + g * rep + r
            shift = tbl_ref[NUM_BUCKETS - 1, head] if minus_far else 0.0
            near_ref[g, :, r * tq:(r + 1) * tq] = jnp.where(visible, _bias_of_dist(dist, tbl_ref, head) - shift, NEG)


SEL_GROUP = 4
SEL_PAD = SEL_GROUP - 1
WIN_PAD = 4


def _nsa_kernel(tbl_ref, q_ref, ks_ref, vs_ref, kw_ref, vw_ref, kc_ref, vc_ref, bg_ref, ovt_ref, o_ref,
                kaug_ref, kwp_ref, vst_ref, vwt_ref, vct_ref, near_ref, edge_ref, band_ref, sc_ref, imp_ref,
                *, tq, seq):
    first_batch = pl.program_id(0) == 0
    qi = pl.program_id(1)
    n_cmp_pad = seq // B_CMP_STRIDE
    n_sel = seq // B_SEL_BLOCK
    top_k = min(B_SEL_TOPK, n_sel)
    nq = seq // tq
    cols = B_REP * tq
    cmp_per_tile = tq // B_CMP_STRIDE
    band = CMP_BAND_BACK + cmp_per_tile

    @pl.when(first_batch & (qi == 0))
    def _static_setup():
        _build_near_bias(near_ref, tbl_ref, A_HEADS, B_GROUPS, B_REP, tq, False, True)
        key = lax.broadcasted_iota(jnp.int32, (tq, cols), 0)
        qry = lax.broadcasted_iota(jnp.int32, (tq, cols), 1) % tq
        edge_ref[...] = jnp.where(qry < key, 0.0, NEG)
        blk = lax.broadcasted_iota(jnp.int32, (band, tq), 0) - CMP_BAND_BACK
        dist = lax.broadcasted_iota(jnp.int32, (band, tq), 1) - (blk * B_CMP_STRIDE + B_CMP_LEN - 1)
        for g in range(B_GROUPS):
            for r in range(B_REP):
                head = A_HEADS + g * B_REP + r
                far = tbl_ref[NUM_BUCKETS - 1, head]
                band_ref[g, :, r * tq:(r + 1) * tq] = jnp.where(dist >= 0, _bias_of_dist(dist, tbl_ref, head) - far, NEG)
        sc_ref[:, 0:CMP_BAND_BACK, :] = jnp.zeros((B_GROUPS, CMP_BAND_BACK, cols), F32)
        pad = SEL_PAD * tq
        row = lax.broadcasted_iota(jnp.int32, (pad + seq, LANES), 0)
        lane = lax.broadcasted_iota(jnp.int32, (pad + seq, LANES), 1)
        blk_id = jnp.where(row < pad, LANES - 1, (row - pad) // B_SEL_BLOCK)
        kaug_ref[:, LANES:2 * LANES] = jnp.where(blk_id == lane, 1.0, 0.0).astype(BF16)
        kaug_ref[0:pad, 0:LANES] = jnp.zeros((pad, LANES), BF16)
        kwp_ref[0:WIN_PAD * tq, :] = jnp.zeros((WIN_PAD * tq, LANES), BF16)
        for c in range(SEL_PAD):
            vst_ref[c] = jnp.zeros(vst_ref.shape[1:], BF16)
        for c in range(WIN_PAD):
            vwt_ref[c] = jnp.zeros(vwt_ref.shape[1:], BF16)

    @pl.when(qi == 0)
    def _per_batch_setup():
        kaug_ref[SEL_PAD * tq:, 0:LANES] = ks_ref[...]
        kwp_ref[WIN_PAD * tq:, :] = kw_ref[...]
        for c in range(nq):
            vst_ref[SEL_PAD + c] = _transpose_bf16(vs_ref[c * tq:(c + 1) * tq, :])
            vwt_ref[WIN_PAD + c] = _transpose_bf16(vw_ref[c * tq:(c + 1) * tq, :])
        for c in range(n_cmp_pad // LANES):
            vct_ref[:, c * LANES:(c + 1) * LANES] = _transpose_bf16(vc_ref[c * LANES:(c + 1) * LANES, :])

    q = q_ref[...]
    gates_t = jax.nn.sigmoid(bg_ref[...]).T
    first_cmp = qi * cmp_per_tile
    cmp_row = lax.broadcasted_iota(jnp.int32, (n_cmp_pad, cols), 0)
    cmp_visible = cmp_row < first_cmp + cmp_per_tile
    blk = lax.broadcasted_iota(jnp.int32, (n_sel, tq), 0)
    cur = (qi * tq + lax.broadcasted_iota(jnp.int32, (n_sel, tq), 1)) // B_SEL_BLOCK
    forced = (blk == 0) | (blk == cur) | (blk == cur - 1)
    future = blk > cur
    halves = [slice(g * HEAD_DIM, (g + 1) * HEAD_DIM) for g in range(B_GROUPS)]

    qgs, o_cs, imps = [], [], []
    for g in range(B_GROUPS):
        qg = _group_queries_t(q, g, B_REP)
        sc_ref[g, CMP_BAND_BACK:, :] = _dot(kc_ref[...], qg)
        band_rows = pl.ds(pl.multiple_of(first_cmp, 8), band)
        sc_ref[g, band_rows, :] = sc_ref[g, band_rows, :] + band_ref[g]
        s = jnp.where(cmp_visible, sc_ref[g, CMP_BAND_BACK:, :], NEG)
        e = jnp.where(s > 0.5 * NEG, jnp.exp(s - jnp.max(s, axis=0, keepdims=True)), 0.0)
        l = jnp.sum(e, axis=0, keepdims=True)
        p_c = e * jnp.where(l > 0.0, 1.0 / jnp.where(l > 0.0, l, 1.0), 0.0)
        o_cs.append(_dot(vct_ref[halves[g], :], p_c.astype(BF16)))
        p_sum = p_c[:, 0:tq]
        for r in range(1, B_REP):
            p_sum = p_sum + p_c[:, r * tq:(r + 1) * tq]
        imp = _dot(ovt_ref[...], p_sum.astype(BF16))
        imp = jnp.where(forced, B_SEL_FORCE, jnp.where(future, -B_SEL_FORCE, imp))
        imp_ref[g] = imp
        qgs.append(qg)
        imps.append(imp)

    def rank_body(other, ranks):
        out = []
        for g in range(B_GROUPS):
            val = imp_ref[g, pl.ds(other, 1), :]
            ahead = (val > imps[g]) | ((val == imps[g]) & (blk > other))
            out.append(ranks[g] + jnp.where(ahead, 1.0, 0.0))
        return tuple(out)

    last_blk = ((qi + 1) * tq - 1) // B_SEL_BLOCK
    ranks = lax.fori_loop(0, last_blk + 1, rank_body, tuple(jnp.zeros((n_sel, tq), F32) for _ in range(B_GROUPS)))
    q_augs = []
    for g in range(B_GROUPS):
        sel_bias = jnp.concatenate([jnp.where(ranks[g] < float(top_k), 0.0, UNSEL),
                                    jnp.full((LANES - n_sel, tq), UNSEL, F32)], axis=0).astype(BF16)
        q_augs.append(jnp.concatenate([qgs[g], jnp.concatenate([sel_bias] * B_REP, axis=1)], axis=0))

    pair = cols
    chains = [(g, slice(0, cols)) for g in range(B_GROUPS)]
    heads_per_chain = pair // tq

    def sel_logits(first_tile):
        start = pl.multiple_of(first_tile * tq, tq)
        keys = kaug_ref[pl.ds(start, SEL_GROUP * tq), :]
        return tuple(_dot(keys, q_augs[g][:, cc]) for g, cc in chains)

    def sel_update(first_tile, logits, carries):
        return tuple(_online_update(carry, s, [vst_ref[first_tile + t, halves[g], :] for t in range(SEL_GROUP)])
                     for (g, cc), s, carry in zip(chains, logits, carries))

    def sel_body(gi, carries):
        first_tile = qi - SEL_GROUP * gi
        return sel_update(first_tile, sel_logits(first_tile), carries)

    far_rows = (SEL_GROUP - 2) * tq
    nearest = tuple(jnp.concatenate([s[0:far_rows], s[far_rows:] + near_ref[g, :, cc]], axis=0)
                    for (g, cc), s in zip(chains, sel_logits(qi)))
    sel = sel_update(qi, nearest, tuple(_softmax_init(HEAD_DIM, pair) for _ in chains))
    sel = lax.fori_loop(1, (qi + SEL_GROUP) // SEL_GROUP, sel_body, sel)

    start = pl.multiple_of(qi * tq, tq)
    win_keys = kwp_ref[pl.ds(start, (WIN_PAD + 1) * tq), :]
    win = []
    for g, cc in chains:
        s_w = _dot(win_keys, qgs[g][:, cc])
        slabs = []
        for t in range(WIN_PAD + 1):
            slab = s_w[t * tq:(t + 1) * tq]
            if t == 0:
                slab = slab + edge_ref[:, cc]
            if t >= WIN_PAD - 1:
                slab = slab + near_ref[g, (t - WIN_PAD + 1) * tq:(t - WIN_PAD + 2) * tq, cc]
            if t < WIN_PAD:
                slab = slab + jnp.where(qi >= WIN_PAD - t, 0.0, NEG)
            slabs.append(slab)
        win.append(_online_update(_softmax_init(HEAD_DIM, pair), jnp.concatenate(slabs, axis=0),
                                  [vwt_ref[qi + t, halves[g], :] for t in range(WIN_PAD + 1)]))

    heads_out = []
    for ci, (g, cc) in enumerate(chains):
        o_s = sel[ci][2] / sel[ci][1]
        o_w = win[ci][2] / win[ci][1]
        o_c = o_cs[g][:, cc]
        for rr in range(heads_per_chain):
            hc = slice(rr * tq, (rr + 1) * tq)
            c0 = (g * B_REP + rr) * 3
            heads_out.append(gates_t[c0:c0 + 1] * o_c[:, hc] + gates_t[c0 + 1:c0 + 2] * o_s[:, hc]
                             + gates_t[c0 + 2:c0 + 3] * o_w[:, hc])
    o_ref[...] = jnp.concatenate(heads_out, axis=0).T.astype(o_ref.dtype)


def _nsa_attention(zb, kc, vc, bgate, tbl, ovt, bsz, seq, tq):
    nq = seq // tq
    n_cmp_pad = seq // B_CMP_STRIDE
    cols = B_REP * tq
    kv = lambda col: pl.BlockSpec((seq, LANES), lambda b, i: (b, col))
    cmp_spec = pl.BlockSpec((None, n_cmp_pad, B_KV), lambda b, i: (b, 0, 0))
    qcols = B_WIDTH // LANES
    return pl.pallas_call(
        functools.partial(_nsa_kernel, tq=tq, seq=seq),
        name="nsa_attn",
        grid=(bsz, nq),
        in_specs=[_smem(),
                  pl.BlockSpec((tq, B_WIDTH), lambda b, i: (b * nq + i, 0)),
                  kv(qcols), kv(qcols + 1), kv(qcols + 2), kv(qcols + 3),
                  cmp_spec, cmp_spec,
                  pl.BlockSpec((tq, LANES), lambda b, i: (b * nq + i, 0)),
                  _resident(ovt.shape)],
        out_specs=pl.BlockSpec((tq, B_WIDTH), lambda b, i: (b * nq + i, 0)),
        out_shape=jax.ShapeDtypeStruct((bsz * seq, B_WIDTH), BF16),
        scratch_shapes=[pltpu.VMEM((SEL_PAD * tq + seq, 2 * LANES), BF16),
                        pltpu.VMEM((WIN_PAD * tq + seq, LANES), BF16),
                        pltpu.VMEM((SEL_PAD + nq, B_KV, tq), BF16),
                        pltpu.VMEM((WIN_PAD + nq, B_KV, tq), BF16),
                        pltpu.VMEM((B_KV, n_cmp_pad), BF16),
                        pltpu.VMEM((B_GROUPS, 2 * tq, cols), F32),
                        pltpu.VMEM((tq, cols), F32),
                        pltpu.VMEM((B_GROUPS, CMP_BAND_BACK + tq // B_CMP_STRIDE, cols), F32),
                        pltpu.VMEM((B_GROUPS, CMP_BAND_BACK + n_cmp_pad, cols), F32),
                        pltpu.VMEM((B_GROUPS, seq // B_SEL_BLOCK, tq), F32)],
        compiler_params=_params("arbitrary", "arbitrary"),
    )(tbl, zb, zb, zb, zb, zb, kc, vc, bgate, ovt)


def _swa_kernel(tbl_ref, sink_ref, q_ref, kp_ref, kd_ref, vp_ref, vd_ref, o_ref, bias_ref, *, tq):
    qi = pl.program_id(1)

    @pl.when((pl.program_id(0) == 0) & (qi == 0))
    def _build_tables():
        _build_near_bias(bias_ref, tbl_ref, A_HEADS + B_HEADS, C_GROUPS, C_REP, tq, True, False)

    q = q_ref[...]
    cols = C_REP * tq
    prev_bias = jnp.where(qi > 0, 0.0, NEG)
    vp_t = _transpose_bf16(vp_ref[...])
    vd_t = _transpose_bf16(vd_ref[...])
    heads_out = []
    for g in range(C_GROUPS):
        half = slice(g * HEAD_DIM, (g + 1) * HEAD_DIM)
        qg = _group_queries_t(q, g, C_REP)
        s = jnp.concatenate([_dot(kp_ref[...], qg) + bias_ref[g, 0:tq, :] + prev_bias,
                             _dot(kd_ref[...], qg) + bias_ref[g, tq:2 * tq, :]], axis=0)
        m, l, acc = _online_update(_softmax_init(HEAD_DIM, cols), s, [vp_t[half], vd_t[half]])
        for r in range(C_REP):
            cc = slice(r * tq, (r + 1) * tq)
            sink = sink_ref[g * C_REP + r]
            m_all = jnp.maximum(m[:, cc], sink)
            scale = jnp.exp(m[:, cc] - m_all)
            heads_out.append(acc[:, cc] * (scale / (l[:, cc] * scale + jnp.exp(sink - m_all))))
    o_ref[...] = jnp.concatenate(heads_out, axis=0).T.astype(o_ref.dtype)


def _swa_attention(zc, tbl, sinks, bsz, seq, tq):
    nq = seq // tq
    qcols = C_WIDTH // LANES
    prev = lambda col: pl.BlockSpec((tq, LANES), lambda b, i: (b * nq + jnp.maximum(i - 1, 0), col))
    diag = lambda col: pl.BlockSpec((tq, LANES), lambda b, i: (b * nq + i, col))
    return pl.pallas_call(
        functools.partial(_swa_kernel, tq=tq),
        name="swa_attn",
        grid=(bsz, nq),
        in_specs=[_smem(), _smem(),
                  pl.BlockSpec((tq, C_WIDTH), lambda b, i: (b * nq + i, 0)),
                  prev(qcols), diag(qcols), prev(qcols + 1), diag(qcols + 1)],
        out_specs=pl.BlockSpec((tq, C_WIDTH), lambda b, i: (b * nq + i, 0)),
        out_shape=jax.ShapeDtypeStruct((bsz * seq, C_WIDTH), BF16),
        scratch_shapes=[pltpu.VMEM((C_GROUPS, 2 * tq, C_REP * tq), F32)],
        compiler_params=_params("arbitrary", "arbitrary"),
    )(tbl, sinks, zc, zc, zc, zc, zc)


def _prep_w_in(w_in):
    sizes = (A_WIDTH, A_WIDTH, A_WIDTH, B_WIDTH, B_KV, B_KV, B_KV, B_KV, B_KV, B_KV, 3 * B_HEADS,
             C_WIDTH, C_KV, C_KV, D_MODEL, D_MODEL, D_MODEL)
    offs = np.concatenate([[0], np.cumsum(sizes)])
    (aq, ak, av, bq, bkc, bvc, bks, bvs, bkw, bvw, bgate, cq, ck, cv, ga, gb, gc) = [
        w_in[..., offs[i]:offs[i + 1]] for i in range(len(sizes))]
    scale = HEAD_DIM ** -0.5
    bgate = jnp.pad(bgate, ((0, 0), (0, 0), (0, LANES - 3 * B_HEADS)))
    cols = [aq * scale, ak, av, bq * scale, bks, bvs, bkw, bvw, bkc, bvc, cq * scale, ck, cv, ga, gb, gc, bgate]
    return jnp.concatenate(cols, axis=-1).astype(BF16)


def _prep_compress(cmp_pos, cmp_w1, cmp_w2):
    depth = cmp_pos.shape[0]
    half = B_CMP_STRIDE
    pos = cmp_pos.reshape(depth, 2, 2, half, 1, B_HEAD_DIM)
    pos = jnp.broadcast_to(pos, (depth, 2, 2, half, B_GROUPS, B_HEAD_DIM)).reshape(depth, 2, 2, 1, half * B_KV)
    w1 = cmp_w1.reshape(depth, 2, 2, half, 1, B_HEAD_DIM, 1, B_CMP_HIDDEN)
    eye = jnp.eye(B_GROUPS, dtype=cmp_w1.dtype).reshape(1, 1, 1, 1, B_GROUPS, 1, B_GROUPS, 1)
    w1 = (w1 * eye).reshape(depth, 2, 2, half * B_KV, B_GROUPS * B_CMP_HIDDEN)
    w2 = cmp_w2.reshape(depth, 2, 1, B_CMP_HIDDEN, 1, B_HEAD_DIM)
    eye2 = jnp.eye(B_GROUPS, dtype=cmp_w2.dtype).reshape(1, 1, B_GROUPS, 1, B_GROUPS, 1)
    w2 = (w2 * eye2).reshape(depth, 2, B_GROUPS * B_CMP_HIDDEN, B_KV)
    return pos.astype(F32), w1.astype(BF16), w2.astype(BF16)


def _overlap_t(seq):
    n_cmp_pad = seq // B_CMP_STRIDE
    start = np.arange(n_cmp_pad) * B_CMP_STRIDE
    sel = np.arange(seq // B_SEL_BLOCK) * B_SEL_BLOCK
    ov = (start[None, :] < sel[:, None] + B_SEL_BLOCK) & (start[None, :] + B_CMP_LEN > sel[:, None])
    ov &= (np.arange(n_cmp_pad) < n_cmp_pad - 1)[None, :]
    return jnp.asarray(ov.astype(np.float32), dtype=BF16)


def kernel(x, p, norm_g, ffn1_wi, ffn1_wo, w_in, diff_lambda, diff_subln, nsa_cmp_pos, nsa_cmp_w1,
           nsa_cmp_w2, swa_sinks, w_branch, w_out, ffn2_wi, ffn2_wo, w_ple, w_ple_gate, rel_bias, final_norm):
    bsz, seq, _ = x.shape
    depth = norm_g.shape[0]
    tokens = bsz * seq
    assert seq % 1024 == 0 and seq // B_SEL_BLOCK <= LANES
    tm, tm_post, tq_a, tq = 512, 256, 256, 128
    assert FAR_DIST + B_CMP_LEN - 1 <= CMP_BAND_BACK * B_CMP_STRIDE and FAR_DIST <= tq
    assert WIN_PAD * tq == B_WINDOW and tq == C_WINDOW and seq // B_SEL_BLOCK < LANES

    wi1, wo1 = ffn1_wi.astype(BF16), ffn1_wo.astype(BF16)
    wi2, wo2 = ffn2_wi.astype(BF16), ffn2_wo.astype(BF16)
    w_in_r = _prep_w_in(w_in)
    cmp_pos, cmp_w1, cmp_w2 = _prep_compress(nsa_cmp_pos, nsa_cmp_w1, nsa_cmp_w2)
    wb, wo = w_branch.astype(BF16), w_out.astype(BF16)
    wpg, wpe = w_ple_gate.astype(BF16), w_ple.astype(BF16)
    tbl = rel_bias.astype(F32)
    ovt = _overlap_t(seq)
    chunk_rows = seq // B_CMP_STRIDE

    h = x.reshape(tokens, D_MODEL)
    for i in range(depth):
        h = _ffn(h, norm_g[i, 0], wi1[i], wo1[i], tm)
        za, zb, zkc, zvc, zc, gates, bgate = _inproj(h, norm_g[i, 1], w_in_r[i], tm)
        lam_init = 0.8 - 0.6 * math.exp(-0.3 * i)
        ya = _diff_attention(za, tbl, diff_lambda[i].astype(F32), diff_subln[i], bsz, seq, lam_init, tq_a, 4)
        kc, vc = _compress(zkc.reshape(bsz, chunk_rows, B_CMP_STRIDE * B_KV),
                           zvc.reshape(bsz, chunk_rows, B_CMP_STRIDE * B_KV), cmp_pos[i], cmp_w1[i], cmp_w2[i])
        yb = _nsa_attention(zb, kc, vc, bgate, tbl, ovt, bsz, seq, tq)
        yc = _swa_attention(zc, tbl, swa_sinks[i].astype(F32), bsz, seq, tq)
        h = _post(h, ya, yb, yc, gates, wb[i], wo[i], tm_post)
        h = _ffn(h, norm_g[i, 2], wi2[i], wo2[i], tm)
        h = _ple(h, p[i].reshape(tokens, PLE_DIM), norm_g[i, 3], wpg[i], wpe[i], final_norm, tm,
                 final=(i == depth - 1))
    return h.reshape(bsz, seq, D_MODEL)
```

```python
import functools
import math

import numpy as np
import jax
import jax.numpy as jnp
from jax import lax
from jax.experimental import pallas as pl
from jax.experimental.pallas import tpu as pltpu

F32 = jnp.float32
BF16 = jnp.bfloat16

D_MODEL = 1024
PLE_DIM = 256
D_FF = 2816
NORM_EPS = 1e-6
NEG = -1e30
NUM_BUCKETS = 32
MAX_DISTANCE = 128
A_HEADS = 4
A_HEAD_DIM = 64
B_HEADS = 8
B_GROUPS = 2
B_REP = B_HEADS // B_GROUPS
B_HEAD_DIM = 64
B_CMP_LEN = 32
B_CMP_STRIDE = 16
B_CMP_HIDDEN = 256
B_SEL_BLOCK = 64
B_SEL_TOPK = 16
B_WINDOW = 512
B_SEL_FORCE = 1e6
C_HEADS = 8
C_GROUPS = 2
C_REP = C_HEADS // C_GROUPS
C_HEAD_DIM = 64
C_WINDOW = 128
A_WIDTH = A_HEADS * 2 * A_HEAD_DIM
B_WIDTH = B_HEADS * B_HEAD_DIM
B_KV = B_GROUPS * B_HEAD_DIM
C_WIDTH = C_HEADS * C_HEAD_DIM
C_KV = C_GROUPS * C_HEAD_DIM
N_BIAS_HEADS = A_HEADS + B_HEADS + C_HEADS

LANES = 128
HEAD_DIM = 64
VMEM_LIMIT = 56 * 1024 * 1024
UNSEL = -float(2 ** 30)
FFN_CHUNKS = ((0, 1536), (1536, 1280))
CMP_BAND_BACK = 16
LOG2E = math.log2(math.e)
ONES_ROWS = 16


def _bucket_thresholds():
    n = np.arange(4 * MAX_DISTANCE)
    max_exact = NUM_BUCKETS // 2
    nf = np.maximum(n, 1).astype(np.float32)
    large = max_exact + (np.log(nf / max_exact) / math.log(MAX_DISTANCE / max_exact)
                         * (NUM_BUCKETS - max_exact)).astype(np.int32)
    bucket = np.where(n < max_exact, n, np.minimum(large, NUM_BUCKETS - 1))
    out = []
    for b in range(1, NUM_BUCKETS):
        hit = np.nonzero(bucket == b)[0]
        if hit.size:
            out.append((b, int(hit[0])))
    return tuple(out)


BUCKET_LO = _bucket_thresholds()
FAR_DIST = BUCKET_LO[-1][1]


def _bias_of_dist(dist, tbl_ref, head, minus_far=False):
    shift = tbl_ref[NUM_BUCKETS - 1, head] if minus_far else 0.0
    out = jnp.full(dist.shape, (tbl_ref[0, head] - shift) * LOG2E, F32)
    for b, lo in BUCKET_LO:
        out = jnp.where(dist >= lo, (tbl_ref[b, head] - shift) * LOG2E, out)
    return out


def _rms(x, g):
    return x * lax.rsqrt(jnp.mean(x * x, axis=-1, keepdims=True) + NORM_EPS) * g


def _dot(a, b):
    return jnp.dot(a, b, preferred_element_type=F32)


def _online_update(carry, s, vts):
    m, acc = carry
    m_new = jnp.maximum(m, jnp.max(s, axis=0, keepdims=True))
    p = jnp.exp2(s - m_new).astype(BF16)
    values = jnp.concatenate([jnp.concatenate(vts, axis=1), jnp.ones((ONES_ROWS, s.shape[0]), BF16)], axis=0)
    return m_new, jnp.exp2(m - m_new) * acc + _dot(values, p)


def _softmax_init(width, queries):
    return jnp.full((1, queries), NEG, F32), jnp.zeros((width + ONES_ROWS, queries), F32)


def _softmax_result(carry):
    _, acc = carry
    width = acc.shape[0] - ONES_ROWS
    return acc[0:width] / acc[width:width + 1]


def _transpose_bf16(x):
    return x.astype(F32).T.astype(BF16)


def _params(*sem):
    return pltpu.CompilerParams(dimension_semantics=sem, vmem_limit_bytes=VMEM_LIMIT)


def _resident(shape):
    zeros = (0,) * len(shape)
    return pl.BlockSpec(shape, lambda *_: zeros, pipeline_mode=pl.Buffered(1))


def _smem():
    return pl.BlockSpec(memory_space=pltpu.SMEM)


def _ffn_kernel(h_ref, g_ref, wi_ref, wo_ref, o_ref):
    x = h_ref[...]
    n = _rms(x, g_ref[...]).astype(BF16)
    acc = x
    for lo, width in FFN_CHUNKS:
        gate = _dot(n, wi_ref[:, lo:lo + width])
        up = _dot(n, wi_ref[:, D_FF + lo:D_FF + lo + width])
        act = (gate * jax.nn.sigmoid(gate) * up).astype(BF16)
        acc = acc + 0.5 * _dot(act, wo_ref[lo:lo + width, :])
    o_ref[...] = acc


def _ffn(h, g, wi, wo, tm):
    t = h.shape[0]
    tile = pl.BlockSpec((tm, D_MODEL), lambda i: (i, 0))
    return pl.pallas_call(
        _ffn_kernel,
        name="ffn",
        grid=(t // tm,),
        in_specs=[tile, _resident((1, D_MODEL)), _resident(wi.shape), _resident(wo.shape)],
        out_specs=tile,
        out_shape=jax.ShapeDtypeStruct(h.shape, F32),
        compiler_params=_params("arbitrary"),
    )(h, g.reshape(1, D_MODEL), wi, wo)


IN_A = 3 * A_WIDTH
IN_B = B_WIDTH + 4 * B_KV
IN_C = C_WIDTH + 2 * C_KV
IN_G = 3 * D_MODEL
IN_SECTIONS = (IN_A, IN_B, B_KV, B_KV, IN_C, IN_G, LANES)


def _inproj_kernel(h_ref, g_ref, w_ref, za_ref, zb_ref, zkc_ref, zvc_ref, zc_ref, gates_ref, bg_ref):
    n = _rms(h_ref[...], g_ref[...]).astype(BF16)
    lo = 0
    for ref, width in zip((za_ref, zb_ref, zkc_ref, zvc_ref, zc_ref, gates_ref, bg_ref), IN_SECTIONS):
        ref[...] = _dot(n, w_ref[:, lo:lo + width]).astype(ref.dtype)
        lo += width


def _inproj(h, g, w, tm):
    t = h.shape[0]
    dtypes = (BF16, BF16, BF16, BF16, BF16, F32, F32)
    return pl.pallas_call(
        _inproj_kernel,
        name="inproj",
        grid=(t // tm,),
        in_specs=[pl.BlockSpec((tm, D_MODEL), lambda i: (i, 0)), _resident((1, D_MODEL)), _resident(w.shape)],
        out_specs=[pl.BlockSpec((tm, width), lambda i: (i, 0)) for width in IN_SECTIONS],
        out_shape=[jax.ShapeDtypeStruct((t, width), dt) for width, dt in zip(IN_SECTIONS, dtypes)],
        compiler_params=_params("arbitrary"),
    )(h, g.reshape(1, D_MODEL), w)


def _post_kernel(h_ref, ya_ref, yb_ref, yc_ref, gates_ref, wb_ref, wo_ref, o_ref):
    merged = jnp.zeros(h_ref.shape, F32)
    for m, y_ref in enumerate((ya_ref, yb_ref, yc_ref)):
        gate = jax.nn.sigmoid(gates_ref[:, m * D_MODEL:(m + 1) * D_MODEL])
        merged = merged + gate * _dot(y_ref[...], wb_ref[m])
    o_ref[...] = h_ref[...] + _dot(merged.astype(BF16), wo_ref[...])


def _post(h, ya, yb, yc, gates, wb, wo, tm):
    t = h.shape[0]
    tile = pl.BlockSpec((tm, D_MODEL), lambda i: (i, 0))
    ytile = pl.BlockSpec((tm, A_WIDTH), lambda i: (i, 0))
    return pl.pallas_call(
        _post_kernel,
        name="post",
        grid=(t // tm,),
        in_specs=[tile, ytile, ytile, ytile, pl.BlockSpec((tm, IN_G), lambda i: (i, 0)),
                  _resident(wb.shape), _resident(wo.shape)],
        out_specs=tile,
        out_shape=jax.ShapeDtypeStruct(h.shape, F32),
        compiler_params=_params("arbitrary"),
    )(h, ya, yb, yc, gates, wb, wo)


def _ple_kernel(h_ref, p_ref, g_ref, wg_ref, we_ref, gf_ref, o_ref, *, final):
    x = h_ref[...]
    gate = jax.nn.sigmoid(_dot(_rms(x, g_ref[...]).astype(BF16), wg_ref[...]))
    x = x + gate * _dot(p_ref[...].astype(BF16), we_ref[...])
    if final:
        x = _rms(x, gf_ref[...])
    o_ref[...] = x


def _ple(h, p, g, wg, we, gf, tm, final):
    t = h.shape[0]
    tile = pl.BlockSpec((tm, D_MODEL), lambda i: (i, 0))
    return pl.pallas_call(
        functools.partial(_ple_kernel, final=final),
        name="ple",
        grid=(t // tm,),
        in_specs=[tile, pl.BlockSpec((tm, PLE_DIM), lambda i: (i, 0)), _resident((1, D_MODEL)),
                  _resident(wg.shape), _resident(we.shape), _resident((1, D_MODEL))],
        out_specs=tile,
        out_shape=jax.ShapeDtypeStruct(h.shape, F32),
        compiler_params=_params("arbitrary"),
    )(h, p, g.reshape(1, D_MODEL), wg, we, gf.reshape(1, D_MODEL))


def _diff_kernel(tbl_ref, lam_ref, subg_ref, q_ref, k_ref, v_ref, o_ref, kaug_ref, vt_ref, near_ref,
                 *, tq, lam_init):
    first_call = (pl.program_id(0) == 0) & (pl.program_id(1) == 0)
    qi = pl.program_id(2)
    width = 2 * A_HEAD_DIM
    n_heads = kaug_ref.shape[0]
    head0 = pl.program_id(1) * n_heads

    @pl.when(first_call & (qi == 0))
    def _static_setup():
        lane = lax.broadcasted_iota(jnp.int32, (tq, 2 * LANES), 1)
        for h in range(n_heads):
            kaug_ref[h, 0:tq, :] = jnp.where(lane == LANES, 1.0, 0.0).astype(BF16)
            kaug_ref[h, tq:, LANES:2 * LANES] = jnp.zeros((kaug_ref.shape[1] - tq, LANES), BF16)
            vt_ref[h, 0] = jnp.zeros(vt_ref.shape[2:], BF16)

    @pl.when(qi == 0)
    def _per_batch_setup():
        key = lax.broadcasted_iota(jnp.int32, (2 * tq, tq), 0)
        qry = lax.broadcasted_iota(jnp.int32, (2 * tq, tq), 1)
        dist = tq + qry - key
        for h in range(n_heads):
            cols = slice(h * width, (h + 1) * width)
            kaug_ref[h, tq:, 0:LANES] = k_ref[:, cols]
            for c in range(vt_ref.shape[1] - 1):
                vt_ref[h, c + 1] = _transpose_bf16(v_ref[c * tq:(c + 1) * tq, cols])
            near_ref[h] = jnp.where(dist >= 0, _bias_of_dist(dist, tbl_ref, head0 + h, True), NEG)

    zero = jnp.zeros((A_HEAD_DIM, tq), F32)
    flag_row = lax.broadcasted_iota(jnp.int32, (LANES, 2 * tq), 0) == 0
    flags = jnp.where(flag_row, UNSEL, 0.0)
    queries = []
    for h in range(n_heads):
        qt = q_ref[:, h * width:(h + 1) * width].astype(F32).T
        both = jnp.concatenate([jnp.concatenate([qt[0:A_HEAD_DIM], zero], axis=0),
                                jnp.concatenate([zero, qt[A_HEAD_DIM:width]], axis=0)], axis=1)
        queries.append(jnp.concatenate([both, flags], axis=0).astype(BF16))

    def logits_of(first_tile):
        start = pl.multiple_of(first_tile * tq, tq)
        return tuple(_dot(kaug_ref[h, pl.ds(start, 2 * tq), :], queries[h]) for h in range(n_heads))

    def update(first_tile, logits, carries):
        return tuple(_online_update(carries[h], logits[h], [vt_ref[h, first_tile], vt_ref[h, first_tile + 1]])
                     for h in range(n_heads))

    def far(gi, carries, count):
        tiles = [qi - 2 * (gi + c) for c in range(count)]
        logits = [logits_of(t) for t in tiles]
        for t, s in zip(tiles, logits):
            carries = update(t, s, carries)
        return carries

    nearest = tuple(s + jnp.concatenate([near_ref[h], near_ref[h]], axis=1) for h, s in enumerate(logits_of(qi)))
    carries = update(qi, nearest, tuple(_softmax_init(width, 2 * tq) for _ in range(n_heads)))
    n_far = qi // 2
    carries = lax.fori_loop(0, n_far // 2, lambda p, c: far(1 + 2 * p, c, 2), carries)
    carries = lax.fori_loop(0, n_far % 2, lambda _, c: far(n_far, c, 1), carries)
    lp = lam_ref[...]
    lam = (jnp.exp(jnp.sum(lp[0:1] * lp[1:2], axis=-1, keepdims=True))
           - jnp.exp(jnp.sum(lp[2:3] * lp[3:4], axis=-1, keepdims=True)) + lam_init)
    for h in range(n_heads):
        out = _softmax_result(carries[h])
        o = out[:, 0:tq] - lam * out[:, tq:2 * tq]
        o = o * lax.rsqrt(jnp.mean(o * o, axis=0, keepdims=True) + NORM_EPS) * (1.0 - lam_init)
        o_ref[:, h * width:(h + 1) * width] = (o.T * subg_ref[...]).astype(o_ref.dtype)


def _diff_attention(za, tbl, lam_p, subg, bsz, seq, lam_init, tq, heads_per_step):
    nq = seq // tq
    width = 2 * A_HEAD_DIM
    steps = A_HEADS // heads_per_step
    wide = heads_per_step * width
    return pl.pallas_call(
        functools.partial(_diff_kernel, tq=tq, lam_init=lam_init),
        name="diff_attn",
        grid=(bsz, steps, nq),
        in_specs=[_smem(), _resident(lam_p.shape), _resident((1, width)),
                  pl.BlockSpec((tq, wide), lambda b, h, i: (b * nq + i, h)),
                  pl.BlockSpec((seq, wide), lambda b, h, i: (b, steps + h)),
                  pl.BlockSpec((seq, wide), lambda b, h, i: (b, 2 * steps + h))],
        out_specs=pl.BlockSpec((tq, wide), lambda b, h, i: (b * nq + i, h)),
        out_shape=jax.ShapeDtypeStruct((bsz * seq, A_WIDTH), BF16),
        scratch_shapes=[pltpu.VMEM((heads_per_step, tq + seq, 2 * LANES), BF16),
                        pltpu.VMEM((heads_per_step, nq + 1, width, tq), BF16),
                        pltpu.VMEM((heads_per_step, 2 * tq, tq), F32)],
        compiler_params=_params("arbitrary", "arbitrary", "arbitrary"),
    )(tbl, lam_p, subg.reshape(1, width), za, za, za)


def _compress_kernel(xk_ref, xv_ref, pos_ref, w1_ref, w2_ref, kc_ref, vc_ref):
    rows = xk_ref.shape[0]
    for t, (x_ref, o_ref) in enumerate(((xk_ref, kc_ref), (xv_ref, vc_ref))):
        x = x_ref[...].astype(F32)
        first = _dot((x + pos_ref[t, 0]).astype(BF16), w1_ref[t, 0])
        second = _dot((x + pos_ref[t, 1]).astype(BF16), w1_ref[t, 1])
        hidden = first + pltpu.roll(second, rows - 1, 0)
        o_ref[...] = _dot(jax.nn.gelu(hidden).astype(BF16), w2_ref[t]).astype(o_ref.dtype)


def _compress(xk, xv, pos, w1, w2):
    bsz, rows, width = xk.shape
    xspec = pl.BlockSpec((None, rows, width), lambda b: (b, 0, 0))
    ospec = pl.BlockSpec((None, rows, B_KV), lambda b: (b, 0, 0))
    return pl.pallas_call(
        _compress_kernel,
        name="nsa_compress",
        grid=(bsz,),
        in_specs=[xspec, xspec, _resident(pos.shape), _resident(w1.shape), _resident(w2.shape)],
        out_specs=[ospec, ospec],
        out_shape=[jax.ShapeDtypeStruct((bsz, rows, B_KV), BF16)] * 2,
        compiler_params=_params("arbitrary"),
    )(xk, xv, pos, w1, w2)


def _group_queries_t(q, group, rep):
    tq = q.shape[0]
    zero = jnp.zeros((HEAD_DIM, tq), F32)
    cols = []
    for r in range(rep):
        head = group * rep + r
        slab = q[:, (head // 2) * LANES:(head // 2 + 1) * LANES].astype(F32).T
        part = slab[(head % 2) * HEAD_DIM:(head % 2 + 1) * HEAD_DIM]
        cols.append(jnp.concatenate([part, zero] if group == 0 else [zero, part], axis=0))
    return jnp.concatenate(cols, axis=1).astype(BF16)


def _build_near_bias(near_ref, tbl_ref, head0, groups, rep, tq, window_is_tile, minus_far):
    key = lax.broadcasted_iota(jnp.int32, (2 * tq, tq), 0)
    qry = lax.broadcasted_iota(jnp.int32, (2 * tq, tq), 1)
    dist = tq + qry - key
    visible = (dist >= 0) & (dist < tq) if window_is_tile else dist >= 0
    for g in range(groups):
        for r in range(rep):
            head = head0 + g * rep + r
            near_ref[g, :, r * tq:(r + 1) * tq] = jnp.where(visible, _bias_of_dist(dist, tbl_ref, head, minus_far), NEG)


SEL_GROUP = 4
SEL_PAD = SEL_GROUP - 1
WIN_PAD = 4


def _nsa_kernel(tbl_ref, q_ref, ks_ref, vs_ref, kw_ref, vw_ref, kc_ref, vc_ref, bg_ref, ovt_ref, o_ref,
                kaug_ref, kwp_ref, vst_ref, vwt_ref, vct_ref, near_ref, edge_ref, band_ref, sc_ref, imp_ref,
                *, tq, seq):
    first_batch = pl.program_id(0) == 0
    qi = pl.program_id(1)
    n_cmp_pad = seq // B_CMP_STRIDE
    n_sel = seq // B_SEL_BLOCK
    top_k = min(B_SEL_TOPK, n_sel)
    nq = seq // tq
    cols = B_REP * tq
    cmp_per_tile = tq // B_CMP_STRIDE
    band = CMP_BAND_BACK + cmp_per_tile

    @pl.when(first_batch & (qi == 0))
    def _static_setup():
        _build_near_bias(near_ref, tbl_ref, A_HEADS, B_GROUPS, B_REP, tq, False, True)
        key = lax.broadcasted_iota(jnp.int32, (tq, cols), 0)
        qry = lax.broadcasted_iota(jnp.int32, (tq, cols), 1) % tq
        edge_ref[...] = jnp.where(qry < key, 0.0, NEG)
        blk = lax.broadcasted_iota(jnp.int32, (band, tq), 0) - CMP_BAND_BACK
        dist = lax.broadcasted_iota(jnp.int32, (band, tq), 1) - (blk * B_CMP_STRIDE + B_CMP_LEN - 1)
        for g in range(B_GROUPS):
            for r in range(B_REP):
                head = A_HEADS + g * B_REP + r
                band_ref[g, :, r * tq:(r + 1) * tq] = jnp.where(dist >= 0, _bias_of_dist(dist, tbl_ref, head, True), NEG)
        sc_ref[:, 0:CMP_BAND_BACK, :] = jnp.zeros((B_GROUPS, CMP_BAND_BACK, cols), F32)
        pad = SEL_PAD * tq
        row = lax.broadcasted_iota(jnp.int32, (pad + seq, LANES), 0)
        lane = lax.broadcasted_iota(jnp.int32, (pad + seq, LANES), 1)
        blk_id = jnp.where(row < pad, LANES - 1, (row - pad) // B_SEL_BLOCK)
        kaug_ref[:, LANES:2 * LANES] = jnp.where(blk_id == lane, 1.0, 0.0).astype(BF16)
        kaug_ref[0:pad, 0:LANES] = jnp.zeros((pad, LANES), BF16)
        kwp_ref[0:WIN_PAD * tq, :] = jnp.zeros((WIN_PAD * tq, LANES), BF16)
        for c in range(SEL_PAD):
            vst_ref[c] = jnp.zeros(vst_ref.shape[1:], BF16)
        for c in range(WIN_PAD):
            vwt_ref[c] = jnp.zeros(vwt_ref.shape[1:], BF16)

    @pl.when(qi == 0)
    def _per_batch_setup():
        kaug_ref[SEL_PAD * tq:, 0:LANES] = ks_ref[...]
        kwp_ref[WIN_PAD * tq:, :] = kw_ref[...]
        for c in range(nq):
            vst_ref[SEL_PAD + c] = _transpose_bf16(vs_ref[c * tq:(c + 1) * tq, :])
            vwt_ref[WIN_PAD + c] = _transpose_bf16(vw_ref[c * tq:(c + 1) * tq, :])
        for c in range(n_cmp_pad // LANES):
            vct_ref[:, c * LANES:(c + 1) * LANES] = _transpose_bf16(vc_ref[c * LANES:(c + 1) * LANES, :])

    q = q_ref[...]
    gates_t = jax.nn.sigmoid(bg_ref[...]).T
    first_cmp = qi * cmp_per_tile
    cmp_row = lax.broadcasted_iota(jnp.int32, (n_cmp_pad, cols), 0)
    cmp_visible = cmp_row < first_cmp + cmp_per_tile
    blk = lax.broadcasted_iota(jnp.int32, (n_sel, tq), 0)
    cur = (qi * tq + lax.broadcasted_iota(jnp.int32, (n_sel, tq), 1)) // B_SEL_BLOCK
    forced = (blk == 0) | (blk == cur) | (blk == cur - 1)
    future = blk > cur
    halves = [slice(g * HEAD_DIM, (g + 1) * HEAD_DIM) for g in range(B_GROUPS)]

    qgs, o_cs, imps = [], [], []
    for g in range(B_GROUPS):
        qg = _group_queries_t(q, g, B_REP)
        sc_ref[g, CMP_BAND_BACK:, :] = _dot(kc_ref[...], qg)
        band_rows = pl.ds(pl.multiple_of(first_cmp, 8), band)
        sc_ref[g, band_rows, :] = sc_ref[g, band_rows, :] + band_ref[g]
        s = jnp.where(cmp_visible, sc_ref[g, CMP_BAND_BACK:, :], NEG)
        e = jnp.where(s > 0.5 * NEG, jnp.exp2(s - jnp.max(s, axis=0, keepdims=True)), 0.0)
        l = jnp.sum(e, axis=0, keepdims=True)
        p_c = e * jnp.where(l > 0.0, 1.0 / jnp.where(l > 0.0, l, 1.0), 0.0)
        o_cs.append(_dot(vct_ref[halves[g], :], p_c.astype(BF16)))
        p_sum = p_c[:, 0:tq]
        for r in range(1, B_REP):
            p_sum = p_sum + p_c[:, r * tq:(r + 1) * tq]
        imp = _dot(ovt_ref[...], p_sum.astype(BF16))
        imp = jnp.where(forced, B_SEL_FORCE, jnp.where(future, -B_SEL_FORCE, imp))
        imp_ref[g] = imp
        qgs.append(qg)
        imps.append(imp)

    def rank_body(other, ranks):
        out = []
        for g in range(B_GROUPS):
            val = imp_ref[g, pl.ds(other, 1), :]
            ahead = (val > imps[g]) | ((val == imps[g]) & (blk > other))
            out.append(ranks[g] + jnp.where(ahead, 1.0, 0.0))
        return tuple(out)

    last_blk = ((qi + 1) * tq - 1) // B_SEL_BLOCK
    ranks = lax.fori_loop(0, last_blk + 1, rank_body, tuple(jnp.zeros((n_sel, tq), F32) for _ in range(B_GROUPS)))
    q_augs = []
    for g in range(B_GROUPS):
        sel_bias = jnp.concatenate([jnp.where(ranks[g] < float(top_k), 0.0, UNSEL),
                                    jnp.full((LANES - n_sel, tq), UNSEL, F32)], axis=0).astype(BF16)
        q_augs.append(jnp.concatenate([qgs[g], jnp.concatenate([sel_bias] * B_REP, axis=1)], axis=0))

    pair = cols
    chains = [(g, slice(0, cols)) for g in range(B_GROUPS)]
    heads_per_chain = pair // tq

    def sel_logits(first_tile):
        start = pl.multiple_of(first_tile * tq, tq)
        keys = kaug_ref[pl.ds(start, SEL_GROUP * tq), :]
        return tuple(_dot(keys, q_augs[g][:, cc]) for g, cc in chains)

    def sel_update(first_tile, logits, carries):
        return tuple(_online_update(carry, s, [vst_ref[first_tile + t, halves[g], :] for t in range(SEL_GROUP)])
                     for (g, cc), s, carry in zip(chains, logits, carries))

    def sel_far(gi, carries, count):
        tiles = [qi - SEL_GROUP * (gi + c) for c in range(count)]
        logits = [sel_logits(t) for t in tiles]
        for t, s in zip(tiles, logits):
            carries = sel_update(t, s, carries)
        return carries

    far_rows = (SEL_GROUP - 2) * tq
    nearest = tuple(jnp.concatenate([s[0:far_rows], s[far_rows:] + near_ref[g, :, cc]], axis=0)
                    for (g, cc), s in zip(chains, sel_logits(qi)))
    sel = sel_update(qi, nearest, tuple(_softmax_init(HEAD_DIM, pair) for _ in chains))
    n_far = qi // SEL_GROUP
    sel = lax.fori_loop(0, n_far // 2, lambda p, c: sel_far(1 + 2 * p, c, 2), sel)
    sel = lax.fori_loop(0, n_far % 2, lambda _, c: sel_far(n_far, c, 1), sel)

    start = pl.multiple_of(qi * tq, tq)
    win_keys = kwp_ref[pl.ds(start, (WIN_PAD + 1) * tq), :]
    win = []
    for g, cc in chains:
        s_w = _dot(win_keys, qgs[g][:, cc])
        slabs = []
        for t in range(WIN_PAD + 1):
            slab = s_w[t * tq:(t + 1) * tq]
            if t == 0:
                slab = slab + edge_ref[:, cc]
            if t >= WIN_PAD - 1:
                slab = slab + near_ref[g, (t - WIN_PAD + 1) * tq:(t - WIN_PAD + 2) * tq, cc]
            if t < WIN_PAD:
                slab = slab + jnp.where(qi >= WIN_PAD - t, 0.0, NEG)
            slabs.append(slab)
        win.append(_online_update(_softmax_init(HEAD_DIM, pair), jnp.concatenate(slabs, axis=0),
                                  [vwt_ref[qi + t, halves[g], :] for t in range(WIN_PAD + 1)]))

    heads_out = []
    for ci, (g, cc) in enumerate(chains):
        o_s = _softmax_result(sel[ci])
        o_w = _softmax_result(win[ci])
        o_c = o_cs[g][:, cc]
        for rr in range(heads_per_chain):
            hc = slice(rr * tq, (rr + 1) * tq)
            c0 = (g * B_REP + rr) * 3
            heads_out.append(gates_t[c0:c0 + 1] * o_c[:, hc] + gates_t[c0 + 1:c0 + 2] * o_s[:, hc]
                             + gates_t[c0 + 2:c0 + 3] * o_w[:, hc])
    o_ref[...] = jnp.concatenate(heads_out, axis=0).T.astype(o_ref.dtype)


def _nsa_attention(zb, kc, vc, bgate, tbl, ovt, bsz, seq, tq):
    nq = seq // tq
    n_cmp_pad = seq // B_CMP_STRIDE
    cols = B_REP * tq
    kv = lambda col: pl.BlockSpec((seq, LANES), lambda b, i: (b, col))
    cmp_spec = pl.BlockSpec((None, n_cmp_pad, B_KV), lambda b, i: (b, 0, 0))
    qcols = B_WIDTH // LANES
    return pl.pallas_call(
        functools.partial(_nsa_kernel, tq=tq, seq=seq),
        name="nsa_attn",
        grid=(bsz, nq),
        in_specs=[_smem(),
                  pl.BlockSpec((tq, B_WIDTH), lambda b, i: (b * nq + i, 0)),
                  kv(qcols), kv(qcols + 1), kv(qcols + 2), kv(qcols + 3),
                  cmp_spec, cmp_spec,
                  pl.BlockSpec((tq, LANES), lambda b, i: (b * nq + i, 0)),
                  _resident(ovt.shape)],
        out_specs=pl.BlockSpec((tq, B_WIDTH), lambda b, i: (b * nq + i, 0)),
        out_shape=jax.ShapeDtypeStruct((bsz * seq, B_WIDTH), BF16),
        scratch_shapes=[pltpu.VMEM((SEL_PAD * tq + seq, 2 * LANES), BF16),
                        pltpu.VMEM((WIN_PAD * tq + seq, LANES), BF16),
                        pltpu.VMEM((SEL_PAD + nq, B_KV, tq), BF16),
                        pltpu.VMEM((WIN_PAD + nq, B_KV, tq), BF16),
                        pltpu.VMEM((B_KV, n_cmp_pad), BF16),
                        pltpu.VMEM((B_GROUPS, 2 * tq, cols), F32),
                        pltpu.VMEM((tq, cols), F32),
                        pltpu.VMEM((B_GROUPS, CMP_BAND_BACK + tq // B_CMP_STRIDE, cols), F32),
                        pltpu.VMEM((B_GROUPS, CMP_BAND_BACK + n_cmp_pad, cols), F32),
                        pltpu.VMEM((B_GROUPS, seq // B_SEL_BLOCK, tq), F32)],
        compiler_params=_params("arbitrary", "arbitrary"),
    )(tbl, zb, zb, zb, zb, zb, kc, vc, bgate, ovt)


def _swa_kernel(tbl_ref, sink_ref, q_ref, kp_ref, kd_ref, vp_ref, vd_ref, o_ref, bias_ref, *, tq):
    qi = pl.program_id(1)

    @pl.when((pl.program_id(0) == 0) & (qi == 0))
    def _build_tables():
        _build_near_bias(bias_ref, tbl_ref, A_HEADS + B_HEADS, C_GROUPS, C_REP, tq, True, False)

    q = q_ref[...]
    cols = C_REP * tq
    prev_bias = jnp.where(qi > 0, 0.0, NEG)
    vp_t = _transpose_bf16(vp_ref[...])
    vd_t = _transpose_bf16(vd_ref[...])
    heads_out = []
    for g in range(C_GROUPS):
        half = slice(g * HEAD_DIM, (g + 1) * HEAD_DIM)
        qg = _group_queries_t(q, g, C_REP)
        s = jnp.concatenate([_dot(kp_ref[...], qg) + bias_ref[g, 0:tq, :] + prev_bias,
                             _dot(kd_ref[...], qg) + bias_ref[g, tq:2 * tq, :]], axis=0)
        m, acc = _online_update(_softmax_init(HEAD_DIM, cols), s, [vp_t[half], vd_t[half]])
        for r in range(C_REP):
            cc = slice(r * tq, (r + 1) * tq)
            sink = sink_ref[g * C_REP + r] * LOG2E
            m_all = jnp.maximum(m[:, cc], sink)
            scale = jnp.exp2(m[:, cc] - m_all)
            denom = acc[HEAD_DIM:HEAD_DIM + 1, cc] * scale + jnp.exp2(sink - m_all)
            heads_out.append(acc[0:HEAD_DIM, cc] * (scale / denom))
    o_ref[...] = jnp.concatenate(heads_out, axis=0).T.astype(o_ref.dtype)


def _swa_attention(zc, tbl, sinks, bsz, seq, tq):
    nq = seq // tq
    qcols = C_WIDTH // LANES
    prev = lambda col: pl.BlockSpec((tq, LANES), lambda b, i: (b * nq + jnp.maximum(i - 1, 0), col))
    diag = lambda col: pl.BlockSpec((tq, LANES), lambda b, i: (b * nq + i, col))
    return pl.pallas_call(
        functools.partial(_swa_kernel, tq=tq),
        name="swa_attn",
        grid=(bsz, nq),
        in_specs=[_smem(), _smem(),
                  pl.BlockSpec((tq, C_WIDTH), lambda b, i: (b * nq + i, 0)),
                  prev(qcols), diag(qcols), prev(qcols + 1), diag(qcols + 1)],
        out_specs=pl.BlockSpec((tq, C_WIDTH), lambda b, i: (b * nq + i, 0)),
        out_shape=jax.ShapeDtypeStruct((bsz * seq, C_WIDTH), BF16),
        scratch_shapes=[pltpu.VMEM((C_GROUPS, 2 * tq, C_REP * tq), F32)],
        compiler_params=_params("arbitrary", "arbitrary"),
    )(tbl, sinks, zc, zc, zc, zc, zc)


def _prep_w_in(w_in):
    sizes = (A_WIDTH, A_WIDTH, A_WIDTH, B_WIDTH, B_KV, B_KV, B_KV, B_KV, B_KV, B_KV, 3 * B_HEADS,
             C_WIDTH, C_KV, C_KV, D_MODEL, D_MODEL, D_MODEL)
    offs = np.concatenate([[0], np.cumsum(sizes)])
    (aq, ak, av, bq, bkc, bvc, bks, bvs, bkw, bvw, bgate, cq, ck, cv, ga, gb, gc) = [
        w_in[..., offs[i]:offs[i + 1]] for i in range(len(sizes))]
    scale = HEAD_DIM ** -0.5 * LOG2E
    bgate = jnp.pad(bgate, ((0, 0), (0, 0), (0, LANES - 3 * B_HEADS)))
    cols = [aq * scale, ak, av, bq * scale, bks, bvs, bkw, bvw, bkc, bvc, cq * scale, ck, cv, ga, gb, gc, bgate]
    return jnp.concatenate(cols, axis=-1).astype(BF16)


def _prep_compress(cmp_pos, cmp_w1, cmp_w2):
    depth = cmp_pos.shape[0]
    half = B_CMP_STRIDE
    pos = cmp_pos.reshape(depth, 2, 2, half, 1, B_HEAD_DIM)
    pos = jnp.broadcast_to(pos, (depth, 2, 2, half, B_GROUPS, B_HEAD_DIM)).reshape(depth, 2, 2, 1, half * B_KV)
    w1 = cmp_w1.reshape(depth, 2, 2, half, 1, B_HEAD_DIM, 1, B_CMP_HIDDEN)
    eye = jnp.eye(B_GROUPS, dtype=cmp_w1.dtype).reshape(1, 1, 1, 1, B_GROUPS, 1, B_GROUPS, 1)
    w1 = (w1 * eye).reshape(depth, 2, 2, half * B_KV, B_GROUPS * B_CMP_HIDDEN)
    w2 = cmp_w2.reshape(depth, 2, 1, B_CMP_HIDDEN, 1, B_HEAD_DIM)
    eye2 = jnp.eye(B_GROUPS, dtype=cmp_w2.dtype).reshape(1, 1, B_GROUPS, 1, B_GROUPS, 1)
    w2 = (w2 * eye2).reshape(depth, 2, B_GROUPS * B_CMP_HIDDEN, B_KV)
    return pos.astype(F32), w1.astype(BF16), w2.astype(BF16)


def _overlap_t(seq):
    n_cmp_pad = seq // B_CMP_STRIDE
    start = np.arange(n_cmp_pad) * B_CMP_STRIDE
    sel = np.arange(seq // B_SEL_BLOCK) * B_SEL_BLOCK
    ov = (start[None, :] < sel[:, None] + B_SEL_BLOCK) & (start[None, :] + B_CMP_LEN > sel[:, None])
    ov &= (np.arange(n_cmp_pad) < n_cmp_pad - 1)[None, :]
    return jnp.asarray(ov.astype(np.float32), dtype=BF16)


def kernel(x, p, norm_g, ffn1_wi, ffn1_wo, w_in, diff_lambda, diff_subln, nsa_cmp_pos, nsa_cmp_w1,
           nsa_cmp_w2, swa_sinks, w_branch, w_out, ffn2_wi, ffn2_wo, w_ple, w_ple_gate, rel_bias, final_norm):
    bsz, seq, _ = x.shape
    depth = norm_g.shape[0]
    tokens = bsz * seq
    assert seq % 1024 == 0 and seq // B_SEL_BLOCK <= LANES
    tm, tm_post, tq_a, tq = 512, 256, 256, 128
    assert FAR_DIST + B_CMP_LEN - 1 <= CMP_BAND_BACK * B_CMP_STRIDE and FAR_DIST <= tq
    assert WIN_PAD * tq == B_WINDOW and tq == C_WINDOW and seq // B_SEL_BLOCK < LANES

    wi1, wo1 = ffn1_wi.astype(BF16), ffn1_wo.astype(BF16)
    wi2, wo2 = ffn2_wi.astype(BF16), ffn2_wo.astype(BF16)
    w_in_r = _prep_w_in(w_in)
    cmp_pos, cmp_w1, cmp_w2 = _prep_compress(nsa_cmp_pos, nsa_cmp_w1, nsa_cmp_w2)
    wb, wo = w_branch.astype(BF16), w_out.astype(BF16)
    wpg, wpe = w_ple_gate.astype(BF16), w_ple.astype(BF16)
    tbl = rel_bias.astype(F32)
    ovt = _overlap_t(seq)
    chunk_rows = seq // B_CMP_STRIDE

    h = x.reshape(tokens, D_MODEL)
    for i in range(depth):
        h = _ffn(h, norm_g[i, 0], wi1[i], wo1[i], tm)
        za, zb, zkc, zvc, zc, gates, bgate = _inproj(h, norm_g[i, 1], w_in_r[i], tm)
        lam_init = 0.8 - 0.6 * math.exp(-0.3 * i)
        ya = _diff_attention(za, tbl, diff_lambda[i].astype(F32), diff_subln[i], bsz, seq, lam_init, tq_a, 4)
        kc, vc = _compress(zkc.reshape(bsz, chunk_rows, B_CMP_STRIDE * B_KV),
                           zvc.reshape(bsz, chunk_rows, B_CMP_STRIDE * B_KV), cmp_pos[i], cmp_w1[i], cmp_w2[i])
        yb = _nsa_attention(zb, kc, vc, bgate, tbl, ovt, bsz, seq, tq)
        yc = _swa_attention(zc, tbl, swa_sinks[i].astype(F32), bsz, seq, tq)
        h = _post(h, ya, yb, yc, gates, wb[i], wo[i], tm_post)
        h = _ffn(h, norm_g[i, 2], wi2[i], wo2[i], tm)
        h = _ple(h, p[i].reshape(tokens, PLE_DIM), norm_g[i, 3], wpg[i], wpe[i], final_norm, tm,
                 final=(i == depth - 1))
    return h.reshape(bsz, seq, D_MODEL)
```

```python
import functools
import math

import numpy as np
import jax
import jax.numpy as jnp
from jax import lax
from jax.experimental import pallas as pl
from jax.experimental.pallas import tpu as pltpu

F32 = jnp.float32
BF16 = jnp.bfloat16

D_MODEL = 1024
PLE_DIM = 256
D_FF = 2816
NORM_EPS = 1e-6
NEG = -1e30
NUM_BUCKETS = 32
MAX_DISTANCE = 128
A_HEADS = 4
A_HEAD_DIM = 64
B_HEADS = 8
B_GROUPS = 2
B_REP = B_HEADS // B_GROUPS
B_HEAD_DIM = 64
B_CMP_LEN = 32
B_CMP_STRIDE = 16
B_CMP_HIDDEN = 256
B_SEL_BLOCK = 64
B_SEL_TOPK = 16
B_WINDOW = 512
B_SEL_FORCE = 1e6
C_HEADS = 8
C_GROUPS = 2
C_REP = C_HEADS // C_GROUPS
C_HEAD_DIM = 64
C_WINDOW = 128
A_WIDTH = A_HEADS * 2 * A_HEAD_DIM
B_WIDTH = B_HEADS * B_HEAD_DIM
B_KV = B_GROUPS * B_HEAD_DIM
C_WIDTH = C_HEADS * C_HEAD_DIM
C_KV = C_GROUPS * C_HEAD_DIM
N_BIAS_HEADS = A_HEADS + B_HEADS + C_HEADS

LANES = 128
HEAD_DIM = 64
VMEM_LIMIT = 56 * 1024 * 1024
UNSEL = -float(2 ** 30)
FFN_CHUNKS = ((0, 1536), (1536, 1280))
CMP_BAND_BACK = 16
LOG2E = math.log2(math.e)
ONES_ROWS = 16


def _bucket_thresholds():
    n = np.arange(4 * MAX_DISTANCE)
    max_exact = NUM_BUCKETS // 2
    nf = np.maximum(n, 1).astype(np.float32)
    large = max_exact + (np.log(nf / max_exact) / math.log(MAX_DISTANCE / max_exact)
                         * (NUM_BUCKETS - max_exact)).astype(np.int32)
    bucket = np.where(n < max_exact, n, np.minimum(large, NUM_BUCKETS - 1))
    out = []
    for b in range(1, NUM_BUCKETS):
        hit = np.nonzero(bucket == b)[0]
        if hit.size:
            out.append((b, int(hit[0])))
    return tuple(out)


BUCKET_LO = _bucket_thresholds()
FAR_DIST = BUCKET_LO[-1][1]


def _bias_of_dist(dist, tbl_ref, head, minus_far=False):
    shift = tbl_ref[NUM_BUCKETS - 1, head] if minus_far else 0.0
    out = jnp.full(dist.shape, (tbl_ref[0, head] - shift) * LOG2E, F32)
    for b, lo in BUCKET_LO:
        out = jnp.where(dist >= lo, (tbl_ref[b, head] - shift) * LOG2E, out)
    return out


def _rms(x, g):
    return x * lax.rsqrt(jnp.mean(x * x, axis=-1, keepdims=True) + NORM_EPS) * g


def _dot(a, b):
    return jnp.dot(a, b, preferred_element_type=F32)


def _online_update(carry, s, vts):
    m, acc = carry
    m_new = jnp.maximum(m, jnp.max(s, axis=0, keepdims=True))
    p = jnp.exp2(s - m_new).astype(BF16)
    values = jnp.concatenate([jnp.concatenate(vts, axis=1), jnp.ones((ONES_ROWS, s.shape[0]), BF16)], axis=0)
    return m_new, jnp.exp2(m - m_new) * acc + _dot(values, p)


def _softmax_init(width, queries):
    return jnp.full((1, queries), NEG, F32), jnp.zeros((width + ONES_ROWS, queries), F32)


def _softmax_result(carry):
    _, acc = carry
    width = acc.shape[0] - ONES_ROWS
    return acc[0:width] / acc[width:width + 1]


def _transpose_bf16(x):
    return x.astype(F32).T.astype(BF16)


def _params(*sem):
    return pltpu.CompilerParams(dimension_semantics=sem, vmem_limit_bytes=VMEM_LIMIT)


def _resident(shape):
    zeros = (0,) * len(shape)
    return pl.BlockSpec(shape, lambda *_: zeros, pipeline_mode=pl.Buffered(1))


def _smem():
    return pl.BlockSpec(memory_space=pltpu.SMEM)


def _ffn_kernel(h_ref, g_ref, wi_ref, wo_ref, o_ref):
    x = h_ref[...]
    n = _rms(x, g_ref[...]).astype(BF16)
    acc = x
    for lo, width in FFN_CHUNKS:
        gate = _dot(n, wi_ref[:, lo:lo + width])
        up = _dot(n, wi_ref[:, D_FF + lo:D_FF + lo + width])
        act = (gate * jax.nn.sigmoid(gate) * up).astype(BF16)
        acc = acc + 0.5 * _dot(act, wo_ref[lo:lo + width, :])
    o_ref[...] = acc


def _ffn(h, g, wi, wo, tm):
    t = h.shape[0]
    tile = pl.BlockSpec((tm, D_MODEL), lambda i: (i, 0))
    return pl.pallas_call(
        _ffn_kernel,
        name="ffn",
        grid=(t // tm,),
        in_specs=[tile, _resident((1, D_MODEL)), _resident(wi.shape), _resident(wo.shape)],
        out_specs=tile,
        out_shape=jax.ShapeDtypeStruct(h.shape, F32),
        compiler_params=_params("arbitrary"),
    )(h, g.reshape(1, D_MODEL), wi, wo)


IN_A = 3 * A_WIDTH
IN_B = B_WIDTH + 4 * B_KV
IN_C = C_WIDTH + 2 * C_KV
IN_G = 3 * D_MODEL
IN_SECTIONS = (IN_A, IN_B, B_KV, B_KV, IN_C, IN_G, LANES)


def _inproj_kernel(h_ref, g_ref, w_ref, za_ref, zb_ref, zkc_ref, zvc_ref, zc_ref, gates_ref, bg_ref):
    n = _rms(h_ref[...], g_ref[...]).astype(BF16)
    lo = 0
    for ref, width in zip((za_ref, zb_ref, zkc_ref, zvc_ref, zc_ref, gates_ref, bg_ref), IN_SECTIONS):
        ref[...] = _dot(n, w_ref[:, lo:lo + width]).astype(ref.dtype)
        lo += width


def _inproj(h, g, w, tm):
    t = h.shape[0]
    dtypes = (BF16, BF16, BF16, BF16, BF16, F32, F32)
    return pl.pallas_call(
        _inproj_kernel,
        name="inproj",
        grid=(t // tm,),
        in_specs=[pl.BlockSpec((tm, D_MODEL), lambda i: (i, 0)), _resident((1, D_MODEL)), _resident(w.shape)],
        out_specs=[pl.BlockSpec((tm, width), lambda i: (i, 0)) for width in IN_SECTIONS],
        out_shape=[jax.ShapeDtypeStruct((t, width), dt) for width, dt in zip(IN_SECTIONS, dtypes)],
        compiler_params=_params("arbitrary"),
    )(h, g.reshape(1, D_MODEL), w)


def _post_kernel(h_ref, ya_ref, yb_ref, yc_ref, gates_ref, wb_ref, wo_ref, o_ref):
    merged = jnp.zeros(h_ref.shape, F32)
    for m, y_ref in enumerate((ya_ref, yb_ref, yc_ref)):
        gate = jax.nn.sigmoid(gates_ref[:, m * D_MODEL:(m + 1) * D_MODEL])
        merged = merged + gate * _dot(y_ref[...], wb_ref[m])
    o_ref[...] = h_ref[...] + _dot(merged.astype(BF16), wo_ref[...])


def _post(h, ya, yb, yc, gates, wb, wo, tm):
    t = h.shape[0]
    tile = pl.BlockSpec((tm, D_MODEL), lambda i: (i, 0))
    ytile = pl.BlockSpec((tm, A_WIDTH), lambda i: (i, 0))
    return pl.pallas_call(
        _post_kernel,
        name="post",
        grid=(t // tm,),
        in_specs=[tile, ytile, ytile, ytile, pl.BlockSpec((tm, IN_G), lambda i: (i, 0)),
                  _resident(wb.shape), _resident(wo.shape)],
        out_specs=tile,
        out_shape=jax.ShapeDtypeStruct(h.shape, F32),
        compiler_params=_params("arbitrary"),
    )(h, ya, yb, yc, gates, wb, wo)


def _ple_kernel(h_ref, p_ref, g_ref, wg_ref, we_ref, gf_ref, o_ref, *, final):
    x = h_ref[...]
    gate = jax.nn.sigmoid(_dot(_rms(x, g_ref[...]).astype(BF16), wg_ref[...]))
    x = x + gate * _dot(p_ref[...].astype(BF16), we_ref[...])
    if final:
        x = _rms(x, gf_ref[...])
    o_ref[...] = x


def _ple(h, p, g, wg, we, gf, tm, final):
    t = h.shape[0]
    tile = pl.BlockSpec((tm, D_MODEL), lambda i: (i, 0))
    return pl.pallas_call(
        functools.partial(_ple_kernel, final=final),
        name="ple",
        grid=(t // tm,),
        in_specs=[tile, pl.BlockSpec((tm, PLE_DIM), lambda i: (i, 0)), _resident((1, D_MODEL)),
                  _resident(wg.shape), _resident(we.shape), _resident((1, D_MODEL))],
        out_specs=tile,
        out_shape=jax.ShapeDtypeStruct(h.shape, F32),
        compiler_params=_params("arbitrary"),
    )(h, p, g.reshape(1, D_MODEL), wg, we, gf.reshape(1, D_MODEL))


def _diff_kernel(tbl_ref, lam_ref, subg_ref, q_ref, k_ref, v_ref, o_ref, kaug_ref, vt_ref, near_ref,
                 *, tq, lam_init):
    first_call = (pl.program_id(0) == 0) & (pl.program_id(1) == 0)
    qi = pl.program_id(2)
    width = 2 * A_HEAD_DIM
    n_heads = kaug_ref.shape[0]
    head0 = pl.program_id(1) * n_heads

    @pl.when(first_call & (qi == 0))
    def _static_setup():
        lane = lax.broadcasted_iota(jnp.int32, (tq, 2 * LANES), 1)
        for h in range(n_heads):
            kaug_ref[h, 0:tq, :] = jnp.where(lane == LANES, 1.0, 0.0).astype(BF16)
            kaug_ref[h, tq:, LANES:2 * LANES] = jnp.zeros((kaug_ref.shape[1] - tq, LANES), BF16)
            vt_ref[h, 0] = jnp.zeros(vt_ref.shape[2:], BF16)

    @pl.when(qi == 0)
    def _per_batch_setup():
        key = lax.broadcasted_iota(jnp.int32, (2 * tq, tq), 0)
        qry = lax.broadcasted_iota(jnp.int32, (2 * tq, tq), 1)
        dist = tq + qry - key
        for h in range(n_heads):
            cols = slice(h * width, (h + 1) * width)
            kaug_ref[h, tq:, 0:LANES] = k_ref[:, cols]
            for c in range(vt_ref.shape[1] - 1):
                vt_ref[h, c + 1] = _transpose_bf16(v_ref[c * tq:(c + 1) * tq, cols])
            near_ref[h] = jnp.where(dist >= 0, _bias_of_dist(dist, tbl_ref, head0 + h, True), NEG)

    zero = jnp.zeros((A_HEAD_DIM, tq), F32)
    flag_row = lax.broadcasted_iota(jnp.int32, (LANES, 2 * tq), 0) == 0
    flags = jnp.where(flag_row, UNSEL, 0.0)
    queries = []
    for h in range(n_heads):
        qt = q_ref[:, h * width:(h + 1) * width].astype(F32).T
        both = jnp.concatenate([jnp.concatenate([qt[0:A_HEAD_DIM], zero], axis=0),
                                jnp.concatenate([zero, qt[A_HEAD_DIM:width]], axis=0)], axis=1)
        queries.append(jnp.concatenate([both, flags], axis=0).astype(BF16))

    def logits_of(first_tile):
        start = pl.multiple_of(first_tile * tq, tq)
        return tuple(_dot(kaug_ref[h, pl.ds(start, 2 * tq), :], queries[h]) for h in range(n_heads))

    def update(first_tile, logits, carries):
        return tuple(_online_update(carries[h], logits[h], [vt_ref[h, first_tile], vt_ref[h, first_tile + 1]])
                     for h in range(n_heads))

    def far(gi, carries, count):
        tiles = [qi - 2 * (gi + c) for c in range(count)]
        logits = [logits_of(t) for t in tiles]
        for t, s in zip(tiles, logits):
            carries = update(t, s, carries)
        return carries

    nearest = tuple(s + jnp.concatenate([near_ref[h], near_ref[h]], axis=1) for h, s in enumerate(logits_of(qi)))
    carries = update(qi, nearest, tuple(_softmax_init(width, 2 * tq) for _ in range(n_heads)))
    n_far = qi // 2
    carries = lax.fori_loop(0, n_far // 2, lambda p, c: far(1 + 2 * p, c, 2), carries)
    carries = lax.fori_loop(0, n_far % 2, lambda _, c: far(n_far, c, 1), carries)
    lp = lam_ref[...]
    lam = (jnp.exp(jnp.sum(lp[0:1] * lp[1:2], axis=-1, keepdims=True))
           - jnp.exp(jnp.sum(lp[2:3] * lp[3:4], axis=-1, keepdims=True)) + lam_init)
    for h in range(n_heads):
        out = _softmax_result(carries[h])
        o = out[:, 0:tq] - lam * out[:, tq:2 * tq]
        o = o * lax.rsqrt(jnp.mean(o * o, axis=0, keepdims=True) + NORM_EPS) * (1.0 - lam_init)
        o_ref[:, h * width:(h + 1) * width] = (o.T * subg_ref[...]).astype(o_ref.dtype)


def _diff_attention(za, tbl, lam_p, subg, bsz, seq, lam_init, tq, heads_per_step):
    nq = seq // tq
    width = 2 * A_HEAD_DIM
    steps = A_HEADS // heads_per_step
    wide = heads_per_step * width
    return pl.pallas_call(
        functools.partial(_diff_kernel, tq=tq, lam_init=lam_init),
        name="diff_attn",
        grid=(bsz, steps, nq),
        in_specs=[_smem(), _resident(lam_p.shape), _resident((1, width)),
                  pl.BlockSpec((tq, wide), lambda b, h, i: (b * nq + i, h)),
                  pl.BlockSpec((seq, wide), lambda b, h, i: (b, steps + h)),
                  pl.BlockSpec((seq, wide), lambda b, h, i: (b, 2 * steps + h))],
        out_specs=pl.BlockSpec((tq, wide), lambda b, h, i: (b * nq + i, h)),
        out_shape=jax.ShapeDtypeStruct((bsz * seq, A_WIDTH), BF16),
        scratch_shapes=[pltpu.VMEM((heads_per_step, tq + seq, 2 * LANES), BF16),
                        pltpu.VMEM((heads_per_step, nq + 1, width, tq), BF16),
                        pltpu.VMEM((heads_per_step, 2 * tq, tq), F32)],
        compiler_params=_params("arbitrary", "arbitrary", "arbitrary"),
    )(tbl, lam_p, subg.reshape(1, width), za, za, za)


def _compress_kernel(xk_ref, xv_ref, pos_ref, w1_ref, w2_ref, kc_ref, vc_ref):
    rows = xk_ref.shape[0]
    for t, (x_ref, o_ref) in enumerate(((xk_ref, kc_ref), (xv_ref, vc_ref))):
        x = x_ref[...].astype(F32)
        first = _dot((x + pos_ref[t, 0]).astype(BF16), w1_ref[t, 0])
        second = _dot((x + pos_ref[t, 1]).astype(BF16), w1_ref[t, 1])
        hidden = first + pltpu.roll(second, rows - 1, 0)
        o_ref[...] = _dot(jax.nn.gelu(hidden).astype(BF16), w2_ref[t]).astype(o_ref.dtype)


def _compress(xk, xv, pos, w1, w2):
    bsz, rows, width = xk.shape
    xspec = pl.BlockSpec((None, rows, width), lambda b: (b, 0, 0))
    ospec = pl.BlockSpec((None, rows, B_KV), lambda b: (b, 0, 0))
    return pl.pallas_call(
        _compress_kernel,
        name="nsa_compress",
        grid=(bsz,),
        in_specs=[xspec, xspec, _resident(pos.shape), _resident(w1.shape), _resident(w2.shape)],
        out_specs=[ospec, ospec],
        out_shape=[jax.ShapeDtypeStruct((bsz, rows, B_KV), BF16)] * 2,
        compiler_params=_params("arbitrary"),
    )(xk, xv, pos, w1, w2)


def _group_queries_t(q, group, rep):
    tq = q.shape[0]
    zero = jnp.zeros((HEAD_DIM, tq), F32)
    cols = []
    for r in range(rep):
        head = group * rep + r
        slab = q[:, (head // 2) * LANES:(head // 2 + 1) * LANES].astype(F32).T
        part = slab[(head % 2) * HEAD_DIM:(head % 2 + 1) * HEAD_DIM]
        cols.append(jnp.concatenate([part, zero] if group == 0 else [zero, part], axis=0))
    return jnp.concatenate(cols, axis=1).astype(BF16)


def _build_near_bias(near_ref, tbl_ref, head0, groups, rep, tq, window_is_tile, minus_far):
    key = lax.broadcasted_iota(jnp.int32, (2 * tq, tq), 0)
    qry = lax.broadcasted_iota(jnp.int32, (2 * tq, tq), 1)
    dist = tq + qry - key
    visible = (dist >= 0) & (dist < tq) if window_is_tile else dist >= 0
    for g in range(groups):
        for r in range(rep):
            head = head0 + g * rep + r
            near_ref[g, :, r * tq:(r + 1) * tq] = jnp.where(visible, _bias_of_dist(dist, tbl_ref, head, minus_far), NEG)


def _swa_tile(q, k_prev, k_diag, v_prev, v_diag, bias_ref, sink_ref, qi, tq):
    cols = C_REP * tq
    prev_bias = jnp.where(qi > 0, 0.0, NEG)
    vp_t = _transpose_bf16(v_prev)
    vd_t = _transpose_bf16(v_diag)
    heads_out = []
    for g in range(C_GROUPS):
        half = slice(g * HEAD_DIM, (g + 1) * HEAD_DIM)
        qg = _group_queries_t(q, g, C_REP)
        s = jnp.concatenate([_dot(k_prev, qg) + bias_ref[g, 0:tq, :] + prev_bias,
                             _dot(k_diag, qg) + bias_ref[g, tq:2 * tq, :]], axis=0)
        m, acc = _online_update(_softmax_init(HEAD_DIM, cols), s, [vp_t[half], vd_t[half]])
        for r in range(C_REP):
            cc = slice(r * tq, (r + 1) * tq)
            sink = sink_ref[g * C_REP + r] * LOG2E
            m_all = jnp.maximum(m[:, cc], sink)
            scale = jnp.exp2(m[:, cc] - m_all)
            denom = acc[HEAD_DIM:HEAD_DIM + 1, cc] * scale + jnp.exp2(sink - m_all)
            heads_out.append(acc[0:HEAD_DIM, cc] * (scale / denom))
    return jnp.concatenate(heads_out, axis=0).T


SEL_GROUP = 4
SEL_PAD = SEL_GROUP - 1
WIN_PAD = 4


def _nsa_kernel(tbl_ref, sink_ref, q_ref, ks_ref, vs_ref, kw_ref, vw_ref, kc_ref, vc_ref, bg_ref, ovt_ref,
                cq_ref, ckp_ref, ckd_ref, cvp_ref, cvd_ref, o_ref, oc_ref,
                kaug_ref, kwp_ref, vst_ref, vwt_ref, vct_ref, near_ref, edge_ref, band_ref, sc_ref, imp_ref,
                swab_ref, *, tq, seq):
    first_batch = pl.program_id(0) == 0
    qi = pl.program_id(1)
    n_cmp_pad = seq // B_CMP_STRIDE
    n_sel = seq // B_SEL_BLOCK
    top_k = min(B_SEL_TOPK, n_sel)
    nq = seq // tq
    cols = B_REP * tq
    cmp_per_tile = tq // B_CMP_STRIDE
    band = CMP_BAND_BACK + cmp_per_tile

    @pl.when(first_batch & (qi == 0))
    def _static_setup():
        _build_near_bias(near_ref, tbl_ref, A_HEADS, B_GROUPS, B_REP, tq, False, True)
        _build_near_bias(swab_ref, tbl_ref, A_HEADS + B_HEADS, C_GROUPS, C_REP, tq, True, False)
        key = lax.broadcasted_iota(jnp.int32, (tq, cols), 0)
        qry = lax.broadcasted_iota(jnp.int32, (tq, cols), 1) % tq
        edge_ref[...] = jnp.where(qry < key, 0.0, NEG)
        blk = lax.broadcasted_iota(jnp.int32, (band, tq), 0) - CMP_BAND_BACK
        dist = lax.broadcasted_iota(jnp.int32, (band, tq), 1) - (blk * B_CMP_STRIDE + B_CMP_LEN - 1)
        for g in range(B_GROUPS):
            for r in range(B_REP):
                head = A_HEADS + g * B_REP + r
                band_ref[g, :, r * tq:(r + 1) * tq] = jnp.where(dist >= 0, _bias_of_dist(dist, tbl_ref, head, True), NEG)
        sc_ref[:, 0:CMP_BAND_BACK, :] = jnp.zeros((B_GROUPS, CMP_BAND_BACK, cols), F32)
        pad = SEL_PAD * tq
        row = lax.broadcasted_iota(jnp.int32, (pad + seq, LANES), 0)
        lane = lax.broadcasted_iota(jnp.int32, (pad + seq, LANES), 1)
        blk_id = jnp.where(row < pad, LANES - 1, (row - pad) // B_SEL_BLOCK)
        kaug_ref[:, LANES:2 * LANES] = jnp.where(blk_id == lane, 1.0, 0.0).astype(BF16)
        kaug_ref[0:pad, 0:LANES] = jnp.zeros((pad, LANES), BF16)
        kwp_ref[0:WIN_PAD * tq, :] = jnp.zeros((WIN_PAD * tq, LANES), BF16)
        for c in range(SEL_PAD):
            vst_ref[c] = jnp.zeros(vst_ref.shape[1:], BF16)
        for c in range(WIN_PAD):
            vwt_ref[c] = jnp.zeros(vwt_ref.shape[1:], BF16)

    @pl.when(qi == 0)
    def _per_batch_setup():
        kaug_ref[SEL_PAD * tq:, 0:LANES] = ks_ref[...]
        kwp_ref[WIN_PAD * tq:, :] = kw_ref[...]
        for c in range(nq):
            vst_ref[SEL_PAD + c] = _transpose_bf16(vs_ref[c * tq:(c + 1) * tq, :])
            vwt_ref[WIN_PAD + c] = _transpose_bf16(vw_ref[c * tq:(c + 1) * tq, :])
        for c in range(n_cmp_pad // LANES):
            vct_ref[:, c * LANES:(c + 1) * LANES] = _transpose_bf16(vc_ref[c * LANES:(c + 1) * LANES, :])

    q = q_ref[...]
    gates_t = jax.nn.sigmoid(bg_ref[...]).T
    first_cmp = qi * cmp_per_tile
    cmp_row = lax.broadcasted_iota(jnp.int32, (n_cmp_pad, cols), 0)
    cmp_visible = cmp_row < first_cmp + cmp_per_tile
    blk = lax.broadcasted_iota(jnp.int32, (n_sel, tq), 0)
    cur = (qi * tq + lax.broadcasted_iota(jnp.int32, (n_sel, tq), 1)) // B_SEL_BLOCK
    forced = (blk == 0) | (blk == cur) | (blk == cur - 1)
    future = blk > cur
    halves = [slice(g * HEAD_DIM, (g + 1) * HEAD_DIM) for g in range(B_GROUPS)]

    qgs, o_cs, imps = [], [], []
    for g in range(B_GROUPS):
        qg = _group_queries_t(q, g, B_REP)
        sc_ref[g, CMP_BAND_BACK:, :] = _dot(kc_ref[...], qg)
        band_rows = pl.ds(pl.multiple_of(first_cmp, 8), band)
        sc_ref[g, band_rows, :] = sc_ref[g, band_rows, :] + band_ref[g]
        s = jnp.where(cmp_visible, sc_ref[g, CMP_BAND_BACK:, :], NEG)
        e = jnp.where(s > 0.5 * NEG, jnp.exp2(s - jnp.max(s, axis=0, keepdims=True)), 0.0)
        l = jnp.sum(e, axis=0, keepdims=True)
        p_c = e * jnp.where(l > 0.0, 1.0 / jnp.where(l > 0.0, l, 1.0), 0.0)
        o_cs.append(_dot(vct_ref[halves[g], :], p_c.astype(BF16)))
        p_sum = p_c[:, 0:tq]
        for r in range(1, B_REP):
            p_sum = p_sum + p_c[:, r * tq:(r + 1) * tq]
        imp = _dot(ovt_ref[...], p_sum.astype(BF16))
        imp = jnp.where(forced, B_SEL_FORCE, jnp.where(future, -B_SEL_FORCE, imp))
        imp_ref[g] = imp
        qgs.append(qg)
        imps.append(imp)

    def rank_body(other, ranks):
        out = []
        for g in range(B_GROUPS):
            val = imp_ref[g, pl.ds(other, 1), :]
            ahead = (val > imps[g]) | ((val == imps[g]) & (blk > other))
            out.append(ranks[g] + jnp.where(ahead, 1.0, 0.0))
        return tuple(out)

    last_blk = ((qi + 1) * tq - 1) // B_SEL_BLOCK
    ranks = lax.fori_loop(0, last_blk + 1, rank_body, tuple(jnp.zeros((n_sel, tq), F32) for _ in range(B_GROUPS)))
    q_augs = []
    for g in range(B_GROUPS):
        sel_bias = jnp.concatenate([jnp.where(ranks[g] < float(top_k), 0.0, UNSEL),
                                    jnp.full((LANES - n_sel, tq), UNSEL, F32)], axis=0).astype(BF16)
        q_augs.append(jnp.concatenate([qgs[g], jnp.concatenate([sel_bias] * B_REP, axis=1)], axis=0))

    pair = cols
    chains = [(g, slice(0, cols)) for g in range(B_GROUPS)]
    heads_per_chain = pair // tq

    def sel_logits(first_tile):
        start = pl.multiple_of(first_tile * tq, tq)
        keys = kaug_ref[pl.ds(start, SEL_GROUP * tq), :]
        return tuple(_dot(keys, q_augs[g][:, cc]) for g, cc in chains)

    def sel_update(first_tile, logits, carries):
        return tuple(_online_update(carry, s, [vst_ref[first_tile + t, halves[g], :] for t in range(SEL_GROUP)])
                     for (g, cc), s, carry in zip(chains, logits, carries))

    start = pl.multiple_of(qi * tq, tq)
    win_keys = kwp_ref[pl.ds(start, (WIN_PAD + 1) * tq), :]
    partial = []
    for g, cc in chains:
        s_w = _dot(win_keys, qgs[g][:, cc])
        slabs = []
        for t in range(WIN_PAD + 1):
            slab = s_w[t * tq:(t + 1) * tq]
            if t == 0:
                slab = slab + edge_ref[:, cc]
            if t >= WIN_PAD - 1:
                slab = slab + near_ref[g, (t - WIN_PAD + 1) * tq:(t - WIN_PAD + 2) * tq, cc]
            if t < WIN_PAD:
                slab = slab + jnp.where(qi >= WIN_PAD - t, 0.0, NEG)
            slabs.append(slab)
        o_w = _softmax_result(_online_update(_softmax_init(HEAD_DIM, pair), jnp.concatenate(slabs, axis=0),
                                             [vwt_ref[qi + t, halves[g], :] for t in range(WIN_PAD + 1)]))
        o_c = o_cs[g][:, cc]
        heads = []
        for rr in range(heads_per_chain):
            hc = slice(rr * tq, (rr + 1) * tq)
            c0 = (g * B_REP + rr) * 3
            heads.append(gates_t[c0:c0 + 1] * o_c[:, hc] + gates_t[c0 + 2:c0 + 3] * o_w[:, hc])
        partial.append(heads)

    oc_ref[...] = _swa_tile(cq_ref[...], ckp_ref[...], ckd_ref[...], cvp_ref[...], cvd_ref[...],
                            swab_ref, sink_ref, qi, tq).astype(oc_ref.dtype)

    def sel_far(gi, carries, count):
        tiles = [qi - SEL_GROUP * (gi + c) for c in range(count)]
        logits = [sel_logits(t) for t in tiles]
        for t, s in zip(tiles, logits):
            carries = sel_update(t, s, carries)
        return carries

    far_rows = (SEL_GROUP - 2) * tq
    nearest = tuple(jnp.concatenate([s[0:far_rows], s[far_rows:] + near_ref[g, :, cc]], axis=0)
                    for (g, cc), s in zip(chains, sel_logits(qi)))
    sel = sel_update(qi, nearest, tuple(_softmax_init(HEAD_DIM, pair) for _ in chains))
    n_far = qi // SEL_GROUP
    sel = lax.fori_loop(0, n_far // 2, lambda p, c: sel_far(1 + 2 * p, c, 2), sel)
    sel = lax.fori_loop(0, n_far % 2, lambda _, c: sel_far(n_far, c, 1), sel)

    heads_out = []
    for ci, (g, cc) in enumerate(chains):
        o_s = _softmax_result(sel[ci])
        for rr in range(heads_per_chain):
            hc = slice(rr * tq, (rr + 1) * tq)
            c1 = (g * B_REP + rr) * 3 + 1
            heads_out.append(partial[ci][rr] + gates_t[c1:c1 + 1] * o_s[:, hc])
    o_ref[...] = jnp.concatenate(heads_out, axis=0).T.astype(o_ref.dtype)


def _nsa_swa_attention(zb, kc, vc, bgate, zc, tbl, sinks, ovt, bsz, seq, tq):
    nq = seq // tq
    n_cmp_pad = seq // B_CMP_STRIDE
    cols = B_REP * tq
    kv = lambda col: pl.BlockSpec((seq, LANES), lambda b, i: (b, col))
    cmp_spec = pl.BlockSpec((None, n_cmp_pad, B_KV), lambda b, i: (b, 0, 0))
    qcols = B_WIDTH // LANES
    ccols = C_WIDTH // LANES
    wide = lambda width: pl.BlockSpec((tq, width), lambda b, i: (b * nq + i, 0))
    prev = lambda col: pl.BlockSpec((tq, LANES), lambda b, i: (b * nq + jnp.maximum(i - 1, 0), col))
    diag = lambda col: pl.BlockSpec((tq, LANES), lambda b, i: (b * nq + i, col))
    return pl.pallas_call(
        functools.partial(_nsa_kernel, tq=tq, seq=seq),
        name="nsa_attn",
        grid=(bsz, nq),
        in_specs=[_smem(), _smem(),
                  wide(B_WIDTH),
                  kv(qcols), kv(qcols + 1), kv(qcols + 2), kv(qcols + 3),
                  cmp_spec, cmp_spec,
                  wide(LANES),
                  _resident(ovt.shape),
                  wide(C_WIDTH), prev(ccols), diag(ccols), prev(ccols + 1), diag(ccols + 1)],
        out_specs=[wide(B_WIDTH), wide(C_WIDTH)],
        out_shape=[jax.ShapeDtypeStruct((bsz * seq, B_WIDTH), BF16),
                   jax.ShapeDtypeStruct((bsz * seq, C_WIDTH), BF16)],
        scratch_shapes=[pltpu.VMEM((SEL_PAD * tq + seq, 2 * LANES), BF16),
                        pltpu.VMEM((WIN_PAD * tq + seq, LANES), BF16),
                        pltpu.VMEM((SEL_PAD + nq, B_KV, tq), BF16),
                        pltpu.VMEM((WIN_PAD + nq, B_KV, tq), BF16),
                        pltpu.VMEM((B_KV, n_cmp_pad), BF16),
                        pltpu.VMEM((B_GROUPS, 2 * tq, cols), F32),
                        pltpu.VMEM((tq, cols), F32),
                        pltpu.VMEM((B_GROUPS, CMP_BAND_BACK + tq // B_CMP_STRIDE, cols), F32),
                        pltpu.VMEM((B_GROUPS, CMP_BAND_BACK + n_cmp_pad, cols), F32),
                        pltpu.VMEM((B_GROUPS, seq // B_SEL_BLOCK, tq), F32),
                        pltpu.VMEM((C_GROUPS, 2 * tq, C_REP * tq), F32)],
        compiler_params=_params("arbitrary", "arbitrary"),
    )(tbl, sinks, zb, zb, zb, zb, zb, kc, vc, bgate, ovt, zc, zc, zc, zc, zc)


def _prep_w_in(w_in):
    sizes = (A_WIDTH, A_WIDTH, A_WIDTH, B_WIDTH, B_KV, B_KV, B_KV, B_KV, B_KV, B_KV, 3 * B_HEADS,
             C_WIDTH, C_KV, C_KV, D_MODEL, D_MODEL, D_MODEL)
    offs = np.concatenate([[0], np.cumsum(sizes)])
    (aq, ak, av, bq, bkc, bvc, bks, bvs, bkw, bvw, bgate, cq, ck, cv, ga, gb, gc) = [
        w_in[..., offs[i]:offs[i + 1]] for i in range(len(sizes))]
    scale = HEAD_DIM ** -0.5 * LOG2E
    bgate = jnp.pad(bgate, ((0, 0), (0, 0), (0, LANES - 3 * B_HEADS)))
    cols = [aq * scale, ak, av, bq * scale, bks, bvs, bkw, bvw, bkc, bvc, cq * scale, ck, cv, ga, gb, gc, bgate]
    return jnp.concatenate(cols, axis=-1).astype(BF16)


def _prep_compress(cmp_pos, cmp_w1, cmp_w2):
    depth = cmp_pos.shape[0]
    half = B_CMP_STRIDE
    pos = cmp_pos.reshape(depth, 2, 2, half, 1, B_HEAD_DIM)
    pos = jnp.broadcast_to(pos, (depth, 2, 2, half, B_GROUPS, B_HEAD_DIM)).reshape(depth, 2, 2, 1, half * B_KV)
    w1 = cmp_w1.reshape(depth, 2, 2, half, 1, B_HEAD_DIM, 1, B_CMP_HIDDEN)
    eye = jnp.eye(B_GROUPS, dtype=cmp_w1.dtype).reshape(1, 1, 1, 1, B_GROUPS, 1, B_GROUPS, 1)
    w1 = (w1 * eye).reshape(depth, 2, 2, half * B_KV, B_GROUPS * B_CMP_HIDDEN)
    w2 = cmp_w2.reshape(depth, 2, 1, B_CMP_HIDDEN, 1, B_HEAD_DIM)
    eye2 = jnp.eye(B_GROUPS, dtype=cmp_w2.dtype).reshape(1, 1, B_GROUPS, 1, B_GROUPS, 1)
    w2 = (w2 * eye2).reshape(depth, 2, B_GROUPS * B_CMP_HIDDEN, B_KV)
    return pos.astype(F32), w1.astype(BF16), w2.astype(BF16)


def _overlap_t(seq):
    n_cmp_pad = seq // B_CMP_STRIDE
    start = np.arange(n_cmp_pad) * B_CMP_STRIDE
    sel = np.arange(seq // B_SEL_BLOCK) * B_SEL_BLOCK
    ov = (start[None, :] < sel[:, None] + B_SEL_BLOCK) & (start[None, :] + B_CMP_LEN > sel[:, None])
    ov &= (np.arange(n_cmp_pad) < n_cmp_pad - 1)[None, :]
    return jnp.asarray(ov.astype(np.float32), dtype=BF16)


def kernel(x, p, norm_g, ffn1_wi, ffn1_wo, w_in, diff_lambda, diff_subln, nsa_cmp_pos, nsa_cmp_w1,
           nsa_cmp_w2, swa_sinks, w_branch, w_out, ffn2_wi, ffn2_wo, w_ple, w_ple_gate, rel_bias, final_norm):
    bsz, seq, _ = x.shape
    depth = norm_g.shape[0]
    tokens = bsz * seq
    assert seq % 1024 == 0 and seq // B_SEL_BLOCK <= LANES
    tm, tm_post, tq_a, tq = 512, 256, 256, 128
    assert FAR_DIST + B_CMP_LEN - 1 <= CMP_BAND_BACK * B_CMP_STRIDE and FAR_DIST <= tq
    assert WIN_PAD * tq == B_WINDOW and tq == C_WINDOW and seq // B_SEL_BLOCK < LANES

    wi1, wo1 = ffn1_wi.astype(BF16), ffn1_wo.astype(BF16)
    wi2, wo2 = ffn2_wi.astype(BF16), ffn2_wo.astype(BF16)
    w_in_r = _prep_w_in(w_in)
    cmp_pos, cmp_w1, cmp_w2 = _prep_compress(nsa_cmp_pos, nsa_cmp_w1, nsa_cmp_w2)
    wb, wo = w_branch.astype(BF16), w_out.astype(BF16)
    wpg, wpe = w_ple_gate.astype(BF16), w_ple.astype(BF16)
    tbl = rel_bias.astype(F32)
    ovt = _overlap_t(seq)
    chunk_rows = seq // B_CMP_STRIDE

    h = x.reshape(tokens, D_MODEL)
    for i in range(depth):
        h = _ffn(h, norm_g[i, 0], wi1[i], wo1[i], tm)
        za, zb, zkc, zvc, zc, gates, bgate = _inproj(h, norm_g[i, 1], w_in_r[i], tm)
        lam_init = 0.8 - 0.6 * math.exp(-0.3 * i)
        ya = _diff_attention(za, tbl, diff_lambda[i].astype(F32), diff_subln[i], bsz, seq, lam_init, tq_a, 4)
        kc, vc = _compress(zkc.reshape(bsz, chunk_rows, B_CMP_STRIDE * B_KV),
                           zvc.reshape(bsz, chunk_rows, B_CMP_STRIDE * B_KV), cmp_pos[i], cmp_w1[i], cmp_w2[i])
        yb, yc = _nsa_swa_attention(zb, kc, vc, bgate, zc, tbl, swa_sinks[i].astype(F32), ovt, bsz, seq, tq)
        h = _post(h, ya, yb, yc, gates, wb[i], wo[i], tm_post)
        h = _ffn(h, norm_g[i, 2], wi2[i], wo2[i], tm)
        h = _ple(h, p[i].reshape(tokens, PLE_DIM), norm_g[i, 3], wpg[i], wpe[i], final_norm, tm,
                 final=(i == depth - 1))
    return h.reshape(bsz, seq, D_MODEL)
```

```python
import functools
import math

import numpy as np
import jax
import jax.numpy as jnp
from jax import lax
from jax.experimental import pallas as pl
from jax.experimental.pallas import tpu as pltpu

F32 = jnp.float32
BF16 = jnp.bfloat16

D_MODEL = 1024
PLE_DIM = 256
D_FF = 2816
NORM_EPS = 1e-6
NEG = -1e30
NUM_BUCKETS = 32
MAX_DISTANCE = 128
A_HEADS = 4
A_HEAD_DIM = 64
B_HEADS = 8
B_GROUPS = 2
B_REP = B_HEADS // B_GROUPS
B_HEAD_DIM = 64
B_CMP_LEN = 32
B_CMP_STRIDE = 16
B_CMP_HIDDEN = 256
B_SEL_BLOCK = 64
B_SEL_TOPK = 16
B_WINDOW = 512
B_SEL_FORCE = 1e6
C_HEADS = 8
C_GROUPS = 2
C_REP = C_HEADS // C_GROUPS
C_HEAD_DIM = 64
C_WINDOW = 128
A_WIDTH = A_HEADS * 2 * A_HEAD_DIM
B_WIDTH = B_HEADS * B_HEAD_DIM
B_KV = B_GROUPS * B_HEAD_DIM
C_WIDTH = C_HEADS * C_HEAD_DIM
C_KV = C_GROUPS * C_HEAD_DIM
N_BIAS_HEADS = A_HEADS + B_HEADS + C_HEADS

LANES = 128
HEAD_DIM = 64
VMEM_LIMIT = 56 * 1024 * 1024
UNSEL = -float(2 ** 30)
FFN_CHUNKS = ((0, 1536), (1536, 1280))
CMP_BAND_BACK = 16
LOG2E = math.log2(math.e)
ONES_ROWS = 16


def _bucket_thresholds():
    n = np.arange(4 * MAX_DISTANCE)
    max_exact = NUM_BUCKETS // 2
    nf = np.maximum(n, 1).astype(np.float32)
    large = max_exact + (np.log(nf / max_exact) / math.log(MAX_DISTANCE / max_exact)
                         * (NUM_BUCKETS - max_exact)).astype(np.int32)
    bucket = np.where(n < max_exact, n, np.minimum(large, NUM_BUCKETS - 1))
    out = []
    for b in range(1, NUM_BUCKETS):
        hit = np.nonzero(bucket == b)[0]
        if hit.size:
            out.append((b, int(hit[0])))
    return tuple(out)


BUCKET_LO = _bucket_thresholds()
FAR_DIST = BUCKET_LO[-1][1]


def _bias_of_dist(dist, tbl_ref, head, minus_far=False):
    shift = tbl_ref[NUM_BUCKETS - 1, head] if minus_far else 0.0
    out = jnp.full(dist.shape, (tbl_ref[0, head] - shift) * LOG2E, F32)
    for b, lo in BUCKET_LO:
        out = jnp.where(dist >= lo, (tbl_ref[b, head] - shift) * LOG2E, out)
    return out


def _rms(x, g):
    return x * lax.rsqrt(jnp.mean(x * x, axis=-1, keepdims=True) + NORM_EPS) * g


def _dot(a, b):
    return jnp.dot(a, b, preferred_element_type=F32)


def _online_update(carry, s, vts):
    m, acc = carry
    m_new = jnp.maximum(m, jnp.max(s, axis=0, keepdims=True))
    p = jnp.exp2(s - m_new).astype(BF16)
    values = jnp.concatenate([jnp.concatenate(vts, axis=1), jnp.ones((ONES_ROWS, s.shape[0]), BF16)], axis=0)
    return m_new, jnp.exp2(m - m_new) * acc + _dot(values, p)


def _softmax_init(width, queries):
    return jnp.full((1, queries), NEG, F32), jnp.zeros((width + ONES_ROWS, queries), F32)


def _softmax_result(carry):
    _, acc = carry
    width = acc.shape[0] - ONES_ROWS
    return acc[0:width] / acc[width:width + 1]


def _transpose_bf16(x):
    return x.astype(F32).T.astype(BF16)


def _params(*sem):
    return pltpu.CompilerParams(dimension_semantics=sem, vmem_limit_bytes=VMEM_LIMIT)


def _resident(shape):
    zeros = (0,) * len(shape)
    return pl.BlockSpec(shape, lambda *_: zeros, pipeline_mode=pl.Buffered(1))


def _smem():
    return pl.BlockSpec(memory_space=pltpu.SMEM)


def _half_ffn(x, g_ref, wi_ref, wo_ref):
    n = _rms(x, g_ref[...]).astype(BF16)
    acc = x
    for lo, width in FFN_CHUNKS:
        gate = _dot(n, wi_ref[:, lo:lo + width])
        up = _dot(n, wi_ref[:, D_FF + lo:D_FF + lo + width])
        act = (gate * jax.nn.sigmoid(gate) * up).astype(BF16)
        acc = acc + 0.5 * _dot(act, wo_ref[lo:lo + width, :])
    return acc


def _ffn_kernel(h_ref, g_ref, wi_ref, wo_ref, o_ref):
    o_ref[...] = _half_ffn(h_ref[...], g_ref, wi_ref, wo_ref)


def _ffn(h, g, wi, wo, tm):
    t = h.shape[0]
    tile = pl.BlockSpec((tm, D_MODEL), lambda i: (i, 0))
    return pl.pallas_call(
        _ffn_kernel,
        name="ffn",
        grid=(t // tm,),
        in_specs=[tile, _resident((1, D_MODEL)), _resident(wi.shape), _resident(wo.shape)],
        out_specs=tile,
        out_shape=jax.ShapeDtypeStruct(h.shape, F32),
        compiler_params=_params("arbitrary"),
    )(h, g.reshape(1, D_MODEL), wi, wo)


IN_A = 3 * A_WIDTH
IN_B = B_WIDTH + 4 * B_KV
IN_C = C_WIDTH + 2 * C_KV
IN_G = 3 * D_MODEL
IN_SECTIONS = (IN_A, IN_B, B_KV, B_KV, IN_C, IN_G, LANES)


def _inproj_kernel(h_ref, g_ref, w_ref, za_ref, zb_ref, zkc_ref, zvc_ref, zc_ref, gates_ref, bg_ref):
    n = _rms(h_ref[...], g_ref[...]).astype(BF16)
    lo = 0
    for ref, width in zip((za_ref, zb_ref, zkc_ref, zvc_ref, zc_ref, gates_ref, bg_ref), IN_SECTIONS):
        ref[...] = _dot(n, w_ref[:, lo:lo + width]).astype(ref.dtype)
        lo += width


def _inproj(h, g, w, tm):
    t = h.shape[0]
    dtypes = (BF16, BF16, BF16, BF16, BF16, F32, F32)
    return pl.pallas_call(
        _inproj_kernel,
        name="inproj",
        grid=(t // tm,),
        in_specs=[pl.BlockSpec((tm, D_MODEL), lambda i: (i, 0)), _resident((1, D_MODEL)), _resident(w.shape)],
        out_specs=[pl.BlockSpec((tm, width), lambda i: (i, 0)) for width in IN_SECTIONS],
        out_shape=[jax.ShapeDtypeStruct((t, width), dt) for width, dt in zip(IN_SECTIONS, dtypes)],
        compiler_params=_params("arbitrary"),
    )(h, g.reshape(1, D_MODEL), w)


def _tail_kernel(h_ref, ya_ref, yb_ref, yc_ref, gates_ref, p_ref, wb_ref, wo_ref, g2_ref, wi_ref, wo2_ref,
                 g3_ref, wg_ref, we_ref, gf_ref, o_ref, *, final):
    merged = jnp.zeros(h_ref.shape, F32)
    for m, y_ref in enumerate((ya_ref, yb_ref, yc_ref)):
        gate = jax.nn.sigmoid(gates_ref[:, m * D_MODEL:(m + 1) * D_MODEL])
        merged = merged + gate * _dot(y_ref[...], wb_ref[m])
    x = h_ref[...] + _dot(merged.astype(BF16), wo_ref[...])
    x = _half_ffn(x, g2_ref, wi_ref, wo2_ref)
    gate = jax.nn.sigmoid(_dot(_rms(x, g3_ref[...]).astype(BF16), wg_ref[...]))
    x = x + gate * _dot(p_ref[...].astype(BF16), we_ref[...])
    if final:
        x = _rms(x, gf_ref[...])
    o_ref[...] = x


def _tail(h, ya, yb, yc, gates, p, wb, wo, g2, wi, wo2, g3, wg, we, gf, tm, final):
    t = h.shape[0]
    rows = lambda width: pl.BlockSpec((tm, width), lambda i: (i, 0))
    vec = _resident((1, D_MODEL))
    return pl.pallas_call(
        functools.partial(_tail_kernel, final=final),
        name="tail",
        grid=(t // tm,),
        in_specs=[rows(D_MODEL), rows(A_WIDTH), rows(B_WIDTH), rows(C_WIDTH), rows(IN_G), rows(PLE_DIM),
                  _resident(wb.shape), _resident(wo.shape), vec, _resident(wi.shape), _resident(wo2.shape),
                  vec, _resident(wg.shape), _resident(we.shape), vec],
        out_specs=rows(D_MODEL),
        out_shape=jax.ShapeDtypeStruct(h.shape, F32),
        compiler_params=_params("arbitrary"),
    )(h, ya, yb, yc, gates, p, wb, wo, g2.reshape(1, D_MODEL), wi, wo2, g3.reshape(1, D_MODEL), wg, we,
      gf.reshape(1, D_MODEL))


def _diff_kernel(tbl_ref, lam_ref, subg_ref, q_ref, k_ref, v_ref, o_ref, kaug_ref, vt_ref, near_ref,
                 *, tq, lam_init):
    first_call = (pl.program_id(0) == 0) & (pl.program_id(1) == 0)
    qi = pl.program_id(2)
    width = 2 * A_HEAD_DIM
    n_heads = kaug_ref.shape[0]
    head0 = pl.program_id(1) * n_heads

    @pl.when(first_call & (qi == 0))
    def _static_setup():
        lane = lax.broadcasted_iota(jnp.int32, (tq, 2 * LANES), 1)
        for h in range(n_heads):
            kaug_ref[h, 0:tq, :] = jnp.where(lane == LANES, 1.0, 0.0).astype(BF16)
            kaug_ref[h, tq:, LANES:2 * LANES] = jnp.zeros((kaug_ref.shape[1] - tq, LANES), BF16)
            vt_ref[h, 0] = jnp.zeros(vt_ref.shape[2:], BF16)

    @pl.when(qi == 0)
    def _per_batch_setup():
        key = lax.broadcasted_iota(jnp.int32, (2 * tq, tq), 0)
        qry = lax.broadcasted_iota(jnp.int32, (2 * tq, tq), 1)
        dist = tq + qry - key
        for h in range(n_heads):
            cols = slice(h * width, (h + 1) * width)
            kaug_ref[h, tq:, 0:LANES] = k_ref[:, cols]
            for c in range(vt_ref.shape[1] - 1):
                vt_ref[h, c + 1] = _transpose_bf16(v_ref[c * tq:(c + 1) * tq, cols])
            near_ref[h] = jnp.where(dist >= 0, _bias_of_dist(dist, tbl_ref, head0 + h, True), NEG)

    zero = jnp.zeros((A_HEAD_DIM, tq), F32)
    flag_row = lax.broadcasted_iota(jnp.int32, (LANES, 2 * tq), 0) == 0
    flags = jnp.where(flag_row, UNSEL, 0.0)
    queries = []
    for h in range(n_heads):
        qt = q_ref[:, h * width:(h + 1) * width].astype(F32).T
        both = jnp.concatenate([jnp.concatenate([qt[0:A_HEAD_DIM], zero], axis=0),
                                jnp.concatenate([zero, qt[A_HEAD_DIM:width]], axis=0)], axis=1)
        queries.append(jnp.concatenate([both, flags], axis=0).astype(BF16))

    def logits_of(first_tile):
        start = pl.multiple_of(first_tile * tq, tq)
        return tuple(_dot(kaug_ref[h, pl.ds(start, 2 * tq), :], queries[h]) for h in range(n_heads))

    def update(first_tile, logits, carries):
        return tuple(_online_update(carries[h], logits[h], [vt_ref[h, first_tile], vt_ref[h, first_tile + 1]])
                     for h in range(n_heads))

    def far(gi, carries, count):
        tiles = [qi - 2 * (gi + c) for c in range(count)]
        logits = [logits_of(t) for t in tiles]
        for t, s in zip(tiles, logits):
            carries = update(t, s, carries)
        return carries

    nearest = tuple(s + jnp.concatenate([near_ref[h], near_ref[h]], axis=1) for h, s in enumerate(logits_of(qi)))
    carries = update(qi, nearest, tuple(_softmax_init(width, 2 * tq) for _ in range(n_heads)))
    n_far = qi // 2
    carries = lax.fori_loop(0, n_far // 2, lambda p, c: far(1 + 2 * p, c, 2), carries)
    carries = lax.fori_loop(0, n_far % 2, lambda _, c: far(n_far, c, 1), carries)
    lp = lam_ref[...]
    lam = (jnp.exp(jnp.sum(lp[0:1] * lp[1:2], axis=-1, keepdims=True))
           - jnp.exp(jnp.sum(lp[2:3] * lp[3:4], axis=-1, keepdims=True)) + lam_init)
    for h in range(n_heads):
        out = _softmax_result(carries[h])
        o = out[:, 0:tq] - lam * out[:, tq:2 * tq]
        o = o * lax.rsqrt(jnp.mean(o * o, axis=0, keepdims=True) + NORM_EPS) * (1.0 - lam_init)
        o_ref[:, h * width:(h + 1) * width] = (o.T * subg_ref[...]).astype(o_ref.dtype)


def _diff_attention(za, tbl, lam_p, subg, bsz, seq, lam_init, tq, heads_per_step):
    nq = seq // tq
    width = 2 * A_HEAD_DIM
    steps = A_HEADS // heads_per_step
    wide = heads_per_step * width
    return pl.pallas_call(
        functools.partial(_diff_kernel, tq=tq, lam_init=lam_init),
        name="diff_attn",
        grid=(bsz, steps, nq),
        in_specs=[_smem(), _resident(lam_p.shape), _resident((1, width)),
                  pl.BlockSpec((tq, wide), lambda b, h, i: (b * nq + i, h)),
                  pl.BlockSpec((seq, wide), lambda b, h, i: (b, steps + h)),
                  pl.BlockSpec((seq, wide), lambda b, h, i: (b, 2 * steps + h))],
        out_specs=pl.BlockSpec((tq, wide), lambda b, h, i: (b * nq + i, h)),
        out_shape=jax.ShapeDtypeStruct((bsz * seq, A_WIDTH), BF16),
        scratch_shapes=[pltpu.VMEM((heads_per_step, tq + seq, 2 * LANES), BF16),
                        pltpu.VMEM((heads_per_step, nq + 1, width, tq), BF16),
                        pltpu.VMEM((heads_per_step, 2 * tq, tq), F32)],
        compiler_params=_params("arbitrary", "arbitrary", "arbitrary"),
    )(tbl, lam_p, subg.reshape(1, width), za, za, za)


def _compress_kernel(xk_ref, xv_ref, pos_ref, w1_ref, w2_ref, kc_ref, vc_ref):
    rows = xk_ref.shape[0]
    for t, (x_ref, o_ref) in enumerate(((xk_ref, kc_ref), (xv_ref, vc_ref))):
        x = x_ref[...].astype(F32)
        first = _dot((x + pos_ref[t, 0]).astype(BF16), w1_ref[t, 0])
        second = _dot((x + pos_ref[t, 1]).astype(BF16), w1_ref[t, 1])
        hidden = first + pltpu.roll(second, rows - 1, 0)
        o_ref[...] = _dot(jax.nn.gelu(hidden).astype(BF16), w2_ref[t]).astype(o_ref.dtype)


def _compress(xk, xv, pos, w1, w2):
    bsz, rows, width = xk.shape
    xspec = pl.BlockSpec((None, rows, width), lambda b: (b, 0, 0))
    ospec = pl.BlockSpec((None, rows, B_KV), lambda b: (b, 0, 0))
    return pl.pallas_call(
        _compress_kernel,
        name="nsa_compress",
        grid=(bsz,),
        in_specs=[xspec, xspec, _resident(pos.shape), _resident(w1.shape), _resident(w2.shape)],
        out_specs=[ospec, ospec],
        out_shape=[jax.ShapeDtypeStruct((bsz, rows, B_KV), BF16)] * 2,
        compiler_params=_params("arbitrary"),
    )(xk, xv, pos, w1, w2)


def _group_queries_t(q, group, rep):
    tq = q.shape[0]
    zero = jnp.zeros((HEAD_DIM, tq), F32)
    cols = []
    for r in range(rep):
        head = group * rep + r
        slab = q[:, (head // 2) * LANES:(head // 2 + 1) * LANES].astype(F32).T
        part = slab[(head % 2) * HEAD_DIM:(head % 2 + 1) * HEAD_DIM]
        cols.append(jnp.concatenate([part, zero] if group == 0 else [zero, part], axis=0))
    return jnp.concatenate(cols, axis=1).astype(BF16)


def _build_near_bias(near_ref, tbl_ref, head0, groups, rep, tq, window_is_tile, minus_far):
    key = lax.broadcasted_iota(jnp.int32, (2 * tq, tq), 0)
    qry = lax.broadcasted_iota(jnp.int32, (2 * tq, tq), 1)
    dist = tq + qry - key
    visible = (dist >= 0) & (dist < tq) if window_is_tile else dist >= 0
    for g in range(groups):
        for r in range(rep):
            head = head0 + g * rep + r
            near_ref[g, :, r * tq:(r + 1) * tq] = jnp.where(visible, _bias_of_dist(dist, tbl_ref, head, minus_far), NEG)


def _swa_tile(q, k_prev, k_diag, v_prev, v_diag, bias_ref, sink_ref, qi, tq):
    cols = C_REP * tq
    prev_bias = jnp.where(qi > 0, 0.0, NEG)
    vp_t = _transpose_bf16(v_prev)
    vd_t = _transpose_bf16(v_diag)
    heads_out = []
    for g in range(C_GROUPS):
        half = slice(g * HEAD_DIM, (g + 1) * HEAD_DIM)
        qg = _group_queries_t(q, g, C_REP)
        s = jnp.concatenate([_dot(k_prev, qg) + bias_ref[g, 0:tq, :] + prev_bias,
                             _dot(k_diag, qg) + bias_ref[g, tq:2 * tq, :]], axis=0)
        m, acc = _online_update(_softmax_init(HEAD_DIM, cols), s, [vp_t[half], vd_t[half]])
        for r in range(C_REP):
            cc = slice(r * tq, (r + 1) * tq)
            sink = sink_ref[g * C_REP + r] * LOG2E
            m_all = jnp.maximum(m[:, cc], sink)
            scale = jnp.exp2(m[:, cc] - m_all)
            denom = acc[HEAD_DIM:HEAD_DIM + 1, cc] * scale + jnp.exp2(sink - m_all)
            heads_out.append(acc[0:HEAD_DIM, cc] * (scale / denom))
    return jnp.concatenate(heads_out, axis=0).T


SEL_GROUP = 4
SEL_PAD = SEL_GROUP - 1
WIN_PAD = 4


def _nsa_kernel(tbl_ref, sink_ref, q_ref, ks_ref, vs_ref, kw_ref, vw_ref, kc_ref, vc_ref, bg_ref, ovt_ref,
                cq_ref, ckp_ref, ckd_ref, cvp_ref, cvd_ref, o_ref, oc_ref,
                kaug_ref, kwp_ref, vst_ref, vwt_ref, vct_ref, near_ref, edge_ref, band_ref, sc_ref, imp_ref,
                swab_ref, *, tq, seq):
    first_batch = pl.program_id(0) == 0
    qi = pl.program_id(1)
    n_cmp_pad = seq // B_CMP_STRIDE
    n_sel = seq // B_SEL_BLOCK
    top_k = min(B_SEL_TOPK, n_sel)
    nq = seq // tq
    cols = B_REP * tq
    cmp_per_tile = tq // B_CMP_STRIDE
    band = CMP_BAND_BACK + cmp_per_tile

    @pl.when(first_batch & (qi == 0))
    def _static_setup():
        _build_near_bias(near_ref, tbl_ref, A_HEADS, B_GROUPS, B_REP, tq, False, True)
        _build_near_bias(swab_ref, tbl_ref, A_HEADS + B_HEADS, C_GROUPS, C_REP, tq, True, False)
        key = lax.broadcasted_iota(jnp.int32, (tq, cols), 0)
        qry = lax.broadcasted_iota(jnp.int32, (tq, cols), 1) % tq
        edge_ref[...] = jnp.where(qry < key, 0.0, NEG)
        blk = lax.broadcasted_iota(jnp.int32, (band, tq), 0) - CMP_BAND_BACK
        dist = lax.broadcasted_iota(jnp.int32, (band, tq), 1) - (blk * B_CMP_STRIDE + B_CMP_LEN - 1)
        for g in range(B_GROUPS):
            for r in range(B_REP):
                head = A_HEADS + g * B_REP + r
                band_ref[g, :, r * tq:(r + 1) * tq] = jnp.where(dist >= 0, _bias_of_dist(dist, tbl_ref, head, True), NEG)
        sc_ref[:, 0:CMP_BAND_BACK, :] = jnp.zeros((B_GROUPS, CMP_BAND_BACK, cols), F32)
        pad = SEL_PAD * tq
        row = lax.broadcasted_iota(jnp.int32, (pad + seq, LANES), 0)
        lane = lax.broadcasted_iota(jnp.int32, (pad + seq, LANES), 1)
        blk_id = jnp.where(row < pad, LANES - 1, (row - pad) // B_SEL_BLOCK)
        kaug_ref[:, LANES:2 * LANES] = jnp.where(blk_id == lane, 1.0, 0.0).astype(BF16)
        kaug_ref[0:pad, 0:LANES] = jnp.zeros((pad, LANES), BF16)
        kwp_ref[0:WIN_PAD * tq, :] = jnp.zeros((WIN_PAD * tq, LANES), BF16)
        for c in range(SEL_PAD):
            vst_ref[c] = jnp.zeros(vst_ref.shape[1:], BF16)
        for c in range(WIN_PAD):
            vwt_ref[c] = jnp.zeros(vwt_ref.shape[1:], BF16)

    @pl.when(qi == 0)
    def _per_batch_setup():
        kaug_ref[SEL_PAD * tq:, 0:LANES] = ks_ref[...]
        kwp_ref[WIN_PAD * tq:, :] = kw_ref[...]
        for c in range(nq):
            vst_ref[SEL_PAD + c] = _transpose_bf16(vs_ref[c * tq:(c + 1) * tq, :])
            vwt_ref[WIN_PAD + c] = _transpose_bf16(vw_ref[c * tq:(c + 1) * tq, :])
        for c in range(n_cmp_pad // LANES):
            vct_ref[:, c * LANES:(c + 1) * LANES] = _transpose_bf16(vc_ref[c * LANES:(c + 1) * LANES, :])

    q = q_ref[...]
    gates_t = jax.nn.sigmoid(bg_ref[...]).T
    first_cmp = qi * cmp_per_tile
    cmp_row = lax.broadcasted_iota(jnp.int32, (n_cmp_pad, cols), 0)
    cmp_visible = cmp_row < first_cmp + cmp_per_tile
    blk = lax.broadcasted_iota(jnp.int32, (n_sel, tq), 0)
    cur = (qi * tq + lax.broadcasted_iota(jnp.int32, (n_sel, tq), 1)) // B_SEL_BLOCK
    forced = (blk == 0) | (blk == cur) | (blk == cur - 1)
    future = blk > cur
    halves = [slice(g * HEAD_DIM, (g + 1) * HEAD_DIM) for g in range(B_GROUPS)]

    qgs, o_cs, imps = [], [], []
    for g in range(B_GROUPS):
        qg = _group_queries_t(q, g, B_REP)
        sc_ref[g, CMP_BAND_BACK:, :] = _dot(kc_ref[...], qg)
        band_rows = pl.ds(pl.multiple_of(first_cmp, 8), band)
        sc_ref[g, band_rows, :] = sc_ref[g, band_rows, :] + band_ref[g]
        s = jnp.where(cmp_visible, sc_ref[g, CMP_BAND_BACK:, :], NEG)
        e = jnp.where(s > 0.5 * NEG, jnp.exp2(s - jnp.max(s, axis=0, keepdims=True)), 0.0)
        l = jnp.sum(e, axis=0, keepdims=True)
        p_c = e * jnp.where(l > 0.0, 1.0 / jnp.where(l > 0.0, l, 1.0), 0.0)
        o_cs.append(_dot(vct_ref[halves[g], :], p_c.astype(BF16)))
        p_sum = p_c[:, 0:tq]
        for r in range(1, B_REP):
            p_sum = p_sum + p_c[:, r * tq:(r + 1) * tq]
        imp = _dot(ovt_ref[...], p_sum.astype(BF16))
        imp = jnp.where(forced, B_SEL_FORCE, jnp.where(future, -B_SEL_FORCE, imp))
        imp_ref[g] = imp
        qgs.append(qg)
        imps.append(imp)

    def rank_body(other, ranks):
        out = []
        for g in range(B_GROUPS):
            val = imp_ref[g, pl.ds(other, 1), :]
            ahead = (val > imps[g]) | ((val == imps[g]) & (blk > other))
            out.append(ranks[g] + jnp.where(ahead, 1.0, 0.0))
        return tuple(out)

    last_blk = ((qi + 1) * tq - 1) // B_SEL_BLOCK
    ranks = lax.fori_loop(0, last_blk + 1, rank_body, tuple(jnp.zeros((n_sel, tq), F32) for _ in range(B_GROUPS)))
    q_augs = []
    for g in range(B_GROUPS):
        sel_bias = jnp.concatenate([jnp.where(ranks[g] < float(top_k), 0.0, UNSEL),
                                    jnp.full((LANES - n_sel, tq), UNSEL, F32)], axis=0).astype(BF16)
        q_augs.append(jnp.concatenate([qgs[g], jnp.concatenate([sel_bias] * B_REP, axis=1)], axis=0))

    pair = cols
    chains = [(g, slice(0, cols)) for g in range(B_GROUPS)]
    heads_per_chain = pair // tq

    def sel_logits(first_tile):
        start = pl.multiple_of(first_tile * tq, tq)
        keys = kaug_ref[pl.ds(start, SEL_GROUP * tq), :]
        return tuple(_dot(keys, q_augs[g][:, cc]) for g, cc in chains)

    def sel_update(first_tile, logits, carries):
        return tuple(_online_update(carry, s, [vst_ref[first_tile + t, halves[g], :] for t in range(SEL_GROUP)])
                     for (g, cc), s, carry in zip(chains, logits, carries))

    start = pl.multiple_of(qi * tq, tq)
    win_keys = kwp_ref[pl.ds(start, (WIN_PAD + 1) * tq), :]
    partial = []
    for g, cc in chains:
        s_w = _dot(win_keys, qgs[g][:, cc])
        slabs = []
        for t in range(WIN_PAD + 1):
            slab = s_w[t * tq:(t + 1) * tq]
            if t == 0:
                slab = slab + edge_ref[:, cc]
            if t >= WIN_PAD - 1:
                slab = slab + near_ref[g, (t - WIN_PAD + 1) * tq:(t - WIN_PAD + 2) * tq, cc]
            if t < WIN_PAD:
                slab = slab + jnp.where(qi >= WIN_PAD - t, 0.0, NEG)
            slabs.append(slab)
        o_w = _softmax_result(_online_update(_softmax_init(HEAD_DIM, pair), jnp.concatenate(slabs, axis=0),
                                             [vwt_ref[qi + t, halves[g], :] for t in range(WIN_PAD + 1)]))
        o_c = o_cs[g][:, cc]
        heads = []
        for rr in range(heads_per_chain):
            hc = slice(rr * tq, (rr + 1) * tq)
            c0 = (g * B_REP + rr) * 3
            heads.append(gates_t[c0:c0 + 1] * o_c[:, hc] + gates_t[c0 + 2:c0 + 3] * o_w[:, hc])
        partial.append(heads)

    oc_ref[...] = _swa_tile(cq_ref[...], ckp_ref[...], ckd_ref[...], cvp_ref[...], cvd_ref[...],
                            swab_ref, sink_ref, qi, tq).astype(oc_ref.dtype)

    def sel_far(gi, carries, count):
        tiles = [qi - SEL_GROUP * (gi + c) for c in range(count)]
        logits = [sel_logits(t) for t in tiles]
        for t, s in zip(tiles, logits):
            carries = sel_update(t, s, carries)
        return carries

    far_rows = (SEL_GROUP - 2) * tq
    nearest = tuple(jnp.concatenate([s[0:far_rows], s[far_rows:] + near_ref[g, :, cc]], axis=0)
                    for (g, cc), s in zip(chains, sel_logits(qi)))
    sel = sel_update(qi, nearest, tuple(_softmax_init(HEAD_DIM, pair) for _ in chains))
    n_far = qi // SEL_GROUP
    sel = lax.fori_loop(0, n_far // 2, lambda p, c: sel_far(1 + 2 * p, c, 2), sel)
    sel = lax.fori_loop(0, n_far % 2, lambda _, c: sel_far(n_far, c, 1), sel)

    heads_out = []
    for ci, (g, cc) in enumerate(chains):
        o_s = _softmax_result(sel[ci])
        for rr in range(heads_per_chain):
            hc = slice(rr * tq, (rr + 1) * tq)
            c1 = (g * B_REP + rr) * 3 + 1
            heads_out.append(partial[ci][rr] + gates_t[c1:c1 + 1] * o_s[:, hc])
    o_ref[...] = jnp.concatenate(heads_out, axis=0).T.astype(o_ref.dtype)


def _nsa_swa_attention(zb, kc, vc, bgate, zc, tbl, sinks, ovt, bsz, seq, tq):
    nq = seq // tq
    n_cmp_pad = seq // B_CMP_STRIDE
    cols = B_REP * tq
    kv = lambda col: pl.BlockSpec((seq, LANES), lambda b, i: (b, col))
    cmp_spec = pl.BlockSpec((None, n_cmp_pad, B_KV), lambda b, i: (b, 0, 0))
    qcols = B_WIDTH // LANES
    ccols = C_WIDTH // LANES
    wide = lambda width: pl.BlockSpec((tq, width), lambda b, i: (b * nq + i, 0))
    prev = lambda col: pl.BlockSpec((tq, LANES), lambda b, i: (b * nq + jnp.maximum(i - 1, 0), col))
    diag = lambda col: pl.BlockSpec((tq, LANES), lambda b, i: (b * nq + i, col))
    return pl.pallas_call(
        functools.partial(_nsa_kernel, tq=tq, seq=seq),
        name="nsa_attn",
        grid=(bsz, nq),
        in_specs=[_smem(), _smem(),
                  wide(B_WIDTH),
                  kv(qcols), kv(qcols + 1), kv(qcols + 2), kv(qcols + 3),
                  cmp_spec, cmp_spec,
                  wide(LANES),
                  _resident(ovt.shape),
                  wide(C_WIDTH), prev(ccols), diag(ccols), prev(ccols + 1), diag(ccols + 1)],
        out_specs=[wide(B_WIDTH), wide(C_WIDTH)],
        out_shape=[jax.ShapeDtypeStruct((bsz * seq, B_WIDTH), BF16),
                   jax.ShapeDtypeStruct((bsz * seq, C_WIDTH), BF16)],
        scratch_shapes=[pltpu.VMEM((SEL_PAD * tq + seq, 2 * LANES), BF16),
                        pltpu.VMEM((WIN_PAD * tq + seq, LANES), BF16),
                        pltpu.VMEM((SEL_PAD + nq, B_KV, tq), BF16),
                        pltpu.VMEM((WIN_PAD + nq, B_KV, tq), BF16),
                        pltpu.VMEM((B_KV, n_cmp_pad), BF16),
                        pltpu.VMEM((B_GROUPS, 2 * tq, cols), F32),
                        pltpu.VMEM((tq, cols), F32),
                        pltpu.VMEM((B_GROUPS, CMP_BAND_BACK + tq // B_CMP_STRIDE, cols), F32),
                        pltpu.VMEM((B_GROUPS, CMP_BAND_BACK + n_cmp_pad, cols), F32),
                        pltpu.VMEM((B_GROUPS, seq // B_SEL_BLOCK, tq), F32),
                        pltpu.VMEM((C_GROUPS, 2 * tq, C_REP * tq), F32)],
        compiler_params=_params("arbitrary", "arbitrary"),
    )(tbl, sinks, zb, zb, zb, zb, zb, kc, vc, bgate, ovt, zc, zc, zc, zc, zc)


def _prep_w_in(w_in):
    sizes = (A_WIDTH, A_WIDTH, A_WIDTH, B_WIDTH, B_KV, B_KV, B_KV, B_KV, B_KV, B_KV, 3 * B_HEADS,
             C_WIDTH, C_KV, C_KV, D_MODEL, D_MODEL, D_MODEL)
    offs = np.concatenate([[0], np.cumsum(sizes)])
    (aq, ak, av, bq, bkc, bvc, bks, bvs, bkw, bvw, bgate, cq, ck, cv, ga, gb, gc) = [
        w_in[..., offs[i]:offs[i + 1]] for i in range(len(sizes))]
    scale = HEAD_DIM ** -0.5 * LOG2E
    bgate = jnp.pad(bgate, ((0, 0), (0, 0), (0, LANES - 3 * B_HEADS)))
    cols = [aq * scale, ak, av, bq * scale, bks, bvs, bkw, bvw, bkc, bvc, cq * scale, ck, cv, ga, gb, gc, bgate]
    return jnp.concatenate(cols, axis=-1).astype(BF16)


def _prep_compress(cmp_pos, cmp_w1, cmp_w2):
    depth = cmp_pos.shape[0]
    half = B_CMP_STRIDE
    pos = cmp_pos.reshape(depth, 2, 2, half, 1, B_HEAD_DIM)
    pos = jnp.broadcast_to(pos, (depth, 2, 2, half, B_GROUPS, B_HEAD_DIM)).reshape(depth, 2, 2, 1, half * B_KV)
    w1 = cmp_w1.reshape(depth, 2, 2, half, 1, B_HEAD_DIM, 1, B_CMP_HIDDEN)
    eye = jnp.eye(B_GROUPS, dtype=cmp_w1.dtype).reshape(1, 1, 1, 1, B_GROUPS, 1, B_GROUPS, 1)
    w1 = (w1 * eye).reshape(depth, 2, 2, half * B_KV, B_GROUPS * B_CMP_HIDDEN)
    w2 = cmp_w2.reshape(depth, 2, 1, B_CMP_HIDDEN, 1, B_HEAD_DIM)
    eye2 = jnp.eye(B_GROUPS, dtype=cmp_w2.dtype).reshape(1, 1, B_GROUPS, 1, B_GROUPS, 1)
    w2 = (w2 * eye2).reshape(depth, 2, B_GROUPS * B_CMP_HIDDEN, B_KV)
    return pos.astype(F32), w1.astype(BF16), w2.astype(BF16)


def _overlap_t(seq):
    n_cmp_pad = seq // B_CMP_STRIDE
    start = np.arange(n_cmp_pad) * B_CMP_STRIDE
    sel = np.arange(seq // B_SEL_BLOCK) * B_SEL_BLOCK
    ov = (start[None, :] < sel[:, None] + B_SEL_BLOCK) & (start[None, :] + B_CMP_LEN > sel[:, None])
    ov &= (np.arange(n_cmp_pad) < n_cmp_pad - 1)[None, :]
    return jnp.asarray(ov.astype(np.float32), dtype=BF16)


def kernel(x, p, norm_g, ffn1_wi, ffn1_wo, w_in, diff_lambda, diff_subln, nsa_cmp_pos, nsa_cmp_w1,
           nsa_cmp_w2, swa_sinks, w_branch, w_out, ffn2_wi, ffn2_wo, w_ple, w_ple_gate, rel_bias, final_norm):
    bsz, seq, _ = x.shape
    depth = norm_g.shape[0]
    tokens = bsz * seq
    assert seq % 1024 == 0 and seq // B_SEL_BLOCK <= LANES
    tm, tm_tail, tq_a, tq = 512, 256, 256, 128
    assert FAR_DIST + B_CMP_LEN - 1 <= CMP_BAND_BACK * B_CMP_STRIDE and FAR_DIST <= tq
    assert WIN_PAD * tq == B_WINDOW and tq == C_WINDOW and seq // B_SEL_BLOCK < LANES

    wi1, wo1 = ffn1_wi.astype(BF16), ffn1_wo.astype(BF16)
    wi2, wo2 = ffn2_wi.astype(BF16), ffn2_wo.astype(BF16)
    w_in_r = _prep_w_in(w_in)
    cmp_pos, cmp_w1, cmp_w2 = _prep_compress(nsa_cmp_pos, nsa_cmp_w1, nsa_cmp_w2)
    wb, wo = w_branch.astype(BF16), w_out.astype(BF16)
    wpg, wpe = w_ple_gate.astype(BF16), w_ple.astype(BF16)
    tbl = rel_bias.astype(F32)
    ovt = _overlap_t(seq)
    chunk_rows = seq // B_CMP_STRIDE

    h = x.reshape(tokens, D_MODEL)
    for i in range(depth):
        h = _ffn(h, norm_g[i, 0], wi1[i], wo1[i], tm)
        za, zb, zkc, zvc, zc, gates, bgate = _inproj(h, norm_g[i, 1], w_in_r[i], tm)
        lam_init = 0.8 - 0.6 * math.exp(-0.3 * i)
        ya = _diff_attention(za, tbl, diff_lambda[i].astype(F32), diff_subln[i], bsz, seq, lam_init, tq_a, 4)
        kc, vc = _compress(zkc.reshape(bsz, chunk_rows, B_CMP_STRIDE * B_KV),
                           zvc.reshape(bsz, chunk_rows, B_CMP_STRIDE * B_KV), cmp_pos[i], cmp_w1[i], cmp_w2[i])
        yb, yc = _nsa_swa_attention(zb, kc, vc, bgate, zc, tbl, swa_sinks[i].astype(F32), ovt, bsz, seq, tq)
        h = _tail(h, ya, yb, yc, gates, p[i].reshape(tokens, PLE_DIM), wb[i], wo[i], norm_g[i, 2], wi2[i], wo2[i],
                  norm_g[i, 3], wpg[i], wpe[i], final_norm, tm_tail, final=(i == depth - 1))
    return h.reshape(bsz, seq, D_MODEL)
```

```python
import functools
import math

import numpy as np
import jax
import jax.numpy as jnp
from jax import lax
from jax.experimental import pallas as pl
from jax.experimental.pallas import tpu as pltpu

F32 = jnp.float32
BF16 = jnp.bfloat16

D_MODEL = 1024
PLE_DIM = 256
D_FF = 2816
NORM_EPS = 1e-6
NEG = -1e30
NUM_BUCKETS = 32
MAX_DISTANCE = 128
A_HEADS = 4
A_HEAD_DIM = 64
B_HEADS = 8
B_GROUPS = 2
B_REP = B_HEADS // B_GROUPS
B_HEAD_DIM = 64
B_CMP_LEN = 32
B_CMP_STRIDE = 16
B_CMP_HIDDEN = 256
B_SEL_BLOCK = 64
B_SEL_TOPK = 16
B_WINDOW = 512
B_SEL_FORCE = 1e6
C_HEADS = 8
C_GROUPS = 2
C_REP = C_HEADS // C_GROUPS
C_HEAD_DIM = 64
C_WINDOW = 128
A_WIDTH = A_HEADS * 2 * A_HEAD_DIM
B_WIDTH = B_HEADS * B_HEAD_DIM
B_KV = B_GROUPS * B_HEAD_DIM
C_WIDTH = C_HEADS * C_HEAD_DIM
C_KV = C_GROUPS * C_HEAD_DIM
N_BIAS_HEADS = A_HEADS + B_HEADS + C_HEADS

LANES = 128
HEAD_DIM = 64
VMEM_LIMIT = 56 * 1024 * 1024
UNSEL = -float(2 ** 30)
TAKEN = -3e38
FFN_CHUNKS = ((0, 1536), (1536, 1280))
CMP_BAND_BACK = 16
LOG2E = math.log2(math.e)
ONES_ROWS = 16


def _bucket_thresholds():
    n = np.arange(4 * MAX_DISTANCE)
    max_exact = NUM_BUCKETS // 2
    nf = np.maximum(n, 1).astype(np.float32)
    large = max_exact + (np.log(nf / max_exact) / math.log(MAX_DISTANCE / max_exact)
                         * (NUM_BUCKETS - max_exact)).astype(np.int32)
    bucket = np.where(n < max_exact, n, np.minimum(large, NUM_BUCKETS - 1))
    out = []
    for b in range(1, NUM_BUCKETS):
        hit = np.nonzero(bucket == b)[0]
        if hit.size:
            out.append((b, int(hit[0])))
    return tuple(out)


BUCKET_LO = _bucket_thresholds()
FAR_DIST = BUCKET_LO[-1][1]


def _bias_of_dist(dist, tbl_ref, head, minus_far=False):
    shift = tbl_ref[NUM_BUCKETS - 1, head] if minus_far else 0.0
    out = jnp.full(dist.shape, (tbl_ref[0, head] - shift) * LOG2E, F32)
    for b, lo in BUCKET_LO:
        out = jnp.where(dist >= lo, (tbl_ref[b, head] - shift) * LOG2E, out)
    return out


def _rms(x, g):
    return x * lax.rsqrt(jnp.mean(x * x, axis=-1, keepdims=True) + NORM_EPS) * g


def _dot(a, b):
    return jnp.dot(a, b, preferred_element_type=F32)


def _online_update(carry, s, vts):
    m, acc = carry
    m_new = jnp.maximum(m, jnp.max(s, axis=0, keepdims=True))
    p = jnp.exp2(s - m_new).astype(BF16)
    values = jnp.concatenate([jnp.concatenate(vts, axis=1), jnp.ones((ONES_ROWS, s.shape[0]), BF16)], axis=0)
    return m_new, jnp.exp2(m - m_new) * acc + _dot(values, p)


def _softmax_init(width, queries):
    return jnp.full((1, queries), NEG, F32), jnp.zeros((width + ONES_ROWS, queries), F32)


def _softmax_result(carry):
    _, acc = carry
    width = acc.shape[0] - ONES_ROWS
    return acc[0:width] / acc[width:width + 1]


def _transpose_bf16(x):
    return x.astype(F32).T.astype(BF16)


def _params(*sem):
    return pltpu.CompilerParams(dimension_semantics=sem, vmem_limit_bytes=VMEM_LIMIT)


def _resident(shape, layer=None):
    if layer is None:
        zeros = (0,) * len(shape)
        return pl.BlockSpec(shape, lambda *_: zeros, pipeline_mode=pl.Buffered(1))
    index = (layer,) + (0,) * (len(shape) - 1)
    return pl.BlockSpec((None,) + tuple(shape[1:]), lambda *_: index, pipeline_mode=pl.Buffered(1))


def _smem():
    return pl.BlockSpec(memory_space=pltpu.SMEM)


def _half_ffn(x, g_ref, wi_ref, wo_ref):
    n = _rms(x, g_ref[...]).astype(BF16)
    acc = x
    for lo, width in FFN_CHUNKS:
        gate = _dot(n, wi_ref[:, lo:lo + width])
        up = _dot(n, wi_ref[:, D_FF + lo:D_FF + lo + width])
        act = (gate * jax.nn.sigmoid(gate) * up).astype(BF16)
        acc = acc + 0.5 * _dot(act, wo_ref[lo:lo + width, :])
    return acc


def _ffn_kernel(h_ref, g_ref, wi_ref, wo_ref, o_ref):
    o_ref[...] = _half_ffn(h_ref[...], g_ref, wi_ref, wo_ref)


def _ffn(h, g, wi, wo, tm, layer):
    t = h.shape[0]
    tile = pl.BlockSpec((tm, D_MODEL), lambda i: (i, 0))
    return pl.pallas_call(
        _ffn_kernel,
        name="ffn",
        grid=(t // tm,),
        in_specs=[tile, _resident((1, D_MODEL)), _resident(wi.shape, layer), _resident(wo.shape, layer)],
        out_specs=tile,
        out_shape=jax.ShapeDtypeStruct(h.shape, F32),
        compiler_params=_params("arbitrary"),
    )(h, g.reshape(1, D_MODEL), wi, wo)


IN_A = 3 * A_WIDTH
IN_B = B_WIDTH + 4 * B_KV
IN_C = C_WIDTH + 2 * C_KV
IN_G = 3 * D_MODEL
IN_SECTIONS = (IN_A, IN_B, B_KV, B_KV, IN_C, IN_G, LANES)


def _inproj_kernel(h_ref, g_ref, w_ref, za_ref, zb_ref, zkc_ref, zvc_ref, zc_ref, gates_ref, bg_ref):
    n = _rms(h_ref[...], g_ref[...]).astype(BF16)
    lo = 0
    for ref, width in zip((za_ref, zb_ref, zkc_ref, zvc_ref, zc_ref, gates_ref, bg_ref), IN_SECTIONS):
        ref[...] = _dot(n, w_ref[:, lo:lo + width]).astype(ref.dtype)
        lo += width


def _inproj(h, g, w, tm, layer):
    t = h.shape[0]
    dtypes = (BF16, BF16, BF16, BF16, BF16, F32, F32)
    return pl.pallas_call(
        _inproj_kernel,
        name="inproj",
        grid=(t // tm,),
        in_specs=[pl.BlockSpec((tm, D_MODEL), lambda i: (i, 0)), _resident((1, D_MODEL)),
                  _resident(w.shape, layer)],
        out_specs=[pl.BlockSpec((tm, width), lambda i: (i, 0)) for width in IN_SECTIONS],
        out_shape=[jax.ShapeDtypeStruct((t, width), dt) for width, dt in zip(IN_SECTIONS, dtypes)],
        compiler_params=_params("arbitrary"),
    )(h, g.reshape(1, D_MODEL), w)


def _tail_kernel(h_ref, ya_ref, yb_ref, yc_ref, gates_ref, p_ref, wb_ref, wo_ref, g2_ref, wi_ref, wo2_ref,
                 g3_ref, wg_ref, we_ref, gf_ref, o_ref, *, final):
    merged = jnp.zeros(h_ref.shape, F32)
    for m, y_ref in enumerate((ya_ref, yb_ref, yc_ref)):
        gate = jax.nn.sigmoid(gates_ref[:, m * D_MODEL:(m + 1) * D_MODEL])
        merged = merged + gate * _dot(y_ref[...], wb_ref[m])
    x = h_ref[...] + _dot(merged.astype(BF16), wo_ref[...])
    x = _half_ffn(x, g2_ref, wi_ref, wo2_ref)
    gate = jax.nn.sigmoid(_dot(_rms(x, g3_ref[...]).astype(BF16), wg_ref[...]))
    x = x + gate * _dot(p_ref[...].astype(BF16), we_ref[...])
    if final:
        x = _rms(x, gf_ref[...])
    o_ref[...] = x


def _tail(h, ya, yb, yc, gates, p, wb, wo, g2, wi, wo2, g3, wg, we, gf, tm, layer, final):
    t = h.shape[0]
    rows = lambda width: pl.BlockSpec((tm, width), lambda i: (i, 0))
    vec = _resident((1, D_MODEL))
    return pl.pallas_call(
        functools.partial(_tail_kernel, final=final),
        name="tail",
        grid=(t // tm,),
        in_specs=[rows(D_MODEL), rows(A_WIDTH), rows(B_WIDTH), rows(C_WIDTH), rows(IN_G),
                  pl.BlockSpec((None, tm, PLE_DIM), lambda i: (layer, i, 0)),
                  _resident(wb.shape, layer), _resident(wo.shape, layer), vec,
                  _resident(wi.shape, layer), _resident(wo2.shape, layer),
                  vec, _resident(wg.shape, layer), _resident(we.shape, layer), vec],
        out_specs=rows(D_MODEL),
        out_shape=jax.ShapeDtypeStruct(h.shape, F32),
        compiler_params=_params("arbitrary"),
    )(h, ya, yb, yc, gates, p, wb, wo, g2.reshape(1, D_MODEL), wi, wo2, g3.reshape(1, D_MODEL), wg, we,
      gf.reshape(1, D_MODEL))


def _diff_kernel(tbl_ref, lam_ref, subg_ref, q_ref, k_ref, v_ref, o_ref, kaug_ref, vt_ref, near_ref,
                 *, tq, lam_init):
    qi = pl.program_id(1)
    width = 2 * A_HEAD_DIM
    n_heads = A_HEADS

    @pl.when((pl.program_id(0) == 0) & (qi == 0))
    def _static_setup():
        lane = lax.broadcasted_iota(jnp.int32, (tq, 2 * LANES), 1)
        key = lax.broadcasted_iota(jnp.int32, (2 * tq, tq), 0)
        qry = lax.broadcasted_iota(jnp.int32, (2 * tq, tq), 1)
        dist = tq + qry - key
        for h in range(n_heads):
            kaug_ref[h, 0:tq, :] = jnp.where(lane == LANES, 1.0, 0.0).astype(BF16)
            kaug_ref[h, tq:, LANES:2 * LANES] = jnp.zeros((kaug_ref.shape[1] - tq, LANES), BF16)
            vt_ref[h, 0] = jnp.zeros(vt_ref.shape[2:], BF16)
            near_ref[h] = jnp.where(dist >= 0, _bias_of_dist(dist, tbl_ref, h, True), NEG)

    @pl.when(qi == 0)
    def _per_batch_setup():
        for h in range(n_heads):
            cols = slice(h * width, (h + 1) * width)
            kaug_ref[h, tq:, 0:LANES] = k_ref[:, cols]
            for c in range(vt_ref.shape[1] - 1):
                vt_ref[h, c + 1] = _transpose_bf16(v_ref[c * tq:(c + 1) * tq, cols])

    zero = jnp.zeros((A_HEAD_DIM, tq), F32)
    flag_row = lax.broadcasted_iota(jnp.int32, (LANES, 2 * tq), 0) == 0
    flags = jnp.where(flag_row, UNSEL, 0.0)
    queries = []
    for h in range(n_heads):
        qt = q_ref[:, h * width:(h + 1) * width].astype(F32).T
        both = jnp.concatenate([jnp.concatenate([qt[0:A_HEAD_DIM], zero], axis=0),
                                jnp.concatenate([zero, qt[A_HEAD_DIM:width]], axis=0)], axis=1)
        queries.append(jnp.concatenate([both, flags], axis=0).astype(BF16))

    def logits_of(first_tile):
        start = pl.multiple_of(first_tile * tq, tq)
        return tuple(_dot(kaug_ref[h, pl.ds(start, 2 * tq), :], queries[h]) for h in range(n_heads))

    def update(first_tile, logits, carries):
        return tuple(_online_update(carries[h], logits[h], [vt_ref[h, first_tile], vt_ref[h, first_tile + 1]])
                     for h in range(n_heads))

    def far(gi, carries, count):
        tiles = [qi - 2 * (gi + c) for c in range(count)]
        logits = [logits_of(t) for t in tiles]
        for t, s in zip(tiles, logits):
            carries = update(t, s, carries)
        return carries

    nearest = tuple(s + jnp.concatenate([near_ref[h], near_ref[h]], axis=1) for h, s in enumerate(logits_of(qi)))
    carries = update(qi, nearest, tuple(_softmax_init(width, 2 * tq) for _ in range(n_heads)))
    n_far = qi // 2
    carries = lax.fori_loop(0, n_far // 2, lambda p, c: far(1 + 2 * p, c, 2), carries)
    carries = lax.fori_loop(0, n_far % 2, lambda _, c: far(n_far, c, 1), carries)
    lp = lam_ref[...]
    lam = (jnp.exp(jnp.sum(lp[0:1] * lp[1:2], axis=-1, keepdims=True))
           - jnp.exp(jnp.sum(lp[2:3] * lp[3:4], axis=-1, keepdims=True)) + lam_init)
    for h in range(n_heads):
        out = _softmax_result(carries[h])
        o = out[:, 0:tq] - lam * out[:, tq:2 * tq]
        o = o * lax.rsqrt(jnp.mean(o * o, axis=0, keepdims=True) + NORM_EPS) * (1.0 - lam_init)
        o_ref[:, h * width:(h + 1) * width] = (o.T * subg_ref[...]).astype(o_ref.dtype)


def _diff_attention(za, tbl, lam_p, subg, bsz, seq, lam_init, tq):
    nq = seq // tq
    width = 2 * A_HEAD_DIM
    return pl.pallas_call(
        functools.partial(_diff_kernel, tq=tq, lam_init=lam_init),
        name="diff_attn",
        grid=(bsz, nq),
        in_specs=[_smem(), _resident(lam_p.shape), _resident((1, width)),
                  pl.BlockSpec((tq, A_WIDTH), lambda b, i: (b * nq + i, 0)),
                  pl.BlockSpec((seq, A_WIDTH), lambda b, i: (b, 1)),
                  pl.BlockSpec((seq, A_WIDTH), lambda b, i: (b, 2))],
        out_specs=pl.BlockSpec((tq, A_WIDTH), lambda b, i: (b * nq + i, 0)),
        out_shape=jax.ShapeDtypeStruct((bsz * seq, A_WIDTH), BF16),
        scratch_shapes=[pltpu.VMEM((A_HEADS, tq + seq, 2 * LANES), BF16),
                        pltpu.VMEM((A_HEADS, nq + 1, width, tq), BF16),
                        pltpu.VMEM((A_HEADS, 2 * tq, tq), F32)],
        compiler_params=_params("arbitrary", "arbitrary"),
    )(tbl, lam_p, subg.reshape(1, width), za, za, za)


def _compress_kernel(xk_ref, xv_ref, pos_ref, w1_ref, w2_ref, kc_ref, vc_ref):
    rows = xk_ref.shape[0]
    for t, (x_ref, o_ref) in enumerate(((xk_ref, kc_ref), (xv_ref, vc_ref))):
        x = x_ref[...].astype(F32)
        first = _dot((x + pos_ref[t, 0]).astype(BF16), w1_ref[t, 0])
        second = _dot((x + pos_ref[t, 1]).astype(BF16), w1_ref[t, 1])
        hidden = first + pltpu.roll(second, rows - 1, 0)
        o_ref[...] = _dot(jax.nn.gelu(hidden).astype(BF16), w2_ref[t]).astype(o_ref.dtype)


def _compress(xk, xv, pos, w1, w2, layer):
    bsz, rows, width = xk.shape
    xspec = pl.BlockSpec((None, rows, width), lambda b: (b, 0, 0))
    ospec = pl.BlockSpec((None, rows, B_KV), lambda b: (b, 0, 0))
    return pl.pallas_call(
        _compress_kernel,
        name="nsa_compress",
        grid=(bsz,),
        in_specs=[xspec, xspec, _resident(pos.shape, layer), _resident(w1.shape, layer), _resident(w2.shape, layer)],
        out_specs=[ospec, ospec],
        out_shape=[jax.ShapeDtypeStruct((bsz, rows, B_KV), BF16)] * 2,
        compiler_params=_params("arbitrary"),
    )(xk, xv, pos, w1, w2)


def _group_queries_t(q, group, rep):
    tq = q.shape[0]
    zero = jnp.zeros((HEAD_DIM, tq), F32)
    cols = []
    for r in range(rep):
        head = group * rep + r
        slab = q[:, (head // 2) * LANES:(head // 2 + 1) * LANES].astype(F32).T
        part = slab[(head % 2) * HEAD_DIM:(head % 2 + 1) * HEAD_DIM]
        cols.append(jnp.concatenate([part, zero] if group == 0 else [zero, part], axis=0))
    return jnp.concatenate(cols, axis=1).astype(BF16)


def _build_near_bias(near_ref, tbl_ref, head0, groups, rep, tq, window_is_tile, minus_far):
    key = lax.broadcasted_iota(jnp.int32, (2 * tq, tq), 0)
    qry = lax.broadcasted_iota(jnp.int32, (2 * tq, tq), 1)
    dist = tq + qry - key
    visible = (dist >= 0) & (dist < tq) if window_is_tile else dist >= 0
    for g in range(groups):
        for r in range(rep):
            head = head0 + g * rep + r
            near_ref[g, :, r * tq:(r + 1) * tq] = jnp.where(visible, _bias_of_dist(dist, tbl_ref, head, minus_far), NEG)


def _swa_tile(q, k_prev, k_diag, v_prev, v_diag, bias_ref, sink_ref, qi, tq):
    cols = C_REP * tq
    prev_bias = jnp.where(qi > 0, 0.0, NEG)
    vp_t = _transpose_bf16(v_prev)
    vd_t = _transpose_bf16(v_diag)
    heads_out = []
    for g in range(C_GROUPS):
        half = slice(g * HEAD_DIM, (g + 1) * HEAD_DIM)
        qg = _group_queries_t(q, g, C_REP)
        s = jnp.concatenate([_dot(k_prev, qg) + bias_ref[g, 0:tq, :] + prev_bias,
                             _dot(k_diag, qg) + bias_ref[g, tq:2 * tq, :]], axis=0)
        m, acc = _online_update(_softmax_init(HEAD_DIM, cols), s, [vp_t[half], vd_t[half]])
        for r in range(C_REP):
            cc = slice(r * tq, (r + 1) * tq)
            sink = sink_ref[g * C_REP + r] * LOG2E
            m_all = jnp.maximum(m[:, cc], sink)
            scale = jnp.exp2(m[:, cc] - m_all)
            denom = acc[HEAD_DIM:HEAD_DIM + 1, cc] * scale + jnp.exp2(sink - m_all)
            heads_out.append(acc[0:HEAD_DIM, cc] * (scale / denom))
    return jnp.concatenate(heads_out, axis=0).T


SEL_GROUP = 4
SEL_PAD = SEL_GROUP - 1
WIN_PAD = 4


def _nsa_kernel(tbl_ref, sink_ref, q_ref, ks_ref, vs_ref, kw_ref, vw_ref, kc_ref, vc_ref, bg_ref, ovt_ref,
                cq_ref, ckp_ref, ckd_ref, cvp_ref, cvd_ref, o_ref, oc_ref,
                kaug_ref, kwp_ref, vst_ref, vwt_ref, vct_ref, near_ref, edge_ref, band_ref, sc_ref, swab_ref,
                *, tq, seq):
    first_batch = pl.program_id(0) == 0
    qi = pl.program_id(1)
    n_cmp_pad = seq // B_CMP_STRIDE
    n_sel = seq // B_SEL_BLOCK
    top_k = min(B_SEL_TOPK, n_sel)
    nq = seq // tq
    cols = B_REP * tq
    cmp_per_tile = tq // B_CMP_STRIDE
    band = CMP_BAND_BACK + cmp_per_tile

    @pl.when(first_batch & (qi == 0))
    def _static_setup():
        _build_near_bias(near_ref, tbl_ref, A_HEADS, B_GROUPS, B_REP, tq, False, True)
        _build_near_bias(swab_ref, tbl_ref, A_HEADS + B_HEADS, C_GROUPS, C_REP, tq, True, False)
        key = lax.broadcasted_iota(jnp.int32, (tq, cols), 0)
        qry = lax.broadcasted_iota(jnp.int32, (tq, cols), 1) % tq
        edge_ref[...] = jnp.where(qry < key, 0.0, NEG)
        blk = lax.broadcasted_iota(jnp.int32, (band, tq), 0) - CMP_BAND_BACK
        dist = lax.broadcasted_iota(jnp.int32, (band, tq), 1) - (blk * B_CMP_STRIDE + B_CMP_LEN - 1)
        for g in range(B_GROUPS):
            for r in range(B_REP):
                head = A_HEADS + g * B_REP + r
                band_ref[g, :, r * tq:(r + 1) * tq] = jnp.where(dist >= 0, _bias_of_dist(dist, tbl_ref, head, True), NEG)
        sc_ref[:, 0:CMP_BAND_BACK, :] = jnp.zeros((B_GROUPS, CMP_BAND_BACK, cols), F32)
        pad = SEL_PAD * tq
        row = lax.broadcasted_iota(jnp.int32, (pad + seq, LANES), 0)
        lane = lax.broadcasted_iota(jnp.int32, (pad + seq, LANES), 1)
        blk_id = jnp.where(row < pad, LANES - 1, (row - pad) // B_SEL_BLOCK)
        kaug_ref[:, LANES:2 * LANES] = jnp.where(blk_id == lane, 1.0, 0.0).astype(BF16)
        kaug_ref[0:pad, 0:LANES] = jnp.zeros((pad, LANES), BF16)
        kwp_ref[0:WIN_PAD * tq, :] = jnp.zeros((WIN_PAD * tq, LANES), BF16)
        for c in range(SEL_PAD):
            vst_ref[c] = jnp.zeros(vst_ref.shape[1:], BF16)
        for c in range(WIN_PAD):
            vwt_ref[c] = jnp.zeros(vwt_ref.shape[1:], BF16)

    @pl.when(qi == 0)
    def _per_batch_setup():
        kaug_ref[SEL_PAD * tq:, 0:LANES] = ks_ref[...]
        kwp_ref[WIN_PAD * tq:, :] = kw_ref[...]
        for c in range(nq):
            vst_ref[SEL_PAD + c] = _transpose_bf16(vs_ref[c * tq:(c + 1) * tq, :])
            vwt_ref[WIN_PAD + c] = _transpose_bf16(vw_ref[c * tq:(c + 1) * tq, :])
        for c in range(n_cmp_pad // LANES):
            vct_ref[:, c * LANES:(c + 1) * LANES] = _transpose_bf16(vc_ref[c * LANES:(c + 1) * LANES, :])

    q = q_ref[...]
    gates_t = jax.nn.sigmoid(bg_ref[...]).T
    first_cmp = qi * cmp_per_tile
    cmp_row = lax.broadcasted_iota(jnp.int32, (n_cmp_pad, cols), 0)
    cmp_visible = cmp_row < first_cmp + cmp_per_tile
    blk = lax.broadcasted_iota(jnp.int32, (n_sel, tq), 0)
    cur = (qi * tq + lax.broadcasted_iota(jnp.int32, (n_sel, tq), 1)) // B_SEL_BLOCK
    forced = (blk == 0) | (blk == cur) | (blk == cur - 1)
    future = blk > cur
    halves = [slice(g * HEAD_DIM, (g + 1) * HEAD_DIM) for g in range(B_GROUPS)]

    qgs, o_cs, imps = [], [], []
    for g in range(B_GROUPS):
        qg = _group_queries_t(q, g, B_REP)
        sc_ref[g, CMP_BAND_BACK:, :] = _dot(kc_ref[...], qg)
        band_rows = pl.ds(pl.multiple_of(first_cmp, 8), band)
        sc_ref[g, band_rows, :] = sc_ref[g, band_rows, :] + band_ref[g]
        s = jnp.where(cmp_visible, sc_ref[g, CMP_BAND_BACK:, :], NEG)
        e = jnp.where(s > 0.5 * NEG, jnp.exp2(s - jnp.max(s, axis=0, keepdims=True)), 0.0)
        l = jnp.sum(e, axis=0, keepdims=True)
        p_c = e * jnp.where(l > 0.0, 1.0 / jnp.where(l > 0.0, l, 1.0), 0.0)
        o_cs.append(_dot(vct_ref[halves[g], :], p_c.astype(BF16)))
        p_sum = p_c[:, 0:tq]
        for r in range(1, B_REP):
            p_sum = p_sum + p_c[:, r * tq:(r + 1) * tq]
        imp = _dot(ovt_ref[...], p_sum.astype(BF16))
        imp = jnp.where(forced, B_SEL_FORCE, jnp.where(future, -B_SEL_FORCE, imp))
        qgs.append(qg)
        imps.append(imp)

    blk_f = blk.astype(F32)
    taken = [jnp.zeros((n_sel, tq), jnp.bool_) for _ in range(B_GROUPS)]
    left = list(imps)
    for _ in range(top_k):
        for g in range(B_GROUPS):
            best = jnp.max(left[g], axis=0, keepdims=True)
            first = jnp.min(jnp.where(left[g] == best, blk_f, float(n_sel)), axis=0, keepdims=True)
            pick = blk_f == first
            taken[g] = taken[g] | pick
            left[g] = jnp.where(pick, TAKEN, left[g])
    q_augs = []
    for g in range(B_GROUPS):
        sel_bias = jnp.concatenate([jnp.where(taken[g], 0.0, UNSEL),
                                    jnp.full((LANES - n_sel, tq), UNSEL, F32)], axis=0).astype(BF16)
        q_augs.append(jnp.concatenate([qgs[g], jnp.concatenate([sel_bias] * B_REP, axis=1)], axis=0))

    pair = cols
    chains = [(g, slice(0, cols)) for g in range(B_GROUPS)]
    heads_per_chain = pair // tq

    def sel_logits(first_tile):
        start = pl.multiple_of(first_tile * tq, tq)
        keys = kaug_ref[pl.ds(start, SEL_GROUP * tq), :]
        return tuple(_dot(keys, q_augs[g][:, cc]) for g, cc in chains)

    def sel_update(first_tile, logits, carries):
        return tuple(_online_update(carry, s, [vst_ref[first_tile + t, halves[g], :] for t in range(SEL_GROUP)])
                     for (g, cc), s, carry in zip(chains, logits, carries))

    start = pl.multiple_of(qi * tq, tq)
    win_keys = kwp_ref[pl.ds(start, (WIN_PAD + 1) * tq), :]
    partial = []
    for g, cc in chains:
        s_w = _dot(win_keys, qgs[g][:, cc])
        slabs = []
        for t in range(WIN_PAD + 1):
            slab = s_w[t * tq:(t + 1) * tq]
            if t == 0:
                slab = slab + edge_ref[:, cc]
            if t >= WIN_PAD - 1:
                slab = slab + near_ref[g, (t - WIN_PAD + 1) * tq:(t - WIN_PAD + 2) * tq, cc]
            if t < WIN_PAD:
                slab = slab + jnp.where(qi >= WIN_PAD - t, 0.0, NEG)
            slabs.append(slab)
        o_w = _softmax_result(_online_update(_softmax_init(HEAD_DIM, pair), jnp.concatenate(slabs, axis=0),
                                             [vwt_ref[qi + t, halves[g], :] for t in range(WIN_PAD + 1)]))
        o_c = o_cs[g][:, cc]
        heads = []
        for rr in range(heads_per_chain):
            hc = slice(rr * tq, (rr + 1) * tq)
            c0 = (g * B_REP + rr) * 3
            heads.append(gates_t[c0:c0 + 1] * o_c[:, hc] + gates_t[c0 + 2:c0 + 3] * o_w[:, hc])
        partial.append(heads)

    oc_ref[...] = _swa_tile(cq_ref[...], ckp_ref[...], ckd_ref[...], cvp_ref[...], cvd_ref[...],
                            swab_ref, sink_ref, qi, tq).astype(oc_ref.dtype)

    def sel_far(gi, carries, count):
        tiles = [qi - SEL_GROUP * (gi + c) for c in range(count)]
        logits = [sel_logits(t) for t in tiles]
        for t, s in zip(tiles, logits):
            carries = sel_update(t, s, carries)
        return carries

    far_rows = (SEL_GROUP - 2) * tq
    nearest = tuple(jnp.concatenate([s[0:far_rows], s[far_rows:] + near_ref[g, :, cc]], axis=0)
                    for (g, cc), s in zip(chains, sel_logits(qi)))
    sel = sel_update(qi, nearest, tuple(_softmax_init(HEAD_DIM, pair) for _ in chains))
    n_far = qi // SEL_GROUP
    sel = lax.fori_loop(0, n_far // 2, lambda p, c: sel_far(1 + 2 * p, c, 2), sel)
    sel = lax.fori_loop(0, n_far % 2, lambda _, c: sel_far(n_far, c, 1), sel)

    heads_out = []
    for ci, (g, cc) in enumerate(chains):
        o_s = _softmax_result(sel[ci])
        for rr in range(heads_per_chain):
            hc = slice(rr * tq, (rr + 1) * tq)
            c1 = (g * B_REP + rr) * 3 + 1
            heads_out.append(partial[ci][rr] + gates_t[c1:c1 + 1] * o_s[:, hc])
    o_ref[...] = jnp.concatenate(heads_out, axis=0).T.astype(o_ref.dtype)


def _nsa_swa_attention(zb, kc, vc, bgate, zc, tbl, sinks, ovt, bsz, seq, tq):
    nq = seq // tq
    n_cmp_pad = seq // B_CMP_STRIDE
    cols = B_REP * tq
    kv = lambda col: pl.BlockSpec((seq, LANES), lambda b, i: (b, col))
    cmp_spec = pl.BlockSpec((None, n_cmp_pad, B_KV), lambda b, i: (b, 0, 0))
    qcols = B_WIDTH // LANES
    ccols = C_WIDTH // LANES
    wide = lambda width: pl.BlockSpec((tq, width), lambda b, i: (b * nq + i, 0))
    prev = lambda col: pl.BlockSpec((tq, LANES), lambda b, i: (b * nq + jnp.maximum(i - 1, 0), col))
    diag = lambda col: pl.BlockSpec((tq, LANES), lambda b, i: (b * nq + i, col))
    return pl.pallas_call(
        functools.partial(_nsa_kernel, tq=tq, seq=seq),
        name="nsa_attn",
        grid=(bsz, nq),
        in_specs=[_smem(), _smem(),
                  wide(B_WIDTH),
                  kv(qcols), kv(qcols + 1), kv(qcols + 2), kv(qcols + 3),
                  cmp_spec, cmp_spec,
                  wide(LANES),
                  _resident(ovt.shape),
                  wide(C_WIDTH), prev(ccols), diag(ccols), prev(ccols + 1), diag(ccols + 1)],
        out_specs=[wide(B_WIDTH), wide(C_WIDTH)],
        out_shape=[jax.ShapeDtypeStruct((bsz * seq, B_WIDTH), BF16),
                   jax.ShapeDtypeStruct((bsz * seq, C_WIDTH), BF16)],
        scratch_shapes=[pltpu.VMEM((SEL_PAD * tq + seq, 2 * LANES), BF16),
                        pltpu.VMEM((WIN_PAD * tq + seq, LANES), BF16),
                        pltpu.VMEM((SEL_PAD + nq, B_KV, tq), BF16),
                        pltpu.VMEM((WIN_PAD + nq, B_KV, tq), BF16),
                        pltpu.VMEM((B_KV, n_cmp_pad), BF16),
                        pltpu.VMEM((B_GROUPS, 2 * tq, cols), F32),
                        pltpu.VMEM((tq, cols), F32),
                        pltpu.VMEM((B_GROUPS, CMP_BAND_BACK + tq // B_CMP_STRIDE, cols), F32),
                        pltpu.VMEM((B_GROUPS, CMP_BAND_BACK + n_cmp_pad, cols), F32),
                        pltpu.VMEM((C_GROUPS, 2 * tq, C_REP * tq), F32)],
        compiler_params=_params("arbitrary", "arbitrary"),
    )(tbl, sinks, zb, zb, zb, zb, zb, kc, vc, bgate, ovt, zc, zc, zc, zc, zc)


def _prep_w_in(w_in):
    sizes = (A_WIDTH, A_WIDTH, A_WIDTH, B_WIDTH, B_KV, B_KV, B_KV, B_KV, B_KV, B_KV, 3 * B_HEADS,
             C_WIDTH, C_KV, C_KV, D_MODEL, D_MODEL, D_MODEL)
    offs = np.concatenate([[0], np.cumsum(sizes)])
    (aq, ak, av, bq, bkc, bvc, bks, bvs, bkw, bvw, bgate, cq, ck, cv, ga, gb, gc) = [
        w_in[..., offs[i]:offs[i + 1]] for i in range(len(sizes))]
    scale = HEAD_DIM ** -0.5 * LOG2E
    bgate = jnp.pad(bgate, ((0, 0), (0, 0), (0, LANES - 3 * B_HEADS)))
    cols = [aq * scale, ak, av, bq * scale, bks, bvs, bkw, bvw, bkc, bvc, cq * scale, ck, cv, ga, gb, gc, bgate]
    return jnp.concatenate([c.astype(BF16) for c in cols], axis=-1)


def _prep_compress(cmp_pos, cmp_w1, cmp_w2):
    depth = cmp_pos.shape[0]
    half = B_CMP_STRIDE
    pos = cmp_pos.reshape(depth, 2, 2, half, 1, B_HEAD_DIM)
    pos = jnp.broadcast_to(pos, (depth, 2, 2, half, B_GROUPS, B_HEAD_DIM)).reshape(depth, 2, 2, 1, half * B_KV)
    w1 = cmp_w1.reshape(depth, 2, 2, half, 1, B_HEAD_DIM, 1, B_CMP_HIDDEN)
    eye = jnp.eye(B_GROUPS, dtype=cmp_w1.dtype).reshape(1, 1, 1, 1, B_GROUPS, 1, B_GROUPS, 1)
    w1 = (w1 * eye).reshape(depth, 2, 2, half * B_KV, B_GROUPS * B_CMP_HIDDEN)
    w2 = cmp_w2.reshape(depth, 2, 1, B_CMP_HIDDEN, 1, B_HEAD_DIM)
    eye2 = jnp.eye(B_GROUPS, dtype=cmp_w2.dtype).reshape(1, 1, B_GROUPS, 1, B_GROUPS, 1)
    w2 = (w2 * eye2).reshape(depth, 2, B_GROUPS * B_CMP_HIDDEN, B_KV)
    return pos.astype(F32), w1.astype(BF16), w2.astype(BF16)


def _overlap_t(seq):
    n_cmp_pad = seq // B_CMP_STRIDE
    start = np.arange(n_cmp_pad) * B_CMP_STRIDE
    sel = np.arange(seq // B_SEL_BLOCK) * B_SEL_BLOCK
    ov = (start[None, :] < sel[:, None] + B_SEL_BLOCK) & (start[None, :] + B_CMP_LEN > sel[:, None])
    ov &= (np.arange(n_cmp_pad) < n_cmp_pad - 1)[None, :]
    return jnp.asarray(ov.astype(np.float32), dtype=BF16)


def kernel(x, p, norm_g, ffn1_wi, ffn1_wo, w_in, diff_lambda, diff_subln, nsa_cmp_pos, nsa_cmp_w1,
           nsa_cmp_w2, swa_sinks, w_branch, w_out, ffn2_wi, ffn2_wo, w_ple, w_ple_gate, rel_bias, final_norm):
    bsz, seq, _ = x.shape
    depth = norm_g.shape[0]
    tokens = bsz * seq
    assert seq % 1024 == 0 and seq // B_SEL_BLOCK <= LANES
    tm, tm_tail, tq_a, tq = 512, 256, 256, 128
    assert FAR_DIST + B_CMP_LEN - 1 <= CMP_BAND_BACK * B_CMP_STRIDE and FAR_DIST <= tq
    assert WIN_PAD * tq == B_WINDOW and tq == C_WINDOW and seq // B_SEL_BLOCK < LANES

    wi1, wo1 = ffn1_wi.astype(BF16), ffn1_wo.astype(BF16)
    wi2, wo2 = ffn2_wi.astype(BF16), ffn2_wo.astype(BF16)
    w_in_r = _prep_w_in(w_in)
    cmp_pos, cmp_w1, cmp_w2 = _prep_compress(nsa_cmp_pos, nsa_cmp_w1, nsa_cmp_w2)
    wb, wo = w_branch.astype(BF16), w_out.astype(BF16)
    wpg, wpe = w_ple_gate.astype(BF16), w_ple.astype(BF16)
    tbl = rel_bias.astype(F32)
    ovt = _overlap_t(seq)
    chunk_rows = seq // B_CMP_STRIDE

    h = x.reshape(tokens, D_MODEL)
    p_rows = p.reshape(depth, tokens, PLE_DIM)
    for i in range(depth):
        h = _ffn(h, norm_g[i, 0], wi1, wo1, tm, i)
        za, zb, zkc, zvc, zc, gates, bgate = _inproj(h, norm_g[i, 1], w_in_r, tm, i)
        lam_init = 0.8 - 0.6 * math.exp(-0.3 * i)
        ya = _diff_attention(za, tbl, diff_lambda[i].astype(F32), diff_subln[i], bsz, seq, lam_init, tq_a)
        kc, vc = _compress(zkc.reshape(bsz, chunk_rows, B_CMP_STRIDE * B_KV),
                           zvc.reshape(bsz, chunk_rows, B_CMP_STRIDE * B_KV), cmp_pos, cmp_w1, cmp_w2, i)
        yb, yc = _nsa_swa_attention(zb, kc, vc, bgate, zc, tbl, swa_sinks[i].astype(F32), ovt, bsz, seq, tq)
        h = _tail(h, ya, yb, yc, gates, p_rows, wb, wo, norm_g[i, 2], wi2, wo2, norm_g[i, 3], wpg, wpe,
                  final_norm, tm_tail, i, final=(i == depth - 1))
    return h.reshape(bsz, seq, D_MODEL)
```

```python
import functools
import math

import numpy as np
import jax
import jax.numpy as jnp
from jax import lax
from jax.experimental import pallas as pl
from jax.experimental.pallas import tpu as pltpu

F32 = jnp.float32
BF16 = jnp.bfloat16

D_MODEL = 1024
PLE_DIM = 256
D_FF = 2816
NORM_EPS = 1e-6
NEG = -1e30
NUM_BUCKETS = 32
MAX_DISTANCE = 128
A_HEADS = 4
A_HEAD_DIM = 64
B_HEADS = 8
B_GROUPS = 2
B_REP = B_HEADS // B_GROUPS
B_HEAD_DIM = 64
B_CMP_LEN = 32
B_CMP_STRIDE = 16
B_CMP_HIDDEN = 256
B_SEL_BLOCK = 64
B_SEL_TOPK = 16
B_WINDOW = 512
B_SEL_FORCE = 1e6
C_HEADS = 8
C_GROUPS = 2
C_REP = C_HEADS // C_GROUPS
C_HEAD_DIM = 64
C_WINDOW = 128
A_WIDTH = A_HEADS * 2 * A_HEAD_DIM
B_WIDTH = B_HEADS * B_HEAD_DIM
B_KV = B_GROUPS * B_HEAD_DIM
C_WIDTH = C_HEADS * C_HEAD_DIM
C_KV = C_GROUPS * C_HEAD_DIM
N_BIAS_HEADS = A_HEADS + B_HEADS + C_HEADS

LANES = 128
HEAD_DIM = 64
VMEM_LIMIT = 56 * 1024 * 1024
UNSEL = -float(2 ** 30)
TAKEN = -3e38
FFN_CHUNKS = ((0, 1536), (1536, 1280))
CMP_BAND_BACK = 16
LOG2E = math.log2(math.e)
ONES_ROWS = 16


def _bucket_thresholds():
    n = np.arange(4 * MAX_DISTANCE)
    max_exact = NUM_BUCKETS // 2
    nf = np.maximum(n, 1).astype(np.float32)
    large = max_exact + (np.log(nf / max_exact) / math.log(MAX_DISTANCE / max_exact)
                         * (NUM_BUCKETS - max_exact)).astype(np.int32)
    bucket = np.where(n < max_exact, n, np.minimum(large, NUM_BUCKETS - 1))
    out = []
    for b in range(1, NUM_BUCKETS):
        hit = np.nonzero(bucket == b)[0]
        if hit.size:
            out.append((b, int(hit[0])))
    return tuple(out)


BUCKET_LO = _bucket_thresholds()
FAR_DIST = BUCKET_LO[-1][1]


def _bias_of_dist(dist, tbl_ref, head, minus_far=False):
    shift = tbl_ref[NUM_BUCKETS - 1, head] if minus_far else 0.0
    out = jnp.full(dist.shape, (tbl_ref[0, head] - shift) * LOG2E, F32)
    for b, lo in BUCKET_LO:
        out = jnp.where(dist >= lo, (tbl_ref[b, head] - shift) * LOG2E, out)
    return out


def _rms(x, g):
    return x * lax.rsqrt(jnp.mean(x * x, axis=-1, keepdims=True) + NORM_EPS) * g


def _dot(a, b):
    return jnp.dot(a, b, preferred_element_type=F32)


def _online_update(carry, s, vts):
    m, acc = carry
    m_new = jnp.maximum(m, jnp.max(s, axis=0, keepdims=True))
    p = jnp.exp2(s - m_new).astype(BF16)
    values = jnp.concatenate([jnp.concatenate(vts, axis=1), jnp.ones((ONES_ROWS, s.shape[0]), BF16)], axis=0)
    return m_new, jnp.exp2(m - m_new) * acc + _dot(values, p)


def _softmax_init(width, queries):
    return jnp.full((1, queries), NEG, F32), jnp.zeros((width + ONES_ROWS, queries), F32)


def _softmax_result(carry):
    _, acc = carry
    width = acc.shape[0] - ONES_ROWS
    return acc[0:width] / acc[width:width + 1]


def _transpose_bf16(x):
    return x.astype(F32).T.astype(BF16)


def _params(*sem):
    return pltpu.CompilerParams(dimension_semantics=sem, vmem_limit_bytes=VMEM_LIMIT)


def _resident(shape, layer=None):
    if layer is None:
        zeros = (0,) * len(shape)
        return pl.BlockSpec(shape, lambda *_: zeros, pipeline_mode=pl.Buffered(1))
    index = (layer,) + (0,) * (len(shape) - 1)
    return pl.BlockSpec((None,) + tuple(shape[1:]), lambda *_: index, pipeline_mode=pl.Buffered(1))


def _smem():
    return pl.BlockSpec(memory_space=pltpu.SMEM)


def _half_ffn(x, g_ref, wi_ref, wo_ref):
    n = _rms(x, g_ref[...]).astype(BF16)
    acc = x
    for lo, width in FFN_CHUNKS:
        gate = _dot(n, wi_ref[:, lo:lo + width])
        up = _dot(n, wi_ref[:, D_FF + lo:D_FF + lo + width])
        act = (gate * jax.nn.sigmoid(gate) * up).astype(BF16)
        acc = acc + 0.5 * _dot(act, wo_ref[lo:lo + width, :])
    return acc


def _ffn_kernel(h_ref, g_ref, wi_ref, wo_ref, o_ref):
    o_ref[...] = _half_ffn(h_ref[...], g_ref, wi_ref, wo_ref)


def _ffn(h, g, wi, wo, tm, layer):
    t = h.shape[0]
    tile = pl.BlockSpec((tm, D_MODEL), lambda i: (i, 0))
    return pl.pallas_call(
        _ffn_kernel,
        name="ffn",
        grid=(t // tm,),
        in_specs=[tile, _resident((1, D_MODEL)), _resident(wi.shape, layer), _resident(wo.shape, layer)],
        out_specs=tile,
        out_shape=jax.ShapeDtypeStruct(h.shape, F32),
        compiler_params=_params("arbitrary"),
    )(h, g.reshape(1, D_MODEL), wi, wo)


IN_A = 3 * A_WIDTH
IN_B = B_WIDTH + 4 * B_KV
IN_C = C_WIDTH + 2 * C_KV
IN_G = 3 * D_MODEL
IN_SECTIONS = (IN_A, IN_B, B_KV, B_KV, IN_C, IN_G, LANES)


def _head_kernel(h_ref, g0_ref, wi_ref, wo_ref, g1_ref, w_ref,
                 o_ref, za_ref, zb_ref, zkc_ref, zvc_ref, zc_ref, gates_ref, bg_ref):
    x = _half_ffn(h_ref[...], g0_ref, wi_ref, wo_ref)
    o_ref[...] = x
    n = _rms(x, g1_ref[...]).astype(BF16)
    lo = 0
    for ref, width in zip((za_ref, zb_ref, zkc_ref, zvc_ref, zc_ref, gates_ref, bg_ref), IN_SECTIONS):
        ref[...] = _dot(n, w_ref[:, lo:lo + width]).astype(ref.dtype)
        lo += width


def _head(h, g0, wi, wo, g1, w, tm, layer):
    t = h.shape[0]
    dtypes = (BF16, BF16, BF16, BF16, BF16, F32, F32)
    rows = lambda width: pl.BlockSpec((tm, width), lambda i: (i, 0))
    vec = _resident((1, D_MODEL))
    return pl.pallas_call(
        _head_kernel,
        name="head",
        grid=(t // tm,),
        in_specs=[rows(D_MODEL), vec, _resident(wi.shape, layer), _resident(wo.shape, layer), vec,
                  _resident(w.shape, layer)],
        out_specs=[rows(D_MODEL)] + [rows(width) for width in IN_SECTIONS],
        out_shape=[jax.ShapeDtypeStruct(h.shape, F32)]
                  + [jax.ShapeDtypeStruct((t, width), dt) for width, dt in zip(IN_SECTIONS, dtypes)],
        compiler_params=_params("arbitrary"),
    )(h, g0.reshape(1, D_MODEL), wi, wo, g1.reshape(1, D_MODEL), w)


def _tail_kernel(h_ref, ya_ref, yb_ref, yc_ref, gates_ref, p_ref, wb_ref, wo_ref, g2_ref, wi_ref, wo2_ref,
                 g3_ref, wg_ref, we_ref, gf_ref, o_ref, *, final):
    merged = jnp.zeros(h_ref.shape, F32)
    for m, y_ref in enumerate((ya_ref, yb_ref, yc_ref)):
        gate = jax.nn.sigmoid(gates_ref[:, m * D_MODEL:(m + 1) * D_MODEL])
        merged = merged + gate * _dot(y_ref[...], wb_ref[m])
    x = h_ref[...] + _dot(merged.astype(BF16), wo_ref[...])
    x = _half_ffn(x, g2_ref, wi_ref, wo2_ref)
    gate = jax.nn.sigmoid(_dot(_rms(x, g3_ref[...]).astype(BF16), wg_ref[...]))
    x = x + gate * _dot(p_ref[...].astype(BF16), we_ref[...])
    if final:
        x = _rms(x, gf_ref[...])
    o_ref[...] = x


def _tail(h, ya, yb, yc, gates, p, wb, wo, g2, wi, wo2, g3, wg, we, gf, tm, layer, final):
    t = h.shape[0]
    rows = lambda width: pl.BlockSpec((tm, width), lambda i: (i, 0))
    vec = _resident((1, D_MODEL))
    return pl.pallas_call(
        functools.partial(_tail_kernel, final=final),
        name="tail",
        grid=(t // tm,),
        in_specs=[rows(D_MODEL), rows(A_WIDTH), rows(B_WIDTH), rows(C_WIDTH), rows(IN_G),
                  pl.BlockSpec((None, tm, PLE_DIM), lambda i: (layer, i, 0)),
                  _resident(wb.shape, layer), _resident(wo.shape, layer), vec,
                  _resident(wi.shape, layer), _resident(wo2.shape, layer),
                  vec, _resident(wg.shape, layer), _resident(we.shape, layer), vec],
        out_specs=rows(D_MODEL),
        out_shape=jax.ShapeDtypeStruct(h.shape, F32),
        compiler_params=_params("arbitrary"),
    )(h, ya, yb, yc, gates, p, wb, wo, g2.reshape(1, D_MODEL), wi, wo2, g3.reshape(1, D_MODEL), wg, we,
      gf.reshape(1, D_MODEL))


def _diff_kernel(tbl_ref, lam_ref, subg_ref, q_ref, k_ref, v_ref, o_ref, kaug_ref, vt_ref, near_ref,
                 *, tq, lam_init):
    qi = pl.program_id(1)
    width = 2 * A_HEAD_DIM
    n_heads = A_HEADS

    @pl.when((pl.program_id(0) == 0) & (qi == 0))
    def _static_setup():
        lane = lax.broadcasted_iota(jnp.int32, (tq, 2 * LANES), 1)
        key = lax.broadcasted_iota(jnp.int32, (2 * tq, tq), 0)
        qry = lax.broadcasted_iota(jnp.int32, (2 * tq, tq), 1)
        dist = tq + qry - key
        for h in range(n_heads):
            kaug_ref[h, 0:tq, :] = jnp.where(lane == LANES, 1.0, 0.0).astype(BF16)
            kaug_ref[h, tq:, LANES:2 * LANES] = jnp.zeros((kaug_ref.shape[1] - tq, LANES), BF16)
            vt_ref[h, 0] = jnp.zeros(vt_ref.shape[2:], BF16)
            near_ref[h] = jnp.where(dist >= 0, _bias_of_dist(dist, tbl_ref, h, True), NEG)

    @pl.when(qi == 0)
    def _per_batch_setup():
        for h in range(n_heads):
            cols = slice(h * width, (h + 1) * width)
            kaug_ref[h, tq:, 0:LANES] = k_ref[:, cols]
            for c in range(vt_ref.shape[1] - 1):
                vt_ref[h, c + 1] = _transpose_bf16(v_ref[c * tq:(c + 1) * tq, cols])

    zero = jnp.zeros((A_HEAD_DIM, tq), F32)
    flag_row = lax.broadcasted_iota(jnp.int32, (LANES, 2 * tq), 0) == 0
    flags = jnp.where(flag_row, UNSEL, 0.0)
    queries = []
    for h in range(n_heads):
        qt = q_ref[:, h * width:(h + 1) * width].astype(F32).T
        both = jnp.concatenate([jnp.concatenate([qt[0:A_HEAD_DIM], zero], axis=0),
                                jnp.concatenate([zero, qt[A_HEAD_DIM:width]], axis=0)], axis=1)
        queries.append(jnp.concatenate([both, flags], axis=0).astype(BF16))

    def logits_of(first_tile):
        start = pl.multiple_of(first_tile * tq, tq)
        return tuple(_dot(kaug_ref[h, pl.ds(start, 2 * tq), :], queries[h]) for h in range(n_heads))

    def update(first_tile, logits, carries):
        return tuple(_online_update(carries[h], logits[h], [vt_ref[h, first_tile], vt_ref[h, first_tile + 1]])
                     for h in range(n_heads))

    def far(gi, carries, count):
        tiles = [qi - 2 * (gi + c) for c in range(count)]
        logits = [logits_of(t) for t in tiles]
        for t, s in zip(tiles, logits):
            carries = update(t, s, carries)
        return carries

    nearest = tuple(s + jnp.concatenate([near_ref[h], near_ref[h]], axis=1) for h, s in enumerate(logits_of(qi)))
    carries = update(qi, nearest, tuple(_softmax_init(width, 2 * tq) for _ in range(n_heads)))
    n_far = qi // 2
    carries = lax.fori_loop(0, n_far // 2, lambda p, c: far(1 + 2 * p, c, 2), carries)
    carries = lax.fori_loop(0, n_far % 2, lambda _, c: far(n_far, c, 1), carries)
    lp = lam_ref[...]
    lam = (jnp.exp(jnp.sum(lp[0:1] * lp[1:2], axis=-1, keepdims=True))
           - jnp.exp(jnp.sum(lp[2:3] * lp[3:4], axis=-1, keepdims=True)) + lam_init)
    for h in range(n_heads):
        out = _softmax_result(carries[h])
        o = out[:, 0:tq] - lam * out[:, tq:2 * tq]
        o = o * lax.rsqrt(jnp.mean(o * o, axis=0, keepdims=True) + NORM_EPS) * (1.0 - lam_init)
        o_ref[:, h * width:(h + 1) * width] = (o.T * subg_ref[...]).astype(o_ref.dtype)


def _diff_attention(za, tbl, lam_p, subg, bsz, seq, lam_init, tq):
    nq = seq // tq
    width = 2 * A_HEAD_DIM
    return pl.pallas_call(
        functools.partial(_diff_kernel, tq=tq, lam_init=lam_init),
        name="diff_attn",
        grid=(bsz, nq),
        in_specs=[_smem(), _resident(lam_p.shape), _resident((1, width)),
                  pl.BlockSpec((tq, A_WIDTH), lambda b, i: (b * nq + i, 0)),
                  pl.BlockSpec((seq, A_WIDTH), lambda b, i: (b, 1)),
                  pl.BlockSpec((seq, A_WIDTH), lambda b, i: (b, 2))],
        out_specs=pl.BlockSpec((tq, A_WIDTH), lambda b, i: (b * nq + i, 0)),
        out_shape=jax.ShapeDtypeStruct((bsz * seq, A_WIDTH), BF16),
        scratch_shapes=[pltpu.VMEM((A_HEADS, tq + seq, 2 * LANES), BF16),
                        pltpu.VMEM((A_HEADS, nq + 1, width, tq), BF16),
                        pltpu.VMEM((A_HEADS, 2 * tq, tq), F32)],
        compiler_params=_params("arbitrary", "arbitrary"),
    )(tbl, lam_p, subg.reshape(1, width), za, za, za)


def _compress_kernel(xk_ref, xv_ref, pos_ref, w1_ref, w2_ref, kc_ref, vc_ref):
    rows = xk_ref.shape[0]
    for t, (x_ref, o_ref) in enumerate(((xk_ref, kc_ref), (xv_ref, vc_ref))):
        x = x_ref[...].astype(F32)
        first = _dot((x + pos_ref[t, 0]).astype(BF16), w1_ref[t, 0])
        second = _dot((x + pos_ref[t, 1]).astype(BF16), w1_ref[t, 1])
        hidden = first + pltpu.roll(second, rows - 1, 0)
        o_ref[...] = _dot(jax.nn.gelu(hidden).astype(BF16), w2_ref[t]).astype(o_ref.dtype)


def _compress(xk, xv, pos, w1, w2, layer):
    bsz, rows, width = xk.shape
    xspec = pl.BlockSpec((None, rows, width), lambda b: (b, 0, 0))
    ospec = pl.BlockSpec((None, rows, B_KV), lambda b: (b, 0, 0))
    return pl.pallas_call(
        _compress_kernel,
        name="nsa_compress",
        grid=(bsz,),
        in_specs=[xspec, xspec, _resident(pos.shape, layer), _resident(w1.shape, layer), _resident(w2.shape, layer)],
        out_specs=[ospec, ospec],
        out_shape=[jax.ShapeDtypeStruct((bsz, rows, B_KV), BF16)] * 2,
        compiler_params=_params("arbitrary"),
    )(xk, xv, pos, w1, w2)


def _group_queries_t(q, group, rep):
    tq = q.shape[0]
    zero = jnp.zeros((HEAD_DIM, tq), F32)
    cols = []
    for r in range(rep):
        head = group * rep + r
        slab = q[:, (head // 2) * LANES:(head // 2 + 1) * LANES].astype(F32).T
        part = slab[(head % 2) * HEAD_DIM:(head % 2 + 1) * HEAD_DIM]
        cols.append(jnp.concatenate([part, zero] if group == 0 else [zero, part], axis=0))
    return jnp.concatenate(cols, axis=1).astype(BF16)


def _build_near_bias(near_ref, tbl_ref, head0, groups, rep, tq, window_is_tile, minus_far):
    key = lax.broadcasted_iota(jnp.int32, (2 * tq, tq), 0)
    qry = lax.broadcasted_iota(jnp.int32, (2 * tq, tq), 1)
    dist = tq + qry - key
    visible = (dist >= 0) & (dist < tq) if window_is_tile else dist >= 0
    for g in range(groups):
        for r in range(rep):
            head = head0 + g * rep + r
            near_ref[g, :, r * tq:(r + 1) * tq] = jnp.where(visible, _bias_of_dist(dist, tbl_ref, head, minus_far), NEG)


def _swa_tile(q, k_prev, k_diag, v_prev, v_diag, bias_ref, sink_ref, qi, tq):
    cols = C_REP * tq
    prev_bias = jnp.where(qi > 0, 0.0, NEG)
    vp_t = _transpose_bf16(v_prev)
    vd_t = _transpose_bf16(v_diag)
    heads_out = []
    for g in range(C_GROUPS):
        half = slice(g * HEAD_DIM, (g + 1) * HEAD_DIM)
        qg = _group_queries_t(q, g, C_REP)
        s = jnp.concatenate([_dot(k_prev, qg) + bias_ref[g, 0:tq, :] + prev_bias,
                             _dot(k_diag, qg) + bias_ref[g, tq:2 * tq, :]], axis=0)
        m, acc = _online_update(_softmax_init(HEAD_DIM, cols), s, [vp_t[half], vd_t[half]])
        for r in range(C_REP):
            cc = slice(r * tq, (r + 1) * tq)
            sink = sink_ref[g * C_REP + r] * LOG2E
            m_all = jnp.maximum(m[:, cc], sink)
            scale = jnp.exp2(m[:, cc] - m_all)
            denom = acc[HEAD_DIM:HEAD_DIM + 1, cc] * scale + jnp.exp2(sink - m_all)
            heads_out.append(acc[0:HEAD_DIM, cc] * (scale / denom))
    return jnp.concatenate(heads_out, axis=0).T


SEL_GROUP = 4
SEL_PAD = SEL_GROUP - 1
WIN_PAD = 4


def _nsa_kernel(tbl_ref, sink_ref, q_ref, ks_ref, vs_ref, kw_ref, vw_ref, kc_ref, vc_ref, bg_ref, ovt_ref,
                cq_ref, ckp_ref, ckd_ref, cvp_ref, cvd_ref, o_ref, oc_ref,
                kaug_ref, kwp_ref, vst_ref, vwt_ref, vct_ref, near_ref, edge_ref, band_ref, sc_ref, swab_ref,
                *, tq, seq):
    first_batch = pl.program_id(0) == 0
    qi = pl.program_id(1)
    n_cmp_pad = seq // B_CMP_STRIDE
    n_sel = seq // B_SEL_BLOCK
    top_k = min(B_SEL_TOPK, n_sel)
    nq = seq // tq
    cols = B_REP * tq
    cmp_per_tile = tq // B_CMP_STRIDE
    band = CMP_BAND_BACK + cmp_per_tile

    @pl.when(first_batch & (qi == 0))
    def _static_setup():
        _build_near_bias(near_ref, tbl_ref, A_HEADS, B_GROUPS, B_REP, tq, False, True)
        _build_near_bias(swab_ref, tbl_ref, A_HEADS + B_HEADS, C_GROUPS, C_REP, tq, True, False)
        key = lax.broadcasted_iota(jnp.int32, (tq, cols), 0)
        qry = lax.broadcasted_iota(jnp.int32, (tq, cols), 1) % tq
        edge_ref[...] = jnp.where(qry < key, 0.0, NEG)
        blk = lax.broadcasted_iota(jnp.int32, (band, tq), 0) - CMP_BAND_BACK
        dist = lax.broadcasted_iota(jnp.int32, (band, tq), 1) - (blk * B_CMP_STRIDE + B_CMP_LEN - 1)
        for g in range(B_GROUPS):
            for r in range(B_REP):
                head = A_HEADS + g * B_REP + r
                band_ref[g, :, r * tq:(r + 1) * tq] = jnp.where(dist >= 0, _bias_of_dist(dist, tbl_ref, head, True), NEG)
        sc_ref[:, 0:CMP_BAND_BACK, :] = jnp.zeros((B_GROUPS, CMP_BAND_BACK, cols), F32)
        pad = SEL_PAD * tq
        row = lax.broadcasted_iota(jnp.int32, (pad + seq, LANES), 0)
        lane = lax.broadcasted_iota(jnp.int32, (pad + seq, LANES), 1)
        blk_id = jnp.where(row < pad, LANES - 1, (row - pad) // B_SEL_BLOCK)
        kaug_ref[:, LANES:2 * LANES] = jnp.where(blk_id == lane, 1.0, 0.0).astype(BF16)
        kaug_ref[0:pad, 0:LANES] = jnp.zeros((pad, LANES), BF16)
        kwp_ref[0:WIN_PAD * tq, :] = jnp.zeros((WIN_PAD * tq, LANES), BF16)
        for c in range(SEL_PAD):
            vst_ref[c] = jnp.zeros(vst_ref.shape[1:], BF16)
        for c in range(WIN_PAD):
            vwt_ref[c] = jnp.zeros(vwt_ref.shape[1:], BF16)

    @pl.when(qi == 0)
    def _per_batch_setup():
        kaug_ref[SEL_PAD * tq:, 0:LANES] = ks_ref[...]
        kwp_ref[WIN_PAD * tq:, :] = kw_ref[...]
        for c in range(nq):
            vst_ref[SEL_PAD + c] = _transpose_bf16(vs_ref[c * tq:(c + 1) * tq, :])
            vwt_ref[WIN_PAD + c] = _transpose_bf16(vw_ref[c * tq:(c + 1) * tq, :])
        for c in range(n_cmp_pad // LANES):
            vct_ref[:, c * LANES:(c + 1) * LANES] = _transpose_bf16(vc_ref[c * LANES:(c + 1) * LANES, :])

    q = q_ref[...]
    gates_t = jax.nn.sigmoid(bg_ref[...]).T
    first_cmp = qi * cmp_per_tile
    cmp_row = lax.broadcasted_iota(jnp.int32, (n_cmp_pad, cols), 0)
    cmp_visible = cmp_row < first_cmp + cmp_per_tile
    blk = lax.broadcasted_iota(jnp.int32, (n_sel, tq), 0)
    cur = (qi * tq + lax.broadcasted_iota(jnp.int32, (n_sel, tq), 1)) // B_SEL_BLOCK
    forced = (blk == 0) | (blk == cur) | (blk == cur - 1)
    future = blk > cur
    halves = [slice(g * HEAD_DIM, (g + 1) * HEAD_DIM) for g in range(B_GROUPS)]

    qgs, o_cs, imps = [], [], []
    for g in range(B_GROUPS):
        qg = _group_queries_t(q, g, B_REP)
        sc_ref[g, CMP_BAND_BACK:, :] = _dot(kc_ref[...], qg)
        band_rows = pl.ds(pl.multiple_of(first_cmp, 8), band)
        sc_ref[g, band_rows, :] = sc_ref[g, band_rows, :] + band_ref[g]
        s = jnp.where(cmp_visible, sc_ref[g, CMP_BAND_BACK:, :], NEG)
        e = jnp.where(s > 0.5 * NEG, jnp.exp2(s - jnp.max(s, axis=0, keepdims=True)), 0.0)
        l = jnp.sum(e, axis=0, keepdims=True)
        p_c = e * jnp.where(l > 0.0, 1.0 / jnp.where(l > 0.0, l, 1.0), 0.0)
        o_cs.append(_dot(vct_ref[halves[g], :], p_c.astype(BF16)))
        p_sum = p_c[:, 0:tq]
        for r in range(1, B_REP):
            p_sum = p_sum + p_c[:, r * tq:(r + 1) * tq]
        imp = _dot(ovt_ref[...], p_sum.astype(BF16))
        imp = jnp.where(forced, B_SEL_FORCE, jnp.where(future, -B_SEL_FORCE, imp))
        qgs.append(qg)
        imps.append(imp)

    blk_f = blk.astype(F32)
    taken = [jnp.zeros((n_sel, tq), jnp.bool_) for _ in range(B_GROUPS)]
    left = list(imps)
    for _ in range(top_k):
        for g in range(B_GROUPS):
            best = jnp.max(left[g], axis=0, keepdims=True)
            first = jnp.min(jnp.where(left[g] == best, blk_f, float(n_sel)), axis=0, keepdims=True)
            pick = blk_f == first
            taken[g] = taken[g] | pick
            left[g] = jnp.where(pick, TAKEN, left[g])
    q_augs = []
    for g in range(B_GROUPS):
        sel_bias = jnp.concatenate([jnp.where(taken[g], 0.0, UNSEL),
                                    jnp.full((LANES - n_sel, tq), UNSEL, F32)], axis=0).astype(BF16)
        q_augs.append(jnp.concatenate([qgs[g], jnp.concatenate([sel_bias] * B_REP, axis=1)], axis=0))

    pair = cols
    chains = [(g, slice(0, cols)) for g in range(B_GROUPS)]
    heads_per_chain = pair // tq

    def sel_logits(first_tile):
        start = pl.multiple_of(first_tile * tq, tq)
        keys = kaug_ref[pl.ds(start, SEL_GROUP * tq), :]
        return tuple(_dot(keys, q_augs[g][:, cc]) for g, cc in chains)

    def sel_update(first_tile, logits, carries):
        return tuple(_online_update(carry, s, [vst_ref[first_tile + t, halves[g], :] for t in range(SEL_GROUP)])
                     for (g, cc), s, carry in zip(chains, logits, carries))

    start = pl.multiple_of(qi * tq, tq)
    win_keys = kwp_ref[pl.ds(start, (WIN_PAD + 1) * tq), :]
    partial = []
    for g, cc in chains:
        s_w = _dot(win_keys, qgs[g][:, cc])
        slabs = []
        for t in range(WIN_PAD + 1):
            slab = s_w[t * tq:(t + 1) * tq]
            if t == 0:
                slab = slab + edge_ref[:, cc]
            if t >= WIN_PAD - 1:
                slab = slab + near_ref[g, (t - WIN_PAD + 1) * tq:(t - WIN_PAD + 2) * tq, cc]
            if t < WIN_PAD:
                slab = slab + jnp.where(qi >= WIN_PAD - t, 0.0, NEG)
            slabs.append(slab)
        o_w = _softmax_result(_online_update(_softmax_init(HEAD_DIM, pair), jnp.concatenate(slabs, axis=0),
                                             [vwt_ref[qi + t, halves[g], :] for t in range(WIN_PAD + 1)]))
        o_c = o_cs[g][:, cc]
        heads = []
        for rr in range(heads_per_chain):
            hc = slice(rr * tq, (rr + 1) * tq)
            c0 = (g * B_REP + rr) * 3
            heads.append(gates_t[c0:c0 + 1] * o_c[:, hc] + gates_t[c0 + 2:c0 + 3] * o_w[:, hc])
        partial.append(heads)

    oc_ref[...] = _swa_tile(cq_ref[...], ckp_ref[...], ckd_ref[...], cvp_ref[...], cvd_ref[...],
                            swab_ref, sink_ref, qi, tq).astype(oc_ref.dtype)

    def sel_far(gi, carries, count):
        tiles = [qi - SEL_GROUP * (gi + c) for c in range(count)]
        logits = [sel_logits(t) for t in tiles]
        for t, s in zip(tiles, logits):
            carries = sel_update(t, s, carries)
        return carries

    far_rows = (SEL_GROUP - 2) * tq
    nearest = tuple(jnp.concatenate([s[0:far_rows], s[far_rows:] + near_ref[g, :, cc]], axis=0)
                    for (g, cc), s in zip(chains, sel_logits(qi)))
    sel = sel_update(qi, nearest, tuple(_softmax_init(HEAD_DIM, pair) for _ in chains))
    n_far = qi // SEL_GROUP
    sel = lax.fori_loop(0, n_far // 2, lambda p, c: sel_far(1 + 2 * p, c, 2), sel)
    sel = lax.fori_loop(0, n_far % 2, lambda _, c: sel_far(n_far, c, 1), sel)

    heads_out = []
    for ci, (g, cc) in enumerate(chains):
        o_s = _softmax_result(sel[ci])
        for rr in range(heads_per_chain):
            hc = slice(rr * tq, (rr + 1) * tq)
            c1 = (g * B_REP + rr) * 3 + 1
            heads_out.append(partial[ci][rr] + gates_t[c1:c1 + 1] * o_s[:, hc])
    o_ref[...] = jnp.concatenate(heads_out, axis=0).T.astype(o_ref.dtype)


def _nsa_swa_attention(zb, kc, vc, bgate, zc, tbl, sinks, ovt, bsz, seq, tq):
    nq = seq // tq
    n_cmp_pad = seq // B_CMP_STRIDE
    cols = B_REP * tq
    kv = lambda col: pl.BlockSpec((seq, LANES), lambda b, i: (b, col))
    cmp_spec = pl.BlockSpec((None, n_cmp_pad, B_KV), lambda b, i: (b, 0, 0))
    qcols = B_WIDTH // LANES
    ccols = C_WIDTH // LANES
    wide = lambda width: pl.BlockSpec((tq, width), lambda b, i: (b * nq + i, 0))
    prev = lambda col: pl.BlockSpec((tq, LANES), lambda b, i: (b * nq + jnp.maximum(i - 1, 0), col))
    diag = lambda col: pl.BlockSpec((tq, LANES), lambda b, i: (b * nq + i, col))
    return pl.pallas_call(
        functools.partial(_nsa_kernel, tq=tq, seq=seq),
        name="nsa_attn",
        grid=(bsz, nq),
        in_specs=[_smem(), _smem(),
                  wide(B_WIDTH),
                  kv(qcols), kv(qcols + 1), kv(qcols + 2), kv(qcols + 3),
                  cmp_spec, cmp_spec,
                  wide(LANES),
                  _resident(ovt.shape),
                  wide(C_WIDTH), prev(ccols), diag(ccols), prev(ccols + 1), diag(ccols + 1)],
        out_specs=[wide(B_WIDTH), wide(C_WIDTH)],
        out_shape=[jax.ShapeDtypeStruct((bsz * seq, B_WIDTH), BF16),
                   jax.ShapeDtypeStruct((bsz * seq, C_WIDTH), BF16)],
        scratch_shapes=[pltpu.VMEM((SEL_PAD * tq + seq, 2 * LANES), BF16),
                        pltpu.VMEM((WIN_PAD * tq + seq, LANES), BF16),
                        pltpu.VMEM((SEL_PAD + nq, B_KV, tq), BF16),
                        pltpu.VMEM((WIN_PAD + nq, B_KV, tq), BF16),
                        pltpu.VMEM((B_KV, n_cmp_pad), BF16),
                        pltpu.VMEM((B_GROUPS, 2 * tq, cols), F32),
                        pltpu.VMEM((tq, cols), F32),
                        pltpu.VMEM((B_GROUPS, CMP_BAND_BACK + tq // B_CMP_STRIDE, cols), F32),
                        pltpu.VMEM((B_GROUPS, CMP_BAND_BACK + n_cmp_pad, cols), F32),
                        pltpu.VMEM((C_GROUPS, 2 * tq, C_REP * tq), F32)],
        compiler_params=_params("arbitrary", "arbitrary"),
    )(tbl, sinks, zb, zb, zb, zb, zb, kc, vc, bgate, ovt, zc, zc, zc, zc, zc)


def _prep_w_in(w_in):
    sizes = (A_WIDTH, A_WIDTH, A_WIDTH, B_WIDTH, B_KV, B_KV, B_KV, B_KV, B_KV, B_KV, 3 * B_HEADS,
             C_WIDTH, C_KV, C_KV, D_MODEL, D_MODEL, D_MODEL)
    offs = np.concatenate([[0], np.cumsum(sizes)])
    (aq, ak, av, bq, bkc, bvc, bks, bvs, bkw, bvw, bgate, cq, ck, cv, ga, gb, gc) = [
        w_in[..., offs[i]:offs[i + 1]] for i in range(len(sizes))]
    scale = HEAD_DIM ** -0.5 * LOG2E
    bgate = jnp.pad(bgate, ((0, 0), (0, 0), (0, LANES - 3 * B_HEADS)))
    cols = [aq * scale, ak, av, bq * scale, bks, bvs, bkw, bvw, bkc, bvc, cq * scale, ck, cv, ga, gb, gc, bgate]
    return jnp.concatenate([c.astype(BF16) for c in cols], axis=-1)


def _prep_compress(cmp_pos, cmp_w1, cmp_w2):
    depth = cmp_pos.shape[0]
    half = B_CMP_STRIDE
    pos = cmp_pos.reshape(depth, 2, 2, half, 1, B_HEAD_DIM)
    pos = jnp.broadcast_to(pos, (depth, 2, 2, half, B_GROUPS, B_HEAD_DIM)).reshape(depth, 2, 2, 1, half * B_KV)
    w1 = cmp_w1.reshape(depth, 2, 2, half, 1, B_HEAD_DIM, 1, B_CMP_HIDDEN)
    eye = jnp.eye(B_GROUPS, dtype=cmp_w1.dtype).reshape(1, 1, 1, 1, B_GROUPS, 1, B_GROUPS, 1)
    w1 = (w1 * eye).reshape(depth, 2, 2, half * B_KV, B_GROUPS * B_CMP_HIDDEN)
    w2 = cmp_w2.reshape(depth, 2, 1, B_CMP_HIDDEN, 1, B_HEAD_DIM)
    eye2 = jnp.eye(B_GROUPS, dtype=cmp_w2.dtype).reshape(1, 1, B_GROUPS, 1, B_GROUPS, 1)
    w2 = (w2 * eye2).reshape(depth, 2, B_GROUPS * B_CMP_HIDDEN, B_KV)
    return pos.astype(F32), w1.astype(BF16), w2.astype(BF16)


def _overlap_t(seq):
    n_cmp_pad = seq // B_CMP_STRIDE
    start = np.arange(n_cmp_pad) * B_CMP_STRIDE
    sel = np.arange(seq // B_SEL_BLOCK) * B_SEL_BLOCK
    ov = (start[None, :] < sel[:, None] + B_SEL_BLOCK) & (start[None, :] + B_CMP_LEN > sel[:, None])
    ov &= (np.arange(n_cmp_pad) < n_cmp_pad - 1)[None, :]
    return jnp.asarray(ov.astype(np.float32), dtype=BF16)


def kernel(x, p, norm_g, ffn1_wi, ffn1_wo, w_in, diff_lambda, diff_subln, nsa_cmp_pos, nsa_cmp_w1,
           nsa_cmp_w2, swa_sinks, w_branch, w_out, ffn2_wi, ffn2_wo, w_ple, w_ple_gate, rel_bias, final_norm):
    bsz, seq, _ = x.shape
    depth = norm_g.shape[0]
    tokens = bsz * seq
    assert seq % 1024 == 0 and seq // B_SEL_BLOCK <= LANES
    tm, tm_tail, tq_a, tq = 512, 256, 256, 128
    assert FAR_DIST + B_CMP_LEN - 1 <= CMP_BAND_BACK * B_CMP_STRIDE and FAR_DIST <= tq
    assert WIN_PAD * tq == B_WINDOW and tq == C_WINDOW and seq // B_SEL_BLOCK < LANES

    wi1, wo1 = ffn1_wi.astype(BF16), ffn1_wo.astype(BF16)
    wi2, wo2 = ffn2_wi.astype(BF16), ffn2_wo.astype(BF16)
    w_in_r = _prep_w_in(w_in)
    cmp_pos, cmp_w1, cmp_w2 = _prep_compress(nsa_cmp_pos, nsa_cmp_w1, nsa_cmp_w2)
    wb, wo = w_branch.astype(BF16), w_out.astype(BF16)
    wpg, wpe = w_ple_gate.astype(BF16), w_ple.astype(BF16)
    tbl = rel_bias.astype(F32)
    ovt = _overlap_t(seq)
    chunk_rows = seq // B_CMP_STRIDE

    h = x.reshape(tokens, D_MODEL)
    p_rows = p.reshape(depth, tokens, PLE_DIM)
    for i in range(depth):
        h, za, zb, zkc, zvc, zc, gates, bgate = _head(h, norm_g[i, 0], wi1, wo1, norm_g[i, 1], w_in_r, tm_tail, i)
        lam_init = 0.8 - 0.6 * math.exp(-0.3 * i)
        ya = _diff_attention(za, tbl, diff_lambda[i].astype(F32), diff_subln[i], bsz, seq, lam_init, tq_a)
        kc, vc = _compress(zkc.reshape(bsz, chunk_rows, B_CMP_STRIDE * B_KV),
                           zvc.reshape(bsz, chunk_rows, B_CMP_STRIDE * B_KV), cmp_pos, cmp_w1, cmp_w2, i)
        yb, yc = _nsa_swa_attention(zb, kc, vc, bgate, zc, tbl, swa_sinks[i].astype(F32), ovt, bsz, seq, tq)
        h = _tail(h, ya, yb, yc, gates, p_rows, wb, wo, norm_g[i, 2], wi2, wo2, norm_g[i, 3], wpg, wpe,
                  final_norm, tm_tail, i, final=(i == depth - 1))
    return h.reshape(bsz, seq, D_MODEL)
```

```python
import functools
import math

import numpy as np
import jax
import jax.numpy as jnp
from jax import lax
from jax.experimental import pallas as pl
from jax.experimental.pallas import tpu as pltpu

F32 = jnp.float32
BF16 = jnp.bfloat16

D_MODEL = 1024
PLE_DIM = 256
D_FF = 2816
NORM_EPS = 1e-6
NEG = -1e30
NUM_BUCKETS = 32
MAX_DISTANCE = 128
A_HEADS = 4
A_HEAD_DIM = 64
B_HEADS = 8
B_GROUPS = 2
B_REP = B_HEADS // B_GROUPS
B_HEAD_DIM = 64
B_CMP_LEN = 32
B_CMP_STRIDE = 16
B_CMP_HIDDEN = 256
B_SEL_BLOCK = 64
B_SEL_TOPK = 16
B_WINDOW = 512
B_SEL_FORCE = 1e6
C_HEADS = 8
C_GROUPS = 2
C_REP = C_HEADS // C_GROUPS
C_HEAD_DIM = 64
C_WINDOW = 128
A_WIDTH = A_HEADS * 2 * A_HEAD_DIM
B_WIDTH = B_HEADS * B_HEAD_DIM
B_KV = B_GROUPS * B_HEAD_DIM
C_WIDTH = C_HEADS * C_HEAD_DIM
C_KV = C_GROUPS * C_HEAD_DIM
N_BIAS_HEADS = A_HEADS + B_HEADS + C_HEADS

LANES = 128
HEAD_DIM = 64
VMEM_LIMIT = 56 * 1024 * 1024
UNSEL = -float(2 ** 30)
TAKEN = -3e38
FFN_CHUNKS = ((0, 1536), (1536, 1280))
CMP_BAND_BACK = 16
LOG2E = math.log2(math.e)
ONES_ROWS = 16
EXP_HEADROOM = 64.0


def _bucket_thresholds():
    n = np.arange(4 * MAX_DISTANCE)
    max_exact = NUM_BUCKETS // 2
    nf = np.maximum(n, 1).astype(np.float32)
    large = max_exact + (np.log(nf / max_exact) / math.log(MAX_DISTANCE / max_exact)
                         * (NUM_BUCKETS - max_exact)).astype(np.int32)
    bucket = np.where(n < max_exact, n, np.minimum(large, NUM_BUCKETS - 1))
    out = []
    for b in range(1, NUM_BUCKETS):
        hit = np.nonzero(bucket == b)[0]
        if hit.size:
            out.append((b, int(hit[0])))
    return tuple(out)


BUCKET_LO = _bucket_thresholds()
FAR_DIST = BUCKET_LO[-1][1]


def _bias_of_dist(dist, tbl_ref, head, minus_far=False):
    shift = tbl_ref[NUM_BUCKETS - 1, head] if minus_far else 0.0
    out = jnp.full(dist.shape, (tbl_ref[0, head] - shift) * LOG2E, F32)
    for b, lo in BUCKET_LO:
        out = jnp.where(dist >= lo, (tbl_ref[b, head] - shift) * LOG2E, out)
    return out


def _rms(x, g):
    return x * lax.rsqrt(jnp.mean(x * x, axis=-1, keepdims=True) + NORM_EPS) * g


def _dot(a, b):
    return jnp.dot(a, b, preferred_element_type=F32)


def _online_update(carry, s, vts):
    m, acc = carry
    m_new = jnp.maximum(m, jnp.max(s, axis=0, keepdims=True))
    p = jnp.exp2(s - m_new).astype(BF16)
    values = jnp.concatenate([jnp.concatenate(vts, axis=1), jnp.ones((ONES_ROWS, s.shape[0]), BF16)], axis=0)
    return m_new, jnp.exp2(m - m_new) * acc + _dot(values, p)


def _one_pass_update(carry, s, vts):
    m, acc = carry
    top = jnp.max(s, axis=0, keepdims=True)
    p = jnp.exp2(s - m).astype(BF16)
    values = jnp.concatenate([jnp.concatenate(vts, axis=1), jnp.ones((ONES_ROWS, s.shape[0]), BF16)], axis=0)
    m_new = jnp.maximum(m, top)
    return (m_new, jnp.exp2(m - m_new) * (acc + _dot(values, p))), jnp.max(top - m) <= EXP_HEADROOM


def _softmax_init(width, queries):
    return jnp.full((1, queries), NEG, F32), jnp.zeros((width + ONES_ROWS, queries), F32)


def _softmax_result(carry):
    _, acc = carry
    width = acc.shape[0] - ONES_ROWS
    return acc[0:width] / acc[width:width + 1]


def _transpose_bf16(x):
    return x.astype(F32).T.astype(BF16)


def _params(*sem):
    return pltpu.CompilerParams(dimension_semantics=sem, vmem_limit_bytes=VMEM_LIMIT)


def _resident(shape, layer=None):
    if layer is None:
        zeros = (0,) * len(shape)
        return pl.BlockSpec(shape, lambda *_: zeros, pipeline_mode=pl.Buffered(1))
    index = (layer,) + (0,) * (len(shape) - 1)
    return pl.BlockSpec((None,) + tuple(shape[1:]), lambda *_: index, pipeline_mode=pl.Buffered(1))


def _smem():
    return pl.BlockSpec(memory_space=pltpu.SMEM)


def _half_ffn(x, g_ref, wi_ref, wo_ref):
    n = _rms(x, g_ref[...]).astype(BF16)
    acc = x
    for lo, width in FFN_CHUNKS:
        gate = _dot(n, wi_ref[:, lo:lo + width])
        up = _dot(n, wi_ref[:, D_FF + lo:D_FF + lo + width])
        act = (gate * jax.nn.sigmoid(gate) * up).astype(BF16)
        acc = acc + 0.5 * _dot(act, wo_ref[lo:lo + width, :])
    return acc


def _ffn_kernel(h_ref, g_ref, wi_ref, wo_ref, o_ref):
    o_ref[...] = _half_ffn(h_ref[...], g_ref, wi_ref, wo_ref)


def _ffn(h, g, wi, wo, tm, layer):
    t = h.shape[0]
    tile = pl.BlockSpec((tm, D_MODEL), lambda i: (i, 0))
    return pl.pallas_call(
        _ffn_kernel,
        name="ffn",
        grid=(t // tm,),
        in_specs=[tile, _resident((1, D_MODEL)), _resident(wi.shape, layer), _resident(wo.shape, layer)],
        out_specs=tile,
        out_shape=jax.ShapeDtypeStruct(h.shape, F32),
        compiler_params=_params("arbitrary"),
    )(h, g.reshape(1, D_MODEL), wi, wo)


IN_A = 3 * A_WIDTH
IN_B = B_WIDTH + 4 * B_KV
IN_C = C_WIDTH + 2 * C_KV
IN_G = 3 * D_MODEL
IN_SECTIONS = (IN_A, IN_B, B_KV, B_KV, IN_C, IN_G, LANES)


def _head_kernel(h_ref, g0_ref, wi_ref, wo_ref, g1_ref, w_ref,
                 o_ref, za_ref, zb_ref, zkc_ref, zvc_ref, zc_ref, gates_ref, bg_ref):
    x = _half_ffn(h_ref[...], g0_ref, wi_ref, wo_ref)
    o_ref[...] = x
    n = _rms(x, g1_ref[...]).astype(BF16)
    lo = 0
    for ref, width in zip((za_ref, zb_ref, zkc_ref, zvc_ref, zc_ref, gates_ref, bg_ref), IN_SECTIONS):
        ref[...] = _dot(n, w_ref[:, lo:lo + width]).astype(ref.dtype)
        lo += width


def _head(h, g0, wi, wo, g1, w, tm, layer):
    t = h.shape[0]
    dtypes = (BF16, BF16, BF16, BF16, BF16, F32, F32)
    rows = lambda width: pl.BlockSpec((tm, width), lambda i: (i, 0))
    vec = _resident((1, D_MODEL))
    return pl.pallas_call(
        _head_kernel,
        name="head",
        grid=(t // tm,),
        in_specs=[rows(D_MODEL), vec, _resident(wi.shape, layer), _resident(wo.shape, layer), vec,
                  _resident(w.shape, layer)],
        out_specs=[rows(D_MODEL)] + [rows(width) for width in IN_SECTIONS],
        out_shape=[jax.ShapeDtypeStruct(h.shape, F32)]
                  + [jax.ShapeDtypeStruct((t, width), dt) for width, dt in zip(IN_SECTIONS, dtypes)],
        compiler_params=_params("arbitrary"),
    )(h, g0.reshape(1, D_MODEL), wi, wo, g1.reshape(1, D_MODEL), w)


def _tail_kernel(h_ref, ya_ref, yb_ref, yc_ref, gates_ref, p_ref, wb_ref, wo_ref, g2_ref, wi_ref, wo2_ref,
                 g3_ref, wg_ref, we_ref, gf_ref, o_ref, *, final):
    merged = jnp.zeros(h_ref.shape, F32)
    for m, y_ref in enumerate((ya_ref, yb_ref, yc_ref)):
        gate = jax.nn.sigmoid(gates_ref[:, m * D_MODEL:(m + 1) * D_MODEL])
        merged = merged + gate * _dot(y_ref[...], wb_ref[m])
    x = h_ref[...] + _dot(merged.astype(BF16), wo_ref[...])
    x = _half_ffn(x, g2_ref, wi_ref, wo2_ref)
    gate = jax.nn.sigmoid(_dot(_rms(x, g3_ref[...]).astype(BF16), wg_ref[...]))
    x = x + gate * _dot(p_ref[...].astype(BF16), we_ref[...])
    if final:
        x = _rms(x, gf_ref[...])
    o_ref[...] = x


def _tail(h, ya, yb, yc, gates, p, wb, wo, g2, wi, wo2, g3, wg, we, gf, tm, layer, final):
    t = h.shape[0]
    rows = lambda width: pl.BlockSpec((tm, width), lambda i: (i, 0))
    vec = _resident((1, D_MODEL))
    return pl.pallas_call(
        functools.partial(_tail_kernel, final=final),
        name="tail",
        grid=(t // tm,),
        in_specs=[rows(D_MODEL), rows(A_WIDTH), rows(B_WIDTH), rows(C_WIDTH), rows(IN_G),
                  pl.BlockSpec((None, tm, PLE_DIM), lambda i: (layer, i, 0)),
                  _resident(wb.shape, layer), _resident(wo.shape, layer), vec,
                  _resident(wi.shape, layer), _resident(wo2.shape, layer),
                  vec, _resident(wg.shape, layer), _resident(we.shape, layer), vec],
        out_specs=rows(D_MODEL),
        out_shape=jax.ShapeDtypeStruct(h.shape, F32),
        compiler_params=_params("arbitrary"),
    )(h, ya, yb, yc, gates, p, wb, wo, g2.reshape(1, D_MODEL), wi, wo2, g3.reshape(1, D_MODEL), wg, we,
      gf.reshape(1, D_MODEL))


def _diff_kernel(tbl_ref, lam_ref, subg_ref, q_ref, k_ref, v_ref, o_ref, kaug_ref, vt_ref, near_ref,
                 *, tq, lam_init):
    qi = pl.program_id(1)
    width = 2 * A_HEAD_DIM
    n_heads = A_HEADS

    @pl.when((pl.program_id(0) == 0) & (qi == 0))
    def _static_setup():
        lane = lax.broadcasted_iota(jnp.int32, (tq, 2 * LANES), 1)
        key = lax.broadcasted_iota(jnp.int32, (2 * tq, tq), 0)
        qry = lax.broadcasted_iota(jnp.int32, (2 * tq, tq), 1)
        dist = tq + qry - key
        for h in range(n_heads):
            kaug_ref[h, 0:tq, :] = jnp.where(lane == LANES, 1.0, 0.0).astype(BF16)
            kaug_ref[h, tq:, LANES:2 * LANES] = jnp.zeros((kaug_ref.shape[1] - tq, LANES), BF16)
            vt_ref[h, 0] = jnp.zeros(vt_ref.shape[2:], BF16)
            near_ref[h] = jnp.where(dist >= 0, _bias_of_dist(dist, tbl_ref, h, True), NEG)

    @pl.when(qi == 0)
    def _per_batch_setup():
        for h in range(n_heads):
            cols = slice(h * width, (h + 1) * width)
            kaug_ref[h, tq:, 0:LANES] = k_ref[:, cols]
            for c in range(vt_ref.shape[1] - 1):
                vt_ref[h, c + 1] = _transpose_bf16(v_ref[c * tq:(c + 1) * tq, cols])

    zero = jnp.zeros((A_HEAD_DIM, tq), F32)
    flag_row = lax.broadcasted_iota(jnp.int32, (LANES, 2 * tq), 0) == 0
    flags = jnp.where(flag_row, UNSEL, 0.0)
    queries = []
    for h in range(n_heads):
        qt = q_ref[:, h * width:(h + 1) * width].astype(F32).T
        both = jnp.concatenate([jnp.concatenate([qt[0:A_HEAD_DIM], zero], axis=0),
                                jnp.concatenate([zero, qt[A_HEAD_DIM:width]], axis=0)], axis=1)
        queries.append(jnp.concatenate([both, flags], axis=0).astype(BF16))

    def logits_of(first_tile):
        start = pl.multiple_of(first_tile * tq, tq)
        return tuple(_dot(kaug_ref[h, pl.ds(start, 2 * tq), :], queries[h]) for h in range(n_heads))

    def update(first_tile, logits, carries):
        return tuple(_online_update(carries[h], logits[h], [vt_ref[h, first_tile], vt_ref[h, first_tile + 1]])
                     for h in range(n_heads))

    def far(gi, carries, count):
        tiles = [qi - 2 * (gi + c) for c in range(count)]
        fast, safe = carries, True
        for t, logits in [(t, logits_of(t)) for t in tiles]:
            steps = [_one_pass_update(fast[h], logits[h], [vt_ref[h, t], vt_ref[h, t + 1]]) for h in range(n_heads)]
            fast = tuple(step[0] for step in steps)
            for step in steps:
                safe = safe & step[1]

        def redo():
            slow = carries
            for t in tiles:
                slow = update(t, logits_of(t), slow)
            return slow

        return lax.cond(safe, lambda: fast, redo)

    nearest = tuple(s + jnp.concatenate([near_ref[h], near_ref[h]], axis=1) for h, s in enumerate(logits_of(qi)))
    carries = update(qi, nearest, tuple(_softmax_init(width, 2 * tq) for _ in range(n_heads)))
    n_far = qi // 2
    carries = lax.fori_loop(0, n_far // 2, lambda p, c: far(1 + 2 * p, c, 2), carries)
    carries = lax.fori_loop(0, n_far % 2, lambda _, c: far(n_far, c, 1), carries)
    lp = lam_ref[...]
    lam = (jnp.exp(jnp.sum(lp[0:1] * lp[1:2], axis=-1, keepdims=True))
           - jnp.exp(jnp.sum(lp[2:3] * lp[3:4], axis=-1, keepdims=True)) + lam_init)
    for h in range(n_heads):
        out = _softmax_result(carries[h])
        o = out[:, 0:tq] - lam * out[:, tq:2 * tq]
        o = o * lax.rsqrt(jnp.mean(o * o, axis=0, keepdims=True) + NORM_EPS) * (1.0 - lam_init)
        o_ref[:, h * width:(h + 1) * width] = (o.T * subg_ref[...]).astype(o_ref.dtype)


def _diff_attention(za, tbl, lam_p, subg, bsz, seq, lam_init, tq):
    nq = seq // tq
    width = 2 * A_HEAD_DIM
    return pl.pallas_call(
        functools.partial(_diff_kernel, tq=tq, lam_init=lam_init),
        name="diff_attn",
        grid=(bsz, nq),
        in_specs=[_smem(), _resident(lam_p.shape), _resident((1, width)),
                  pl.BlockSpec((tq, A_WIDTH), lambda b, i: (b * nq + i, 0)),
                  pl.BlockSpec((seq, A_WIDTH), lambda b, i: (b, 1)),
                  pl.BlockSpec((seq, A_WIDTH), lambda b, i: (b, 2))],
        out_specs=pl.BlockSpec((tq, A_WIDTH), lambda b, i: (b * nq + i, 0)),
        out_shape=jax.ShapeDtypeStruct((bsz * seq, A_WIDTH), BF16),
        scratch_shapes=[pltpu.VMEM((A_HEADS, tq + seq, 2 * LANES), BF16),
                        pltpu.VMEM((A_HEADS, nq + 1, width, tq), BF16),
                        pltpu.VMEM((A_HEADS, 2 * tq, tq), F32)],
        compiler_params=_params("arbitrary", "arbitrary"),
    )(tbl, lam_p, subg.reshape(1, width), za, za, za)


def _compress_kernel(xk_ref, xv_ref, pos_ref, w1_ref, w2_ref, kc_ref, vc_ref):
    rows = xk_ref.shape[0]
    for t, (x_ref, o_ref) in enumerate(((xk_ref, kc_ref), (xv_ref, vc_ref))):
        x = x_ref[...].astype(F32)
        first = _dot((x + pos_ref[t, 0]).astype(BF16), w1_ref[t, 0])
        second = _dot((x + pos_ref[t, 1]).astype(BF16), w1_ref[t, 1])
        hidden = first + pltpu.roll(second, rows - 1, 0)
        o_ref[...] = _dot(jax.nn.gelu(hidden).astype(BF16), w2_ref[t]).astype(o_ref.dtype)


def _compress(xk, xv, pos, w1, w2, layer):
    bsz, rows, width = xk.shape
    xspec = pl.BlockSpec((None, rows, width), lambda b: (b, 0, 0))
    ospec = pl.BlockSpec((None, rows, B_KV), lambda b: (b, 0, 0))
    return pl.pallas_call(
        _compress_kernel,
        name="nsa_compress",
        grid=(bsz,),
        in_specs=[xspec, xspec, _resident(pos.shape, layer), _resident(w1.shape, layer), _resident(w2.shape, layer)],
        out_specs=[ospec, ospec],
        out_shape=[jax.ShapeDtypeStruct((bsz, rows, B_KV), BF16)] * 2,
        compiler_params=_params("arbitrary"),
    )(xk, xv, pos, w1, w2)


def _group_queries_t(q, group, rep):
    tq = q.shape[0]
    zero = jnp.zeros((HEAD_DIM, tq), F32)
    cols = []
    for r in range(rep):
        head = group * rep + r
        slab = q[:, (head // 2) * LANES:(head // 2 + 1) * LANES].astype(F32).T
        part = slab[(head % 2) * HEAD_DIM:(head % 2 + 1) * HEAD_DIM]
        cols.append(jnp.concatenate([part, zero] if group == 0 else [zero, part], axis=0))
    return jnp.concatenate(cols, axis=1).astype(BF16)


def _build_near_bias(near_ref, tbl_ref, head0, groups, rep, tq, window_is_tile, minus_far):
    key = lax.broadcasted_iota(jnp.int32, (2 * tq, tq), 0)
    qry = lax.broadcasted_iota(jnp.int32, (2 * tq, tq), 1)
    dist = tq + qry - key
    visible = (dist >= 0) & (dist < tq) if window_is_tile else dist >= 0
    for g in range(groups):
        for r in range(rep):
            head = head0 + g * rep + r
            near_ref[g, :, r * tq:(r + 1) * tq] = jnp.where(visible, _bias_of_dist(dist, tbl_ref, head, minus_far), NEG)


def _swa_tile(q, k_prev, k_diag, v_prev, v_diag, bias_ref, sink_ref, qi, tq):
    cols = C_REP * tq
    prev_bias = jnp.where(qi > 0, 0.0, NEG)
    vp_t = _transpose_bf16(v_prev)
    vd_t = _transpose_bf16(v_diag)
    heads_out = []
    for g in range(C_GROUPS):
        half = slice(g * HEAD_DIM, (g + 1) * HEAD_DIM)
        qg = _group_queries_t(q, g, C_REP)
        s = jnp.concatenate([_dot(k_prev, qg) + bias_ref[g, 0:tq, :] + prev_bias,
                             _dot(k_diag, qg) + bias_ref[g, tq:2 * tq, :]], axis=0)
        m, acc = _online_update(_softmax_init(HEAD_DIM, cols), s, [vp_t[half], vd_t[half]])
        for r in range(C_REP):
            cc = slice(r * tq, (r + 1) * tq)
            sink = sink_ref[g * C_REP + r] * LOG2E
            m_all = jnp.maximum(m[:, cc], sink)
            scale = jnp.exp2(m[:, cc] - m_all)
            denom = acc[HEAD_DIM:HEAD_DIM + 1, cc] * scale + jnp.exp2(sink - m_all)
            heads_out.append(acc[0:HEAD_DIM, cc] * (scale / denom))
    return jnp.concatenate(heads_out, axis=0).T


SEL_GROUP = 4
SEL_PAD = SEL_GROUP - 1
WIN_PAD = 4


def _nsa_kernel(tbl_ref, sink_ref, q_ref, ks_ref, vs_ref, kw_ref, vw_ref, kc_ref, vc_ref, bg_ref, ovt_ref,
                cq_ref, ckp_ref, ckd_ref, cvp_ref, cvd_ref, o_ref, oc_ref,
                kaug_ref, kwp_ref, vst_ref, vwt_ref, vct_ref, near_ref, edge_ref, band_ref, sc_ref, swab_ref,
                *, tq, seq):
    first_batch = pl.program_id(0) == 0
    qi = pl.program_id(1)
    n_cmp_pad = seq // B_CMP_STRIDE
    n_sel = seq // B_SEL_BLOCK
    top_k = min(B_SEL_TOPK, n_sel)
    nq = seq // tq
    cols = B_REP * tq
    cmp_per_tile = tq // B_CMP_STRIDE
    band = CMP_BAND_BACK + cmp_per_tile

    @pl.when(first_batch & (qi == 0))
    def _static_setup():
        _build_near_bias(near_ref, tbl_ref, A_HEADS, B_GROUPS, B_REP, tq, False, True)
        _build_near_bias(swab_ref, tbl_ref, A_HEADS + B_HEADS, C_GROUPS, C_REP, tq, True, False)
        key = lax.broadcasted_iota(jnp.int32, (tq, cols), 0)
        qry = lax.broadcasted_iota(jnp.int32, (tq, cols), 1) % tq
        edge_ref[...] = jnp.where(qry < key, 0.0, NEG)
        blk = lax.broadcasted_iota(jnp.int32, (band, tq), 0) - CMP_BAND_BACK
        dist = lax.broadcasted_iota(jnp.int32, (band, tq), 1) - (blk * B_CMP_STRIDE + B_CMP_LEN - 1)
        for g in range(B_GROUPS):
            for r in range(B_REP):
                head = A_HEADS + g * B_REP + r
                band_ref[g, :, r * tq:(r + 1) * tq] = jnp.where(dist >= 0, _bias_of_dist(dist, tbl_ref, head, True), NEG)
        sc_ref[:, 0:CMP_BAND_BACK, :] = jnp.zeros((B_GROUPS, CMP_BAND_BACK, cols), F32)
        pad = SEL_PAD * tq
        row = lax.broadcasted_iota(jnp.int32, (pad + seq, LANES), 0)
        lane = lax.broadcasted_iota(jnp.int32, (pad + seq, LANES), 1)
        blk_id = jnp.where(row < pad, LANES - 1, (row - pad) // B_SEL_BLOCK)
        kaug_ref[:, LANES:2 * LANES] = jnp.where(blk_id == lane, 1.0, 0.0).astype(BF16)
        kaug_ref[0:pad, 0:LANES] = jnp.zeros((pad, LANES), BF16)
        kwp_ref[0:WIN_PAD * tq, :] = jnp.zeros((WIN_PAD * tq, LANES), BF16)
        for c in range(SEL_PAD):
            vst_ref[c] = jnp.zeros(vst_ref.shape[1:], BF16)
        for c in range(WIN_PAD):
            vwt_ref[c] = jnp.zeros(vwt_ref.shape[1:], BF16)

    @pl.when(qi == 0)
    def _per_batch_setup():
        kaug_ref[SEL_PAD * tq:, 0:LANES] = ks_ref[...]
        kwp_ref[WIN_PAD * tq:, :] = kw_ref[...]
        for c in range(nq):
            vst_ref[SEL_PAD + c] = _transpose_bf16(vs_ref[c * tq:(c + 1) * tq, :])
            vwt_ref[WIN_PAD + c] = _transpose_bf16(vw_ref[c * tq:(c + 1) * tq, :])
        for c in range(n_cmp_pad // LANES):
            vct_ref[:, c * LANES:(c + 1) * LANES] = _transpose_bf16(vc_ref[c * LANES:(c + 1) * LANES, :])

    q = q_ref[...]
    gates_t = jax.nn.sigmoid(bg_ref[...]).T
    first_cmp = qi * cmp_per_tile
    cmp_row = lax.broadcasted_iota(jnp.int32, (n_cmp_pad, cols), 0)
    cmp_visible = cmp_row < first_cmp + cmp_per_tile
    blk = lax.broadcasted_iota(jnp.int32, (n_sel, tq), 0)
    cur = (qi * tq + lax.broadcasted_iota(jnp.int32, (n_sel, tq), 1)) // B_SEL_BLOCK
    forced = (blk == 0) | (blk == cur) | (blk == cur - 1)
    future = blk > cur
    halves = [slice(g * HEAD_DIM, (g + 1) * HEAD_DIM) for g in range(B_GROUPS)]

    qgs, o_cs, imps = [], [], []
    for g in range(B_GROUPS):
        qg = _group_queries_t(q, g, B_REP)
        sc_ref[g, CMP_BAND_BACK:, :] = _dot(kc_ref[...], qg)
        band_rows = pl.ds(pl.multiple_of(first_cmp, 8), band)
        sc_ref[g, band_rows, :] = sc_ref[g, band_rows, :] + band_ref[g]
        s = jnp.where(cmp_visible, sc_ref[g, CMP_BAND_BACK:, :], NEG)
        e = jnp.where(s > 0.5 * NEG, jnp.exp2(s - jnp.max(s, axis=0, keepdims=True)), 0.0)
        l = jnp.sum(e, axis=0, keepdims=True)
        p_c = e * jnp.where(l > 0.0, 1.0 / jnp.where(l > 0.0, l, 1.0), 0.0)
        o_cs.append(_dot(vct_ref[halves[g], :], p_c.astype(BF16)))
        p_sum = p_c[:, 0:tq]
        for r in range(1, B_REP):
            p_sum = p_sum + p_c[:, r * tq:(r + 1) * tq]
        imp = _dot(ovt_ref[...], p_sum.astype(BF16))
        imp = jnp.where(forced, B_SEL_FORCE, jnp.where(future, -B_SEL_FORCE, imp))
        qgs.append(qg)
        imps.append(imp)

    blk_f = blk.astype(F32)
    taken = [jnp.zeros((n_sel, tq), jnp.bool_) for _ in range(B_GROUPS)]
    left = list(imps)
    for _ in range(top_k):
        for g in range(B_GROUPS):
            best = jnp.max(left[g], axis=0, keepdims=True)
            first = jnp.min(jnp.where(left[g] == best, blk_f, float(n_sel)), axis=0, keepdims=True)
            pick = blk_f == first
            taken[g] = taken[g] | pick
            left[g] = jnp.where(pick, TAKEN, left[g])
    q_augs = []
    for g in range(B_GROUPS):
        sel_bias = jnp.concatenate([jnp.where(taken[g], 0.0, UNSEL),
                                    jnp.full((LANES - n_sel, tq), UNSEL, F32)], axis=0).astype(BF16)
        q_augs.append(jnp.concatenate([qgs[g], jnp.concatenate([sel_bias] * B_REP, axis=1)], axis=0))

    pair = cols
    chains = [(g, slice(0, cols)) for g in range(B_GROUPS)]
    heads_per_chain = pair // tq

    def sel_logits(first_tile):
        start = pl.multiple_of(first_tile * tq, tq)
        keys = kaug_ref[pl.ds(start, SEL_GROUP * tq), :]
        return tuple(_dot(keys, q_augs[g][:, cc]) for g, cc in chains)

    def sel_update(first_tile, logits, carries):
        return tuple(_online_update(carry, s, [vst_ref[first_tile + t, halves[g], :] for t in range(SEL_GROUP)])
                     for (g, cc), s, carry in zip(chains, logits, carries))

    start = pl.multiple_of(qi * tq, tq)
    win_keys = kwp_ref[pl.ds(start, (WIN_PAD + 1) * tq), :]
    partial = []
    for g, cc in chains:
        s_w = _dot(win_keys, qgs[g][:, cc])
        slabs = []
        for t in range(WIN_PAD + 1):
            slab = s_w[t * tq:(t + 1) * tq]
            if t == 0:
                slab = slab + edge_ref[:, cc]
            if t >= WIN_PAD - 1:
                slab = slab + near_ref[g, (t - WIN_PAD + 1) * tq:(t - WIN_PAD + 2) * tq, cc]
            if t < WIN_PAD:
                slab = slab + jnp.where(qi >= WIN_PAD - t, 0.0, NEG)
            slabs.append(slab)
        o_w = _softmax_result(_online_update(_softmax_init(HEAD_DIM, pair), jnp.concatenate(slabs, axis=0),
                                             [vwt_ref[qi + t, halves[g], :] for t in range(WIN_PAD + 1)]))
        o_c = o_cs[g][:, cc]
        heads = []
        for rr in range(heads_per_chain):
            hc = slice(rr * tq, (rr + 1) * tq)
            c0 = (g * B_REP + rr) * 3
            heads.append(gates_t[c0:c0 + 1] * o_c[:, hc] + gates_t[c0 + 2:c0 + 3] * o_w[:, hc])
        partial.append(heads)

    oc_ref[...] = _swa_tile(cq_ref[...], ckp_ref[...], ckd_ref[...], cvp_ref[...], cvd_ref[...],
                            swab_ref, sink_ref, qi, tq).astype(oc_ref.dtype)

    def sel_far(gi, carries, count):
        tiles = [qi - SEL_GROUP * (gi + c) for c in range(count)]
        fast, safe = carries, True
        for t, logits in [(t, sel_logits(t)) for t in tiles]:
            steps = [_one_pass_update(carry, s, [vst_ref[t + k, halves[g], :] for k in range(SEL_GROUP)])
                     for (g, cc), s, carry in zip(chains, logits, fast)]
            fast = tuple(step[0] for step in steps)
            for step in steps:
                safe = safe & step[1]

        def redo():
            slow = carries
            for t in tiles:
                slow = sel_update(t, sel_logits(t), slow)
            return slow

        return lax.cond(safe, lambda: fast, redo)

    far_rows = (SEL_GROUP - 2) * tq
    nearest = tuple(jnp.concatenate([s[0:far_rows], s[far_rows:] + near_ref[g, :, cc]], axis=0)
                    for (g, cc), s in zip(chains, sel_logits(qi)))
    sel = sel_update(qi, nearest, tuple(_softmax_init(HEAD_DIM, pair) for _ in chains))
    n_far = qi // SEL_GROUP
    sel = lax.fori_loop(0, n_far // 2, lambda p, c: sel_far(1 + 2 * p, c, 2), sel)
    sel = lax.fori_loop(0, n_far % 2, lambda _, c: sel_far(n_far, c, 1), sel)

    heads_out = []
    for ci, (g, cc) in enumerate(chains):
        o_s = _softmax_result(sel[ci])
        for rr in range(heads_per_chain):
            hc = slice(rr * tq, (rr + 1) * tq)
            c1 = (g * B_REP + rr) * 3 + 1
            heads_out.append(partial[ci][rr] + gates_t[c1:c1 + 1] * o_s[:, hc])
    o_ref[...] = jnp.concatenate(heads_out, axis=0).T.astype(o_ref.dtype)


def _nsa_swa_attention(zb, kc, vc, bgate, zc, tbl, sinks, ovt, bsz, seq, tq):
    nq = seq // tq
    n_cmp_pad = seq // B_CMP_STRIDE
    cols = B_REP * tq
    kv = lambda col: pl.BlockSpec((seq, LANES), lambda b, i: (b, col))
    cmp_spec = pl.BlockSpec((None, n_cmp_pad, B_KV), lambda b, i: (b, 0, 0))
    qcols = B_WIDTH // LANES
    ccols = C_WIDTH // LANES
    wide = lambda width: pl.BlockSpec((tq, width), lambda b, i: (b * nq + i, 0))
    prev = lambda col: pl.BlockSpec((tq, LANES), lambda b, i: (b * nq + jnp.maximum(i - 1, 0), col))
    diag = lambda col: pl.BlockSpec((tq, LANES), lambda b, i: (b * nq + i, col))
    return pl.pallas_call(
        functools.partial(_nsa_kernel, tq=tq, seq=seq),
        name="nsa_attn",
        grid=(bsz, nq),
        in_specs=[_smem(), _smem(),
                  wide(B_WIDTH),
                  kv(qcols), kv(qcols + 1), kv(qcols + 2), kv(qcols + 3),
                  cmp_spec, cmp_spec,
                  wide(LANES),
                  _resident(ovt.shape),
                  wide(C_WIDTH), prev(ccols), diag(ccols), prev(ccols + 1), diag(ccols + 1)],
        out_specs=[wide(B_WIDTH), wide(C_WIDTH)],
        out_shape=[jax.ShapeDtypeStruct((bsz * seq, B_WIDTH), BF16),
                   jax.ShapeDtypeStruct((bsz * seq, C_WIDTH), BF16)],
        scratch_shapes=[pltpu.VMEM((SEL_PAD * tq + seq, 2 * LANES), BF16),
                        pltpu.VMEM((WIN_PAD * tq + seq, LANES), BF16),
                        pltpu.VMEM((SEL_PAD + nq, B_KV, tq), BF16),
                        pltpu.VMEM((WIN_PAD + nq, B_KV, tq), BF16),
                        pltpu.VMEM((B_KV, n_cmp_pad), BF16),
                        pltpu.VMEM((B_GROUPS, 2 * tq, cols), F32),
                        pltpu.VMEM((tq, cols), F32),
                        pltpu.VMEM((B_GROUPS, CMP_BAND_BACK + tq // B_CMP_STRIDE, cols), F32),
                        pltpu.VMEM((B_GROUPS, CMP_BAND_BACK + n_cmp_pad, cols), F32),
                        pltpu.VMEM((C_GROUPS, 2 * tq, C_REP * tq), F32)],
        compiler_params=_params("arbitrary", "arbitrary"),
    )(tbl, sinks, zb, zb, zb, zb, zb, kc, vc, bgate, ovt, zc, zc, zc, zc, zc)


def _prep_w_in(w_in):
    sizes = (A_WIDTH, A_WIDTH, A_WIDTH, B_WIDTH, B_KV, B_KV, B_KV, B_KV, B_KV, B_KV, 3 * B_HEADS,
             C_WIDTH, C_KV, C_KV, D_MODEL, D_MODEL, D_MODEL)
    offs = np.concatenate([[0], np.cumsum(sizes)])
    (aq, ak, av, bq, bkc, bvc, bks, bvs, bkw, bvw, bgate, cq, ck, cv, ga, gb, gc) = [
        w_in[..., offs[i]:offs[i + 1]] for i in range(len(sizes))]
    scale = HEAD_DIM ** -0.5 * LOG2E
    bgate = jnp.pad(bgate, ((0, 0), (0, 0), (0, LANES - 3 * B_HEADS)))
    cols = [aq * scale, ak, av, bq * scale, bks, bvs, bkw, bvw, bkc, bvc, cq * scale, ck, cv, ga, gb, gc, bgate]
    return jnp.concatenate([c.astype(BF16) for c in cols], axis=-1)


def _prep_compress(cmp_pos, cmp_w1, cmp_w2):
    depth = cmp_pos.shape[0]
    half = B_CMP_STRIDE
    pos = cmp_pos.reshape(depth, 2, 2, half, 1, B_HEAD_DIM)
    pos = jnp.broadcast_to(pos, (depth, 2, 2, half, B_GROUPS, B_HEAD_DIM)).reshape(depth, 2, 2, 1, half * B_KV)
    w1 = cmp_w1.reshape(depth, 2, 2, half, 1, B_HEAD_DIM, 1, B_CMP_HIDDEN)
    eye = jnp.eye(B_GROUPS, dtype=cmp_w1.dtype).reshape(1, 1, 1, 1, B_GROUPS, 1, B_GROUPS, 1)
    w1 = (w1 * eye).reshape(depth, 2, 2, half * B_KV, B_GROUPS * B_CMP_HIDDEN)
    w2 = cmp_w2.reshape(depth, 2, 1, B_CMP_HIDDEN, 1, B_HEAD_DIM)
    eye2 = jnp.eye(B_GROUPS, dtype=cmp_w2.dtype).reshape(1, 1, B_GROUPS, 1, B_GROUPS, 1)
    w2 = (w2 * eye2).reshape(depth, 2, B_GROUPS * B_CMP_HIDDEN, B_KV)
    return pos.astype(F32), w1.astype(BF16), w2.astype(BF16)


def _overlap_t(seq):
    n_cmp_pad = seq // B_CMP_STRIDE
    start = np.arange(n_cmp_pad) * B_CMP_STRIDE
    sel = np.arange(seq // B_SEL_BLOCK) * B_SEL_BLOCK
    ov = (start[None, :] < sel[:, None] + B_SEL_BLOCK) & (start[None, :] + B_CMP_LEN > sel[:, None])
    ov &= (np.arange(n_cmp_pad) < n_cmp_pad - 1)[None, :]
    return jnp.asarray(ov.astype(np.float32), dtype=BF16)


def kernel(x, p, norm_g, ffn1_wi, ffn1_wo, w_in, diff_lambda, diff_subln, nsa_cmp_pos, nsa_cmp_w1,
           nsa_cmp_w2, swa_sinks, w_branch, w_out, ffn2_wi, ffn2_wo, w_ple, w_ple_gate, rel_bias, final_norm):
    bsz, seq, _ = x.shape
    depth = norm_g.shape[0]
    tokens = bsz * seq
    assert seq % 1024 == 0 and seq // B_SEL_BLOCK <= LANES
    tm, tm_tail, tq_a, tq = 512, 256, 256, 128
    assert FAR_DIST + B_CMP_LEN - 1 <= CMP_BAND_BACK * B_CMP_STRIDE and FAR_DIST <= tq
    assert WIN_PAD * tq == B_WINDOW and tq == C_WINDOW and seq // B_SEL_BLOCK < LANES

    wi1, wo1 = ffn1_wi.astype(BF16), ffn1_wo.astype(BF16)
    wi2, wo2 = ffn2_wi.astype(BF16), ffn2_wo.astype(BF16)
    w_in_r = _prep_w_in(w_in)
    cmp_pos, cmp_w1, cmp_w2 = _prep_compress(nsa_cmp_pos, nsa_cmp_w1, nsa_cmp_w2)
    wb, wo = w_branch.astype(BF16), w_out.astype(BF16)
    wpg, wpe = w_ple_gate.astype(BF16), w_ple.astype(BF16)
    tbl = rel_bias.astype(F32)
    ovt = _overlap_t(seq)
    chunk_rows = seq // B_CMP_STRIDE

    h = x.reshape(tokens, D_MODEL)
    p_rows = p.reshape(depth, tokens, PLE_DIM)
    for i in range(depth):
        h, za, zb, zkc, zvc, zc, gates, bgate = _head(h, norm_g[i, 0], wi1, wo1, norm_g[i, 1], w_in_r, tm_tail, i)
        lam_init = 0.8 - 0.6 * math.exp(-0.3 * i)
        ya = _diff_attention(za, tbl, diff_lambda[i].astype(F32), diff_subln[i], bsz, seq, lam_init, tq_a)
        kc, vc = _compress(zkc.reshape(bsz, chunk_rows, B_CMP_STRIDE * B_KV),
                           zvc.reshape(bsz, chunk_rows, B_CMP_STRIDE * B_KV), cmp_pos, cmp_w1, cmp_w2, i)
        yb, yc = _nsa_swa_attention(zb, kc, vc, bgate, zc, tbl, swa_sinks[i].astype(F32), ovt, bsz, seq, tq)
        h = _tail(h, ya, yb, yc, gates, p_rows, wb, wo, norm_g[i, 2], wi2, wo2, norm_g[i, 3], wpg, wpe,
                  final_norm, tm_tail, i, final=(i == depth - 1))
    return h.reshape(bsz, seq, D_MODEL)
```

```python
import functools
import math

import numpy as np
import jax
import jax.numpy as jnp
from jax import lax
from jax.experimental import pallas as pl
from jax.experimental.pallas import tpu as pltpu

F32 = jnp.float32
BF16 = jnp.bfloat16

D_MODEL = 1024
PLE_DIM = 256
D_FF = 2816
NORM_EPS = 1e-6
NEG = -1e30
NUM_BUCKETS = 32
MAX_DISTANCE = 128
A_HEADS = 4
A_HEAD_DIM = 64
B_HEADS = 8
B_GROUPS = 2
B_REP = B_HEADS // B_GROUPS
B_HEAD_DIM = 64
B_CMP_LEN = 32
B_CMP_STRIDE = 16
B_CMP_HIDDEN = 256
B_SEL_BLOCK = 64
B_SEL_TOPK = 16
B_WINDOW = 512
B_SEL_FORCE = 1e6
C_HEADS = 8
C_GROUPS = 2
C_REP = C_HEADS // C_GROUPS
C_HEAD_DIM = 64
C_WINDOW = 128
A_WIDTH = A_HEADS * 2 * A_HEAD_DIM
B_WIDTH = B_HEADS * B_HEAD_DIM
B_KV = B_GROUPS * B_HEAD_DIM
C_WIDTH = C_HEADS * C_HEAD_DIM
C_KV = C_GROUPS * C_HEAD_DIM

LANES = 128
SUBLANES = 8
HEAD_DIM = 64
VMEM_LIMIT = 56 * 1024 * 1024
TOKEN_TILE = 256
A_QUERY_TILE = 256
BC_QUERY_TILE = 128
UNSEL = -float(2 ** 30)
TAKEN = -3e38
FFN_CHUNKS = ((0, 1536), (1536, 1280))
CMP_BAND_BACK = 16
LOG2E = math.log2(math.e)
ONES_ROWS = 16
EXP_HEADROOM = 64.0


def _bucket_thresholds():
    n = np.arange(4 * MAX_DISTANCE)
    max_exact = NUM_BUCKETS // 2
    nf = np.maximum(n, 1).astype(np.float32)
    large = max_exact + (np.log(nf / max_exact) / math.log(MAX_DISTANCE / max_exact)
                         * (NUM_BUCKETS - max_exact)).astype(np.int32)
    bucket = np.where(n < max_exact, n, np.minimum(large, NUM_BUCKETS - 1))
    out = []
    for b in range(1, NUM_BUCKETS):
        hit = np.nonzero(bucket == b)[0]
        if hit.size:
            out.append((b, int(hit[0])))
    return tuple(out)


BUCKET_LO = _bucket_thresholds()
FAR_DIST = BUCKET_LO[-1][1]


def _bias_of_dist(dist, tbl_ref, head, minus_far=False):
    shift = tbl_ref[NUM_BUCKETS - 1, head] if minus_far else 0.0
    out = jnp.full(dist.shape, (tbl_ref[0, head] - shift) * LOG2E, F32)
    for b, lo in BUCKET_LO:
        out = jnp.where(dist >= lo, (tbl_ref[b, head] - shift) * LOG2E, out)
    return out


def _rms(x, g):
    return x * lax.rsqrt(jnp.mean(x * x, axis=-1, keepdims=True) + NORM_EPS) * g


def _dot(a, b):
    return jnp.dot(a, b, preferred_element_type=F32)


def _values_and_ones(vts):
    values = jnp.concatenate(vts, axis=1)
    return jnp.concatenate([values, jnp.ones((ONES_ROWS, values.shape[1]), BF16)], axis=0)


def _online_update(carry, s, vts):
    m, acc = carry
    m_new = jnp.maximum(m, jnp.max(s, axis=0, keepdims=True))
    p = jnp.exp2(s - m_new).astype(BF16)
    return m_new, jnp.exp2(m - m_new) * acc + _dot(_values_and_ones(vts), p)


def _one_pass_update(carry, s, vts):
    m, acc = carry
    top = jnp.max(s, axis=0, keepdims=True)
    p = jnp.exp2(s - m).astype(BF16)
    m_new = jnp.maximum(m, top)
    return (m_new, jnp.exp2(m - m_new) * (acc + _dot(_values_and_ones(vts), p))), jnp.max(top - m) <= EXP_HEADROOM


def _softmax_init(width, queries):
    return jnp.full((1, queries), NEG, F32), jnp.zeros((width + ONES_ROWS, queries), F32)


def _softmax_result(carry):
    _, acc = carry
    width = acc.shape[0] - ONES_ROWS
    return acc[0:width] / acc[width:width + 1]


def _transpose_bf16(x):
    return x.astype(F32).T.astype(BF16)


def _params(*sem):
    return pltpu.CompilerParams(dimension_semantics=sem, vmem_limit_bytes=VMEM_LIMIT)


def _resident(shape, layer=None):
    if layer is None:
        zeros = (0,) * len(shape)
        return pl.BlockSpec(shape, lambda *_: zeros, pipeline_mode=pl.Buffered(1))
    index = (layer,) + (0,) * (len(shape) - 1)
    return pl.BlockSpec((None,) + tuple(shape[1:]), lambda *_: index, pipeline_mode=pl.Buffered(1))


def _smem():
    return pl.BlockSpec(memory_space=pltpu.SMEM)


def _half_ffn(x, g_ref, wi_ref, wo_ref):
    n = _rms(x, g_ref[...]).astype(BF16)
    acc = x
    for lo, width in FFN_CHUNKS:
        gate = _dot(n, wi_ref[:, lo:lo + width])
        up = _dot(n, wi_ref[:, D_FF + lo:D_FF + lo + width])
        act = (gate * jax.nn.sigmoid(gate) * up).astype(BF16)
        acc = acc + 0.5 * _dot(act, wo_ref[lo:lo + width, :])
    return acc


IN_A = 3 * A_WIDTH
IN_B = B_WIDTH + 4 * B_KV
IN_C = C_WIDTH + 2 * C_KV
IN_G = 3 * D_MODEL
IN_SECTIONS = (IN_A, IN_B, B_KV, B_KV, IN_C, IN_G, LANES)


def _head_kernel(h_ref, g0_ref, wi_ref, wo_ref, g1_ref, w_ref,
                 o_ref, za_ref, zb_ref, zkc_ref, zvc_ref, zc_ref, gates_ref, bg_ref):
    x = _half_ffn(h_ref[...], g0_ref, wi_ref, wo_ref)
    o_ref[...] = x
    n = _rms(x, g1_ref[...]).astype(BF16)
    lo = 0
    for ref, width in zip((za_ref, zb_ref, zkc_ref, zvc_ref, zc_ref, gates_ref, bg_ref), IN_SECTIONS):
        ref[...] = _dot(n, w_ref[:, lo:lo + width]).astype(ref.dtype)
        lo += width


def _head(h, g0, wi, wo, g1, w, tm, layer):
    t = h.shape[0]
    dtypes = (BF16, BF16, BF16, BF16, BF16, F32, F32)
    rows = lambda width: pl.BlockSpec((tm, width), lambda i: (i, 0))
    vec = _resident((1, D_MODEL))
    return pl.pallas_call(
        _head_kernel,
        name="head",
        grid=(t // tm,),
        in_specs=[rows(D_MODEL), vec, _resident(wi.shape, layer), _resident(wo.shape, layer), vec,
                  _resident(w.shape, layer)],
        out_specs=[rows(D_MODEL)] + [rows(width) for width in IN_SECTIONS],
        out_shape=[jax.ShapeDtypeStruct(h.shape, F32)]
                  + [jax.ShapeDtypeStruct((t, width), dt) for width, dt in zip(IN_SECTIONS, dtypes)],
        compiler_params=_params("arbitrary"),
    )(h, g0.reshape(1, D_MODEL), wi, wo, g1.reshape(1, D_MODEL), w)


def _tail_kernel(h_ref, ya_ref, yb_ref, yc_ref, gates_ref, p_ref, wb_ref, wo_ref, g2_ref, wi_ref, wo2_ref,
                 g3_ref, wg_ref, we_ref, gf_ref, o_ref, *, final):
    merged = jnp.zeros(h_ref.shape, F32)
    for m, y_ref in enumerate((ya_ref, yb_ref, yc_ref)):
        gate = jax.nn.sigmoid(gates_ref[:, m * D_MODEL:(m + 1) * D_MODEL])
        merged = merged + gate * _dot(y_ref[...], wb_ref[m])
    x = h_ref[...] + _dot(merged.astype(BF16), wo_ref[...])
    x = _half_ffn(x, g2_ref, wi_ref, wo2_ref)
    gate = jax.nn.sigmoid(_dot(_rms(x, g3_ref[...]).astype(BF16), wg_ref[...]))
    x = x + gate * _dot(p_ref[...].astype(BF16), we_ref[...])
    if final:
        x = _rms(x, gf_ref[...])
    o_ref[...] = x


def _tail(h, ya, yb, yc, gates, p, wb, wo, g2, wi, wo2, g3, wg, we, gf, tm, layer, final):
    t = h.shape[0]
    rows = lambda width: pl.BlockSpec((tm, width), lambda i: (i, 0))
    vec = _resident((1, D_MODEL))
    return pl.pallas_call(
        functools.partial(_tail_kernel, final=final),
        name="tail",
        grid=(t // tm,),
        in_specs=[rows(D_MODEL), rows(A_WIDTH), rows(B_WIDTH), rows(C_WIDTH), rows(IN_G),
                  pl.BlockSpec((None, tm, PLE_DIM), lambda i: (layer, i, 0)),
                  _resident(wb.shape, layer), _resident(wo.shape, layer), vec,
                  _resident(wi.shape, layer), _resident(wo2.shape, layer),
                  vec, _resident(wg.shape, layer), _resident(we.shape, layer), vec],
        out_specs=rows(D_MODEL),
        out_shape=jax.ShapeDtypeStruct(h.shape, F32),
        compiler_params=_params("arbitrary"),
    )(h, ya, yb, yc, gates, p, wb, wo, g2.reshape(1, D_MODEL), wi, wo2, g3.reshape(1, D_MODEL), wg, we,
      gf.reshape(1, D_MODEL))


def _diff_kernel(tbl_ref, lam_ref, subg_ref, q_ref, k_ref, v_ref, o_ref, kaug_ref, vt_ref, near_ref,
                 *, tq, lam_init):
    qi = pl.program_id(1)
    width = 2 * A_HEAD_DIM
    n_heads = A_HEADS

    @pl.when((pl.program_id(0) == 0) & (qi == 0))
    def _static_setup():
        lane = lax.broadcasted_iota(jnp.int32, (tq, 2 * LANES), 1)
        key = lax.broadcasted_iota(jnp.int32, (2 * tq, tq), 0)
        qry = lax.broadcasted_iota(jnp.int32, (2 * tq, tq), 1)
        dist = tq + qry - key
        for h in range(n_heads):
            kaug_ref[h, 0:tq, :] = jnp.where(lane == LANES, 1.0, 0.0).astype(BF16)
            kaug_ref[h, tq:, LANES:2 * LANES] = jnp.zeros((kaug_ref.shape[1] - tq, LANES), BF16)
            vt_ref[h, 0] = jnp.zeros(vt_ref.shape[2:], BF16)
            near_ref[h] = jnp.where(dist >= 0, _bias_of_dist(dist, tbl_ref, h, True), NEG)

    @pl.when(qi == 0)
    def _per_batch_setup():
        for h in range(n_heads):
            cols = slice(h * width, (h + 1) * width)
            kaug_ref[h, tq:, 0:LANES] = k_ref[:, cols]
            for c in range(vt_ref.shape[1] - 1):
                vt_ref[h, c + 1] = _transpose_bf16(v_ref[c * tq:(c + 1) * tq, cols])

    zero = jnp.zeros((A_HEAD_DIM, tq), F32)
    flag_row = lax.broadcasted_iota(jnp.int32, (LANES, 2 * tq), 0) == 0
    flags = jnp.where(flag_row, UNSEL, 0.0)
    queries = []
    for h in range(n_heads):
        qt = q_ref[:, h * width:(h + 1) * width].astype(F32).T
        both = jnp.concatenate([jnp.concatenate([qt[0:A_HEAD_DIM], zero], axis=0),
                                jnp.concatenate([zero, qt[A_HEAD_DIM:width]], axis=0)], axis=1)
        queries.append(jnp.concatenate([both, flags], axis=0).astype(BF16))

    def logits_of(first_tile):
        start = pl.multiple_of(first_tile * tq, tq)
        return tuple(_dot(kaug_ref[h, pl.ds(start, 2 * tq), :], queries[h]) for h in range(n_heads))

    def update(first_tile, logits, carries):
        return tuple(_online_update(carries[h], logits[h], [vt_ref[h, first_tile], vt_ref[h, first_tile + 1]])
                     for h in range(n_heads))

    def far(gi, carries, count):
        tiles = [qi - 2 * (gi + c) for c in range(count)]
        fast, safe = carries, True
        for t, logits in [(t, logits_of(t)) for t in tiles]:
            steps = [_one_pass_update(fast[h], logits[h], [vt_ref[h, t], vt_ref[h, t + 1]]) for h in range(n_heads)]
            fast = tuple(step[0] for step in steps)
            for step in steps:
                safe = safe & step[1]

        def redo():
            slow = carries
            for t in tiles:
                slow = update(t, logits_of(t), slow)
            return slow

        return lax.cond(safe, lambda: fast, redo)

    nearest = tuple(s + jnp.concatenate([near_ref[h], near_ref[h]], axis=1) for h, s in enumerate(logits_of(qi)))
    carries = update(qi, nearest, tuple(_softmax_init(width, 2 * tq) for _ in range(n_heads)))
    n_far = qi // 2
    carries = lax.fori_loop(0, n_far // 2, lambda p, c: far(1 + 2 * p, c, 2), carries)
    carries = lax.fori_loop(0, n_far % 2, lambda _, c: far(n_far, c, 1), carries)
    lp = lam_ref[...]
    lam = (jnp.exp(jnp.sum(lp[0:1] * lp[1:2], axis=-1, keepdims=True))
           - jnp.exp(jnp.sum(lp[2:3] * lp[3:4], axis=-1, keepdims=True)) + lam_init)
    for h in range(n_heads):
        out = _softmax_result(carries[h])
        o = out[:, 0:tq] - lam * out[:, tq:2 * tq]
        o = o * lax.rsqrt(jnp.mean(o * o, axis=0, keepdims=True) + NORM_EPS) * (1.0 - lam_init)
        o_ref[:, h * width:(h + 1) * width] = (o.T * subg_ref[...]).astype(o_ref.dtype)


def _diff_attention(za, tbl, lam_p, subg, bsz, seq, lam_init, tq):
    nq = seq // tq
    width = 2 * A_HEAD_DIM
    return pl.pallas_call(
        functools.partial(_diff_kernel, tq=tq, lam_init=lam_init),
        name="diff_attn",
        grid=(bsz, nq),
        in_specs=[_smem(), _resident(lam_p.shape), _resident((1, width)),
                  pl.BlockSpec((tq, A_WIDTH), lambda b, i: (b * nq + i, 0)),
                  pl.BlockSpec((seq, A_WIDTH), lambda b, i: (b, 1)),
                  pl.BlockSpec((seq, A_WIDTH), lambda b, i: (b, 2))],
        out_specs=pl.BlockSpec((tq, A_WIDTH), lambda b, i: (b * nq + i, 0)),
        out_shape=jax.ShapeDtypeStruct((bsz * seq, A_WIDTH), BF16),
        scratch_shapes=[pltpu.VMEM((A_HEADS, tq + seq, 2 * LANES), BF16),
                        pltpu.VMEM((A_HEADS, nq + 1, width, tq), BF16),
                        pltpu.VMEM((A_HEADS, 2 * tq, tq), F32)],
        compiler_params=_params("arbitrary", "arbitrary"),
    )(tbl, lam_p, subg.reshape(1, width), za, za, za)


def _compress_kernel(xk_ref, xv_ref, pos_ref, w1_ref, w2_ref, kc_ref, vc_ref):
    rows = xk_ref.shape[0]
    for t, (x_ref, o_ref) in enumerate(((xk_ref, kc_ref), (xv_ref, vc_ref))):
        x = x_ref[...].astype(F32)
        first = _dot((x + pos_ref[t, 0]).astype(BF16), w1_ref[t, 0])
        second = _dot((x + pos_ref[t, 1]).astype(BF16), w1_ref[t, 1])
        hidden = first + pltpu.roll(second, rows - 1, 0)
        o_ref[...] = _dot(jax.nn.gelu(hidden).astype(BF16), w2_ref[t]).astype(o_ref.dtype)


def _compress(xk, xv, pos, w1, w2, layer):
    bsz, rows, width = xk.shape
    xspec = pl.BlockSpec((None, rows, width), lambda b: (b, 0, 0))
    ospec = pl.BlockSpec((None, rows, B_KV), lambda b: (b, 0, 0))
    return pl.pallas_call(
        _compress_kernel,
        name="nsa_compress",
        grid=(bsz,),
        in_specs=[xspec, xspec, _resident(pos.shape, layer), _resident(w1.shape, layer), _resident(w2.shape, layer)],
        out_specs=[ospec, ospec],
        out_shape=[jax.ShapeDtypeStruct((bsz, rows, B_KV), BF16)] * 2,
        compiler_params=_params("arbitrary"),
    )(xk, xv, pos, w1, w2)


def _group_queries_t(q, group, rep):
    tq = q.shape[0]
    zero = jnp.zeros((HEAD_DIM, tq), F32)
    cols = []
    for r in range(rep):
        head = group * rep + r
        slab = q[:, (head // 2) * LANES:(head // 2 + 1) * LANES].astype(F32).T
        part = slab[(head % 2) * HEAD_DIM:(head % 2 + 1) * HEAD_DIM]
        cols.append(jnp.concatenate([part, zero] if group == 0 else [zero, part], axis=0))
    return jnp.concatenate(cols, axis=1).astype(BF16)


def _build_near_bias(near_ref, tbl_ref, head0, groups, rep, tq, window_is_tile, minus_far):
    key = lax.broadcasted_iota(jnp.int32, (2 * tq, tq), 0)
    qry = lax.broadcasted_iota(jnp.int32, (2 * tq, tq), 1)
    dist = tq + qry - key
    visible = (dist >= 0) & (dist < tq) if window_is_tile else dist >= 0
    for g in range(groups):
        for r in range(rep):
            head = head0 + g * rep + r
            near_ref[g, :, r * tq:(r + 1) * tq] = jnp.where(visible, _bias_of_dist(dist, tbl_ref, head, minus_far), NEG)


def _swa_tile(q, k_prev, k_diag, v_prev, v_diag, bias_ref, sink_ref, qi, tq):
    cols = C_REP * tq
    prev_bias = jnp.where(qi > 0, 0.0, NEG)
    vp_t = _transpose_bf16(v_prev)
    vd_t = _transpose_bf16(v_diag)
    heads_out = []
    for g in range(C_GROUPS):
        half = slice(g * HEAD_DIM, (g + 1) * HEAD_DIM)
        qg = _group_queries_t(q, g, C_REP)
        s = jnp.concatenate([_dot(k_prev, qg) + bias_ref[g, 0:tq, :] + prev_bias,
                             _dot(k_diag, qg) + bias_ref[g, tq:2 * tq, :]], axis=0)
        m, acc = _online_update(_softmax_init(HEAD_DIM, cols), s, [vp_t[half], vd_t[half]])
        for r in range(C_REP):
            cc = slice(r * tq, (r + 1) * tq)
            sink = sink_ref[g * C_REP + r] * LOG2E
            m_all = jnp.maximum(m[:, cc], sink)
            scale = jnp.exp2(m[:, cc] - m_all)
            denom = acc[HEAD_DIM:HEAD_DIM + 1, cc] * scale + jnp.exp2(sink - m_all)
            heads_out.append(acc[0:HEAD_DIM, cc] * (scale / denom))
    return jnp.concatenate(heads_out, axis=0).T


SEL_GROUP = 4
SEL_PAD = SEL_GROUP - 1
WIN_PAD = B_WINDOW // BC_QUERY_TILE


def _nsa_kernel(tbl_ref, sink_ref, q_ref, ks_ref, vs_ref, kw_ref, vw_ref, kc_ref, vc_ref, bg_ref, ovt_ref,
                cq_ref, ckp_ref, ckd_ref, cvp_ref, cvd_ref, o_ref, oc_ref,
                kaug_ref, kwp_ref, vst_ref, vwt_ref, vct_ref, near_ref, edge_ref, band_ref, sc_ref, swab_ref,
                *, tq, seq):
    first_batch = pl.program_id(0) == 0
    qi = pl.program_id(1)
    n_cmp_pad = seq // B_CMP_STRIDE
    n_sel = seq // B_SEL_BLOCK
    top_k = min(B_SEL_TOPK, n_sel)
    nq = seq // tq
    cols = B_REP * tq
    cmp_per_tile = tq // B_CMP_STRIDE
    band = CMP_BAND_BACK + cmp_per_tile

    @pl.when(first_batch & (qi == 0))
    def _static_setup():
        _build_near_bias(near_ref, tbl_ref, A_HEADS, B_GROUPS, B_REP, tq, False, True)
        _build_near_bias(swab_ref, tbl_ref, A_HEADS + B_HEADS, C_GROUPS, C_REP, tq, True, False)
        key = lax.broadcasted_iota(jnp.int32, (tq, cols), 0)
        qry = lax.broadcasted_iota(jnp.int32, (tq, cols), 1) % tq
        edge_ref[...] = jnp.where(qry < key, 0.0, NEG)
        blk = lax.broadcasted_iota(jnp.int32, (band, tq), 0) - CMP_BAND_BACK
        dist = lax.broadcasted_iota(jnp.int32, (band, tq), 1) - (blk * B_CMP_STRIDE + B_CMP_LEN - 1)
        for g in range(B_GROUPS):
            for r in range(B_REP):
                head = A_HEADS + g * B_REP + r
                band_ref[g, :, r * tq:(r + 1) * tq] = jnp.where(dist >= 0, _bias_of_dist(dist, tbl_ref, head, True), NEG)
        sc_ref[:, 0:CMP_BAND_BACK, :] = jnp.zeros((B_GROUPS, CMP_BAND_BACK, cols), F32)
        pad = SEL_PAD * tq
        row = lax.broadcasted_iota(jnp.int32, (pad + seq, LANES), 0)
        lane = lax.broadcasted_iota(jnp.int32, (pad + seq, LANES), 1)
        blk_id = jnp.where(row < pad, LANES - 1, (row - pad) // B_SEL_BLOCK)
        kaug_ref[:, LANES:2 * LANES] = jnp.where(blk_id == lane, 1.0, 0.0).astype(BF16)
        kaug_ref[0:pad, 0:LANES] = jnp.zeros((pad, LANES), BF16)
        kwp_ref[0:WIN_PAD * tq, :] = jnp.zeros((WIN_PAD * tq, LANES), BF16)
        for c in range(SEL_PAD):
            vst_ref[c] = jnp.zeros(vst_ref.shape[1:], BF16)
        for c in range(WIN_PAD):
            vwt_ref[c] = jnp.zeros(vwt_ref.shape[1:], BF16)

    @pl.when(qi == 0)
    def _per_batch_setup():
        kaug_ref[SEL_PAD * tq:, 0:LANES] = ks_ref[...]
        kwp_ref[WIN_PAD * tq:, :] = kw_ref[...]
        for c in range(nq):
            vst_ref[SEL_PAD + c] = _transpose_bf16(vs_ref[c * tq:(c + 1) * tq, :])
            vwt_ref[WIN_PAD + c] = _transpose_bf16(vw_ref[c * tq:(c + 1) * tq, :])
        for c in range(n_cmp_pad // LANES):
            vct_ref[:, c * LANES:(c + 1) * LANES] = _transpose_bf16(vc_ref[c * LANES:(c + 1) * LANES, :])

    q = q_ref[...]
    gates_t = jax.nn.sigmoid(bg_ref[...]).T
    first_cmp = qi * cmp_per_tile
    cmp_row = lax.broadcasted_iota(jnp.int32, (n_cmp_pad, cols), 0)
    cmp_visible = cmp_row < first_cmp + cmp_per_tile
    blk = lax.broadcasted_iota(jnp.int32, (n_sel, tq), 0)
    cur = (qi * tq + lax.broadcasted_iota(jnp.int32, (n_sel, tq), 1)) // B_SEL_BLOCK
    forced = (blk == 0) | (blk == cur) | (blk == cur - 1)
    future = blk > cur
    halves = [slice(g * HEAD_DIM, (g + 1) * HEAD_DIM) for g in range(B_GROUPS)]

    qgs, o_cs, imps = [], [], []
    for g in range(B_GROUPS):
        qg = _group_queries_t(q, g, B_REP)
        sc_ref[g, CMP_BAND_BACK:, :] = _dot(kc_ref[...], qg)
        band_rows = pl.ds(pl.multiple_of(first_cmp, SUBLANES), band)
        sc_ref[g, band_rows, :] = sc_ref[g, band_rows, :] + band_ref[g]
        s = jnp.where(cmp_visible, sc_ref[g, CMP_BAND_BACK:, :], NEG)
        e = jnp.where(s > 0.5 * NEG, jnp.exp2(s - jnp.max(s, axis=0, keepdims=True)), 0.0)
        l = jnp.sum(e, axis=0, keepdims=True)
        p_c = e * jnp.where(l > 0.0, 1.0 / jnp.where(l > 0.0, l, 1.0), 0.0)
        o_cs.append(_dot(vct_ref[halves[g], :], p_c.astype(BF16)))
        p_sum = p_c[:, 0:tq]
        for r in range(1, B_REP):
            p_sum = p_sum + p_c[:, r * tq:(r + 1) * tq]
        imp = _dot(ovt_ref[...], p_sum.astype(BF16))
        imp = jnp.where(forced, B_SEL_FORCE, jnp.where(future, -B_SEL_FORCE, imp))
        qgs.append(qg)
        imps.append(imp)

    blk_f = blk.astype(F32)
    taken = [jnp.zeros((n_sel, tq), jnp.bool_) for _ in range(B_GROUPS)]
    left = list(imps)
    for _ in range(top_k):
        for g in range(B_GROUPS):
            best = jnp.max(left[g], axis=0, keepdims=True)
            first = jnp.min(jnp.where(left[g] == best, blk_f, float(n_sel)), axis=0, keepdims=True)
            pick = blk_f == first
            taken[g] = taken[g] | pick
            left[g] = jnp.where(pick, TAKEN, left[g])
    q_augs = []
    for g in range(B_GROUPS):
        sel_bias = jnp.concatenate([jnp.where(taken[g], 0.0, UNSEL),
                                    jnp.full((LANES - n_sel, tq), UNSEL, F32)], axis=0).astype(BF16)
        q_augs.append(jnp.concatenate([qgs[g], jnp.concatenate([sel_bias] * B_REP, axis=1)], axis=0))

    pair = cols
    chains = [(g, slice(0, cols)) for g in range(B_GROUPS)]
    heads_per_chain = pair // tq

    def sel_logits(first_tile):
        start = pl.multiple_of(first_tile * tq, tq)
        keys = kaug_ref[pl.ds(start, SEL_GROUP * tq), :]
        return tuple(_dot(keys, q_augs[g][:, cc]) for g, cc in chains)

    def sel_update(first_tile, logits, carries):
        return tuple(_online_update(carry, s, [vst_ref[first_tile + t, halves[g], :] for t in range(SEL_GROUP)])
                     for (g, cc), s, carry in zip(chains, logits, carries))

    start = pl.multiple_of(qi * tq, tq)
    win_keys = kwp_ref[pl.ds(start, (WIN_PAD + 1) * tq), :]
    partial = []
    for g, cc in chains:
        s_w = _dot(win_keys, qgs[g][:, cc])
        slabs = []
        for t in range(WIN_PAD + 1):
            slab = s_w[t * tq:(t + 1) * tq]
            if t == 0:
                slab = slab + edge_ref[:, cc]
            if t >= WIN_PAD - 1:
                slab = slab + near_ref[g, (t - WIN_PAD + 1) * tq:(t - WIN_PAD + 2) * tq, cc]
            if t < WIN_PAD:
                slab = slab + jnp.where(qi >= WIN_PAD - t, 0.0, NEG)
            slabs.append(slab)
        o_w = _softmax_result(_online_update(_softmax_init(HEAD_DIM, pair), jnp.concatenate(slabs, axis=0),
                                             [vwt_ref[qi + t, halves[g], :] for t in range(WIN_PAD + 1)]))
        o_c = o_cs[g][:, cc]
        heads = []
        for rr in range(heads_per_chain):
            hc = slice(rr * tq, (rr + 1) * tq)
            c0 = (g * B_REP + rr) * 3
            heads.append(gates_t[c0:c0 + 1] * o_c[:, hc] + gates_t[c0 + 2:c0 + 3] * o_w[:, hc])
        partial.append(heads)

    oc_ref[...] = _swa_tile(cq_ref[...], ckp_ref[...], ckd_ref[...], cvp_ref[...], cvd_ref[...],
                            swab_ref, sink_ref, qi, tq).astype(oc_ref.dtype)

    def sel_far(gi, carries, count):
        tiles = [qi - SEL_GROUP * (gi + c) for c in range(count)]
        fast, safe = carries, True
        for t, logits in [(t, sel_logits(t)) for t in tiles]:
            steps = [_one_pass_update(carry, s, [vst_ref[t + k, halves[g], :] for k in range(SEL_GROUP)])
                     for (g, cc), s, carry in zip(chains, logits, fast)]
            fast = tuple(step[0] for step in steps)
            for step in steps:
                safe = safe & step[1]

        def redo():
            slow = carries
            for t in tiles:
                slow = sel_update(t, sel_logits(t), slow)
            return slow

        return lax.cond(safe, lambda: fast, redo)

    far_rows = (SEL_GROUP - 2) * tq
    nearest = tuple(jnp.concatenate([s[0:far_rows], s[far_rows:] + near_ref[g, :, cc]], axis=0)
                    for (g, cc), s in zip(chains, sel_logits(qi)))
    sel = sel_update(qi, nearest, tuple(_softmax_init(HEAD_DIM, pair) for _ in chains))
    n_far = qi // SEL_GROUP
    sel = lax.fori_loop(0, n_far // 2, lambda p, c: sel_far(1 + 2 * p, c, 2), sel)
    sel = lax.fori_loop(0, n_far % 2, lambda _, c: sel_far(n_far, c, 1), sel)

    heads_out = []
    for ci, (g, cc) in enumerate(chains):
        o_s = _softmax_result(sel[ci])
        for rr in range(heads_per_chain):
            hc = slice(rr * tq, (rr + 1) * tq)
            c1 = (g * B_REP + rr) * 3 + 1
            heads_out.append(partial[ci][rr] + gates_t[c1:c1 + 1] * o_s[:, hc])
    o_ref[...] = jnp.concatenate(heads_out, axis=0).T.astype(o_ref.dtype)


def _nsa_swa_attention(zb, kc, vc, bgate, zc, tbl, sinks, ovt, bsz, seq, tq):
    nq = seq // tq
    n_cmp_pad = seq // B_CMP_STRIDE
    cols = B_REP * tq
    kv = lambda col: pl.BlockSpec((seq, LANES), lambda b, i: (b, col))
    cmp_spec = pl.BlockSpec((None, n_cmp_pad, B_KV), lambda b, i: (b, 0, 0))
    qcols = B_WIDTH // LANES
    ccols = C_WIDTH // LANES
    wide = lambda width: pl.BlockSpec((tq, width), lambda b, i: (b * nq + i, 0))
    prev = lambda col: pl.BlockSpec((tq, LANES), lambda b, i: (b * nq + jnp.maximum(i - 1, 0), col))
    diag = lambda col: pl.BlockSpec((tq, LANES), lambda b, i: (b * nq + i, col))
    return pl.pallas_call(
        functools.partial(_nsa_kernel, tq=tq, seq=seq),
        name="nsa_attn",
        grid=(bsz, nq),
        in_specs=[_smem(), _smem(),
                  wide(B_WIDTH),
                  kv(qcols), kv(qcols + 1), kv(qcols + 2), kv(qcols + 3),
                  cmp_spec, cmp_spec,
                  wide(LANES),
                  _resident(ovt.shape),
                  wide(C_WIDTH), prev(ccols), diag(ccols), prev(ccols + 1), diag(ccols + 1)],
        out_specs=[wide(B_WIDTH), wide(C_WIDTH)],
        out_shape=[jax.ShapeDtypeStruct((bsz * seq, B_WIDTH), BF16),
                   jax.ShapeDtypeStruct((bsz * seq, C_WIDTH), BF16)],
        scratch_shapes=[pltpu.VMEM((SEL_PAD * tq + seq, 2 * LANES), BF16),
                        pltpu.VMEM((WIN_PAD * tq + seq, LANES), BF16),
                        pltpu.VMEM((SEL_PAD + nq, B_KV, tq), BF16),
                        pltpu.VMEM((WIN_PAD + nq, B_KV, tq), BF16),
                        pltpu.VMEM((B_KV, n_cmp_pad), BF16),
                        pltpu.VMEM((B_GROUPS, 2 * tq, cols), F32),
                        pltpu.VMEM((tq, cols), F32),
                        pltpu.VMEM((B_GROUPS, CMP_BAND_BACK + tq // B_CMP_STRIDE, cols), F32),
                        pltpu.VMEM((B_GROUPS, CMP_BAND_BACK + n_cmp_pad, cols), F32),
                        pltpu.VMEM((C_GROUPS, 2 * tq, C_REP * tq), F32)],
        compiler_params=_params("arbitrary", "arbitrary"),
    )(tbl, sinks, zb, zb, zb, zb, zb, kc, vc, bgate, ovt, zc, zc, zc, zc, zc)


def _prep_w_in(w_in):
    sizes = (A_WIDTH, A_WIDTH, A_WIDTH, B_WIDTH, B_KV, B_KV, B_KV, B_KV, B_KV, B_KV, 3 * B_HEADS,
             C_WIDTH, C_KV, C_KV, D_MODEL, D_MODEL, D_MODEL)
    offs = np.concatenate([[0], np.cumsum(sizes)])
    (aq, ak, av, bq, bkc, bvc, bks, bvs, bkw, bvw, bgate, cq, ck, cv, ga, gb, gc) = [
        w_in[..., offs[i]:offs[i + 1]] for i in range(len(sizes))]
    scale = HEAD_DIM ** -0.5 * LOG2E
    bgate = jnp.pad(bgate, ((0, 0), (0, 0), (0, LANES - 3 * B_HEADS)))
    cols = [aq * scale, ak, av, bq * scale, bks, bvs, bkw, bvw, bkc, bvc, cq * scale, ck, cv, ga, gb, gc, bgate]
    return jnp.concatenate([c.astype(BF16) for c in cols], axis=-1)


def _prep_compress(cmp_pos, cmp_w1, cmp_w2):
    depth = cmp_pos.shape[0]
    half = B_CMP_STRIDE
    pos = cmp_pos.reshape(depth, 2, 2, half, 1, B_HEAD_DIM)
    pos = jnp.broadcast_to(pos, (depth, 2, 2, half, B_GROUPS, B_HEAD_DIM)).reshape(depth, 2, 2, 1, half * B_KV)
    w1 = cmp_w1.reshape(depth, 2, 2, half, 1, B_HEAD_DIM, 1, B_CMP_HIDDEN)
    eye = jnp.eye(B_GROUPS, dtype=cmp_w1.dtype).reshape(1, 1, 1, 1, B_GROUPS, 1, B_GROUPS, 1)
    w1 = (w1 * eye).reshape(depth, 2, 2, half * B_KV, B_GROUPS * B_CMP_HIDDEN)
    w2 = cmp_w2.reshape(depth, 2, 1, B_CMP_HIDDEN, 1, B_HEAD_DIM)
    eye2 = jnp.eye(B_GROUPS, dtype=cmp_w2.dtype).reshape(1, 1, B_GROUPS, 1, B_GROUPS, 1)
    w2 = (w2 * eye2).reshape(depth, 2, B_GROUPS * B_CMP_HIDDEN, B_KV)
    return pos.astype(F32), w1.astype(BF16), w2.astype(BF16)


def _overlap_t(seq):
    n_cmp_pad = seq // B_CMP_STRIDE
    start = np.arange(n_cmp_pad) * B_CMP_STRIDE
    sel = np.arange(seq // B_SEL_BLOCK) * B_SEL_BLOCK
    ov = (start[None, :] < sel[:, None] + B_SEL_BLOCK) & (start[None, :] + B_CMP_LEN > sel[:, None])
    ov &= (np.arange(n_cmp_pad) < n_cmp_pad - 1)[None, :]
    return jnp.asarray(ov.astype(np.float32), dtype=BF16)


def kernel(x, p, norm_g, ffn1_wi, ffn1_wo, w_in, diff_lambda, diff_subln, nsa_cmp_pos, nsa_cmp_w1,
           nsa_cmp_w2, swa_sinks, w_branch, w_out, ffn2_wi, ffn2_wo, w_ple, w_ple_gate, rel_bias, final_norm):
    bsz, seq, _ = x.shape
    depth = norm_g.shape[0]
    tokens = bsz * seq
    tq = BC_QUERY_TILE
    assert seq % (B_CMP_STRIDE * LANES) == 0 and seq // B_SEL_BLOCK < LANES and tokens % TOKEN_TILE == 0
    assert FAR_DIST + B_CMP_LEN - 1 <= CMP_BAND_BACK * B_CMP_STRIDE and FAR_DIST <= tq
    assert WIN_PAD * tq == B_WINDOW and tq == C_WINDOW and tq % (B_CMP_STRIDE * SUBLANES) == 0

    wi1, wo1 = ffn1_wi.astype(BF16), ffn1_wo.astype(BF16)
    wi2, wo2 = ffn2_wi.astype(BF16), ffn2_wo.astype(BF16)
    w_in_r = _prep_w_in(w_in)
    cmp_pos, cmp_w1, cmp_w2 = _prep_compress(nsa_cmp_pos, nsa_cmp_w1, nsa_cmp_w2)
    wb, wo = w_branch.astype(BF16), w_out.astype(BF16)
    wpg, wpe = w_ple_gate.astype(BF16), w_ple.astype(BF16)
    tbl = rel_bias.astype(F32)
    ovt = _overlap_t(seq)
    chunk_rows = seq // B_CMP_STRIDE

    h = x.reshape(tokens, D_MODEL)
    p_rows = p.reshape(depth, tokens, PLE_DIM)
    for i in range(depth):
        h, za, zb, zkc, zvc, zc, gates, bgate = _head(h, norm_g[i, 0], wi1, wo1, norm_g[i, 1], w_in_r,
                                                      TOKEN_TILE, i)
        lam_init = 0.8 - 0.6 * math.exp(-0.3 * i)
        ya = _diff_attention(za, tbl, diff_lambda[i].astype(F32), diff_subln[i], bsz, seq, lam_init, A_QUERY_TILE)
        kc, vc = _compress(zkc.reshape(bsz, chunk_rows, B_CMP_STRIDE * B_KV),
                           zvc.reshape(bsz, chunk_rows, B_CMP_STRIDE * B_KV), cmp_pos, cmp_w1, cmp_w2, i)
        yb, yc = _nsa_swa_attention(zb, kc, vc, bgate, zc, tbl, swa_sinks[i].astype(F32), ovt, bsz, seq, tq)
        h = _tail(h, ya, yb, yc, gates, p_rows, wb, wo, norm_g[i, 2], wi2, wo2, norm_g[i, 3], wpg, wpe,
                  final_norm, TOKEN_TILE, i, final=(i == depth - 1))
    return h.reshape(bsz, seq, D_MODEL)
```

```python
import functools
import math

import numpy as np
import jax
import jax.numpy as jnp
from jax import lax
from jax.experimental import pallas as pl
from jax.experimental.pallas import tpu as pltpu

F32 = jnp.float32
BF16 = jnp.bfloat16

D_MODEL = 1024
PLE_DIM = 256
D_FF = 2816
NORM_EPS = 1e-6
NEG = -1e30
NUM_BUCKETS = 32
MAX_DISTANCE = 128
A_HEADS = 4
A_HEAD_DIM = 64
B_HEADS = 8
B_GROUPS = 2
B_REP = B_HEADS // B_GROUPS
B_HEAD_DIM = 64
B_CMP_LEN = 32
B_CMP_STRIDE = 16
B_CMP_HIDDEN = 256
B_SEL_BLOCK = 64
B_SEL_TOPK = 16
B_WINDOW = 512
B_SEL_FORCE = 1e6
C_HEADS = 8
C_GROUPS = 2
C_REP = C_HEADS // C_GROUPS
C_HEAD_DIM = 64
C_WINDOW = 128
A_WIDTH = A_HEADS * 2 * A_HEAD_DIM
B_WIDTH = B_HEADS * B_HEAD_DIM
B_KV = B_GROUPS * B_HEAD_DIM
C_WIDTH = C_HEADS * C_HEAD_DIM
C_KV = C_GROUPS * C_HEAD_DIM

LANES = 128
SUBLANES = 8
HEAD_DIM = 64
VMEM_LIMIT = 56 * 1024 * 1024
TOKEN_TILE = 256
A_QUERY_TILE = 256
BC_QUERY_TILE = 128
UNSEL = -float(2 ** 30)
TAKEN = -3e38
FFN_CHUNKS = ((0, 1536), (1536, 1280))
CMP_BAND_BACK = 16
LOG2E = math.log2(math.e)
ONES_ROWS = 16
EXP_HEADROOM = 64.0


def _bucket_thresholds():
    n = np.arange(4 * MAX_DISTANCE)
    max_exact = NUM_BUCKETS // 2
    nf = np.maximum(n, 1).astype(np.float32)
    large = max_exact + (np.log(nf / max_exact) / math.log(MAX_DISTANCE / max_exact)
                         * (NUM_BUCKETS - max_exact)).astype(np.int32)
    bucket = np.where(n < max_exact, n, np.minimum(large, NUM_BUCKETS - 1))
    out = []
    for b in range(1, NUM_BUCKETS):
        hit = np.nonzero(bucket == b)[0]
        if hit.size:
            out.append((b, int(hit[0])))
    return tuple(out)


BUCKET_LO = _bucket_thresholds()
FAR_DIST = BUCKET_LO[-1][1]


def _bias_of_dist(dist, tbl_ref, head, minus_far=False):
    shift = tbl_ref[NUM_BUCKETS - 1, head] if minus_far else 0.0
    out = jnp.full(dist.shape, (tbl_ref[0, head] - shift) * LOG2E, F32)
    for b, lo in BUCKET_LO:
        out = jnp.where(dist >= lo, (tbl_ref[b, head] - shift) * LOG2E, out)
    return out


def _rms(x, g):
    return x * lax.rsqrt(jnp.mean(x * x, axis=-1, keepdims=True) + NORM_EPS) * g


def _dot(a, b):
    return jnp.dot(a, b, preferred_element_type=F32)


def _values_and_ones(vts):
    values = jnp.concatenate(vts, axis=1)
    return jnp.concatenate([values, jnp.ones((ONES_ROWS, values.shape[1]), BF16)], axis=0)


def _online_update(carry, s, vts):
    m, acc = carry
    m_new = jnp.maximum(m, jnp.max(s, axis=0, keepdims=True))
    p = jnp.exp2(s - m_new).astype(BF16)
    return m_new, jnp.exp2(m - m_new) * acc + _dot(_values_and_ones(vts), p)


def _one_pass_update(carry, s, vts):
    m, acc = carry
    top = jnp.max(s, axis=0, keepdims=True)
    p = jnp.exp2(s - m).astype(BF16)
    m_new = jnp.maximum(m, top)
    return (m_new, jnp.exp2(m - m_new) * (acc + _dot(_values_and_ones(vts), p))), jnp.max(top - m) <= EXP_HEADROOM


def _softmax_init(width, queries):
    return jnp.full((1, queries), NEG, F32), jnp.zeros((width + ONES_ROWS, queries), F32)


def _softmax_result(carry):
    _, acc = carry
    width = acc.shape[0] - ONES_ROWS
    return acc[0:width] / acc[width:width + 1]


def _transpose_bf16(x):
    return x.astype(F32).T.astype(BF16)


def _params(*sem):
    return pltpu.CompilerParams(dimension_semantics=sem, vmem_limit_bytes=VMEM_LIMIT)


def _resident(shape, layer=None):
    if layer is None:
        zeros = (0,) * len(shape)
        return pl.BlockSpec(shape, lambda *_: zeros, pipeline_mode=pl.Buffered(1))
    index = (layer,) + (0,) * (len(shape) - 1)
    return pl.BlockSpec((None,) + tuple(shape[1:]), lambda *_: index, pipeline_mode=pl.Buffered(1))


def _smem():
    return pl.BlockSpec(memory_space=pltpu.SMEM)


def _half_ffn(x, g_ref, wi_ref, wo_ref):
    n = _rms(x, g_ref[...]).astype(BF16)
    acc = x
    for lo, width in FFN_CHUNKS:
        gate = _dot(n, wi_ref[:, lo:lo + width])
        up = _dot(n, wi_ref[:, D_FF + lo:D_FF + lo + width])
        act = (gate * jax.nn.sigmoid(gate) * up).astype(BF16)
        acc = acc + 0.5 * _dot(act, wo_ref[lo:lo + width, :])
    return acc


IN_A = 3 * A_WIDTH
IN_B = B_WIDTH + 4 * B_KV
IN_C = C_WIDTH + 2 * C_KV
IN_G = 3 * D_MODEL
IN_SECTIONS = (IN_A, IN_B, B_KV, B_KV, IN_C, IN_G, LANES)


def _head_kernel(h_ref, g0_ref, wi_ref, wo_ref, g1_ref, w_ref,
                 o_ref, za_ref, zb_ref, zkc_ref, zvc_ref, zc_ref, gates_ref, bg_ref):
    x = _half_ffn(h_ref[...], g0_ref, wi_ref, wo_ref)
    o_ref[...] = x
    n = _rms(x, g1_ref[...]).astype(BF16)
    lo = 0
    for ref, width in zip((za_ref, zb_ref, zkc_ref, zvc_ref, zc_ref, gates_ref, bg_ref), IN_SECTIONS):
        ref[...] = _dot(n, w_ref[:, lo:lo + width]).astype(ref.dtype)
        lo += width


def _head(h, g0, wi, wo, g1, w, tm, layer):
    t = h.shape[0]
    dtypes = (BF16, BF16, BF16, BF16, BF16, F32, F32)
    rows = lambda width: pl.BlockSpec((tm, width), lambda i: (i, 0))
    vec = _resident((1, D_MODEL))
    return pl.pallas_call(
        _head_kernel,
        name="head",
        grid=(t // tm,),
        in_specs=[rows(D_MODEL), vec, _resident(wi.shape, layer), _resident(wo.shape, layer), vec,
                  _resident(w.shape, layer)],
        out_specs=[rows(D_MODEL)] + [rows(width) for width in IN_SECTIONS],
        out_shape=[jax.ShapeDtypeStruct(h.shape, F32)]
                  + [jax.ShapeDtypeStruct((t, width), dt) for width, dt in zip(IN_SECTIONS, dtypes)],
        compiler_params=_params("arbitrary"),
    )(h, g0.reshape(1, D_MODEL), wi, wo, g1.reshape(1, D_MODEL), w)


def _tail_kernel(h_ref, ya_ref, yb_ref, yc_ref, gates_ref, p_ref, wb_ref, wo_ref, g2_ref, wi_ref, wo2_ref,
                 g3_ref, wg_ref, we_ref, gf_ref, o_ref, *, final):
    merged = jnp.zeros(h_ref.shape, F32)
    for m, y_ref in enumerate((ya_ref, yb_ref, yc_ref)):
        gate = jax.nn.sigmoid(gates_ref[:, m * D_MODEL:(m + 1) * D_MODEL])
        merged = merged + gate * _dot(y_ref[...], wb_ref[m])
    x = h_ref[...] + _dot(merged.astype(BF16), wo_ref[...])
    x = _half_ffn(x, g2_ref, wi_ref, wo2_ref)
    gate = jax.nn.sigmoid(_dot(_rms(x, g3_ref[...]).astype(BF16), wg_ref[...]))
    x = x + gate * _dot(p_ref[...].astype(BF16), we_ref[...])
    if final:
        x = _rms(x, gf_ref[...])
    o_ref[...] = x


def _tail(h, ya, yb, yc, gates, p, wb, wo, g2, wi, wo2, g3, wg, we, gf, tm, layer, final):
    t = h.shape[0]
    rows = lambda width: pl.BlockSpec((tm, width), lambda i: (i, 0))
    vec = _resident((1, D_MODEL))
    return pl.pallas_call(
        functools.partial(_tail_kernel, final=final),
        name="tail",
        grid=(t // tm,),
        in_specs=[rows(D_MODEL), rows(A_WIDTH), rows(B_WIDTH), rows(C_WIDTH), rows(IN_G),
                  pl.BlockSpec((None, tm, PLE_DIM), lambda i: (layer, i, 0)),
                  _resident(wb.shape, layer), _resident(wo.shape, layer), vec,
                  _resident(wi.shape, layer), _resident(wo2.shape, layer),
                  vec, _resident(wg.shape, layer), _resident(we.shape, layer), vec],
        out_specs=rows(D_MODEL),
        out_shape=jax.ShapeDtypeStruct(h.shape, F32),
        compiler_params=_params("arbitrary"),
    )(h, ya, yb, yc, gates, p, wb, wo, g2.reshape(1, D_MODEL), wi, wo2, g3.reshape(1, D_MODEL), wg, we,
      gf.reshape(1, D_MODEL))


def _diff_kernel(tbl_ref, lam_ref, subg_ref, q_ref, k_ref, v_ref, o_ref, kaug_ref, vt_ref, near_ref,
                 *, tq, lam_init):
    qi = pl.program_id(1)
    width = 2 * A_HEAD_DIM
    n_heads = A_HEADS

    @pl.when((pl.program_id(0) == 0) & (qi == 0))
    def _static_setup():
        lane = lax.broadcasted_iota(jnp.int32, (tq, 2 * LANES), 1)
        key = lax.broadcasted_iota(jnp.int32, (2 * tq, tq), 0)
        qry = lax.broadcasted_iota(jnp.int32, (2 * tq, tq), 1)
        dist = tq + qry - key
        for h in range(n_heads):
            kaug_ref[h, 0:tq, :] = jnp.where(lane == LANES, 1.0, 0.0).astype(BF16)
            kaug_ref[h, tq:, LANES:2 * LANES] = jnp.zeros((kaug_ref.shape[1] - tq, LANES), BF16)
            vt_ref[h, 0] = jnp.zeros(vt_ref.shape[2:], BF16)
            near_ref[h] = jnp.where(dist >= 0, _bias_of_dist(dist, tbl_ref, h, True), NEG)

    @pl.when(qi == 0)
    def _per_batch_setup():
        for h in range(n_heads):
            cols = slice(h * width, (h + 1) * width)
            kaug_ref[h, tq:, 0:LANES] = k_ref[:, cols]
            for c in range(vt_ref.shape[1] - 1):
                vt_ref[h, c + 1] = _transpose_bf16(v_ref[c * tq:(c + 1) * tq, cols])

    zero = jnp.zeros((A_HEAD_DIM, tq), F32)
    flag_row = lax.broadcasted_iota(jnp.int32, (LANES, 2 * tq), 0) == 0
    flags = jnp.where(flag_row, UNSEL, 0.0)
    queries = []
    for h in range(n_heads):
        qt = q_ref[:, h * width:(h + 1) * width].astype(F32).T
        both = jnp.concatenate([jnp.concatenate([qt[0:A_HEAD_DIM], zero], axis=0),
                                jnp.concatenate([zero, qt[A_HEAD_DIM:width]], axis=0)], axis=1)
        queries.append(jnp.concatenate([both, flags], axis=0).astype(BF16))

    def logits_of(first_tile):
        start = pl.multiple_of(first_tile * tq, tq)
        return tuple(_dot(kaug_ref[h, pl.ds(start, 2 * tq), :], queries[h]) for h in range(n_heads))

    def update(first_tile, logits, carries):
        return tuple(_online_update(carries[h], logits[h], [vt_ref[h, first_tile], vt_ref[h, first_tile + 1]])
                     for h in range(n_heads))

    def far(gi, carries, count):
        tiles = [qi - 2 * (gi + c) for c in range(count)]
        fast, safe = carries, True
        for t, logits in [(t, logits_of(t)) for t in tiles]:
            steps = [_one_pass_update(fast[h], logits[h], [vt_ref[h, t], vt_ref[h, t + 1]]) for h in range(n_heads)]
            fast = tuple(step[0] for step in steps)
            for step in steps:
                safe = safe & step[1]

        def redo():
            slow = carries
            for t in tiles:
                slow = update(t, logits_of(t), slow)
            return slow

        return lax.cond(safe, lambda: fast, redo)

    nearest = tuple(s + jnp.concatenate([near_ref[h], near_ref[h]], axis=1) for h, s in enumerate(logits_of(qi)))
    carries = update(qi, nearest, tuple(_softmax_init(width, 2 * tq) for _ in range(n_heads)))
    n_far = qi // 2
    carries = lax.fori_loop(0, n_far // 2, lambda p, c: far(1 + 2 * p, c, 2), carries)
    carries = lax.fori_loop(0, n_far % 2, lambda _, c: far(n_far, c, 1), carries)
    lp = lam_ref[...]
    lam = (jnp.exp(jnp.sum(lp[0:1] * lp[1:2], axis=-1, keepdims=True))
           - jnp.exp(jnp.sum(lp[2:3] * lp[3:4], axis=-1, keepdims=True)) + lam_init)
    for h in range(n_heads):
        out = _softmax_result(carries[h])
        o = out[:, 0:tq] - lam * out[:, tq:2 * tq]
        o = o * lax.rsqrt(jnp.mean(o * o, axis=0, keepdims=True) + NORM_EPS) * (1.0 - lam_init)
        o_ref[:, h * width:(h + 1) * width] = (o.T * subg_ref[...]).astype(o_ref.dtype)


def _diff_attention(za, tbl, lam_p, subg, bsz, seq, lam_init, tq):
    nq = seq // tq
    width = 2 * A_HEAD_DIM
    return pl.pallas_call(
        functools.partial(_diff_kernel, tq=tq, lam_init=lam_init),
        name="diff_attn",
        grid=(bsz, nq),
        in_specs=[_smem(), _resident(lam_p.shape), _resident((1, width)),
                  pl.BlockSpec((tq, A_WIDTH), lambda b, i: (b * nq + i, 0)),
                  pl.BlockSpec((seq, A_WIDTH), lambda b, i: (b, 1)),
                  pl.BlockSpec((seq, A_WIDTH), lambda b, i: (b, 2))],
        out_specs=pl.BlockSpec((tq, A_WIDTH), lambda b, i: (b * nq + i, 0)),
        out_shape=jax.ShapeDtypeStruct((bsz * seq, A_WIDTH), BF16),
        scratch_shapes=[pltpu.VMEM((A_HEADS, tq + seq, 2 * LANES), BF16),
                        pltpu.VMEM((A_HEADS, nq + 1, width, tq), BF16),
                        pltpu.VMEM((A_HEADS, 2 * tq, tq), F32)],
        compiler_params=_params("arbitrary", "arbitrary"),
    )(tbl, lam_p, subg.reshape(1, width), za, za, za)


def _compress_kernel(xk_ref, xv_ref, pos_ref, w1_ref, w2_ref, kc_ref, vc_ref):
    rows = xk_ref.shape[0]
    for t, (x_ref, o_ref) in enumerate(((xk_ref, kc_ref), (xv_ref, vc_ref))):
        x = x_ref[...].astype(F32)
        first = _dot((x + pos_ref[t, 0]).astype(BF16), w1_ref[t, 0])
        second = _dot((x + pos_ref[t, 1]).astype(BF16), w1_ref[t, 1])
        hidden = first + pltpu.roll(second, rows - 1, 0)
        o_ref[...] = _dot(jax.nn.gelu(hidden).astype(BF16), w2_ref[t]).astype(o_ref.dtype)


def _compress(xk, xv, pos, w1, w2, layer):
    bsz, rows, width = xk.shape
    xspec = pl.BlockSpec((None, rows, width), lambda b: (b, 0, 0))
    ospec = pl.BlockSpec((None, rows, B_KV), lambda b: (b, 0, 0))
    return pl.pallas_call(
        _compress_kernel,
        name="nsa_compress",
        grid=(bsz,),
        in_specs=[xspec, xspec, _resident(pos.shape, layer), _resident(w1.shape, layer), _resident(w2.shape, layer)],
        out_specs=[ospec, ospec],
        out_shape=[jax.ShapeDtypeStruct((bsz, rows, B_KV), BF16)] * 2,
        compiler_params=_params("arbitrary"),
    )(xk, xv, pos, w1, w2)


def _group_queries_t(q, group, rep):
    tq = q.shape[0]
    zero = jnp.zeros((HEAD_DIM, tq), F32)
    cols = []
    for r in range(rep):
        head = group * rep + r
        slab = q[:, (head // 2) * LANES:(head // 2 + 1) * LANES].astype(F32).T
        part = slab[(head % 2) * HEAD_DIM:(head % 2 + 1) * HEAD_DIM]
        cols.append(jnp.concatenate([part, zero] if group == 0 else [zero, part], axis=0))
    return jnp.concatenate(cols, axis=1).astype(BF16)


def _build_near_bias(near_ref, tbl_ref, head0, groups, rep, tq, window_is_tile, minus_far):
    key = lax.broadcasted_iota(jnp.int32, (2 * tq, tq), 0)
    qry = lax.broadcasted_iota(jnp.int32, (2 * tq, tq), 1)
    dist = tq + qry - key
    visible = (dist >= 0) & (dist < tq) if window_is_tile else dist >= 0
    for g in range(groups):
        for r in range(rep):
            head = head0 + g * rep + r
            near_ref[g, :, r * tq:(r + 1) * tq] = jnp.where(visible, _bias_of_dist(dist, tbl_ref, head, minus_far), NEG)


def _swa_tile(q, k_prev, k_diag, v_prev, v_diag, bias_ref, sink_ref, qi, tq):
    cols = C_REP * tq
    prev_bias = jnp.where(qi > 0, 0.0, NEG)
    vp_t = _transpose_bf16(v_prev)
    vd_t = _transpose_bf16(v_diag)
    heads_out = []
    for g in range(C_GROUPS):
        half = slice(g * HEAD_DIM, (g + 1) * HEAD_DIM)
        qg = _group_queries_t(q, g, C_REP)
        s = jnp.concatenate([_dot(k_prev, qg) + bias_ref[g, 0:tq, :] + prev_bias,
                             _dot(k_diag, qg) + bias_ref[g, tq:2 * tq, :]], axis=0)
        m, acc = _online_update(_softmax_init(HEAD_DIM, cols), s, [vp_t[half], vd_t[half]])
        for r in range(C_REP):
            cc = slice(r * tq, (r + 1) * tq)
            sink = sink_ref[g * C_REP + r] * LOG2E
            m_all = jnp.maximum(m[:, cc], sink)
            scale = jnp.exp2(m[:, cc] - m_all)
            denom = acc[HEAD_DIM:HEAD_DIM + 1, cc] * scale + jnp.exp2(sink - m_all)
            heads_out.append(acc[0:HEAD_DIM, cc] * (scale / denom))
    return jnp.concatenate(heads_out, axis=0).T


SEL_GROUP = 4
SEL_PAD = SEL_GROUP - 1
WIN_PAD = B_WINDOW // BC_QUERY_TILE


def _nsa_kernel(tbl_ref, sink_ref, q_ref, ks_ref, vs_ref, kw_ref, vw_ref, kc_ref, vc_ref, bg_ref, ovt_ref,
                cq_ref, ckp_ref, ckd_ref, cvp_ref, cvd_ref, o_ref, oc_ref,
                kaug_ref, kwp_ref, vst_ref, vwt_ref, vct_ref, near_ref, edge_ref, band_ref, sc_ref, swab_ref,
                *, tq, seq):
    first_batch = pl.program_id(0) == 0
    qi = pl.program_id(1)
    n_cmp_pad = seq // B_CMP_STRIDE
    n_sel = seq // B_SEL_BLOCK
    top_k = min(B_SEL_TOPK, n_sel)
    nq = seq // tq
    cols = B_REP * tq
    cmp_per_tile = tq // B_CMP_STRIDE
    band = CMP_BAND_BACK + cmp_per_tile

    @pl.when(first_batch & (qi == 0))
    def _static_setup():
        _build_near_bias(near_ref, tbl_ref, A_HEADS, B_GROUPS, B_REP, tq, False, True)
        _build_near_bias(swab_ref, tbl_ref, A_HEADS + B_HEADS, C_GROUPS, C_REP, tq, True, False)
        key = lax.broadcasted_iota(jnp.int32, (tq, cols), 0)
        qry = lax.broadcasted_iota(jnp.int32, (tq, cols), 1) % tq
        edge_ref[...] = jnp.where(qry < key, 0.0, NEG)
        blk = lax.broadcasted_iota(jnp.int32, (band, tq), 0) - CMP_BAND_BACK
        dist = lax.broadcasted_iota(jnp.int32, (band, tq), 1) - (blk * B_CMP_STRIDE + B_CMP_LEN - 1)
        for g in range(B_GROUPS):
            for r in range(B_REP):
                head = A_HEADS + g * B_REP + r
                band_ref[g, :, r * tq:(r + 1) * tq] = jnp.where(dist >= 0, _bias_of_dist(dist, tbl_ref, head, True), NEG)
        sc_ref[:, 0:CMP_BAND_BACK, :] = jnp.zeros((B_GROUPS, CMP_BAND_BACK, cols), F32)
        pad = SEL_PAD * tq
        row = lax.broadcasted_iota(jnp.int32, (pad + seq, LANES), 0)
        lane = lax.broadcasted_iota(jnp.int32, (pad + seq, LANES), 1)
        blk_id = jnp.where(row < pad, LANES - 1, (row - pad) // B_SEL_BLOCK)
        kaug_ref[:, LANES:2 * LANES] = jnp.where(blk_id == lane, 1.0, 0.0).astype(BF16)
        kaug_ref[0:pad, 0:LANES] = jnp.zeros((pad, LANES), BF16)
        pad_lane = lax.broadcasted_iota(jnp.int32, (WIN_PAD * tq, 2 * LANES), 1) == LANES
        kwp_ref[0:WIN_PAD * tq, :] = jnp.where(pad_lane, 1.0, 0.0).astype(BF16)
        kwp_ref[WIN_PAD * tq:, LANES:2 * LANES] = jnp.zeros((seq, LANES), BF16)
        for c in range(SEL_PAD):
            vst_ref[c] = jnp.zeros(vst_ref.shape[1:], BF16)
        for c in range(WIN_PAD):
            vwt_ref[c] = jnp.zeros(vwt_ref.shape[1:], BF16)

    @pl.when(qi == 0)
    def _per_batch_setup():
        kaug_ref[SEL_PAD * tq:, 0:LANES] = ks_ref[...]
        kwp_ref[WIN_PAD * tq:, 0:LANES] = kw_ref[...]
        for c in range(nq):
            vst_ref[SEL_PAD + c] = _transpose_bf16(vs_ref[c * tq:(c + 1) * tq, :])
            vwt_ref[WIN_PAD + c] = _transpose_bf16(vw_ref[c * tq:(c + 1) * tq, :])
        for c in range(n_cmp_pad // LANES):
            vct_ref[:, c * LANES:(c + 1) * LANES] = _transpose_bf16(vc_ref[c * LANES:(c + 1) * LANES, :])

    q = q_ref[...]
    gates_t = jax.nn.sigmoid(bg_ref[...]).T
    first_cmp = qi * cmp_per_tile
    cmp_row = lax.broadcasted_iota(jnp.int32, (n_cmp_pad, cols), 0)
    cmp_visible = cmp_row < first_cmp + cmp_per_tile
    blk = lax.broadcasted_iota(jnp.int32, (n_sel, tq), 0)
    cur = (qi * tq + lax.broadcasted_iota(jnp.int32, (n_sel, tq), 1)) // B_SEL_BLOCK
    forced = (blk == 0) | (blk == cur) | (blk == cur - 1)
    future = blk > cur
    halves = [slice(g * HEAD_DIM, (g + 1) * HEAD_DIM) for g in range(B_GROUPS)]

    qgs, o_cs, imps = [], [], []
    for g in range(B_GROUPS):
        qg = _group_queries_t(q, g, B_REP)
        sc_ref[g, CMP_BAND_BACK:, :] = _dot(kc_ref[...], qg)
        band_rows = pl.ds(pl.multiple_of(first_cmp, SUBLANES), band)
        sc_ref[g, band_rows, :] = sc_ref[g, band_rows, :] + band_ref[g]
        s = jnp.where(cmp_visible, sc_ref[g, CMP_BAND_BACK:, :], NEG)
        top = jnp.max(s, axis=0, keepdims=True)
        e = jnp.exp2(s - top)
        inv = jnp.where(top > 0.5 * NEG, 1.0 / jnp.sum(e, axis=0, keepdims=True), 0.0)
        o_cs.append(_dot(vct_ref[halves[g], :], e.astype(BF16)) * inv)
        p_sum = e[:, 0:tq] * inv[:, 0:tq]
        for r in range(1, B_REP):
            p_sum = p_sum + e[:, r * tq:(r + 1) * tq] * inv[:, r * tq:(r + 1) * tq]
        imp = _dot(ovt_ref[...], p_sum.astype(BF16))
        imp = jnp.where(forced, B_SEL_FORCE, jnp.where(future, -B_SEL_FORCE, imp))
        qgs.append(qg)
        imps.append(imp)

    blk_f = blk.astype(F32)
    taken = [jnp.zeros((n_sel, tq), jnp.bool_) for _ in range(B_GROUPS)]
    left = list(imps)
    for _ in range(top_k):
        for g in range(B_GROUPS):
            best = jnp.max(left[g], axis=0, keepdims=True)
            first = jnp.min(jnp.where(left[g] == best, blk_f, float(n_sel)), axis=0, keepdims=True)
            pick = blk_f == first
            taken[g] = taken[g] | pick
            left[g] = jnp.where(pick, TAKEN, left[g])
    q_augs = []
    for g in range(B_GROUPS):
        sel_bias = jnp.concatenate([jnp.where(taken[g], 0.0, UNSEL),
                                    jnp.full((LANES - n_sel, tq), UNSEL, F32)], axis=0).astype(BF16)
        q_augs.append(jnp.concatenate([qgs[g], jnp.concatenate([sel_bias] * B_REP, axis=1)], axis=0))

    pair = cols
    chains = [(g, slice(0, cols)) for g in range(B_GROUPS)]
    heads_per_chain = pair // tq

    def sel_logits(first_tile):
        start = pl.multiple_of(first_tile * tq, tq)
        keys = kaug_ref[pl.ds(start, SEL_GROUP * tq), :]
        return tuple(_dot(keys, q_augs[g][:, cc]) for g, cc in chains)

    def sel_update(first_tile, logits, carries):
        return tuple(_online_update(carry, s, [vst_ref[first_tile + t, halves[g], :] for t in range(SEL_GROUP)])
                     for (g, cc), s, carry in zip(chains, logits, carries))

    start = pl.multiple_of(qi * tq, tq)
    win_keys = kwp_ref[pl.ds(start, (WIN_PAD + 1) * tq), :]
    pad_flags = jnp.where(lax.broadcasted_iota(jnp.int32, (LANES, cols), 0) == 0, UNSEL, 0.0).astype(BF16)
    partial = []
    for g, cc in chains:
        s_w = _dot(win_keys, jnp.concatenate([qgs[g], pad_flags], axis=0)[:, cc])
        slabs = []
        for t in range(WIN_PAD + 1):
            slab = s_w[t * tq:(t + 1) * tq]
            if t == 0:
                slab = slab + edge_ref[:, cc]
            if t >= WIN_PAD - 1:
                slab = slab + near_ref[g, (t - WIN_PAD + 1) * tq:(t - WIN_PAD + 2) * tq, cc]
            slabs.append(slab)
        o_w = _softmax_result(_online_update(_softmax_init(HEAD_DIM, pair), jnp.concatenate(slabs, axis=0),
                                             [vwt_ref[qi + t, halves[g], :] for t in range(WIN_PAD + 1)]))
        o_c = o_cs[g][:, cc]
        heads = []
        for rr in range(heads_per_chain):
            hc = slice(rr * tq, (rr + 1) * tq)
            c0 = (g * B_REP + rr) * 3
            heads.append(gates_t[c0:c0 + 1] * o_c[:, hc] + gates_t[c0 + 2:c0 + 3] * o_w[:, hc])
        partial.append(heads)

    oc_ref[...] = _swa_tile(cq_ref[...], ckp_ref[...], ckd_ref[...], cvp_ref[...], cvd_ref[...],
                            swab_ref, sink_ref, qi, tq).astype(oc_ref.dtype)

    def sel_far(gi, carries, count):
        tiles = [qi - SEL_GROUP * (gi + c) for c in range(count)]
        fast, safe = carries, True
        for t, logits in [(t, sel_logits(t)) for t in tiles]:
            steps = [_one_pass_update(carry, s, [vst_ref[t + k, halves[g], :] for k in range(SEL_GROUP)])
                     for (g, cc), s, carry in zip(chains, logits, fast)]
            fast = tuple(step[0] for step in steps)
            for step in steps:
                safe = safe & step[1]

        def redo():
            slow = carries
            for t in tiles:
                slow = sel_update(t, sel_logits(t), slow)
            return slow

        return lax.cond(safe, lambda: fast, redo)

    far_rows = (SEL_GROUP - 2) * tq
    nearest = tuple(jnp.concatenate([s[0:far_rows], s[far_rows:] + near_ref[g, :, cc]], axis=0)
                    for (g, cc), s in zip(chains, sel_logits(qi)))
    sel = sel_update(qi, nearest, tuple(_softmax_init(HEAD_DIM, pair) for _ in chains))
    n_far = qi // SEL_GROUP
    sel = lax.fori_loop(0, n_far // 2, lambda p, c: sel_far(1 + 2 * p, c, 2), sel)
    sel = lax.fori_loop(0, n_far % 2, lambda _, c: sel_far(n_far, c, 1), sel)

    heads_out = []
    for ci, (g, cc) in enumerate(chains):
        o_s = _softmax_result(sel[ci])
        for rr in range(heads_per_chain):
            hc = slice(rr * tq, (rr + 1) * tq)
            c1 = (g * B_REP + rr) * 3 + 1
            heads_out.append(partial[ci][rr] + gates_t[c1:c1 + 1] * o_s[:, hc])
    o_ref[...] = jnp.concatenate(heads_out, axis=0).T.astype(o_ref.dtype)


def _nsa_swa_attention(zb, kc, vc, bgate, zc, tbl, sinks, ovt, bsz, seq, tq):
    nq = seq // tq
    n_cmp_pad = seq // B_CMP_STRIDE
    cols = B_REP * tq
    kv = lambda col: pl.BlockSpec((seq, LANES), lambda b, i: (b, col))
    cmp_spec = pl.BlockSpec((None, n_cmp_pad, B_KV), lambda b, i: (b, 0, 0))
    qcols = B_WIDTH // LANES
    ccols = C_WIDTH // LANES
    wide = lambda width: pl.BlockSpec((tq, width), lambda b, i: (b * nq + i, 0))
    prev = lambda col: pl.BlockSpec((tq, LANES), lambda b, i: (b * nq + jnp.maximum(i - 1, 0), col))
    diag = lambda col: pl.BlockSpec((tq, LANES), lambda b, i: (b * nq + i, col))
    return pl.pallas_call(
        functools.partial(_nsa_kernel, tq=tq, seq=seq),
        name="nsa_attn",
        grid=(bsz, nq),
        in_specs=[_smem(), _smem(),
                  wide(B_WIDTH),
                  kv(qcols), kv(qcols + 1), kv(qcols + 2), kv(qcols + 3),
                  cmp_spec, cmp_spec,
                  wide(LANES),
                  _resident(ovt.shape),
                  wide(C_WIDTH), prev(ccols), diag(ccols), prev(ccols + 1), diag(ccols + 1)],
        out_specs=[wide(B_WIDTH), wide(C_WIDTH)],
        out_shape=[jax.ShapeDtypeStruct((bsz * seq, B_WIDTH), BF16),
                   jax.ShapeDtypeStruct((bsz * seq, C_WIDTH), BF16)],
        scratch_shapes=[pltpu.VMEM((SEL_PAD * tq + seq, 2 * LANES), BF16),
                        pltpu.VMEM((WIN_PAD * tq + seq, 2 * LANES), BF16),
                        pltpu.VMEM((SEL_PAD + nq, B_KV, tq), BF16),
                        pltpu.VMEM((WIN_PAD + nq, B_KV, tq), BF16),
                        pltpu.VMEM((B_KV, n_cmp_pad), BF16),
                        pltpu.VMEM((B_GROUPS, 2 * tq, cols), F32),
                        pltpu.VMEM((tq, cols), F32),
                        pltpu.VMEM((B_GROUPS, CMP_BAND_BACK + tq // B_CMP_STRIDE, cols), F32),
                        pltpu.VMEM((B_GROUPS, CMP_BAND_BACK + n_cmp_pad, cols), F32),
                        pltpu.VMEM((C_GROUPS, 2 * tq, C_REP * tq), F32)],
        compiler_params=_params("arbitrary", "arbitrary"),
    )(tbl, sinks, zb, zb, zb, zb, zb, kc, vc, bgate, ovt, zc, zc, zc, zc, zc)


def _prep_w_in(w_in):
    sizes = (A_WIDTH, A_WIDTH, A_WIDTH, B_WIDTH, B_KV, B_KV, B_KV, B_KV, B_KV, B_KV, 3 * B_HEADS,
             C_WIDTH, C_KV, C_KV, D_MODEL, D_MODEL, D_MODEL)
    offs = np.concatenate([[0], np.cumsum(sizes)])
    (aq, ak, av, bq, bkc, bvc, bks, bvs, bkw, bvw, bgate, cq, ck, cv, ga, gb, gc) = [
        w_in[..., offs[i]:offs[i + 1]] for i in range(len(sizes))]
    scale = HEAD_DIM ** -0.5 * LOG2E
    bgate = jnp.pad(bgate, ((0, 0), (0, 0), (0, LANES - 3 * B_HEADS)))
    cols = [aq * scale, ak, av, bq * scale, bks, bvs, bkw, bvw, bkc, bvc, cq * scale, ck, cv, ga, gb, gc, bgate]
    return jnp.concatenate([c.astype(BF16) for c in cols], axis=-1)


def _prep_compress(cmp_pos, cmp_w1, cmp_w2):
    depth = cmp_pos.shape[0]
    half = B_CMP_STRIDE
    pos = cmp_pos.reshape(depth, 2, 2, half, 1, B_HEAD_DIM)
    pos = jnp.broadcast_to(pos, (depth, 2, 2, half, B_GROUPS, B_HEAD_DIM)).reshape(depth, 2, 2, 1, half * B_KV)
    w1 = cmp_w1.reshape(depth, 2, 2, half, 1, B_HEAD_DIM, 1, B_CMP_HIDDEN)
    eye = jnp.eye(B_GROUPS, dtype=cmp_w1.dtype).reshape(1, 1, 1, 1, B_GROUPS, 1, B_GROUPS, 1)
    w1 = (w1 * eye).reshape(depth, 2, 2, half * B_KV, B_GROUPS * B_CMP_HIDDEN)
    w2 = cmp_w2.reshape(depth, 2, 1, B_CMP_HIDDEN, 1, B_HEAD_DIM)
    eye2 = jnp.eye(B_GROUPS, dtype=cmp_w2.dtype).reshape(1, 1, B_GROUPS, 1, B_GROUPS, 1)
    w2 = (w2 * eye2).reshape(depth, 2, B_GROUPS * B_CMP_HIDDEN, B_KV)
    return pos.astype(F32), w1.astype(BF16), w2.astype(BF16)


def _overlap_t(seq):
    n_cmp_pad = seq // B_CMP_STRIDE
    start = np.arange(n_cmp_pad) * B_CMP_STRIDE
    sel = np.arange(seq // B_SEL_BLOCK) * B_SEL_BLOCK
    ov = (start[None, :] < sel[:, None] + B_SEL_BLOCK) & (start[None, :] + B_CMP_LEN > sel[:, None])
    ov &= (np.arange(n_cmp_pad) < n_cmp_pad - 1)[None, :]
    return jnp.asarray(ov.astype(np.float32), dtype=BF16)


def kernel(x, p, norm_g, ffn1_wi, ffn1_wo, w_in, diff_lambda, diff_subln, nsa_cmp_pos, nsa_cmp_w1,
           nsa_cmp_w2, swa_sinks, w_branch, w_out, ffn2_wi, ffn2_wo, w_ple, w_ple_gate, rel_bias, final_norm):
    bsz, seq, _ = x.shape
    depth = norm_g.shape[0]
    tokens = bsz * seq
    tq = BC_QUERY_TILE
    assert seq % (B_CMP_STRIDE * LANES) == 0 and seq // B_SEL_BLOCK < LANES and tokens % TOKEN_TILE == 0
    assert FAR_DIST + B_CMP_LEN - 1 <= CMP_BAND_BACK * B_CMP_STRIDE and FAR_DIST <= tq
    assert WIN_PAD * tq == B_WINDOW and tq == C_WINDOW and tq % (B_CMP_STRIDE * SUBLANES) == 0

    wi1, wo1 = ffn1_wi.astype(BF16), ffn1_wo.astype(BF16)
    wi2, wo2 = ffn2_wi.astype(BF16), ffn2_wo.astype(BF16)
    w_in_r = _prep_w_in(w_in)
    cmp_pos, cmp_w1, cmp_w2 = _prep_compress(nsa_cmp_pos, nsa_cmp_w1, nsa_cmp_w2)
    wb, wo = w_branch.astype(BF16), w_out.astype(BF16)
    wpg, wpe = w_ple_gate.astype(BF16), w_ple.astype(BF16)
    tbl = rel_bias.astype(F32)
    ovt = _overlap_t(seq)
    chunk_rows = seq // B_CMP_STRIDE

    h = x.reshape(tokens, D_MODEL)
    p_rows = p.reshape(depth, tokens, PLE_DIM)
    for i in range(depth):
        h, za, zb, zkc, zvc, zc, gates, bgate = _head(h, norm_g[i, 0], wi1, wo1, norm_g[i, 1], w_in_r,
                                                      TOKEN_TILE, i)
        lam_init = 0.8 - 0.6 * math.exp(-0.3 * i)
        ya = _diff_attention(za, tbl, diff_lambda[i].astype(F32), diff_subln[i], bsz, seq, lam_init, A_QUERY_TILE)
        kc, vc = _compress(zkc.reshape(bsz, chunk_rows, B_CMP_STRIDE * B_KV),
                           zvc.reshape(bsz, chunk_rows, B_CMP_STRIDE * B_KV), cmp_pos, cmp_w1, cmp_w2, i)
        yb, yc = _nsa_swa_attention(zb, kc, vc, bgate, zc, tbl, swa_sinks[i].astype(F32), ovt, bsz, seq, tq)
        h = _tail(h, ya, yb, yc, gates, p_rows, wb, wo, norm_g[i, 2], wi2, wo2, norm_g[i, 3], wpg, wpe,
                  final_norm, TOKEN_TILE, i, final=(i == depth - 1))
    return h.reshape(bsz, seq, D_MODEL)
```

```python
import functools
import math

import numpy as np
import jax
import jax.numpy as jnp
from jax import lax
from jax.experimental import pallas as pl
from jax.experimental.pallas import tpu as pltpu

F32 = jnp.float32
BF16 = jnp.bfloat16

D_MODEL = 1024
PLE_DIM = 256
D_FF = 2816
NORM_EPS = 1e-6
NEG = -1e30
NUM_BUCKETS = 32
MAX_DISTANCE = 128
A_HEADS = 4
A_HEAD_DIM = 64
B_HEADS = 8
B_GROUPS = 2
B_REP = B_HEADS // B_GROUPS
B_HEAD_DIM = 64
B_CMP_LEN = 32
B_CMP_STRIDE = 16
B_CMP_HIDDEN = 256
B_SEL_BLOCK = 64
B_SEL_TOPK = 16
B_WINDOW = 512
B_SEL_FORCE = 1e6
C_HEADS = 8
C_GROUPS = 2
C_REP = C_HEADS // C_GROUPS
C_HEAD_DIM = 64
C_WINDOW = 128
A_WIDTH = A_HEADS * 2 * A_HEAD_DIM
B_WIDTH = B_HEADS * B_HEAD_DIM
B_KV = B_GROUPS * B_HEAD_DIM
C_WIDTH = C_HEADS * C_HEAD_DIM
C_KV = C_GROUPS * C_HEAD_DIM

LANES = 128
SUBLANES = 8
HEAD_DIM = 64
VMEM_LIMIT = 56 * 1024 * 1024
TOKEN_TILE = 256
A_QUERY_TILE = 256
BC_QUERY_TILE = 128
UNSEL = -float(2 ** 30)
TAKEN = -3e38
FFN_CHUNKS = ((0, 1536), (1536, 1280))
CMP_BAND_BACK = 16
LOG2E = math.log2(math.e)
ONES_ROWS = 16
EXP_HEADROOM = 64.0


def _bucket_thresholds():
    n = np.arange(4 * MAX_DISTANCE)
    max_exact = NUM_BUCKETS // 2
    nf = np.maximum(n, 1).astype(np.float32)
    large = max_exact + (np.log(nf / max_exact) / math.log(MAX_DISTANCE / max_exact)
                         * (NUM_BUCKETS - max_exact)).astype(np.int32)
    bucket = np.where(n < max_exact, n, np.minimum(large, NUM_BUCKETS - 1))
    out = []
    for b in range(1, NUM_BUCKETS):
        hit = np.nonzero(bucket == b)[0]
        if hit.size:
            out.append((b, int(hit[0])))
    return tuple(out)


BUCKET_LO = _bucket_thresholds()
FAR_DIST = BUCKET_LO[-1][1]


def _bias_of_dist(dist, tbl_ref, head, minus_far=False):
    shift = tbl_ref[NUM_BUCKETS - 1, head] if minus_far else 0.0
    out = jnp.full(dist.shape, (tbl_ref[0, head] - shift) * LOG2E, F32)
    for b, lo in BUCKET_LO:
        out = jnp.where(dist >= lo, (tbl_ref[b, head] - shift) * LOG2E, out)
    return out


def _rms(x, g):
    return x * lax.rsqrt(jnp.mean(x * x, axis=-1, keepdims=True) + NORM_EPS) * g


def _dot(a, b):
    return jnp.dot(a, b, preferred_element_type=F32)


def _values_and_ones(vts):
    values = jnp.concatenate(vts, axis=1)
    return jnp.concatenate([values, jnp.ones((ONES_ROWS, values.shape[1]), BF16)], axis=0)


def _online_update(carry, s, vts):
    m, acc = carry
    m_new = jnp.maximum(m, jnp.max(s, axis=0, keepdims=True))
    p = jnp.exp2(s - m_new).astype(BF16)
    return m_new, jnp.exp2(m - m_new) * acc + _dot(_values_and_ones(vts), p)


def _one_pass_update(carry, s, vts):
    m, acc = carry
    top = jnp.max(s, axis=0, keepdims=True)
    p = jnp.exp2(s - m).astype(BF16)
    m_new = jnp.maximum(m, top)
    return (m_new, jnp.exp2(m - m_new) * (acc + _dot(_values_and_ones(vts), p))), jnp.max(top - m) <= EXP_HEADROOM


def _softmax_init(width, queries):
    return jnp.full((1, queries), NEG, F32), jnp.zeros((width + ONES_ROWS, queries), F32)


def _softmax_result(carry):
    _, acc = carry
    width = acc.shape[0] - ONES_ROWS
    return acc[0:width] / acc[width:width + 1]


def _transpose_bf16(x):
    return x.astype(F32).T.astype(BF16)


def _params(*sem):
    return pltpu.CompilerParams(dimension_semantics=sem, vmem_limit_bytes=VMEM_LIMIT)


def _resident(shape, layer=None):
    if layer is None:
        zeros = (0,) * len(shape)
        return pl.BlockSpec(shape, lambda *_: zeros, pipeline_mode=pl.Buffered(1))
    index = (layer,) + (0,) * (len(shape) - 1)
    return pl.BlockSpec((None,) + tuple(shape[1:]), lambda *_: index, pipeline_mode=pl.Buffered(1))


def _smem():
    return pl.BlockSpec(memory_space=pltpu.SMEM)


def _half_ffn(x, g_ref, wi_ref, wo_ref):
    n = _rms(x, g_ref[...]).astype(BF16)
    acc = x
    for lo, width in FFN_CHUNKS:
        gate = _dot(n, wi_ref[:, lo:lo + width])
        up = _dot(n, wi_ref[:, D_FF + lo:D_FF + lo + width])
        act = (gate * jax.nn.sigmoid(gate) * up).astype(BF16)
        acc = acc + 0.5 * _dot(act, wo_ref[lo:lo + width, :])
    return acc


IN_A = 3 * A_WIDTH
IN_B = B_WIDTH + 4 * B_KV
IN_C = C_WIDTH + 2 * C_KV
IN_G = 3 * D_MODEL
IN_SECTIONS = (IN_A, IN_B, B_KV, B_KV, IN_C, IN_G, LANES)


def _head_kernel(h_ref, g0_ref, wi_ref, wo_ref, g1_ref, w_ref,
                 o_ref, za_ref, zb_ref, zkc_ref, zvc_ref, zc_ref, gates_ref, bg_ref):
    x = _half_ffn(h_ref[...], g0_ref, wi_ref, wo_ref)
    o_ref[...] = x
    n = _rms(x, g1_ref[...]).astype(BF16)
    lo = 0
    for ref, width in zip((za_ref, zb_ref, zkc_ref, zvc_ref, zc_ref, gates_ref, bg_ref), IN_SECTIONS):
        ref[...] = _dot(n, w_ref[:, lo:lo + width]).astype(ref.dtype)
        lo += width


def _head(h, g0, wi, wo, g1, w, tm, layer):
    t = h.shape[0]
    dtypes = (BF16, BF16, BF16, BF16, BF16, F32, F32)
    rows = lambda width: pl.BlockSpec((tm, width), lambda i: (i, 0))
    vec = _resident((1, D_MODEL))
    return pl.pallas_call(
        _head_kernel,
        name="head",
        grid=(t // tm,),
        in_specs=[rows(D_MODEL), vec, _resident(wi.shape, layer), _resident(wo.shape, layer), vec,
                  _resident(w.shape, layer)],
        out_specs=[rows(D_MODEL)] + [rows(width) for width in IN_SECTIONS],
        out_shape=[jax.ShapeDtypeStruct(h.shape, F32)]
                  + [jax.ShapeDtypeStruct((t, width), dt) for width, dt in zip(IN_SECTIONS, dtypes)],
        compiler_params=_params("arbitrary"),
    )(h, g0.reshape(1, D_MODEL), wi, wo, g1.reshape(1, D_MODEL), w)


def _tail_kernel(h_ref, ya_ref, yb_ref, yc_ref, gates_ref, p_ref, wb_ref, wo_ref, g2_ref, wi_ref, wo2_ref,
                 g3_ref, wg_ref, we_ref, gf_ref, o_ref, *, final):
    merged = jnp.zeros(h_ref.shape, F32)
    for m, y_ref in enumerate((ya_ref, yb_ref, yc_ref)):
        gate = jax.nn.sigmoid(gates_ref[:, m * D_MODEL:(m + 1) * D_MODEL])
        merged = merged + gate * _dot(y_ref[...], wb_ref[m])
    x = h_ref[...] + _dot(merged.astype(BF16), wo_ref[...])
    x = _half_ffn(x, g2_ref, wi_ref, wo2_ref)
    gate = jax.nn.sigmoid(_dot(_rms(x, g3_ref[...]).astype(BF16), wg_ref[...]))
    x = x + gate * _dot(p_ref[...].astype(BF16), we_ref[...])
    if final:
        x = _rms(x, gf_ref[...])
    o_ref[...] = x


def _tail(h, ya, yb, yc, gates, p, wb, wo, g2, wi, wo2, g3, wg, we, gf, tm, layer, final):
    t = h.shape[0]
    rows = lambda width: pl.BlockSpec((tm, width), lambda i: (i, 0))
    vec = _resident((1, D_MODEL))
    return pl.pallas_call(
        functools.partial(_tail_kernel, final=final),
        name="tail",
        grid=(t // tm,),
        in_specs=[rows(D_MODEL), rows(A_WIDTH), rows(B_WIDTH), rows(C_WIDTH), rows(IN_G),
                  pl.BlockSpec((None, tm, PLE_DIM), lambda i: (layer, i, 0)),
                  _resident(wb.shape, layer), _resident(wo.shape, layer), vec,
                  _resident(wi.shape, layer), _resident(wo2.shape, layer),
                  vec, _resident(wg.shape, layer), _resident(we.shape, layer), vec],
        out_specs=rows(D_MODEL),
        out_shape=jax.ShapeDtypeStruct(h.shape, F32),
        compiler_params=_params("arbitrary"),
    )(h, ya, yb, yc, gates, p, wb, wo, g2.reshape(1, D_MODEL), wi, wo2, g3.reshape(1, D_MODEL), wg, we,
      gf.reshape(1, D_MODEL))


def _diff_kernel(tbl_ref, lam_ref, subg_ref, q_ref, k_ref, v_ref, o_ref, kaug_ref, vt_ref, near_ref,
                 *, tq, lam_init):
    qi = pl.program_id(1)
    width = 2 * A_HEAD_DIM
    n_heads = A_HEADS

    @pl.when((pl.program_id(0) == 0) & (qi == 0))
    def _static_setup():
        lane = lax.broadcasted_iota(jnp.int32, (tq, 2 * LANES), 1)
        key = lax.broadcasted_iota(jnp.int32, (2 * tq, tq), 0)
        qry = lax.broadcasted_iota(jnp.int32, (2 * tq, tq), 1)
        dist = tq + qry - key
        for h in range(n_heads):
            kaug_ref[h, 0:tq, :] = jnp.where(lane == LANES, 1.0, 0.0).astype(BF16)
            kaug_ref[h, tq:, LANES:2 * LANES] = jnp.zeros((kaug_ref.shape[1] - tq, LANES), BF16)
            vt_ref[h, 0] = jnp.zeros(vt_ref.shape[2:], BF16)
            near_ref[h] = jnp.where(dist >= 0, _bias_of_dist(dist, tbl_ref, h, True), NEG)

    @pl.when(qi == 0)
    def _per_batch_setup():
        for h in range(n_heads):
            cols = slice(h * width, (h + 1) * width)
            kaug_ref[h, tq:, 0:LANES] = k_ref[:, cols]
            for c in range(vt_ref.shape[1] - 1):
                vt_ref[h, c + 1] = _transpose_bf16(v_ref[c * tq:(c + 1) * tq, cols])

    zero = jnp.zeros((A_HEAD_DIM, tq), F32)
    flag_row = lax.broadcasted_iota(jnp.int32, (LANES, 2 * tq), 0) == 0
    flags = jnp.where(flag_row, UNSEL, 0.0)
    queries = []
    for h in range(n_heads):
        qt = q_ref[:, h * width:(h + 1) * width].astype(F32).T
        both = jnp.concatenate([jnp.concatenate([qt[0:A_HEAD_DIM], zero], axis=0),
                                jnp.concatenate([zero, qt[A_HEAD_DIM:width]], axis=0)], axis=1)
        queries.append(jnp.concatenate([both, flags], axis=0).astype(BF16))

    def logits_of(first_tile):
        start = pl.multiple_of(first_tile * tq, tq)
        return tuple(_dot(kaug_ref[h, pl.ds(start, 2 * tq), :], queries[h]) for h in range(n_heads))

    def update(first_tile, logits, carries):
        return tuple(_online_update(carries[h], logits[h], [vt_ref[h, first_tile], vt_ref[h, first_tile + 1]])
                     for h in range(n_heads))

    def far(gi, carries, count):
        tiles = [qi - 2 * (gi + c) for c in range(count)]
        fast, safe = carries, True
        for t, logits in [(t, logits_of(t)) for t in tiles]:
            steps = [_one_pass_update(fast[h], logits[h], [vt_ref[h, t], vt_ref[h, t + 1]]) for h in range(n_heads)]
            fast = tuple(step[0] for step in steps)
            for step in steps:
                safe = safe & step[1]

        def redo():
            slow = carries
            for t in tiles:
                slow = update(t, logits_of(t), slow)
            return slow

        return lax.cond(safe, lambda: fast, redo)

    nearest = tuple(s + jnp.concatenate([near_ref[h], near_ref[h]], axis=1) for h, s in enumerate(logits_of(qi)))
    carries = update(qi, nearest, tuple(_softmax_init(width, 2 * tq) for _ in range(n_heads)))
    n_far = qi // 2
    carries = lax.fori_loop(0, n_far // 2, lambda p, c: far(1 + 2 * p, c, 2), carries)
    carries = lax.fori_loop(0, n_far % 2, lambda _, c: far(n_far, c, 1), carries)
    lp = lam_ref[...]
    lam = (jnp.exp(jnp.sum(lp[0:1] * lp[1:2], axis=-1, keepdims=True))
           - jnp.exp(jnp.sum(lp[2:3] * lp[3:4], axis=-1, keepdims=True)) + lam_init)
    for h in range(n_heads):
        out = _softmax_result(carries[h])
        o = out[:, 0:tq] - lam * out[:, tq:2 * tq]
        o = o * lax.rsqrt(jnp.mean(o * o, axis=0, keepdims=True) + NORM_EPS) * (1.0 - lam_init)
        o_ref[:, h * width:(h + 1) * width] = (o.T * subg_ref[...]).astype(o_ref.dtype)


def _diff_attention(za, tbl, lam_p, subg, bsz, seq, lam_init, tq):
    nq = seq // tq
    width = 2 * A_HEAD_DIM
    return pl.pallas_call(
        functools.partial(_diff_kernel, tq=tq, lam_init=lam_init),
        name="diff_attn",
        grid=(bsz, nq),
        in_specs=[_smem(), _resident(lam_p.shape), _resident((1, width)),
                  pl.BlockSpec((tq, A_WIDTH), lambda b, i: (b * nq + i, 0)),
                  pl.BlockSpec((seq, A_WIDTH), lambda b, i: (b, 1)),
                  pl.BlockSpec((seq, A_WIDTH), lambda b, i: (b, 2))],
        out_specs=pl.BlockSpec((tq, A_WIDTH), lambda b, i: (b * nq + i, 0)),
        out_shape=jax.ShapeDtypeStruct((bsz * seq, A_WIDTH), BF16),
        scratch_shapes=[pltpu.VMEM((A_HEADS, tq + seq, 2 * LANES), BF16),
                        pltpu.VMEM((A_HEADS, nq + 1, width, tq), BF16),
                        pltpu.VMEM((A_HEADS, 2 * tq, tq), F32)],
        compiler_params=_params("arbitrary", "arbitrary"),
    )(tbl, lam_p, subg.reshape(1, width), za, za, za)


def _compress_kernel(xk_ref, xv_ref, pos_ref, w1_ref, w2_ref, kc_ref, vc_ref):
    rows = xk_ref.shape[0]
    for t, (x_ref, o_ref) in enumerate(((xk_ref, kc_ref), (xv_ref, vc_ref))):
        x = x_ref[...].astype(F32)
        first = _dot((x + pos_ref[t, 0]).astype(BF16), w1_ref[t, 0])
        second = _dot((x + pos_ref[t, 1]).astype(BF16), w1_ref[t, 1])
        hidden = first + pltpu.roll(second, rows - 1, 0)
        o_ref[...] = _dot(jax.nn.gelu(hidden).astype(BF16), w2_ref[t]).astype(o_ref.dtype)


def _compress(xk, xv, pos, w1, w2, layer):
    bsz, rows, width = xk.shape
    xspec = pl.BlockSpec((None, rows, width), lambda b: (b, 0, 0))
    ospec = pl.BlockSpec((None, rows, B_KV), lambda b: (b, 0, 0))
    return pl.pallas_call(
        _compress_kernel,
        name="nsa_compress",
        grid=(bsz,),
        in_specs=[xspec, xspec, _resident(pos.shape, layer), _resident(w1.shape, layer), _resident(w2.shape, layer)],
        out_specs=[ospec, ospec],
        out_shape=[jax.ShapeDtypeStruct((bsz, rows, B_KV), BF16)] * 2,
        compiler_params=_params("arbitrary"),
    )(xk, xv, pos, w1, w2)


def _group_queries_t(q, group, rep):
    tq = q.shape[0]
    zero = jnp.zeros((HEAD_DIM, tq), F32)
    cols = []
    for r in range(rep):
        head = group * rep + r
        slab = q[:, (head // 2) * LANES:(head // 2 + 1) * LANES].astype(F32).T
        part = slab[(head % 2) * HEAD_DIM:(head % 2 + 1) * HEAD_DIM]
        cols.append(jnp.concatenate([part, zero] if group == 0 else [zero, part], axis=0))
    return jnp.concatenate(cols, axis=1).astype(BF16)


def _build_near_bias(near_ref, tbl_ref, head0, groups, rep, tq, window_is_tile, minus_far):
    key = lax.broadcasted_iota(jnp.int32, (2 * tq, tq), 0)
    qry = lax.broadcasted_iota(jnp.int32, (2 * tq, tq), 1)
    dist = tq + qry - key
    visible = (dist >= 0) & (dist < tq) if window_is_tile else dist >= 0
    for g in range(groups):
        for r in range(rep):
            head = head0 + g * rep + r
            near_ref[g, :, r * tq:(r + 1) * tq] = jnp.where(visible, _bias_of_dist(dist, tbl_ref, head, minus_far), NEG)


def _swa_tile(q, k_prev, k_diag, v_prev, v_diag, bias_ref, sink_ref, qi, tq):
    cols = C_REP * tq
    prev_bias = jnp.where(qi > 0, 0.0, NEG)
    vp_t = _transpose_bf16(v_prev)
    vd_t = _transpose_bf16(v_diag)
    heads_out = []
    for g in range(C_GROUPS):
        half = slice(g * HEAD_DIM, (g + 1) * HEAD_DIM)
        qg = _group_queries_t(q, g, C_REP)
        s = jnp.concatenate([_dot(k_prev, qg) + bias_ref[g, 0:tq, :] + prev_bias,
                             _dot(k_diag, qg) + bias_ref[g, tq:2 * tq, :]], axis=0)
        m, acc = _online_update(_softmax_init(HEAD_DIM, cols), s, [vp_t[half], vd_t[half]])
        for r in range(C_REP):
            cc = slice(r * tq, (r + 1) * tq)
            sink = sink_ref[g * C_REP + r] * LOG2E
            m_all = jnp.maximum(m[:, cc], sink)
            scale = jnp.exp2(m[:, cc] - m_all)
            denom = acc[HEAD_DIM:HEAD_DIM + 1, cc] * scale + jnp.exp2(sink - m_all)
            heads_out.append(acc[0:HEAD_DIM, cc] * (scale / denom))
    return jnp.concatenate(heads_out, axis=0).T


SEL_GROUP = 4
SEL_PAD = SEL_GROUP - 1
WIN_PAD = B_WINDOW // BC_QUERY_TILE


def _nsa_kernel(tbl_ref, sink_ref, q_ref, ks_ref, vs_ref, kw_ref, vw_ref, kc_ref, vc_ref, bg_ref, ovt_ref,
                cq_ref, ckp_ref, ckd_ref, cvp_ref, cvd_ref, o_ref, oc_ref,
                kaug_ref, kwp_ref, vst_ref, vwt_ref, vct_ref, near_ref, edge_ref, band_ref, sc_ref, swab_ref,
                *, tq, seq):
    first_batch = pl.program_id(0) == 0
    qi = pl.program_id(1)
    n_cmp_pad = seq // B_CMP_STRIDE
    n_sel = seq // B_SEL_BLOCK
    top_k = min(B_SEL_TOPK, n_sel)
    nq = seq // tq
    cols = B_REP * tq
    cmp_per_tile = tq // B_CMP_STRIDE
    band = CMP_BAND_BACK + cmp_per_tile

    @pl.when(first_batch & (qi == 0))
    def _static_setup():
        _build_near_bias(near_ref, tbl_ref, A_HEADS, B_GROUPS, B_REP, tq, False, True)
        _build_near_bias(swab_ref, tbl_ref, A_HEADS + B_HEADS, C_GROUPS, C_REP, tq, True, False)
        key = lax.broadcasted_iota(jnp.int32, (tq, cols), 0)
        qry = lax.broadcasted_iota(jnp.int32, (tq, cols), 1) % tq
        edge_ref[...] = jnp.where(qry < key, 0.0, NEG)
        blk = lax.broadcasted_iota(jnp.int32, (band, tq), 0) - CMP_BAND_BACK
        dist = lax.broadcasted_iota(jnp.int32, (band, tq), 1) - (blk * B_CMP_STRIDE + B_CMP_LEN - 1)
        for g in range(B_GROUPS):
            for r in range(B_REP):
                head = A_HEADS + g * B_REP + r
                band_ref[g, :, r * tq:(r + 1) * tq] = jnp.where(dist >= 0, _bias_of_dist(dist, tbl_ref, head, True), NEG)
        sc_ref[:, 0:CMP_BAND_BACK, :] = jnp.zeros((B_GROUPS, CMP_BAND_BACK, cols), F32)
        pad = SEL_PAD * tq
        row = lax.broadcasted_iota(jnp.int32, (pad + seq, LANES), 0)
        lane = lax.broadcasted_iota(jnp.int32, (pad + seq, LANES), 1)
        blk_id = jnp.where(row < pad, LANES - 1, (row - pad) // B_SEL_BLOCK)
        kaug_ref[:, LANES:2 * LANES] = jnp.where(blk_id == lane, 1.0, 0.0).astype(BF16)
        kaug_ref[0:pad, 0:LANES] = jnp.zeros((pad, LANES), BF16)
        pad_lane = lax.broadcasted_iota(jnp.int32, (WIN_PAD * tq, 2 * LANES), 1) == LANES
        kwp_ref[0:WIN_PAD * tq, :] = jnp.where(pad_lane, 1.0, 0.0).astype(BF16)
        kwp_ref[WIN_PAD * tq:, LANES:2 * LANES] = jnp.zeros((seq, LANES), BF16)
        for c in range(SEL_PAD):
            vst_ref[c] = jnp.zeros(vst_ref.shape[1:], BF16)
        for c in range(WIN_PAD):
            vwt_ref[c] = jnp.zeros(vwt_ref.shape[1:], BF16)

    @pl.when(qi == 0)
    def _per_batch_setup():
        kaug_ref[SEL_PAD * tq:, 0:LANES] = ks_ref[...]
        kwp_ref[WIN_PAD * tq:, 0:LANES] = kw_ref[...]
        for c in range(nq):
            vst_ref[SEL_PAD + c] = _transpose_bf16(vs_ref[c * tq:(c + 1) * tq, :])
            vwt_ref[WIN_PAD + c] = _transpose_bf16(vw_ref[c * tq:(c + 1) * tq, :])
        for c in range(n_cmp_pad // LANES):
            vct_ref[:, c * LANES:(c + 1) * LANES] = _transpose_bf16(vc_ref[c * LANES:(c + 1) * LANES, :])

    q = q_ref[...]
    gates_t = jax.nn.sigmoid(bg_ref[...]).T
    first_cmp = qi * cmp_per_tile
    cmp_row = lax.broadcasted_iota(jnp.int32, (n_cmp_pad, cols), 0)
    cmp_visible = cmp_row < first_cmp + cmp_per_tile
    blk = lax.broadcasted_iota(jnp.int32, (n_sel, tq), 0)
    cur = (qi * tq + lax.broadcasted_iota(jnp.int32, (n_sel, tq), 1)) // B_SEL_BLOCK
    forced = (blk == 0) | (blk == cur) | (blk == cur - 1)
    future = blk > cur
    halves = [slice(g * HEAD_DIM, (g + 1) * HEAD_DIM) for g in range(B_GROUPS)]

    qgs, o_cs, imps = [], [], []
    for g in range(B_GROUPS):
        qg = _group_queries_t(q, g, B_REP)
        sc_ref[g, CMP_BAND_BACK:, :] = _dot(kc_ref[...], qg)
        band_rows = pl.ds(pl.multiple_of(first_cmp, SUBLANES), band)
        sc_ref[g, band_rows, :] = sc_ref[g, band_rows, :] + band_ref[g]
        s = jnp.where(cmp_visible, sc_ref[g, CMP_BAND_BACK:, :], NEG)
        top = jnp.max(s, axis=0, keepdims=True)
        e = jnp.exp2(s - top)
        inv = jnp.where(top > 0.5 * NEG, 1.0 / jnp.sum(e, axis=0, keepdims=True), 0.0)
        o_cs.append(_dot(vct_ref[halves[g], :], e.astype(BF16)) * inv)
        p_sum = e[:, 0:tq] * inv[:, 0:tq]
        for r in range(1, B_REP):
            p_sum = p_sum + e[:, r * tq:(r + 1) * tq] * inv[:, r * tq:(r + 1) * tq]
        imp = _dot(ovt_ref[...], p_sum.astype(BF16))
        imp = jnp.where(forced, B_SEL_FORCE, jnp.where(future, -B_SEL_FORCE, imp))
        qgs.append(qg)
        imps.append(imp)

    blk_f = blk.astype(F32)
    taken = [jnp.zeros((n_sel, tq), jnp.bool_) for _ in range(B_GROUPS)]
    left = list(imps)
    for _ in range(top_k):
        for g in range(B_GROUPS):
            best = jnp.max(left[g], axis=0, keepdims=True)
            first = jnp.min(jnp.where(left[g] == best, blk_f, float(n_sel)), axis=0, keepdims=True)
            pick = blk_f == first
            taken[g] = taken[g] | pick
            left[g] = jnp.where(pick, TAKEN, left[g])
    q_augs = []
    for g in range(B_GROUPS):
        sel_bias = jnp.concatenate([jnp.where(taken[g], 0.0, UNSEL),
                                    jnp.full((LANES - n_sel, tq), UNSEL, F32)], axis=0).astype(BF16)
        q_augs.append(jnp.concatenate([qgs[g], jnp.concatenate([sel_bias] * B_REP, axis=1)], axis=0))

    pair = cols
    chains = [(g, slice(0, cols)) for g in range(B_GROUPS)]
    heads_per_chain = pair // tq

    def sel_logits(first_tile):
        start = pl.multiple_of(first_tile * tq, tq)
        keys = kaug_ref[pl.ds(start, SEL_GROUP * tq), :]
        return tuple(_dot(keys, q_augs[g][:, cc]) for g, cc in chains)

    def sel_update(first_tile, logits, carries):
        return tuple(_online_update(carry, s, [vst_ref[first_tile + t, halves[g], :] for t in range(SEL_GROUP)])
                     for (g, cc), s, carry in zip(chains, logits, carries))

    start = pl.multiple_of(qi * tq, tq)
    win_keys = kwp_ref[pl.ds(start, (WIN_PAD + 1) * tq), :]
    pad_flags = jnp.where(lax.broadcasted_iota(jnp.int32, (LANES, cols), 0) == 0, UNSEL, 0.0).astype(BF16)
    partial = []
    for g, cc in chains:
        s_w = _dot(win_keys, jnp.concatenate([qgs[g], pad_flags], axis=0)[:, cc])
        slabs = []
        for t in range(WIN_PAD + 1):
            slab = s_w[t * tq:(t + 1) * tq]
            if t == 0:
                slab = slab + edge_ref[:, cc]
            if t >= WIN_PAD - 1:
                slab = slab + near_ref[g, (t - WIN_PAD + 1) * tq:(t - WIN_PAD + 2) * tq, cc]
            slabs.append(slab)
        o_w = _softmax_result(_online_update(_softmax_init(HEAD_DIM, pair), jnp.concatenate(slabs, axis=0),
                                             [vwt_ref[qi + t, halves[g], :] for t in range(WIN_PAD + 1)]))
        o_c = o_cs[g][:, cc]
        heads = []
        for rr in range(heads_per_chain):
            hc = slice(rr * tq, (rr + 1) * tq)
            c0 = (g * B_REP + rr) * 3
            heads.append(gates_t[c0:c0 + 1] * o_c[:, hc] + gates_t[c0 + 2:c0 + 3] * o_w[:, hc])
        partial.append(heads)

    oc_ref[...] = _swa_tile(cq_ref[...], ckp_ref[...], ckd_ref[...], cvp_ref[...], cvd_ref[...],
                            swab_ref, sink_ref, qi, tq).astype(oc_ref.dtype)

    def sel_far(gi, carries, count):
        tiles = [qi - SEL_GROUP * (gi + c) for c in range(count)]
        fast, safe = carries, True
        for t, logits in [(t, sel_logits(t)) for t in tiles]:
            steps = [_one_pass_update(carry, s, [vst_ref[t + k, halves[g], :] for k in range(SEL_GROUP)])
                     for (g, cc), s, carry in zip(chains, logits, fast)]
            fast = tuple(step[0] for step in steps)
            for step in steps:
                safe = safe & step[1]

        def redo():
            slow = carries
            for t in tiles:
                slow = sel_update(t, sel_logits(t), slow)
            return slow

        return lax.cond(safe, lambda: fast, redo)

    far_rows = (SEL_GROUP - 2) * tq
    nearest = tuple(jnp.concatenate([s[0:far_rows], s[far_rows:] + near_ref[g, :, cc]], axis=0)
                    for (g, cc), s in zip(chains, sel_logits(qi)))
    sel = sel_update(qi, nearest, tuple(_softmax_init(HEAD_DIM, pair) for _ in chains))
    n_far = qi // SEL_GROUP
    sel = lax.fori_loop(0, n_far // 2, lambda p, c: sel_far(1 + 2 * p, c, 2), sel)
    sel = lax.fori_loop(0, n_far % 2, lambda _, c: sel_far(n_far, c, 1), sel)

    heads_out = []
    for ci, (g, cc) in enumerate(chains):
        o_s = _softmax_result(sel[ci])
        for rr in range(heads_per_chain):
            hc = slice(rr * tq, (rr + 1) * tq)
            c1 = (g * B_REP + rr) * 3 + 1
            heads_out.append(partial[ci][rr] + gates_t[c1:c1 + 1] * o_s[:, hc])
    o_ref[...] = jnp.concatenate(heads_out, axis=0).T.astype(o_ref.dtype)


def _nsa_swa_attention(zb, kc, vc, bgate, zc, tbl, sinks, ovt, bsz, seq, tq):
    nq = seq // tq
    n_cmp_pad = seq // B_CMP_STRIDE
    cols = B_REP * tq
    kv = lambda col: pl.BlockSpec((seq, LANES), lambda b, i: (b, col))
    cmp_spec = pl.BlockSpec((None, n_cmp_pad, B_KV), lambda b, i: (b, 0, 0))
    qcols = B_WIDTH // LANES
    ccols = C_WIDTH // LANES
    wide = lambda width: pl.BlockSpec((tq, width), lambda b, i: (b * nq + i, 0))
    prev = lambda col: pl.BlockSpec((tq, LANES), lambda b, i: (b * nq + jnp.maximum(i - 1, 0), col))
    diag = lambda col: pl.BlockSpec((tq, LANES), lambda b, i: (b * nq + i, col))
    return pl.pallas_call(
        functools.partial(_nsa_kernel, tq=tq, seq=seq),
        name="nsa_attn",
        grid=(bsz, nq),
        in_specs=[_smem(), _smem(),
                  wide(B_WIDTH),
                  kv(qcols), kv(qcols + 1), kv(qcols + 2), kv(qcols + 3),
                  cmp_spec, cmp_spec,
                  wide(LANES),
                  _resident(ovt.shape),
                  wide(C_WIDTH), prev(ccols), diag(ccols), prev(ccols + 1), diag(ccols + 1)],
        out_specs=[wide(B_WIDTH), wide(C_WIDTH)],
        out_shape=[jax.ShapeDtypeStruct((bsz * seq, B_WIDTH), BF16),
                   jax.ShapeDtypeStruct((bsz * seq, C_WIDTH), BF16)],
        scratch_shapes=[pltpu.VMEM((SEL_PAD * tq + seq, 2 * LANES), BF16),
                        pltpu.VMEM((WIN_PAD * tq + seq, 2 * LANES), BF16),
                        pltpu.VMEM((SEL_PAD + nq, B_KV, tq), BF16),
                        pltpu.VMEM((WIN_PAD + nq, B_KV, tq), BF16),
                        pltpu.VMEM((B_KV, n_cmp_pad), BF16),
                        pltpu.VMEM((B_GROUPS, 2 * tq, cols), F32),
                        pltpu.VMEM((tq, cols), F32),
                        pltpu.VMEM((B_GROUPS, CMP_BAND_BACK + tq // B_CMP_STRIDE, cols), F32),
                        pltpu.VMEM((B_GROUPS, CMP_BAND_BACK + n_cmp_pad, cols), F32),
                        pltpu.VMEM((C_GROUPS, 2 * tq, C_REP * tq), F32)],
        compiler_params=_params("arbitrary", "arbitrary"),
    )(tbl, sinks, zb, zb, zb, zb, zb, kc, vc, bgate, ovt, zc, zc, zc, zc, zc)


def _prep_w_in(w_in):
    sizes = (A_WIDTH, A_WIDTH, A_WIDTH, B_WIDTH, B_KV, B_KV, B_KV, B_KV, B_KV, B_KV, 3 * B_HEADS,
             C_WIDTH, C_KV, C_KV, D_MODEL, D_MODEL, D_MODEL)
    offs = np.concatenate([[0], np.cumsum(sizes)])
    (aq, ak, av, bq, bkc, bvc, bks, bvs, bkw, bvw, bgate, cq, ck, cv, ga, gb, gc) = [
        w_in[..., offs[i]:offs[i + 1]] for i in range(len(sizes))]
    scale = HEAD_DIM ** -0.5 * LOG2E
    bgate = jnp.pad(bgate, ((0, 0), (0, 0), (0, LANES - 3 * B_HEADS)))
    cols = [aq * scale, ak, av, bq * scale, bks, bvs, bkw, bvw, bkc, bvc, cq * scale, ck, cv, ga, gb, gc, bgate]
    return jnp.concatenate([c.astype(BF16) for c in cols], axis=-1)


def _prep_compress(cmp_pos, cmp_w1, cmp_w2):
    depth = cmp_pos.shape[0]
    half = B_CMP_STRIDE
    pos = cmp_pos.reshape(depth, 2, 2, half, 1, B_HEAD_DIM)
    pos = jnp.broadcast_to(pos, (depth, 2, 2, half, B_GROUPS, B_HEAD_DIM)).reshape(depth, 2, 2, 1, half * B_KV)
    w1 = cmp_w1.reshape(depth, 2, 2, half, 1, B_HEAD_DIM, 1, B_CMP_HIDDEN)
    eye = jnp.eye(B_GROUPS, dtype=cmp_w1.dtype).reshape(1, 1, 1, 1, B_GROUPS, 1, B_GROUPS, 1)
    w1 = (w1 * eye).reshape(depth, 2, 2, half * B_KV, B_GROUPS * B_CMP_HIDDEN)
    w2 = cmp_w2.reshape(depth, 2, 1, B_CMP_HIDDEN, 1, B_HEAD_DIM)
    eye2 = jnp.eye(B_GROUPS, dtype=cmp_w2.dtype).reshape(1, 1, B_GROUPS, 1, B_GROUPS, 1)
    w2 = (w2 * eye2).reshape(depth, 2, B_GROUPS * B_CMP_HIDDEN, B_KV)
    return pos.astype(F32), w1.astype(BF16), w2.astype(BF16)


def _overlap_t(seq):
    n_cmp_pad = seq // B_CMP_STRIDE
    start = np.arange(n_cmp_pad) * B_CMP_STRIDE
    sel = np.arange(seq // B_SEL_BLOCK) * B_SEL_BLOCK
    ov = (start[None, :] < sel[:, None] + B_SEL_BLOCK) & (start[None, :] + B_CMP_LEN > sel[:, None])
    ov &= (np.arange(n_cmp_pad) < n_cmp_pad - 1)[None, :]
    return jnp.asarray(ov.astype(np.float32), dtype=BF16)


def kernel(x, p, norm_g, ffn1_wi, ffn1_wo, w_in, diff_lambda, diff_subln, nsa_cmp_pos, nsa_cmp_w1,
           nsa_cmp_w2, swa_sinks, w_branch, w_out, ffn2_wi, ffn2_wo, w_ple, w_ple_gate, rel_bias, final_norm):
    bsz, seq, _ = x.shape
    depth = norm_g.shape[0]
    tokens = bsz * seq
    tq = BC_QUERY_TILE
    assert seq % (B_CMP_STRIDE * LANES) == 0 and seq // B_SEL_BLOCK < LANES and tokens % TOKEN_TILE == 0
    assert FAR_DIST + B_CMP_LEN - 1 <= CMP_BAND_BACK * B_CMP_STRIDE and FAR_DIST <= tq
    assert WIN_PAD * tq == B_WINDOW and tq == C_WINDOW and tq % (B_CMP_STRIDE * SUBLANES) == 0

    wi1, wo1 = ffn1_wi.astype(BF16), ffn1_wo.astype(BF16)
    wi2, wo2 = ffn2_wi.astype(BF16), ffn2_wo.astype(BF16)
    w_in_r = _prep_w_in(w_in)
    cmp_pos, cmp_w1, cmp_w2 = _prep_compress(nsa_cmp_pos, nsa_cmp_w1, nsa_cmp_w2)
    wb, wo = w_branch.astype(BF16), w_out.astype(BF16)
    wpg, wpe = w_ple_gate.astype(BF16), w_ple.astype(BF16)
    tbl = rel_bias.astype(F32)
    ovt = _overlap_t(seq)
    chunk_rows = seq // B_CMP_STRIDE

    h = x.reshape(tokens, D_MODEL)
    p_rows = p.reshape(depth, tokens, PLE_DIM)
    for i in range(depth):
        h, za, zb, zkc, zvc, zc, gates, bgate = _head(h, norm_g[i, 0], wi1, wo1, norm_g[i, 1], w_in_r,
                                                      TOKEN_TILE, i)
        lam_init = 0.8 - 0.6 * math.exp(-0.3 * i)
        ya = _diff_attention(za, tbl, diff_lambda[i].astype(F32), diff_subln[i], bsz, seq, lam_init, A_QUERY_TILE)
        kc, vc = _compress(zkc.reshape(bsz, chunk_rows, B_CMP_STRIDE * B_KV),
                           zvc.reshape(bsz, chunk_rows, B_CMP_STRIDE * B_KV), cmp_pos, cmp_w1, cmp_w2, i)
        yb, yc = _nsa_swa_attention(zb, kc, vc, bgate, zc, tbl, swa_sinks[i].astype(F32), ovt, bsz, seq, tq)
        h = _tail(h, ya, yb, yc, gates, p_rows, wb, wo, norm_g[i, 2], wi2, wo2, norm_g[i, 3], wpg, wpe,
                  final_norm, 2 * TOKEN_TILE, i, final=(i == depth - 1))
    return h.reshape(bsz, seq, D_MODEL)
```

```python
import functools
import math

import numpy as np
import jax
import jax.numpy as jnp
from jax import lax
from jax.experimental import pallas as pl
from jax.experimental.pallas import tpu as pltpu

F32 = jnp.float32
BF16 = jnp.bfloat16

D_MODEL = 1024
PLE_DIM = 256
D_FF = 2816
NORM_EPS = 1e-6
NEG = -1e30
NUM_BUCKETS = 32
MAX_DISTANCE = 128
A_HEADS = 4
A_HEAD_DIM = 64
B_HEADS = 8
B_GROUPS = 2
B_REP = B_HEADS // B_GROUPS
B_HEAD_DIM = 64
B_CMP_LEN = 32
B_CMP_STRIDE = 16
B_CMP_HIDDEN = 256
B_SEL_BLOCK = 64
B_SEL_TOPK = 16
B_WINDOW = 512
B_SEL_FORCE = 1e6
C_HEADS = 8
C_GROUPS = 2
C_REP = C_HEADS // C_GROUPS
C_HEAD_DIM = 64
C_WINDOW = 128
A_WIDTH = A_HEADS * 2 * A_HEAD_DIM
B_WIDTH = B_HEADS * B_HEAD_DIM
B_KV = B_GROUPS * B_HEAD_DIM
C_WIDTH = C_HEADS * C_HEAD_DIM
C_KV = C_GROUPS * C_HEAD_DIM

LANES = 128
SUBLANES = 8
HEAD_DIM = 64
VMEM_LIMIT = 58 * 1024 * 1024
HEAD_TOKEN_TILE = 512
TAIL_TOKEN_TILE = 512
A_QUERY_TILE = 256
BC_QUERY_TILE = 128
UNSEL = -float(2 ** 30)
TAKEN = -3e38
FFN_CHUNKS = ((0, 1536), (1536, 1280))
CMP_BAND_BACK = 16
LOG2E = math.log2(math.e)
ONES_ROWS = 16
EXP_HEADROOM = 64.0


def _bucket_thresholds():
    n = np.arange(4 * MAX_DISTANCE)
    max_exact = NUM_BUCKETS // 2
    nf = np.maximum(n, 1).astype(np.float32)
    large = max_exact + (np.log(nf / max_exact) / math.log(MAX_DISTANCE / max_exact)
                         * (NUM_BUCKETS - max_exact)).astype(np.int32)
    bucket = np.where(n < max_exact, n, np.minimum(large, NUM_BUCKETS - 1))
    out = []
    for b in range(1, NUM_BUCKETS):
        hit = np.nonzero(bucket == b)[0]
        if hit.size:
            out.append((b, int(hit[0])))
    return tuple(out)


BUCKET_LO = _bucket_thresholds()
FAR_DIST = BUCKET_LO[-1][1]


def _bias_of_dist(dist, tbl_ref, head, minus_far=False):
    shift = tbl_ref[NUM_BUCKETS - 1, head] if minus_far else 0.0
    out = jnp.full(dist.shape, (tbl_ref[0, head] - shift) * LOG2E, F32)
    for b, lo in BUCKET_LO:
        out = jnp.where(dist >= lo, (tbl_ref[b, head] - shift) * LOG2E, out)
    return out


def _rms(x, g):
    return x * lax.rsqrt(jnp.mean(x * x, axis=-1, keepdims=True) + NORM_EPS) * g


def _dot(a, b):
    return jnp.dot(a, b, preferred_element_type=F32)


def _values_and_ones(vts):
    values = jnp.concatenate(vts, axis=1)
    return jnp.concatenate([values, jnp.ones((ONES_ROWS, values.shape[1]), BF16)], axis=0)


def _online_update(carry, s, vts):
    m, acc = carry
    m_new = jnp.maximum(m, jnp.max(s, axis=0, keepdims=True))
    p = jnp.exp2(s - m_new).astype(BF16)
    return m_new, jnp.exp2(m - m_new) * acc + _dot(_values_and_ones(vts), p)


def _one_pass_update(carry, s, vts):
    m, acc = carry
    top = jnp.max(s, axis=0, keepdims=True)
    p = jnp.exp2(s - m).astype(BF16)
    m_new = jnp.maximum(m, top)
    return (m_new, jnp.exp2(m - m_new) * (acc + _dot(_values_and_ones(vts), p))), jnp.max(top - m) <= EXP_HEADROOM


def _softmax_init(width, queries):
    return jnp.full((1, queries), NEG, F32), jnp.zeros((width + ONES_ROWS, queries), F32)


def _softmax_result(carry):
    _, acc = carry
    width = acc.shape[0] - ONES_ROWS
    return acc[0:width] / acc[width:width + 1]


def _transpose_bf16(x):
    return x.astype(F32).T.astype(BF16)


def _params(*sem):
    return pltpu.CompilerParams(dimension_semantics=sem, vmem_limit_bytes=VMEM_LIMIT)


def _resident(shape, layer=None):
    if layer is None:
        zeros = (0,) * len(shape)
        return pl.BlockSpec(shape, lambda *_: zeros, pipeline_mode=pl.Buffered(1))
    index = (layer,) + (0,) * (len(shape) - 1)
    return pl.BlockSpec((None,) + tuple(shape[1:]), lambda *_: index, pipeline_mode=pl.Buffered(1))


def _smem():
    return pl.BlockSpec(memory_space=pltpu.SMEM)


def _half_ffn(x, g_ref, wi_ref, wo_ref):
    n = _rms(x, g_ref[...]).astype(BF16)
    acc = x
    for lo, width in FFN_CHUNKS:
        gate = _dot(n, wi_ref[:, lo:lo + width])
        up = _dot(n, wi_ref[:, D_FF + lo:D_FF + lo + width])
        act = (gate * jax.nn.sigmoid(gate) * up).astype(BF16)
        acc = acc + 0.5 * _dot(act, wo_ref[lo:lo + width, :])
    return acc


IN_A = 3 * A_WIDTH
IN_B = B_WIDTH + 4 * B_KV
IN_C = C_WIDTH + 2 * C_KV
IN_G = 3 * D_MODEL
IN_SECTIONS = (IN_A, IN_B, B_KV, B_KV, IN_C, IN_G, LANES)


def _head_kernel(h_ref, g0_ref, wi_ref, wo_ref, g1_ref, w_ref,
                 o_ref, za_ref, zb_ref, zkc_ref, zvc_ref, zc_ref, gates_ref, bg_ref):
    x = _half_ffn(h_ref[...], g0_ref, wi_ref, wo_ref)
    o_ref[...] = x
    n = _rms(x, g1_ref[...]).astype(BF16)
    lo = 0
    for ref, width in zip((za_ref, zb_ref, zkc_ref, zvc_ref, zc_ref, gates_ref, bg_ref), IN_SECTIONS):
        ref[...] = _dot(n, w_ref[:, lo:lo + width]).astype(ref.dtype)
        lo += width


def _head(h, g0, wi, wo, g1, w, tm, layer):
    t = h.shape[0]
    dtypes = (BF16, BF16, BF16, BF16, BF16, BF16, F32)
    rows = lambda width: pl.BlockSpec((tm, width), lambda i: (i, 0))
    vec = _resident((1, D_MODEL))
    return pl.pallas_call(
        _head_kernel,
        name="head",
        grid=(t // tm,),
        in_specs=[rows(D_MODEL), vec, _resident(wi.shape, layer), _resident(wo.shape, layer), vec,
                  _resident(w.shape, layer)],
        out_specs=[rows(D_MODEL)] + [rows(width) for width in IN_SECTIONS],
        out_shape=[jax.ShapeDtypeStruct(h.shape, F32)]
                  + [jax.ShapeDtypeStruct((t, width), dt) for width, dt in zip(IN_SECTIONS, dtypes)],
        compiler_params=_params("arbitrary"),
    )(h, g0.reshape(1, D_MODEL), wi, wo, g1.reshape(1, D_MODEL), w)


def _tail_kernel(h_ref, ya_ref, yb_ref, yc_ref, gates_ref, p_ref, wb_ref, wo_ref, g2_ref, wi_ref, wo2_ref,
                 g3_ref, wg_ref, we_ref, gf_ref, o_ref, *, final):
    merged = jnp.zeros(h_ref.shape, F32)
    for m, y_ref in enumerate((ya_ref, yb_ref, yc_ref)):
        gate = jax.nn.sigmoid(gates_ref[:, m * D_MODEL:(m + 1) * D_MODEL].astype(F32))
        merged = merged + gate * _dot(y_ref[...], wb_ref[m])
    x = h_ref[...] + _dot(merged.astype(BF16), wo_ref[...])
    x = _half_ffn(x, g2_ref, wi_ref, wo2_ref)
    gate = jax.nn.sigmoid(_dot(_rms(x, g3_ref[...]).astype(BF16), wg_ref[...]))
    x = x + gate * _dot(p_ref[...].astype(BF16), we_ref[...])
    if final:
        x = _rms(x, gf_ref[...])
    o_ref[...] = x


def _tail(h, ya, yb, yc, gates, p, wb, wo, g2, wi, wo2, g3, wg, we, gf, tm, layer, final):
    t = h.shape[0]
    rows = lambda width: pl.BlockSpec((tm, width), lambda i: (i, 0))
    vec = _resident((1, D_MODEL))
    return pl.pallas_call(
        functools.partial(_tail_kernel, final=final),
        name="tail",
        grid=(t // tm,),
        in_specs=[rows(D_MODEL), rows(A_WIDTH), rows(B_WIDTH), rows(C_WIDTH), rows(IN_G),
                  pl.BlockSpec((None, tm, PLE_DIM), lambda i: (layer, i, 0)),
                  _resident(wb.shape, layer), _resident(wo.shape, layer), vec,
                  _resident(wi.shape, layer), _resident(wo2.shape, layer),
                  vec, _resident(wg.shape, layer), _resident(we.shape, layer), vec],
        out_specs=rows(D_MODEL),
        out_shape=jax.ShapeDtypeStruct(h.shape, F32),
        compiler_params=_params("arbitrary"),
    )(h, ya, yb, yc, gates, p, wb, wo, g2.reshape(1, D_MODEL), wi, wo2, g3.reshape(1, D_MODEL), wg, we,
      gf.reshape(1, D_MODEL))


def _diff_kernel(tbl_ref, lam_ref, subg_ref, q_ref, k_ref, v_ref, o_ref, kaug_ref, vt_ref, near_ref,
                 *, tq, lam_init):
    qi = pl.program_id(1)
    width = 2 * A_HEAD_DIM
    n_heads = A_HEADS

    @pl.when((pl.program_id(0) == 0) & (qi == 0))
    def _static_setup():
        lane = lax.broadcasted_iota(jnp.int32, (tq, 2 * LANES), 1)
        key = lax.broadcasted_iota(jnp.int32, (2 * tq, tq), 0)
        qry = lax.broadcasted_iota(jnp.int32, (2 * tq, tq), 1)
        dist = tq + qry - key
        for h in range(n_heads):
            kaug_ref[h, 0:tq, :] = jnp.where(lane == LANES, 1.0, 0.0).astype(BF16)
            kaug_ref[h, tq:, LANES:2 * LANES] = jnp.zeros((kaug_ref.shape[1] - tq, LANES), BF16)
            vt_ref[h, 0] = jnp.zeros(vt_ref.shape[2:], BF16)
            near_ref[h] = jnp.where(dist >= 0, _bias_of_dist(dist, tbl_ref, h, True), NEG)

    @pl.when(qi == 0)
    def _per_batch_setup():
        for h in range(n_heads):
            cols = slice(h * width, (h + 1) * width)
            kaug_ref[h, tq:, 0:LANES] = k_ref[:, cols]
            for c in range(vt_ref.shape[1] - 1):
                vt_ref[h, c + 1] = _transpose_bf16(v_ref[c * tq:(c + 1) * tq, cols])

    zero = jnp.zeros((A_HEAD_DIM, tq), F32)
    flag_row = lax.broadcasted_iota(jnp.int32, (LANES, 2 * tq), 0) == 0
    flags = jnp.where(flag_row, UNSEL, 0.0)
    queries = []
    for h in range(n_heads):
        qt = q_ref[:, h * width:(h + 1) * width].astype(F32).T
        both = jnp.concatenate([jnp.concatenate([qt[0:A_HEAD_DIM], zero], axis=0),
                                jnp.concatenate([zero, qt[A_HEAD_DIM:width]], axis=0)], axis=1)
        queries.append(jnp.concatenate([both, flags], axis=0).astype(BF16))

    def logits_of(first_tile):
        start = pl.multiple_of(first_tile * tq, tq)
        return tuple(_dot(kaug_ref[h, pl.ds(start, 2 * tq), :], queries[h]) for h in range(n_heads))

    def update(first_tile, logits, carries):
        return tuple(_online_update(carries[h], logits[h], [vt_ref[h, first_tile], vt_ref[h, first_tile + 1]])
                     for h in range(n_heads))

    def far(gi, carries, count):
        tiles = [qi - 2 * (gi + c) for c in range(count)]
        fast, safe = carries, True
        for t, logits in [(t, logits_of(t)) for t in tiles]:
            steps = [_one_pass_update(fast[h], logits[h], [vt_ref[h, t], vt_ref[h, t + 1]]) for h in range(n_heads)]
            fast = tuple(step[0] for step in steps)
            for step in steps:
                safe = safe & step[1]

        def redo():
            slow = carries
            for t in tiles:
                slow = update(t, logits_of(t), slow)
            return slow

        return lax.cond(safe, lambda: fast, redo)

    nearest = tuple(s + jnp.concatenate([near_ref[h], near_ref[h]], axis=1) for h, s in enumerate(logits_of(qi)))
    carries = update(qi, nearest, tuple(_softmax_init(width, 2 * tq) for _ in range(n_heads)))
    n_far = qi // 2
    carries = lax.fori_loop(0, n_far // 2, lambda p, c: far(1 + 2 * p, c, 2), carries)
    carries = lax.fori_loop(0, n_far % 2, lambda _, c: far(n_far, c, 1), carries)
    lp = lam_ref[...]
    lam = (jnp.exp(jnp.sum(lp[0:1] * lp[1:2], axis=-1, keepdims=True))
           - jnp.exp(jnp.sum(lp[2:3] * lp[3:4], axis=-1, keepdims=True)) + lam_init)
    for h in range(n_heads):
        out = _softmax_result(carries[h])
        o = out[:, 0:tq] - lam * out[:, tq:2 * tq]
        o = o * lax.rsqrt(jnp.mean(o * o, axis=0, keepdims=True) + NORM_EPS) * (1.0 - lam_init)
        o_ref[:, h * width:(h + 1) * width] = (o.T * subg_ref[...]).astype(o_ref.dtype)


def _diff_attention(za, tbl, lam_p, subg, bsz, seq, lam_init, tq):
    nq = seq // tq
    width = 2 * A_HEAD_DIM
    return pl.pallas_call(
        functools.partial(_diff_kernel, tq=tq, lam_init=lam_init),
        name="diff_attn",
        grid=(bsz, nq),
        in_specs=[_smem(), _resident(lam_p.shape), _resident((1, width)),
                  pl.BlockSpec((tq, A_WIDTH), lambda b, i: (b * nq + i, 0)),
                  pl.BlockSpec((seq, A_WIDTH), lambda b, i: (b, 1)),
                  pl.BlockSpec((seq, A_WIDTH), lambda b, i: (b, 2))],
        out_specs=pl.BlockSpec((tq, A_WIDTH), lambda b, i: (b * nq + i, 0)),
        out_shape=jax.ShapeDtypeStruct((bsz * seq, A_WIDTH), BF16),
        scratch_shapes=[pltpu.VMEM((A_HEADS, tq + seq, 2 * LANES), BF16),
                        pltpu.VMEM((A_HEADS, nq + 1, width, tq), BF16),
                        pltpu.VMEM((A_HEADS, 2 * tq, tq), F32)],
        compiler_params=_params("arbitrary", "arbitrary"),
    )(tbl, lam_p, subg.reshape(1, width), za, za, za)


def _compress_kernel(xk_ref, xv_ref, pos_ref, w1_ref, w2_ref, kc_ref, vc_ref):
    rows = xk_ref.shape[0]
    for t, (x_ref, o_ref) in enumerate(((xk_ref, kc_ref), (xv_ref, vc_ref))):
        x = x_ref[...].astype(F32)
        first = _dot((x + pos_ref[t, 0]).astype(BF16), w1_ref[t, 0])
        second = _dot((x + pos_ref[t, 1]).astype(BF16), w1_ref[t, 1])
        hidden = first + pltpu.roll(second, rows - 1, 0)
        o_ref[...] = _dot(jax.nn.gelu(hidden).astype(BF16), w2_ref[t]).astype(o_ref.dtype)


def _compress(xk, xv, pos, w1, w2, layer):
    bsz, rows, width = xk.shape
    xspec = pl.BlockSpec((None, rows, width), lambda b: (b, 0, 0))
    ospec = pl.BlockSpec((None, rows, B_KV), lambda b: (b, 0, 0))
    return pl.pallas_call(
        _compress_kernel,
        name="nsa_compress",
        grid=(bsz,),
        in_specs=[xspec, xspec, _resident(pos.shape, layer), _resident(w1.shape, layer), _resident(w2.shape, layer)],
        out_specs=[ospec, ospec],
        out_shape=[jax.ShapeDtypeStruct((bsz, rows, B_KV), BF16)] * 2,
        compiler_params=_params("arbitrary"),
    )(xk, xv, pos, w1, w2)


def _group_queries_t(q, group, rep):
    tq = q.shape[0]
    zero = jnp.zeros((HEAD_DIM, tq), F32)
    cols = []
    for r in range(rep):
        head = group * rep + r
        slab = q[:, (head // 2) * LANES:(head // 2 + 1) * LANES].astype(F32).T
        part = slab[(head % 2) * HEAD_DIM:(head % 2 + 1) * HEAD_DIM]
        cols.append(jnp.concatenate([part, zero] if group == 0 else [zero, part], axis=0))
    return jnp.concatenate(cols, axis=1).astype(BF16)


def _build_near_bias(near_ref, tbl_ref, head0, groups, rep, tq, window_is_tile, minus_far):
    key = lax.broadcasted_iota(jnp.int32, (2 * tq, tq), 0)
    qry = lax.broadcasted_iota(jnp.int32, (2 * tq, tq), 1)
    dist = tq + qry - key
    visible = (dist >= 0) & (dist < tq) if window_is_tile else dist >= 0
    for g in range(groups):
        for r in range(rep):
            head = head0 + g * rep + r
            near_ref[g, :, r * tq:(r + 1) * tq] = jnp.where(visible, _bias_of_dist(dist, tbl_ref, head, minus_far), NEG)


def _swa_tile(q, k_prev, k_diag, v_prev, v_diag, bias_ref, sink_ref, qi, tq):
    cols = C_REP * tq
    prev_bias = jnp.where(qi > 0, 0.0, NEG)
    vp_t = _transpose_bf16(v_prev)
    vd_t = _transpose_bf16(v_diag)
    heads_out = []
    for g in range(C_GROUPS):
        half = slice(g * HEAD_DIM, (g + 1) * HEAD_DIM)
        qg = _group_queries_t(q, g, C_REP)
        s = jnp.concatenate([_dot(k_prev, qg) + bias_ref[g, 0:tq, :] + prev_bias,
                             _dot(k_diag, qg) + bias_ref[g, tq:2 * tq, :]], axis=0)
        m, acc = _online_update(_softmax_init(HEAD_DIM, cols), s, [vp_t[half], vd_t[half]])
        for r in range(C_REP):
            cc = slice(r * tq, (r + 1) * tq)
            sink = sink_ref[g * C_REP + r] * LOG2E
            m_all = jnp.maximum(m[:, cc], sink)
            scale = jnp.exp2(m[:, cc] - m_all)
            denom = acc[HEAD_DIM:HEAD_DIM + 1, cc] * scale + jnp.exp2(sink - m_all)
            heads_out.append(acc[0:HEAD_DIM, cc] * (scale / denom))
    return jnp.concatenate(heads_out, axis=0).T


SEL_GROUP = 4
SEL_PAD = SEL_GROUP - 1
WIN_PAD = B_WINDOW // BC_QUERY_TILE


def _nsa_kernel(tbl_ref, sink_ref, q_ref, ks_ref, vs_ref, kw_ref, vw_ref, kc_ref, vc_ref, bg_ref, ovt_ref,
                cq_ref, ckp_ref, ckd_ref, cvp_ref, cvd_ref, o_ref, oc_ref,
                kaug_ref, kwp_ref, vst_ref, vwt_ref, vct_ref, near_ref, edge_ref, band_ref, sc_ref, swab_ref,
                *, tq, seq):
    first_batch = pl.program_id(0) == 0
    qi = pl.program_id(1)
    n_cmp_pad = seq // B_CMP_STRIDE
    n_sel = seq // B_SEL_BLOCK
    top_k = min(B_SEL_TOPK, n_sel)
    nq = seq // tq
    cols = B_REP * tq
    cmp_per_tile = tq // B_CMP_STRIDE
    band = CMP_BAND_BACK + cmp_per_tile

    @pl.when(first_batch & (qi == 0))
    def _static_setup():
        _build_near_bias(near_ref, tbl_ref, A_HEADS, B_GROUPS, B_REP, tq, False, True)
        _build_near_bias(swab_ref, tbl_ref, A_HEADS + B_HEADS, C_GROUPS, C_REP, tq, True, False)
        key = lax.broadcasted_iota(jnp.int32, (tq, cols), 0)
        qry = lax.broadcasted_iota(jnp.int32, (tq, cols), 1) % tq
        edge_ref[...] = jnp.where(qry < key, 0.0, NEG)
        blk = lax.broadcasted_iota(jnp.int32, (band, tq), 0) - CMP_BAND_BACK
        dist = lax.broadcasted_iota(jnp.int32, (band, tq), 1) - (blk * B_CMP_STRIDE + B_CMP_LEN - 1)
        for g in range(B_GROUPS):
            for r in range(B_REP):
                head = A_HEADS + g * B_REP + r
                band_ref[g, :, r * tq:(r + 1) * tq] = jnp.where(dist >= 0, _bias_of_dist(dist, tbl_ref, head, True), NEG)
        sc_ref[:, 0:CMP_BAND_BACK, :] = jnp.zeros((B_GROUPS, CMP_BAND_BACK, cols), F32)
        pad = SEL_PAD * tq
        row = lax.broadcasted_iota(jnp.int32, (pad + seq, LANES), 0)
        lane = lax.broadcasted_iota(jnp.int32, (pad + seq, LANES), 1)
        blk_id = jnp.where(row < pad, LANES - 1, (row - pad) // B_SEL_BLOCK)
        kaug_ref[:, LANES:2 * LANES] = jnp.where(blk_id == lane, 1.0, 0.0).astype(BF16)
        kaug_ref[0:pad, 0:LANES] = jnp.zeros((pad, LANES), BF16)
        pad_lane = lax.broadcasted_iota(jnp.int32, (WIN_PAD * tq, 2 * LANES), 1) == LANES
        kwp_ref[0:WIN_PAD * tq, :] = jnp.where(pad_lane, 1.0, 0.0).astype(BF16)
        kwp_ref[WIN_PAD * tq:, LANES:2 * LANES] = jnp.zeros((seq, LANES), BF16)
        for c in range(SEL_PAD):
            vst_ref[c] = jnp.zeros(vst_ref.shape[1:], BF16)
        for c in range(WIN_PAD):
            vwt_ref[c] = jnp.zeros(vwt_ref.shape[1:], BF16)

    @pl.when(qi == 0)
    def _per_batch_setup():
        kaug_ref[SEL_PAD * tq:, 0:LANES] = ks_ref[...]
        kwp_ref[WIN_PAD * tq:, 0:LANES] = kw_ref[...]
        for c in range(nq):
            vst_ref[SEL_PAD + c] = _transpose_bf16(vs_ref[c * tq:(c + 1) * tq, :])
            vwt_ref[WIN_PAD + c] = _transpose_bf16(vw_ref[c * tq:(c + 1) * tq, :])
        for c in range(n_cmp_pad // LANES):
            vct_ref[:, c * LANES:(c + 1) * LANES] = _transpose_bf16(vc_ref[c * LANES:(c + 1) * LANES, :])

    q = q_ref[...]
    gates_t = jax.nn.sigmoid(bg_ref[...]).T
    first_cmp = qi * cmp_per_tile
    cmp_row = lax.broadcasted_iota(jnp.int32, (n_cmp_pad, cols), 0)
    cmp_visible = cmp_row < first_cmp + cmp_per_tile
    blk = lax.broadcasted_iota(jnp.int32, (n_sel, tq), 0)
    cur = (qi * tq + lax.broadcasted_iota(jnp.int32, (n_sel, tq), 1)) // B_SEL_BLOCK
    forced = (blk == 0) | (blk == cur) | (blk == cur - 1)
    future = blk > cur
    halves = [slice(g * HEAD_DIM, (g + 1) * HEAD_DIM) for g in range(B_GROUPS)]

    qgs, o_cs, imps = [], [], []
    for g in range(B_GROUPS):
        qg = _group_queries_t(q, g, B_REP)
        sc_ref[g, CMP_BAND_BACK:, :] = _dot(kc_ref[...], qg)
        band_rows = pl.ds(pl.multiple_of(first_cmp, SUBLANES), band)
        sc_ref[g, band_rows, :] = sc_ref[g, band_rows, :] + band_ref[g]
        s = jnp.where(cmp_visible, sc_ref[g, CMP_BAND_BACK:, :], NEG)
        top = jnp.max(s, axis=0, keepdims=True)
        e = jnp.exp2(s - top)
        inv = jnp.where(top > 0.5 * NEG, 1.0 / jnp.sum(e, axis=0, keepdims=True), 0.0)
        o_cs.append(_dot(vct_ref[halves[g], :], e.astype(BF16)) * inv)
        p_sum = e[:, 0:tq] * inv[:, 0:tq]
        for r in range(1, B_REP):
            p_sum = p_sum + e[:, r * tq:(r + 1) * tq] * inv[:, r * tq:(r + 1) * tq]
        imp = _dot(ovt_ref[...], p_sum.astype(BF16))
        imp = jnp.where(forced, B_SEL_FORCE, jnp.where(future, -B_SEL_FORCE, imp))
        qgs.append(qg)
        imps.append(imp)

    blk_f = blk.astype(F32)
    taken = [jnp.zeros((n_sel, tq), jnp.bool_) for _ in range(B_GROUPS)]
    left = list(imps)
    for _ in range(top_k):
        for g in range(B_GROUPS):
            best = jnp.max(left[g], axis=0, keepdims=True)
            first = jnp.min(jnp.where(left[g] == best, blk_f, float(n_sel)), axis=0, keepdims=True)
            pick = blk_f == first
            taken[g] = taken[g] | pick
            left[g] = jnp.where(pick, TAKEN, left[g])
    q_augs = []
    for g in range(B_GROUPS):
        sel_bias = jnp.concatenate([jnp.where(taken[g], 0.0, UNSEL),
                                    jnp.full((LANES - n_sel, tq), UNSEL, F32)], axis=0).astype(BF16)
        q_augs.append(jnp.concatenate([qgs[g], jnp.concatenate([sel_bias] * B_REP, axis=1)], axis=0))

    pair = cols
    chains = [(g, slice(0, cols)) for g in range(B_GROUPS)]
    heads_per_chain = pair // tq

    def sel_logits(first_tile):
        start = pl.multiple_of(first_tile * tq, tq)
        keys = kaug_ref[pl.ds(start, SEL_GROUP * tq), :]
        return tuple(_dot(keys, q_augs[g][:, cc]) for g, cc in chains)

    def sel_update(first_tile, logits, carries):
        return tuple(_online_update(carry, s, [vst_ref[first_tile + t, halves[g], :] for t in range(SEL_GROUP)])
                     for (g, cc), s, carry in zip(chains, logits, carries))

    start = pl.multiple_of(qi * tq, tq)
    win_keys = kwp_ref[pl.ds(start, (WIN_PAD + 1) * tq), :]
    pad_flags = jnp.where(lax.broadcasted_iota(jnp.int32, (LANES, cols), 0) == 0, UNSEL, 0.0).astype(BF16)
    partial = []
    for g, cc in chains:
        s_w = _dot(win_keys, jnp.concatenate([qgs[g], pad_flags], axis=0)[:, cc])
        slabs = []
        for t in range(WIN_PAD + 1):
            slab = s_w[t * tq:(t + 1) * tq]
            if t == 0:
                slab = slab + edge_ref[:, cc]
            if t >= WIN_PAD - 1:
                slab = slab + near_ref[g, (t - WIN_PAD + 1) * tq:(t - WIN_PAD + 2) * tq, cc]
            slabs.append(slab)
        o_w = _softmax_result(_online_update(_softmax_init(HEAD_DIM, pair), jnp.concatenate(slabs, axis=0),
                                             [vwt_ref[qi + t, halves[g], :] for t in range(WIN_PAD + 1)]))
        o_c = o_cs[g][:, cc]
        heads = []
        for rr in range(heads_per_chain):
            hc = slice(rr * tq, (rr + 1) * tq)
            c0 = (g * B_REP + rr) * 3
            heads.append(gates_t[c0:c0 + 1] * o_c[:, hc] + gates_t[c0 + 2:c0 + 3] * o_w[:, hc])
        partial.append(heads)

    oc_ref[...] = _swa_tile(cq_ref[...], ckp_ref[...], ckd_ref[...], cvp_ref[...], cvd_ref[...],
                            swab_ref, sink_ref, qi, tq).astype(oc_ref.dtype)

    def sel_far(gi, carries, count):
        tiles = [qi - SEL_GROUP * (gi + c) for c in range(count)]
        fast, safe = carries, True
        for t, logits in [(t, sel_logits(t)) for t in tiles]:
            steps = [_one_pass_update(carry, s, [vst_ref[t + k, halves[g], :] for k in range(SEL_GROUP)])
                     for (g, cc), s, carry in zip(chains, logits, fast)]
            fast = tuple(step[0] for step in steps)
            for step in steps:
                safe = safe & step[1]

        def redo():
            slow = carries
            for t in tiles:
                slow = sel_update(t, sel_logits(t), slow)
            return slow

        return lax.cond(safe, lambda: fast, redo)

    far_rows = (SEL_GROUP - 2) * tq
    nearest = tuple(jnp.concatenate([s[0:far_rows], s[far_rows:] + near_ref[g, :, cc]], axis=0)
                    for (g, cc), s in zip(chains, sel_logits(qi)))
    sel = sel_update(qi, nearest, tuple(_softmax_init(HEAD_DIM, pair) for _ in chains))
    n_far = qi // SEL_GROUP
    sel = lax.fori_loop(0, n_far // 2, lambda p, c: sel_far(1 + 2 * p, c, 2), sel)
    sel = lax.fori_loop(0, n_far % 2, lambda _, c: sel_far(n_far, c, 1), sel)

    heads_out = []
    for ci, (g, cc) in enumerate(chains):
        o_s = _softmax_result(sel[ci])
        for rr in range(heads_per_chain):
            hc = slice(rr * tq, (rr + 1) * tq)
            c1 = (g * B_REP + rr) * 3 + 1
            heads_out.append(partial[ci][rr] + gates_t[c1:c1 + 1] * o_s[:, hc])
    o_ref[...] = jnp.concatenate(heads_out, axis=0).T.astype(o_ref.dtype)


def _nsa_swa_attention(zb, kc, vc, bgate, zc, tbl, sinks, ovt, bsz, seq, tq):
    nq = seq // tq
    n_cmp_pad = seq // B_CMP_STRIDE
    cols = B_REP * tq
    kv = lambda col: pl.BlockSpec((seq, LANES), lambda b, i: (b, col))
    cmp_spec = pl.BlockSpec((None, n_cmp_pad, B_KV), lambda b, i: (b, 0, 0))
    qcols = B_WIDTH // LANES
    ccols = C_WIDTH // LANES
    wide = lambda width: pl.BlockSpec((tq, width), lambda b, i: (b * nq + i, 0))
    prev = lambda col: pl.BlockSpec((tq, LANES), lambda b, i: (b * nq + jnp.maximum(i - 1, 0), col))
    diag = lambda col: pl.BlockSpec((tq, LANES), lambda b, i: (b * nq + i, col))
    return pl.pallas_call(
        functools.partial(_nsa_kernel, tq=tq, seq=seq),
        name="nsa_attn",
        grid=(bsz, nq),
        in_specs=[_smem(), _smem(),
                  wide(B_WIDTH),
                  kv(qcols), kv(qcols + 1), kv(qcols + 2), kv(qcols + 3),
                  cmp_spec, cmp_spec,
                  wide(LANES),
                  _resident(ovt.shape),
                  wide(C_WIDTH), prev(ccols), diag(ccols), prev(ccols + 1), diag(ccols + 1)],
        out_specs=[wide(B_WIDTH), wide(C_WIDTH)],
        out_shape=[jax.ShapeDtypeStruct((bsz * seq, B_WIDTH), BF16),
                   jax.ShapeDtypeStruct((bsz * seq, C_WIDTH), BF16)],
        scratch_shapes=[pltpu.VMEM((SEL_PAD * tq + seq, 2 * LANES), BF16),
                        pltpu.VMEM((WIN_PAD * tq + seq, 2 * LANES), BF16),
                        pltpu.VMEM((SEL_PAD + nq, B_KV, tq), BF16),
                        pltpu.VMEM((WIN_PAD + nq, B_KV, tq), BF16),
                        pltpu.VMEM((B_KV, n_cmp_pad), BF16),
                        pltpu.VMEM((B_GROUPS, 2 * tq, cols), F32),
                        pltpu.VMEM((tq, cols), F32),
                        pltpu.VMEM((B_GROUPS, CMP_BAND_BACK + tq // B_CMP_STRIDE, cols), F32),
                        pltpu.VMEM((B_GROUPS, CMP_BAND_BACK + n_cmp_pad, cols), F32),
                        pltpu.VMEM((C_GROUPS, 2 * tq, C_REP * tq), F32)],
        compiler_params=_params("arbitrary", "arbitrary"),
    )(tbl, sinks, zb, zb, zb, zb, zb, kc, vc, bgate, ovt, zc, zc, zc, zc, zc)


def _prep_w_in(w_in):
    sizes = (A_WIDTH, A_WIDTH, A_WIDTH, B_WIDTH, B_KV, B_KV, B_KV, B_KV, B_KV, B_KV, 3 * B_HEADS,
             C_WIDTH, C_KV, C_KV, D_MODEL, D_MODEL, D_MODEL)
    offs = np.concatenate([[0], np.cumsum(sizes)])
    (aq, ak, av, bq, bkc, bvc, bks, bvs, bkw, bvw, bgate, cq, ck, cv, ga, gb, gc) = [
        w_in[..., offs[i]:offs[i + 1]] for i in range(len(sizes))]
    scale = HEAD_DIM ** -0.5 * LOG2E
    bgate = jnp.pad(bgate, ((0, 0), (0, 0), (0, LANES - 3 * B_HEADS)))
    cols = [aq * scale, ak, av, bq * scale, bks, bvs, bkw, bvw, bkc, bvc, cq * scale, ck, cv, ga, gb, gc, bgate]
    return jnp.concatenate([c.astype(BF16) for c in cols], axis=-1)


def _prep_compress(cmp_pos, cmp_w1, cmp_w2):
    depth = cmp_pos.shape[0]
    half = B_CMP_STRIDE
    pos = cmp_pos.reshape(depth, 2, 2, half, 1, B_HEAD_DIM)
    pos = jnp.broadcast_to(pos, (depth, 2, 2, half, B_GROUPS, B_HEAD_DIM)).reshape(depth, 2, 2, 1, half * B_KV)
    w1 = cmp_w1.reshape(depth, 2, 2, half, 1, B_HEAD_DIM, 1, B_CMP_HIDDEN)
    eye = jnp.eye(B_GROUPS, dtype=cmp_w1.dtype).reshape(1, 1, 1, 1, B_GROUPS, 1, B_GROUPS, 1)
    w1 = (w1 * eye).reshape(depth, 2, 2, half * B_KV, B_GROUPS * B_CMP_HIDDEN)
    w2 = cmp_w2.reshape(depth, 2, 1, B_CMP_HIDDEN, 1, B_HEAD_DIM)
    eye2 = jnp.eye(B_GROUPS, dtype=cmp_w2.dtype).reshape(1, 1, B_GROUPS, 1, B_GROUPS, 1)
    w2 = (w2 * eye2).reshape(depth, 2, B_GROUPS * B_CMP_HIDDEN, B_KV)
    return pos.astype(F32), w1.astype(BF16), w2.astype(BF16)


def _overlap_t(seq):
    n_cmp_pad = seq // B_CMP_STRIDE
    start = np.arange(n_cmp_pad) * B_CMP_STRIDE
    sel = np.arange(seq // B_SEL_BLOCK) * B_SEL_BLOCK
    ov = (start[None, :] < sel[:, None] + B_SEL_BLOCK) & (start[None, :] + B_CMP_LEN > sel[:, None])
    ov &= (np.arange(n_cmp_pad) < n_cmp_pad - 1)[None, :]
    return jnp.asarray(ov.astype(np.float32), dtype=BF16)


def kernel(x, p, norm_g, ffn1_wi, ffn1_wo, w_in, diff_lambda, diff_subln, nsa_cmp_pos, nsa_cmp_w1,
           nsa_cmp_w2, swa_sinks, w_branch, w_out, ffn2_wi, ffn2_wo, w_ple, w_ple_gate, rel_bias, final_norm):
    bsz, seq, _ = x.shape
    depth = norm_g.shape[0]
    tokens = bsz * seq
    tq = BC_QUERY_TILE
    assert seq % (B_CMP_STRIDE * LANES) == 0 and seq // B_SEL_BLOCK < LANES and tokens % TAIL_TOKEN_TILE == 0
    assert FAR_DIST + B_CMP_LEN - 1 <= CMP_BAND_BACK * B_CMP_STRIDE and FAR_DIST <= tq
    assert WIN_PAD * tq == B_WINDOW and tq == C_WINDOW and tq % (B_CMP_STRIDE * SUBLANES) == 0

    wi1, wo1 = ffn1_wi.astype(BF16), ffn1_wo.astype(BF16)
    wi2, wo2 = ffn2_wi.astype(BF16), ffn2_wo.astype(BF16)
    w_in_r = _prep_w_in(w_in)
    cmp_pos, cmp_w1, cmp_w2 = _prep_compress(nsa_cmp_pos, nsa_cmp_w1, nsa_cmp_w2)
    wb, wo = w_branch.astype(BF16), w_out.astype(BF16)
    wpg, wpe = w_ple_gate.astype(BF16), w_ple.astype(BF16)
    tbl = rel_bias.astype(F32)
    ovt = _overlap_t(seq)
    chunk_rows = seq // B_CMP_STRIDE

    h = x.reshape(tokens, D_MODEL)
    p_rows = p.reshape(depth, tokens, PLE_DIM)
    for i in range(depth):
        h, za, zb, zkc, zvc, zc, gates, bgate = _head(h, norm_g[i, 0], wi1, wo1, norm_g[i, 1], w_in_r,
                                                      HEAD_TOKEN_TILE, i)
        lam_init = 0.8 - 0.6 * math.exp(-0.3 * i)
        ya = _diff_attention(za, tbl, diff_lambda[i].astype(F32), diff_subln[i], bsz, seq, lam_init, A_QUERY_TILE)
        kc, vc = _compress(zkc.reshape(bsz, chunk_rows, B_CMP_STRIDE * B_KV),
                           zvc.reshape(bsz, chunk_rows, B_CMP_STRIDE * B_KV), cmp_pos, cmp_w1, cmp_w2, i)
        yb, yc = _nsa_swa_attention(zb, kc, vc, bgate, zc, tbl, swa_sinks[i].astype(F32), ovt, bsz, seq, tq)
        h = _tail(h, ya, yb, yc, gates, p_rows, wb, wo, norm_g[i, 2], wi2, wo2, norm_g[i, 3], wpg, wpe,
                  final_norm, TAIL_TOKEN_TILE, i, final=(i == depth - 1))
    return h.reshape(bsz, seq, D_MODEL)
```

```python
import functools
import math

import numpy as np
import jax
import jax.numpy as jnp
from jax import lax
from jax.experimental import pallas as pl
from jax.experimental.pallas import tpu as pltpu

F32 = jnp.float32
BF16 = jnp.bfloat16

D_MODEL = 1024
PLE_DIM = 256
D_FF = 2816
NORM_EPS = 1e-6
NEG = -1e30
NUM_BUCKETS = 32
MAX_DISTANCE = 128
A_HEADS = 4
A_HEAD_DIM = 64
B_HEADS = 8
B_GROUPS = 2
B_REP = B_HEADS // B_GROUPS
B_HEAD_DIM = 64
B_CMP_LEN = 32
B_CMP_STRIDE = 16
B_CMP_HIDDEN = 256
B_SEL_BLOCK = 64
B_SEL_TOPK = 16
B_WINDOW = 512
B_SEL_FORCE = 1e6
C_HEADS = 8
C_GROUPS = 2
C_REP = C_HEADS // C_GROUPS
C_HEAD_DIM = 64
C_WINDOW = 128
A_WIDTH = A_HEADS * 2 * A_HEAD_DIM
B_WIDTH = B_HEADS * B_HEAD_DIM
B_KV = B_GROUPS * B_HEAD_DIM
C_WIDTH = C_HEADS * C_HEAD_DIM
C_KV = C_GROUPS * C_HEAD_DIM

LANES = 128
SUBLANES = 8
HEAD_DIM = 64
VMEM_LIMIT = 56 * 1024 * 1024
HEAD_TOKEN_TILE = 256
TAIL_TOKEN_TILE = 512
A_QUERY_TILE = 256
BC_QUERY_TILE = 128
UNSEL = -float(2 ** 30)
TAKEN = -3e38
FFN_CHUNKS = ((0, 1536), (1536, 1280))
CMP_BAND_BACK = 16
LOG2E = math.log2(math.e)
ONES_ROWS = 16
EXP_HEADROOM = 64.0


def _bucket_thresholds():
    n = np.arange(4 * MAX_DISTANCE)
    max_exact = NUM_BUCKETS // 2
    nf = np.maximum(n, 1).astype(np.float32)
    large = max_exact + (np.log(nf / max_exact) / math.log(MAX_DISTANCE / max_exact)
                         * (NUM_BUCKETS - max_exact)).astype(np.int32)
    bucket = np.where(n < max_exact, n, np.minimum(large, NUM_BUCKETS - 1))
    out = []
    for b in range(1, NUM_BUCKETS):
        hit = np.nonzero(bucket == b)[0]
        if hit.size:
            out.append((b, int(hit[0])))
    return tuple(out)


BUCKET_LO = _bucket_thresholds()
FAR_DIST = BUCKET_LO[-1][1]


def _bias_of_dist(dist, tbl_ref, head, minus_far=False):
    shift = tbl_ref[NUM_BUCKETS - 1, head] if minus_far else 0.0
    out = jnp.full(dist.shape, (tbl_ref[0, head] - shift) * LOG2E, F32)
    for b, lo in BUCKET_LO:
        out = jnp.where(dist >= lo, (tbl_ref[b, head] - shift) * LOG2E, out)
    return out


def _rms(x, g):
    return x * lax.rsqrt(jnp.mean(x * x, axis=-1, keepdims=True) + NORM_EPS) * g


def _dot(a, b):
    return jnp.dot(a, b, preferred_element_type=F32)


def _values_and_ones(vts):
    values = jnp.concatenate(vts, axis=1)
    return jnp.concatenate([values, jnp.ones((ONES_ROWS, values.shape[1]), BF16)], axis=0)


def _online_update(carry, s, vts):
    m, acc = carry
    m_new = jnp.maximum(m, jnp.max(s, axis=0, keepdims=True))
    p = jnp.exp2(s - m_new).astype(BF16)
    return m_new, jnp.exp2(m - m_new) * acc + _dot(_values_and_ones(vts), p)


def _one_pass_update(carry, s, vts):
    m, acc = carry
    top = jnp.max(s, axis=0, keepdims=True)
    p = jnp.exp2(s - m).astype(BF16)
    m_new = jnp.maximum(m, top)
    return (m_new, jnp.exp2(m - m_new) * (acc + _dot(_values_and_ones(vts), p))), jnp.max(top - m) <= EXP_HEADROOM


def _softmax_init(width, queries):
    return jnp.full((1, queries), NEG, F32), jnp.zeros((width + ONES_ROWS, queries), F32)


def _softmax_result(carry):
    _, acc = carry
    width = acc.shape[0] - ONES_ROWS
    return acc[0:width] / acc[width:width + 1]


def _transpose_bf16(x):
    return x.astype(F32).T.astype(BF16)


def _params(*sem):
    return pltpu.CompilerParams(dimension_semantics=sem, vmem_limit_bytes=VMEM_LIMIT)


def _resident(shape, layer=None):
    if layer is None:
        zeros = (0,) * len(shape)
        return pl.BlockSpec(shape, lambda *_: zeros, pipeline_mode=pl.Buffered(1))
    index = (layer,) + (0,) * (len(shape) - 1)
    return pl.BlockSpec((None,) + tuple(shape[1:]), lambda *_: index, pipeline_mode=pl.Buffered(1))


def _smem():
    return pl.BlockSpec(memory_space=pltpu.SMEM)


def _half_ffn(x, g_ref, wi_ref, wo_ref):
    n = _rms(x, g_ref[...]).astype(BF16)
    acc = x
    for lo, width in FFN_CHUNKS:
        gate = _dot(n, wi_ref[:, lo:lo + width])
        up = _dot(n, wi_ref[:, D_FF + lo:D_FF + lo + width])
        act = (gate * jax.nn.sigmoid(gate) * up).astype(BF16)
        acc = acc + 0.5 * _dot(act, wo_ref[lo:lo + width, :])
    return acc


IN_A = 3 * A_WIDTH
IN_B = B_WIDTH + 4 * B_KV
IN_C = C_WIDTH + 2 * C_KV
IN_G = 3 * D_MODEL
IN_SECTIONS = (IN_A, IN_B, B_KV, B_KV, IN_C, IN_G, LANES)


def _head_kernel(h_ref, g0_ref, wi_ref, wo_ref, g1_ref, w_ref,
                 o_ref, za_ref, zb_ref, zkc_ref, zvc_ref, zc_ref, gates_ref, bg_ref):
    x = _half_ffn(h_ref[...], g0_ref, wi_ref, wo_ref)
    o_ref[...] = x
    n = _rms(x, g1_ref[...]).astype(BF16)
    lo = 0
    for ref, width in zip((za_ref, zb_ref, zkc_ref, zvc_ref, zc_ref, gates_ref, bg_ref), IN_SECTIONS):
        ref[...] = _dot(n, w_ref[:, lo:lo + width]).astype(ref.dtype)
        lo += width


def _head(h, g0, wi, wo, g1, w, tm, layer):
    t = h.shape[0]
    dtypes = (BF16, BF16, BF16, BF16, BF16, F32, F32)
    rows = lambda width: pl.BlockSpec((tm, width), lambda i: (i, 0))
    vec = _resident((1, D_MODEL))
    return pl.pallas_call(
        _head_kernel,
        name="head",
        grid=(t // tm,),
        in_specs=[rows(D_MODEL), vec, _resident(wi.shape, layer), _resident(wo.shape, layer), vec,
                  _resident(w.shape, layer)],
        out_specs=[rows(D_MODEL)] + [rows(width) for width in IN_SECTIONS],
        out_shape=[jax.ShapeDtypeStruct(h.shape, F32)]
                  + [jax.ShapeDtypeStruct((t, width), dt) for width, dt in zip(IN_SECTIONS, dtypes)],
        compiler_params=_params("arbitrary"),
    )(h, g0.reshape(1, D_MODEL), wi, wo, g1.reshape(1, D_MODEL), w)


def _tail_kernel(h_ref, ya_ref, yb_ref, yc_ref, gates_ref, p_ref, wb_ref, wo_ref, g2_ref, wi_ref, wo2_ref,
                 g3_ref, wg_ref, we_ref, gf_ref, o_ref, *, final):
    merged = jnp.zeros(h_ref.shape, F32)
    for m, y_ref in enumerate((ya_ref, yb_ref, yc_ref)):
        gate = jax.nn.sigmoid(gates_ref[:, m * D_MODEL:(m + 1) * D_MODEL])
        merged = merged + gate * _dot(y_ref[...], wb_ref[m])
    x = h_ref[...] + _dot(merged.astype(BF16), wo_ref[...])
    x = _half_ffn(x, g2_ref, wi_ref, wo2_ref)
    gate = jax.nn.sigmoid(_dot(_rms(x, g3_ref[...]).astype(BF16), wg_ref[...]))
    x = x + gate * _dot(p_ref[...].astype(BF16), we_ref[...])
    if final:
        x = _rms(x, gf_ref[...])
    o_ref[...] = x


def _tail(h, ya, yb, yc, gates, p, wb, wo, g2, wi, wo2, g3, wg, we, gf, tm, layer, final):
    t = h.shape[0]
    rows = lambda width: pl.BlockSpec((tm, width), lambda i: (i, 0))
    vec = _resident((1, D_MODEL))
    return pl.pallas_call(
        functools.partial(_tail_kernel, final=final),
        name="tail",
        grid=(t // tm,),
        in_specs=[rows(D_MODEL), rows(A_WIDTH), rows(B_WIDTH), rows(C_WIDTH), rows(IN_G),
                  pl.BlockSpec((None, tm, PLE_DIM), lambda i: (layer, i, 0)),
                  _resident(wb.shape, layer), _resident(wo.shape, layer), vec,
                  _resident(wi.shape, layer), _resident(wo2.shape, layer),
                  vec, _resident(wg.shape, layer), _resident(we.shape, layer), vec],
        out_specs=rows(D_MODEL),
        out_shape=jax.ShapeDtypeStruct(h.shape, F32),
        compiler_params=_params("arbitrary"),
    )(h, ya, yb, yc, gates, p, wb, wo, g2.reshape(1, D_MODEL), wi, wo2, g3.reshape(1, D_MODEL), wg, we,
      gf.reshape(1, D_MODEL))


def _diff_kernel(tbl_ref, lam_ref, subg_ref, q_ref, k_ref, v_ref, o_ref, kaug_ref, vt_ref, near_ref,
                 *, tq, lam_init):
    qi = pl.program_id(1)
    width = 2 * A_HEAD_DIM
    n_heads = A_HEADS

    @pl.when((pl.program_id(0) == 0) & (qi == 0))
    def _static_setup():
        lane = lax.broadcasted_iota(jnp.int32, (tq, 2 * LANES), 1)
        key = lax.broadcasted_iota(jnp.int32, (2 * tq, tq), 0)
        qry = lax.broadcasted_iota(jnp.int32, (2 * tq, tq), 1)
        dist = tq + qry - key
        for h in range(n_heads):
            kaug_ref[h, 0:tq, :] = jnp.where(lane == LANES, 1.0, 0.0).astype(BF16)
            kaug_ref[h, tq:, LANES:2 * LANES] = jnp.zeros((kaug_ref.shape[1] - tq, LANES), BF16)
            vt_ref[h, 0] = jnp.zeros(vt_ref.shape[2:], BF16)
            near_ref[h] = jnp.where(dist >= 0, _bias_of_dist(dist, tbl_ref, h, True), NEG)

    @pl.when(qi == 0)
    def _per_batch_setup():
        for h in range(n_heads):
            cols = slice(h * width, (h + 1) * width)
            kaug_ref[h, tq:, 0:LANES] = k_ref[:, cols]
            for c in range(vt_ref.shape[1] - 1):
                vt_ref[h, c + 1] = _transpose_bf16(v_ref[c * tq:(c + 1) * tq, cols])

    zero = jnp.zeros((A_HEAD_DIM, tq), F32)
    flag_row = lax.broadcasted_iota(jnp.int32, (LANES, 2 * tq), 0) == 0
    flags = jnp.where(flag_row, UNSEL, 0.0)
    queries = []
    for h in range(n_heads):
        qt = q_ref[:, h * width:(h + 1) * width].astype(F32).T
        both = jnp.concatenate([jnp.concatenate([qt[0:A_HEAD_DIM], zero], axis=0),
                                jnp.concatenate([zero, qt[A_HEAD_DIM:width]], axis=0)], axis=1)
        queries.append(jnp.concatenate([both, flags], axis=0).astype(BF16))

    def logits_of(first_tile):
        start = pl.multiple_of(first_tile * tq, tq)
        return tuple(_dot(kaug_ref[h, pl.ds(start, 2 * tq), :], queries[h]) for h in range(n_heads))

    def update(first_tile, logits, carries):
        return tuple(_online_update(carries[h], logits[h], [vt_ref[h, first_tile], vt_ref[h, first_tile + 1]])
                     for h in range(n_heads))

    def far(gi, carries, count):
        tiles = [qi - 2 * (gi + c) for c in range(count)]
        fast, safe = carries, True
        for t, logits in [(t, logits_of(t)) for t in tiles]:
            steps = [_one_pass_update(fast[h], logits[h], [vt_ref[h, t], vt_ref[h, t + 1]]) for h in range(n_heads)]
            fast = tuple(step[0] for step in steps)
            for step in steps:
                safe = safe & step[1]

        def redo():
            slow = carries
            for t in tiles:
                slow = update(t, logits_of(t), slow)
            return slow

        return lax.cond(safe, lambda: fast, redo)

    nearest = tuple(s + jnp.concatenate([near_ref[h], near_ref[h]], axis=1) for h, s in enumerate(logits_of(qi)))
    carries = update(qi, nearest, tuple(_softmax_init(width, 2 * tq) for _ in range(n_heads)))
    n_far = qi // 2
    carries = lax.fori_loop(0, n_far // 2, lambda p, c: far(1 + 2 * p, c, 2), carries)
    carries = lax.fori_loop(0, n_far % 2, lambda _, c: far(n_far, c, 1), carries)
    lp = lam_ref[...]
    lam = (jnp.exp(jnp.sum(lp[0:1] * lp[1:2], axis=-1, keepdims=True))
           - jnp.exp(jnp.sum(lp[2:3] * lp[3:4], axis=-1, keepdims=True)) + lam_init)
    for h in range(n_heads):
        out = _softmax_result(carries[h])
        o = out[:, 0:tq] - lam * out[:, tq:2 * tq]
        o = o * lax.rsqrt(jnp.mean(o * o, axis=0, keepdims=True) + NORM_EPS) * (1.0 - lam_init)
        o_ref[:, h * width:(h + 1) * width] = (o.T * subg_ref[...]).astype(o_ref.dtype)


def _diff_attention(za, tbl, lam_p, subg, bsz, seq, lam_init, tq):
    nq = seq // tq
    width = 2 * A_HEAD_DIM
    return pl.pallas_call(
        functools.partial(_diff_kernel, tq=tq, lam_init=lam_init),
        name="diff_attn",
        grid=(bsz, nq),
        in_specs=[_smem(), _resident(lam_p.shape), _resident((1, width)),
                  pl.BlockSpec((tq, A_WIDTH), lambda b, i: (b * nq + i, 0)),
                  pl.BlockSpec((seq, A_WIDTH), lambda b, i: (b, 1)),
                  pl.BlockSpec((seq, A_WIDTH), lambda b, i: (b, 2))],
        out_specs=pl.BlockSpec((tq, A_WIDTH), lambda b, i: (b * nq + i, 0)),
        out_shape=jax.ShapeDtypeStruct((bsz * seq, A_WIDTH), BF16),
        scratch_shapes=[pltpu.VMEM((A_HEADS, tq + seq, 2 * LANES), BF16),
                        pltpu.VMEM((A_HEADS, nq + 1, width, tq), BF16),
                        pltpu.VMEM((A_HEADS, 2 * tq, tq), F32)],
        compiler_params=_params("arbitrary", "arbitrary"),
    )(tbl, lam_p, subg.reshape(1, width), za, za, za)


def _compress_kernel(xk_ref, xv_ref, pos_ref, w1_ref, w2_ref, kc_ref, vc_ref):
    rows = xk_ref.shape[0]
    for t, (x_ref, o_ref) in enumerate(((xk_ref, kc_ref), (xv_ref, vc_ref))):
        x = x_ref[...].astype(F32)
        first = _dot((x + pos_ref[t, 0]).astype(BF16), w1_ref[t, 0])
        second = _dot((x + pos_ref[t, 1]).astype(BF16), w1_ref[t, 1])
        hidden = first + pltpu.roll(second, rows - 1, 0)
        o_ref[...] = _dot(jax.nn.gelu(hidden).astype(BF16), w2_ref[t]).astype(o_ref.dtype)


def _compress(xk, xv, pos, w1, w2, layer):
    bsz, rows, width = xk.shape
    xspec = pl.BlockSpec((None, rows, width), lambda b: (b, 0, 0))
    ospec = pl.BlockSpec((None, rows, B_KV), lambda b: (b, 0, 0))
    return pl.pallas_call(
        _compress_kernel,
        name="nsa_compress",
        grid=(bsz,),
        in_specs=[xspec, xspec, _resident(pos.shape, layer), _resident(w1.shape, layer), _resident(w2.shape, layer)],
        out_specs=[ospec, ospec],
        out_shape=[jax.ShapeDtypeStruct((bsz, rows, B_KV), BF16)] * 2,
        compiler_params=_params("arbitrary"),
    )(xk, xv, pos, w1, w2)


def _group_queries_t(q, group, rep):
    tq = q.shape[0]
    zero = jnp.zeros((HEAD_DIM, tq), F32)
    cols = []
    for r in range(rep):
        head = group * rep + r
        slab = q[:, (head // 2) * LANES:(head // 2 + 1) * LANES].astype(F32).T
        part = slab[(head % 2) * HEAD_DIM:(head % 2 + 1) * HEAD_DIM]
        cols.append(jnp.concatenate([part, zero] if group == 0 else [zero, part], axis=0))
    return jnp.concatenate(cols, axis=1).astype(BF16)


def _build_near_bias(near_ref, tbl_ref, head0, groups, rep, tq, window_is_tile, minus_far):
    key = lax.broadcasted_iota(jnp.int32, (2 * tq, tq), 0)
    qry = lax.broadcasted_iota(jnp.int32, (2 * tq, tq), 1)
    dist = tq + qry - key
    visible = (dist >= 0) & (dist < tq) if window_is_tile else dist >= 0
    for g in range(groups):
        for r in range(rep):
            head = head0 + g * rep + r
            near_ref[g, :, r * tq:(r + 1) * tq] = jnp.where(visible, _bias_of_dist(dist, tbl_ref, head, minus_far), NEG)


def _swa_tile(q, k_prev, k_diag, v_prev, v_diag, bias_ref, sink_ref, qi, tq):
    cols = C_REP * tq
    prev_bias = jnp.where(qi > 0, 0.0, NEG)
    vp_t = _transpose_bf16(v_prev)
    vd_t = _transpose_bf16(v_diag)
    heads_out = []
    for g in range(C_GROUPS):
        half = slice(g * HEAD_DIM, (g + 1) * HEAD_DIM)
        qg = _group_queries_t(q, g, C_REP)
        s = jnp.concatenate([_dot(k_prev, qg) + bias_ref[g, 0:tq, :] + prev_bias,
                             _dot(k_diag, qg) + bias_ref[g, tq:2 * tq, :]], axis=0)
        m, acc = _online_update(_softmax_init(HEAD_DIM, cols), s, [vp_t[half], vd_t[half]])
        for r in range(C_REP):
            cc = slice(r * tq, (r + 1) * tq)
            sink = sink_ref[g * C_REP + r] * LOG2E
            m_all = jnp.maximum(m[:, cc], sink)
            scale = jnp.exp2(m[:, cc] - m_all)
            denom = acc[HEAD_DIM:HEAD_DIM + 1, cc] * scale + jnp.exp2(sink - m_all)
            heads_out.append(acc[0:HEAD_DIM, cc] * (scale / denom))
    return jnp.concatenate(heads_out, axis=0).T


SEL_GROUP = 4
SEL_PAD = SEL_GROUP - 1
WIN_PAD = B_WINDOW // BC_QUERY_TILE


def _nsa_kernel(tbl_ref, sink_ref, q_ref, ks_ref, vs_ref, kw_ref, vw_ref, kc_ref, vc_ref, bg_ref, ovt_ref,
                cq_ref, ckp_ref, ckd_ref, cvp_ref, cvd_ref, o_ref, oc_ref,
                kaug_ref, kwp_ref, vst_ref, vwt_ref, vct_ref, near_ref, edge_ref, band_ref, sc_ref, swab_ref,
                *, tq, seq):
    first_batch = pl.program_id(0) == 0
    qi = pl.program_id(1)
    n_cmp_pad = seq // B_CMP_STRIDE
    n_sel = seq // B_SEL_BLOCK
    top_k = min(B_SEL_TOPK, n_sel)
    nq = seq // tq
    cols = B_REP * tq
    cmp_per_tile = tq // B_CMP_STRIDE
    band = CMP_BAND_BACK + cmp_per_tile

    @pl.when(first_batch & (qi == 0))
    def _static_setup():
        _build_near_bias(near_ref, tbl_ref, A_HEADS, B_GROUPS, B_REP, tq, False, True)
        _build_near_bias(swab_ref, tbl_ref, A_HEADS + B_HEADS, C_GROUPS, C_REP, tq, True, False)
        key = lax.broadcasted_iota(jnp.int32, (tq, cols), 0)
        qry = lax.broadcasted_iota(jnp.int32, (tq, cols), 1) % tq
        edge_ref[...] = jnp.where(qry < key, 0.0, NEG)
        blk = lax.broadcasted_iota(jnp.int32, (band, tq), 0) - CMP_BAND_BACK
        dist = lax.broadcasted_iota(jnp.int32, (band, tq), 1) - (blk * B_CMP_STRIDE + B_CMP_LEN - 1)
        for g in range(B_GROUPS):
            for r in range(B_REP):
                head = A_HEADS + g * B_REP + r
                band_ref[g, :, r * tq:(r + 1) * tq] = jnp.where(dist >= 0, _bias_of_dist(dist, tbl_ref, head, True), NEG)
        sc_ref[:, 0:CMP_BAND_BACK, :] = jnp.zeros((B_GROUPS, CMP_BAND_BACK, cols), F32)
        pad = SEL_PAD * tq
        row = lax.broadcasted_iota(jnp.int32, (pad + seq, LANES), 0)
        lane = lax.broadcasted_iota(jnp.int32, (pad + seq, LANES), 1)
        blk_id = jnp.where(row < pad, LANES - 1, (row - pad) // B_SEL_BLOCK)
        kaug_ref[:, LANES:2 * LANES] = jnp.where(blk_id == lane, 1.0, 0.0).astype(BF16)
        kaug_ref[0:pad, 0:LANES] = jnp.zeros((pad, LANES), BF16)
        pad_lane = lax.broadcasted_iota(jnp.int32, (WIN_PAD * tq, 2 * LANES), 1) == LANES
        kwp_ref[0:WIN_PAD * tq, :] = jnp.where(pad_lane, 1.0, 0.0).astype(BF16)
        kwp_ref[WIN_PAD * tq:, LANES:2 * LANES] = jnp.zeros((seq, LANES), BF16)
        for c in range(SEL_PAD):
            vst_ref[c] = jnp.zeros(vst_ref.shape[1:], BF16)
        for c in range(WIN_PAD):
            vwt_ref[c] = jnp.zeros(vwt_ref.shape[1:], BF16)

    @pl.when(qi == 0)
    def _per_batch_setup():
        kaug_ref[SEL_PAD * tq:, 0:LANES] = ks_ref[...]
        kwp_ref[WIN_PAD * tq:, 0:LANES] = kw_ref[...]
        for c in range(nq):
            vst_ref[SEL_PAD + c] = _transpose_bf16(vs_ref[c * tq:(c + 1) * tq, :])
            vwt_ref[WIN_PAD + c] = _transpose_bf16(vw_ref[c * tq:(c + 1) * tq, :])
        for c in range(n_cmp_pad // LANES):
            vct_ref[:, c * LANES:(c + 1) * LANES] = _transpose_bf16(vc_ref[c * LANES:(c + 1) * LANES, :])

    q = q_ref[...]
    gates_t = jax.nn.sigmoid(bg_ref[...]).T
    first_cmp = qi * cmp_per_tile
    cmp_row = lax.broadcasted_iota(jnp.int32, (n_cmp_pad, cols), 0)
    cmp_visible = cmp_row < first_cmp + cmp_per_tile
    blk = lax.broadcasted_iota(jnp.int32, (n_sel, tq), 0)
    cur = (qi * tq + lax.broadcasted_iota(jnp.int32, (n_sel, tq), 1)) // B_SEL_BLOCK
    forced = (blk == 0) | (blk == cur) | (blk == cur - 1)
    future = blk > cur
    halves = [slice(g * HEAD_DIM, (g + 1) * HEAD_DIM) for g in range(B_GROUPS)]

    qgs, o_cs, imps = [], [], []
    for g in range(B_GROUPS):
        qg = _group_queries_t(q, g, B_REP)
        sc_ref[g, CMP_BAND_BACK:, :] = _dot(kc_ref[...], qg)
        band_rows = pl.ds(pl.multiple_of(first_cmp, SUBLANES), band)
        sc_ref[g, band_rows, :] = sc_ref[g, band_rows, :] + band_ref[g]
        s = jnp.where(cmp_visible, sc_ref[g, CMP_BAND_BACK:, :], NEG)
        top = jnp.max(s, axis=0, keepdims=True)
        e = jnp.exp2(s - top)
        inv = jnp.where(top > 0.5 * NEG, 1.0 / jnp.sum(e, axis=0, keepdims=True), 0.0)
        o_cs.append(_dot(vct_ref[halves[g], :], e.astype(BF16)) * inv)
        p_sum = e[:, 0:tq] * inv[:, 0:tq]
        for r in range(1, B_REP):
            p_sum = p_sum + e[:, r * tq:(r + 1) * tq] * inv[:, r * tq:(r + 1) * tq]
        imp = _dot(ovt_ref[...], p_sum.astype(BF16))
        imp = jnp.where(forced, B_SEL_FORCE, jnp.where(future, -B_SEL_FORCE, imp))
        qgs.append(qg)
        imps.append(imp)

    blk_f = blk.astype(F32)
    taken = [jnp.zeros((n_sel, tq), jnp.bool_) for _ in range(B_GROUPS)]
    left = list(imps)
    for _ in range(top_k):
        for g in range(B_GROUPS):
            best = jnp.max(left[g], axis=0, keepdims=True)
            first = jnp.min(jnp.where(left[g] == best, blk_f, float(n_sel)), axis=0, keepdims=True)
            pick = blk_f == first
            taken[g] = taken[g] | pick
            left[g] = jnp.where(pick, TAKEN, left[g])
    q_augs = []
    for g in range(B_GROUPS):
        sel_bias = jnp.concatenate([jnp.where(taken[g], 0.0, UNSEL),
                                    jnp.full((LANES - n_sel, tq), UNSEL, F32)], axis=0).astype(BF16)
        q_augs.append(jnp.concatenate([qgs[g], jnp.concatenate([sel_bias] * B_REP, axis=1)], axis=0))

    pair = cols
    chains = [(g, slice(0, cols)) for g in range(B_GROUPS)]
    heads_per_chain = pair // tq

    def sel_logits(first_tile):
        start = pl.multiple_of(first_tile * tq, tq)
        keys = kaug_ref[pl.ds(start, SEL_GROUP * tq), :]
        return tuple(_dot(keys, q_augs[g][:, cc]) for g, cc in chains)

    def sel_update(first_tile, logits, carries):
        return tuple(_online_update(carry, s, [vst_ref[first_tile + t, halves[g], :] for t in range(SEL_GROUP)])
                     for (g, cc), s, carry in zip(chains, logits, carries))

    start = pl.multiple_of(qi * tq, tq)
    win_keys = kwp_ref[pl.ds(start, (WIN_PAD + 1) * tq), :]
    pad_flags = jnp.where(lax.broadcasted_iota(jnp.int32, (LANES, cols), 0) == 0, UNSEL, 0.0).astype(BF16)
    partial = []
    for g, cc in chains:
        s_w = _dot(win_keys, jnp.concatenate([qgs[g], pad_flags], axis=0)[:, cc])
        slabs = []
        for t in range(WIN_PAD + 1):
            slab = s_w[t * tq:(t + 1) * tq]
            if t == 0:
                slab = slab + edge_ref[:, cc]
            if t >= WIN_PAD - 1:
                slab = slab + near_ref[g, (t - WIN_PAD + 1) * tq:(t - WIN_PAD + 2) * tq, cc]
            slabs.append(slab)
        o_w = _softmax_result(_online_update(_softmax_init(HEAD_DIM, pair), jnp.concatenate(slabs, axis=0),
                                             [vwt_ref[qi + t, halves[g], :] for t in range(WIN_PAD + 1)]))
        o_c = o_cs[g][:, cc]
        heads = []
        for rr in range(heads_per_chain):
            hc = slice(rr * tq, (rr + 1) * tq)
            c0 = (g * B_REP + rr) * 3
            heads.append(gates_t[c0:c0 + 1] * o_c[:, hc] + gates_t[c0 + 2:c0 + 3] * o_w[:, hc])
        partial.append(heads)

    oc_ref[...] = _swa_tile(cq_ref[...], ckp_ref[...], ckd_ref[...], cvp_ref[...], cvd_ref[...],
                            swab_ref, sink_ref, qi, tq).astype(oc_ref.dtype)

    def sel_far(gi, carries, count):
        tiles = [qi - SEL_GROUP * (gi + c) for c in range(count)]
        fast, safe = carries, True
        for t, logits in [(t, sel_logits(t)) for t in tiles]:
            steps = [_one_pass_update(carry, s, [vst_ref[t + k, halves[g], :] for k in range(SEL_GROUP)])
                     for (g, cc), s, carry in zip(chains, logits, fast)]
            fast = tuple(step[0] for step in steps)
            for step in steps:
                safe = safe & step[1]

        def redo():
            slow = carries
            for t in tiles:
                slow = sel_update(t, sel_logits(t), slow)
            return slow

        return lax.cond(safe, lambda: fast, redo)

    far_rows = (SEL_GROUP - 2) * tq
    nearest = tuple(jnp.concatenate([s[0:far_rows], s[far_rows:] + near_ref[g, :, cc]], axis=0)
                    for (g, cc), s in zip(chains, sel_logits(qi)))
    sel = sel_update(qi, nearest, tuple(_softmax_init(HEAD_DIM, pair) for _ in chains))
    n_far = qi // SEL_GROUP
    sel = lax.fori_loop(0, n_far // 2, lambda p, c: sel_far(1 + 2 * p, c, 2), sel)
    sel = lax.fori_loop(0, n_far % 2, lambda _, c: sel_far(n_far, c, 1), sel)

    heads_out = []
    for ci, (g, cc) in enumerate(chains):
        o_s = _softmax_result(sel[ci])
        for rr in range(heads_per_chain):
            hc = slice(rr * tq, (rr + 1) * tq)
            c1 = (g * B_REP + rr) * 3 + 1
            heads_out.append(partial[ci][rr] + gates_t[c1:c1 + 1] * o_s[:, hc])
    o_ref[...] = jnp.concatenate(heads_out, axis=0).T.astype(o_ref.dtype)


def _nsa_swa_attention(zb, kc, vc, bgate, zc, tbl, sinks, ovt, bsz, seq, tq):
    nq = seq // tq
    n_cmp_pad = seq // B_CMP_STRIDE
    cols = B_REP * tq
    kv = lambda col: pl.BlockSpec((seq, LANES), lambda b, i: (b, col))
    cmp_spec = pl.BlockSpec((None, n_cmp_pad, B_KV), lambda b, i: (b, 0, 0))
    qcols = B_WIDTH // LANES
    ccols = C_WIDTH // LANES
    wide = lambda width: pl.BlockSpec((tq, width), lambda b, i: (b * nq + i, 0))
    prev = lambda col: pl.BlockSpec((tq, LANES), lambda b, i: (b * nq + jnp.maximum(i - 1, 0), col))
    diag = lambda col: pl.BlockSpec((tq, LANES), lambda b, i: (b * nq + i, col))
    return pl.pallas_call(
        functools.partial(_nsa_kernel, tq=tq, seq=seq),
        name="nsa_attn",
        grid=(bsz, nq),
        in_specs=[_smem(), _smem(),
                  wide(B_WIDTH),
                  kv(qcols), kv(qcols + 1), kv(qcols + 2), kv(qcols + 3),
                  cmp_spec, cmp_spec,
                  wide(LANES),
                  _resident(ovt.shape),
                  wide(C_WIDTH), prev(ccols), diag(ccols), prev(ccols + 1), diag(ccols + 1)],
        out_specs=[wide(B_WIDTH), wide(C_WIDTH)],
        out_shape=[jax.ShapeDtypeStruct((bsz * seq, B_WIDTH), BF16),
                   jax.ShapeDtypeStruct((bsz * seq, C_WIDTH), BF16)],
        scratch_shapes=[pltpu.VMEM((SEL_PAD * tq + seq, 2 * LANES), BF16),
                        pltpu.VMEM((WIN_PAD * tq + seq, 2 * LANES), BF16),
                        pltpu.VMEM((SEL_PAD + nq, B_KV, tq), BF16),
                        pltpu.VMEM((WIN_PAD + nq, B_KV, tq), BF16),
                        pltpu.VMEM((B_KV, n_cmp_pad), BF16),
                        pltpu.VMEM((B_GROUPS, 2 * tq, cols), F32),
                        pltpu.VMEM((tq, cols), F32),
                        pltpu.VMEM((B_GROUPS, CMP_BAND_BACK + tq // B_CMP_STRIDE, cols), F32),
                        pltpu.VMEM((B_GROUPS, CMP_BAND_BACK + n_cmp_pad, cols), F32),
                        pltpu.VMEM((C_GROUPS, 2 * tq, C_REP * tq), F32)],
        compiler_params=_params("arbitrary", "arbitrary"),
    )(tbl, sinks, zb, zb, zb, zb, zb, kc, vc, bgate, ovt, zc, zc, zc, zc, zc)


def _prep_w_in(w_in):
    sizes = (A_WIDTH, A_WIDTH, A_WIDTH, B_WIDTH, B_KV, B_KV, B_KV, B_KV, B_KV, B_KV, 3 * B_HEADS,
             C_WIDTH, C_KV, C_KV, D_MODEL, D_MODEL, D_MODEL)
    offs = np.concatenate([[0], np.cumsum(sizes)])
    (aq, ak, av, bq, bkc, bvc, bks, bvs, bkw, bvw, bgate, cq, ck, cv, ga, gb, gc) = [
        w_in[..., offs[i]:offs[i + 1]] for i in range(len(sizes))]
    scale = HEAD_DIM ** -0.5 * LOG2E
    bgate = jnp.pad(bgate, ((0, 0), (0, 0), (0, LANES - 3 * B_HEADS)))
    cols = [aq * scale, ak, av, bq * scale, bks, bvs, bkw, bvw, bkc, bvc, cq * scale, ck, cv, ga, gb, gc, bgate]
    return jnp.concatenate([c.astype(BF16) for c in cols], axis=-1)


def _prep_compress(cmp_pos, cmp_w1, cmp_w2):
    depth = cmp_pos.shape[0]
    half = B_CMP_STRIDE
    pos = cmp_pos.reshape(depth, 2, 2, half, 1, B_HEAD_DIM)
    pos = jnp.broadcast_to(pos, (depth, 2, 2, half, B_GROUPS, B_HEAD_DIM)).reshape(depth, 2, 2, 1, half * B_KV)
    w1 = cmp_w1.reshape(depth, 2, 2, half, 1, B_HEAD_DIM, 1, B_CMP_HIDDEN)
    eye = jnp.eye(B_GROUPS, dtype=cmp_w1.dtype).reshape(1, 1, 1, 1, B_GROUPS, 1, B_GROUPS, 1)
    w1 = (w1 * eye).reshape(depth, 2, 2, half * B_KV, B_GROUPS * B_CMP_HIDDEN)
    w2 = cmp_w2.reshape(depth, 2, 1, B_CMP_HIDDEN, 1, B_HEAD_DIM)
    eye2 = jnp.eye(B_GROUPS, dtype=cmp_w2.dtype).reshape(1, 1, B_GROUPS, 1, B_GROUPS, 1)
    w2 = (w2 * eye2).reshape(depth, 2, B_GROUPS * B_CMP_HIDDEN, B_KV)
    return pos.astype(F32), w1.astype(BF16), w2.astype(BF16)


def _overlap_t(seq):
    n_cmp_pad = seq // B_CMP_STRIDE
    start = np.arange(n_cmp_pad) * B_CMP_STRIDE
    sel = np.arange(seq // B_SEL_BLOCK) * B_SEL_BLOCK
    ov = (start[None, :] < sel[:, None] + B_SEL_BLOCK) & (start[None, :] + B_CMP_LEN > sel[:, None])
    ov &= (np.arange(n_cmp_pad) < n_cmp_pad - 1)[None, :]
    return jnp.asarray(ov.astype(np.float32), dtype=BF16)


def kernel(x, p, norm_g, ffn1_wi, ffn1_wo, w_in, diff_lambda, diff_subln, nsa_cmp_pos, nsa_cmp_w1,
           nsa_cmp_w2, swa_sinks, w_branch, w_out, ffn2_wi, ffn2_wo, w_ple, w_ple_gate, rel_bias, final_norm):
    bsz, seq, _ = x.shape
    depth = norm_g.shape[0]
    tokens = bsz * seq
    tq = BC_QUERY_TILE
    assert seq % (B_CMP_STRIDE * LANES) == 0 and seq // B_SEL_BLOCK < LANES and tokens % TAIL_TOKEN_TILE == 0
    assert FAR_DIST + B_CMP_LEN - 1 <= CMP_BAND_BACK * B_CMP_STRIDE and FAR_DIST <= tq
    assert WIN_PAD * tq == B_WINDOW and tq == C_WINDOW and tq % (B_CMP_STRIDE * SUBLANES) == 0

    wi1, wo1 = ffn1_wi.astype(BF16), ffn1_wo.astype(BF16)
    wi2, wo2 = ffn2_wi.astype(BF16), ffn2_wo.astype(BF16)
    w_in_r = _prep_w_in(w_in)
    cmp_pos, cmp_w1, cmp_w2 = _prep_compress(nsa_cmp_pos, nsa_cmp_w1, nsa_cmp_w2)
    wb, wo = w_branch.astype(BF16), w_out.astype(BF16)
    wpg, wpe = w_ple_gate.astype(BF16), w_ple.astype(BF16)
    tbl = rel_bias.astype(F32)
    ovt = _overlap_t(seq)
    chunk_rows = seq // B_CMP_STRIDE

    h = x.reshape(tokens, D_MODEL)
    p_rows = p.reshape(depth, tokens, PLE_DIM)
    for i in range(depth):
        h, za, zb, zkc, zvc, zc, gates, bgate = _head(h, norm_g[i, 0], wi1, wo1, norm_g[i, 1], w_in_r,
                                                      HEAD_TOKEN_TILE, i)
        lam_init = 0.8 - 0.6 * math.exp(-0.3 * i)
        ya = _diff_attention(za, tbl, diff_lambda[i].astype(F32), diff_subln[i], bsz, seq, lam_init, A_QUERY_TILE)
        kc, vc = _compress(zkc.reshape(bsz, chunk_rows, B_CMP_STRIDE * B_KV),
                           zvc.reshape(bsz, chunk_rows, B_CMP_STRIDE * B_KV), cmp_pos, cmp_w1, cmp_w2, i)
        yb, yc = _nsa_swa_attention(zb, kc, vc, bgate, zc, tbl, swa_sinks[i].astype(F32), ovt, bsz, seq, tq)
        h = _tail(h, ya, yb, yc, gates, p_rows, wb, wo, norm_g[i, 2], wi2, wo2, norm_g[i, 3], wpg, wpe,
                  final_norm, TAIL_TOKEN_TILE, i, final=(i == depth - 1))
    return h.reshape(bsz, seq, D_MODEL)
```

```python
import functools
import math

import numpy as np
import jax
import jax.numpy as jnp
from jax import lax
from jax.experimental import pallas as pl
from jax.experimental.pallas import tpu as pltpu

F32 = jnp.float32
BF16 = jnp.bfloat16

D_MODEL = 1024
PLE_DIM = 256
D_FF = 2816
NORM_EPS = 1e-6
NEG = -1e30
NUM_BUCKETS = 32
MAX_DISTANCE = 128
A_HEADS = 4
A_HEAD_DIM = 64
B_HEADS = 8
B_GROUPS = 2
B_REP = B_HEADS // B_GROUPS
B_HEAD_DIM = 64
B_CMP_LEN = 32
B_CMP_STRIDE = 16
B_CMP_HIDDEN = 256
B_SEL_BLOCK = 64
B_SEL_TOPK = 16
B_WINDOW = 512
B_SEL_FORCE = 1e6
C_HEADS = 8
C_GROUPS = 2
C_REP = C_HEADS // C_GROUPS
C_HEAD_DIM = 64
C_WINDOW = 128
A_WIDTH = A_HEADS * 2 * A_HEAD_DIM
B_WIDTH = B_HEADS * B_HEAD_DIM
B_KV = B_GROUPS * B_HEAD_DIM
C_WIDTH = C_HEADS * C_HEAD_DIM
C_KV = C_GROUPS * C_HEAD_DIM

LANES = 128
SUBLANES = 8
HEAD_DIM = 64
VMEM_LIMIT = 56 * 1024 * 1024
HEAD_TOKEN_TILE = 256
TAIL_TOKEN_TILE = 512
A_QUERY_TILE = 256
BC_QUERY_TILE = 128
UNSEL = -float(2 ** 30)
TAKEN = -3e38
FFN_CHUNKS = ((0, 1536), (1536, 1280))
CMP_BAND_BACK = 16
LOG2E = math.log2(math.e)
ONES_ROWS = 16
EXP_HEADROOM = 64.0


def _bucket_thresholds():
    n = np.arange(4 * MAX_DISTANCE)
    max_exact = NUM_BUCKETS // 2
    nf = np.maximum(n, 1).astype(np.float32)
    large = max_exact + (np.log(nf / max_exact) / math.log(MAX_DISTANCE / max_exact)
                         * (NUM_BUCKETS - max_exact)).astype(np.int32)
    bucket = np.where(n < max_exact, n, np.minimum(large, NUM_BUCKETS - 1))
    out = []
    for b in range(1, NUM_BUCKETS):
        hit = np.nonzero(bucket == b)[0]
        if hit.size:
            out.append((b, int(hit[0])))
    return tuple(out)


BUCKET_LO = _bucket_thresholds()
FAR_DIST = BUCKET_LO[-1][1]


def _bias_of_dist(dist, tbl_ref, head, minus_far=False):
    shift = tbl_ref[NUM_BUCKETS - 1, head] if minus_far else 0.0
    out = jnp.full(dist.shape, (tbl_ref[0, head] - shift) * LOG2E, F32)
    for b, lo in BUCKET_LO:
        out = jnp.where(dist >= lo, (tbl_ref[b, head] - shift) * LOG2E, out)
    return out


def _rms(x, g):
    return x * lax.rsqrt(jnp.mean(x * x, axis=-1, keepdims=True) + NORM_EPS) * g


def _dot(a, b):
    return jnp.dot(a, b, preferred_element_type=F32)


def _values_and_ones(vts):
    values = jnp.concatenate(vts, axis=1)
    return jnp.concatenate([values, jnp.ones((ONES_ROWS, values.shape[1]), BF16)], axis=0)


def _online_update(carry, s, vts):
    m, acc = carry
    m_new = jnp.maximum(m, jnp.max(s, axis=0, keepdims=True))
    p = jnp.exp2(s - m_new).astype(BF16)
    return m_new, jnp.exp2(m - m_new) * acc + _dot(_values_and_ones(vts), p)


def _one_pass_update(carry, s, vts):
    m, acc = carry
    top = jnp.max(s, axis=0, keepdims=True)
    p = jnp.exp2(s - m).astype(BF16)
    m_new = jnp.maximum(m, top)
    return (m_new, jnp.exp2(m - m_new) * (acc + _dot(_values_and_ones(vts), p))), jnp.max(top - m) <= EXP_HEADROOM


def _softmax_init(width, queries):
    return jnp.full((1, queries), NEG, F32), jnp.zeros((width + ONES_ROWS, queries), F32)


def _softmax_result(carry):
    _, acc = carry
    width = acc.shape[0] - ONES_ROWS
    return acc[0:width] / acc[width:width + 1]


def _transpose_bf16(x):
    return x.astype(F32).T.astype(BF16)


def _params(*sem):
    return pltpu.CompilerParams(dimension_semantics=sem, vmem_limit_bytes=VMEM_LIMIT)


def _resident(shape, layer=None):
    if layer is None:
        zeros = (0,) * len(shape)
        return pl.BlockSpec(shape, lambda *_: zeros, pipeline_mode=pl.Buffered(1))
    index = (layer,) + (0,) * (len(shape) - 1)
    return pl.BlockSpec((None,) + tuple(shape[1:]), lambda *_: index, pipeline_mode=pl.Buffered(1))


def _smem():
    return pl.BlockSpec(memory_space=pltpu.SMEM)


def _half_ffn(x, g_ref, wi_ref, wo_ref):
    n = _rms(x, g_ref[...]).astype(BF16)
    acc = x
    for lo, width in FFN_CHUNKS:
        gate = _dot(n, wi_ref[:, lo:lo + width])
        up = _dot(n, wi_ref[:, D_FF + lo:D_FF + lo + width])
        act = (gate * jax.nn.sigmoid(gate) * up).astype(BF16)
        acc = acc + 0.5 * _dot(act, wo_ref[lo:lo + width, :])
    return acc


IN_A = 3 * A_WIDTH
IN_B = B_WIDTH + 4 * B_KV
IN_C = C_WIDTH + 2 * C_KV
IN_G = 3 * D_MODEL
IN_SECTIONS = (IN_A, IN_B, B_KV, B_KV, IN_C, IN_G, LANES)


def _head_kernel(h_ref, g0_ref, wi_ref, wo_ref, g1_ref, w_ref,
                 o_ref, za_ref, zb_ref, zkc_ref, zvc_ref, zc_ref, gates_ref, bg_ref):
    x = _half_ffn(h_ref[...], g0_ref, wi_ref, wo_ref)
    o_ref[...] = x
    n = _rms(x, g1_ref[...]).astype(BF16)
    lo = 0
    for ref, width in zip((za_ref, zb_ref, zkc_ref, zvc_ref, zc_ref, gates_ref, bg_ref), IN_SECTIONS):
        ref[...] = _dot(n, w_ref[:, lo:lo + width]).astype(ref.dtype)
        lo += width


def _head(h, g0, wi, wo, g1, w, tm, layer):
    t = h.shape[0]
    dtypes = (BF16, BF16, BF16, BF16, BF16, F32, F32)
    rows = lambda width: pl.BlockSpec((tm, width), lambda i: (i, 0))
    vec = _resident((1, D_MODEL))
    return pl.pallas_call(
        _head_kernel,
        name="head",
        grid=(t // tm,),
        in_specs=[rows(D_MODEL), vec, _resident(wi.shape, layer), _resident(wo.shape, layer), vec,
                  _resident(w.shape, layer)],
        out_specs=[rows(D_MODEL)] + [rows(width) for width in IN_SECTIONS],
        out_shape=[jax.ShapeDtypeStruct(h.shape, F32)]
                  + [jax.ShapeDtypeStruct((t, width), dt) for width, dt in zip(IN_SECTIONS, dtypes)],
        compiler_params=_params("arbitrary"),
    )(h, g0.reshape(1, D_MODEL), wi, wo, g1.reshape(1, D_MODEL), w)


def _tail_kernel(h_ref, ya_ref, yb_ref, yc_ref, gates_ref, p_ref, wb_ref, wo_ref, g2_ref, wi_ref, wo2_ref,
                 g3_ref, wg_ref, we_ref, gf_ref, o_ref, *, final):
    merged = jnp.zeros(h_ref.shape, F32)
    for m, y_ref in enumerate((ya_ref, yb_ref, yc_ref)):
        gate = jax.nn.sigmoid(gates_ref[:, m * D_MODEL:(m + 1) * D_MODEL])
        merged = merged + gate * _dot(y_ref[...], wb_ref[m])
    x = h_ref[...] + _dot(merged.astype(BF16), wo_ref[...])
    x = _half_ffn(x, g2_ref, wi_ref, wo2_ref)
    gate = jax.nn.sigmoid(_dot(_rms(x, g3_ref[...]).astype(BF16), wg_ref[...]))
    x = x + gate * _dot(p_ref[...].astype(BF16), we_ref[...])
    if final:
        x = _rms(x, gf_ref[...])
    o_ref[...] = x


def _tail(h, ya, yb, yc, gates, p, wb, wo, g2, wi, wo2, g3, wg, we, gf, tm, layer, final):
    t = h.shape[0]
    rows = lambda width: pl.BlockSpec((tm, width), lambda i: (i, 0))
    vec = _resident((1, D_MODEL))
    return pl.pallas_call(
        functools.partial(_tail_kernel, final=final),
        name="tail",
        grid=(t // tm,),
        in_specs=[rows(D_MODEL), rows(A_WIDTH), rows(B_WIDTH), rows(C_WIDTH), rows(IN_G),
                  pl.BlockSpec((None, tm, PLE_DIM), lambda i: (layer, i, 0)),
                  _resident(wb.shape, layer), _resident(wo.shape, layer), vec,
                  _resident(wi.shape, layer), _resident(wo2.shape, layer),
                  vec, _resident(wg.shape, layer), _resident(we.shape, layer), vec],
        out_specs=rows(D_MODEL),
        out_shape=jax.ShapeDtypeStruct(h.shape, F32),
        compiler_params=_params("arbitrary"),
    )(h, ya, yb, yc, gates, p, wb, wo, g2.reshape(1, D_MODEL), wi, wo2, g3.reshape(1, D_MODEL), wg, we,
      gf.reshape(1, D_MODEL))


def _diff_kernel(tbl_ref, lam_ref, subg_ref, q_ref, k_ref, v_ref, o_ref, kaug_ref, vt_ref, near_ref,
                 *, tq, lam_init):
    qi = pl.program_id(1)
    width = 2 * A_HEAD_DIM
    n_heads = A_HEADS

    @pl.when((pl.program_id(0) == 0) & (qi == 0))
    def _static_setup():
        lane = lax.broadcasted_iota(jnp.int32, (tq, 2 * LANES), 1)
        key = lax.broadcasted_iota(jnp.int32, (2 * tq, tq), 0)
        qry = lax.broadcasted_iota(jnp.int32, (2 * tq, tq), 1)
        dist = tq + qry - key
        for h in range(n_heads):
            kaug_ref[h, 0:tq, :] = jnp.where(lane == LANES, 1.0, 0.0).astype(BF16)
            kaug_ref[h, tq:, LANES:2 * LANES] = jnp.zeros((kaug_ref.shape[1] - tq, LANES), BF16)
            vt_ref[h, 0] = jnp.zeros(vt_ref.shape[2:], BF16)
            near_ref[h] = jnp.where(dist >= 0, _bias_of_dist(dist, tbl_ref, h, True), NEG)

    @pl.when(qi == 0)
    def _per_batch_setup():
        for h in range(n_heads):
            cols = slice(h * width, (h + 1) * width)
            kaug_ref[h, tq:, 0:LANES] = k_ref[:, cols]
            for c in range(vt_ref.shape[1] - 1):
                vt_ref[h, c + 1] = _transpose_bf16(v_ref[c * tq:(c + 1) * tq, cols])

    zero = jnp.zeros((A_HEAD_DIM, tq), F32)
    flag_row = lax.broadcasted_iota(jnp.int32, (LANES, 2 * tq), 0) == 0
    flags = jnp.where(flag_row, UNSEL, 0.0)
    queries = []
    for h in range(n_heads):
        qt = q_ref[:, h * width:(h + 1) * width].astype(F32).T
        both = jnp.concatenate([jnp.concatenate([qt[0:A_HEAD_DIM], zero], axis=0),
                                jnp.concatenate([zero, qt[A_HEAD_DIM:width]], axis=0)], axis=1)
        queries.append(jnp.concatenate([both, flags], axis=0).astype(BF16))

    def logits_of(first_tile):
        start = pl.multiple_of(first_tile * tq, tq)
        return tuple(_dot(kaug_ref[h, pl.ds(start, 2 * tq), :], queries[h]) for h in range(n_heads))

    def update(first_tile, logits, carries):
        return tuple(_online_update(carries[h], logits[h], [vt_ref[h, first_tile], vt_ref[h, first_tile + 1]])
                     for h in range(n_heads))

    def far(gi, carries, count):
        tiles = [qi - 2 * (gi + c) for c in range(count)]
        fast, safe = carries, True
        for t, logits in [(t, logits_of(t)) for t in tiles]:
            steps = [_one_pass_update(fast[h], logits[h], [vt_ref[h, t], vt_ref[h, t + 1]]) for h in range(n_heads)]
            fast = tuple(step[0] for step in steps)
            for step in steps:
                safe = safe & step[1]

        def redo():
            slow = carries
            for t in tiles:
                slow = update(t, logits_of(t), slow)
            return slow

        return lax.cond(safe, lambda: fast, redo)

    nearest = tuple(s + jnp.concatenate([near_ref[h], near_ref[h]], axis=1) for h, s in enumerate(logits_of(qi)))
    carries = update(qi, nearest, tuple(_softmax_init(width, 2 * tq) for _ in range(n_heads)))
    n_far = qi // 2
    carries = lax.fori_loop(0, n_far // 2, lambda p, c: far(1 + 2 * p, c, 2), carries)
    carries = lax.fori_loop(0, n_far % 2, lambda _, c: far(n_far, c, 1), carries)
    lp = lam_ref[...]
    lam = (jnp.exp(jnp.sum(lp[0:1] * lp[1:2], axis=-1, keepdims=True))
           - jnp.exp(jnp.sum(lp[2:3] * lp[3:4], axis=-1, keepdims=True)) + lam_init)
    for h in range(n_heads):
        out = _softmax_result(carries[h])
        o = out[:, 0:tq] - lam * out[:, tq:2 * tq]
        o = o * lax.rsqrt(jnp.mean(o * o, axis=0, keepdims=True) + NORM_EPS) * (1.0 - lam_init)
        o_ref[:, h * width:(h + 1) * width] = (o.T * subg_ref[...]).astype(o_ref.dtype)


def _diff_attention(za, tbl, lam_p, subg, bsz, seq, lam_init, tq):
    nq = seq // tq
    width = 2 * A_HEAD_DIM
    return pl.pallas_call(
        functools.partial(_diff_kernel, tq=tq, lam_init=lam_init),
        name="diff_attn",
        grid=(bsz, nq),
        in_specs=[_smem(), _resident(lam_p.shape), _resident((1, width)),
                  pl.BlockSpec((tq, A_WIDTH), lambda b, i: (b * nq + i, 0)),
                  pl.BlockSpec((seq, A_WIDTH), lambda b, i: (b, 1)),
                  pl.BlockSpec((seq, A_WIDTH), lambda b, i: (b, 2))],
        out_specs=pl.BlockSpec((tq, A_WIDTH), lambda b, i: (b * nq + i, 0)),
        out_shape=jax.ShapeDtypeStruct((bsz * seq, A_WIDTH), BF16),
        scratch_shapes=[pltpu.VMEM((A_HEADS, tq + seq, 2 * LANES), BF16),
                        pltpu.VMEM((A_HEADS, nq + 1, width, tq), BF16),
                        pltpu.VMEM((A_HEADS, 2 * tq, tq), F32)],
        compiler_params=_params("arbitrary", "arbitrary"),
    )(tbl, lam_p, subg.reshape(1, width), za, za, za)


def _compress_kernel(xk_ref, xv_ref, pos_ref, w1_ref, w2_ref, kc_ref, vc_ref):
    rows = xk_ref.shape[0] // B_CMP_STRIDE
    for t, (x_ref, o_ref) in enumerate(((xk_ref, kc_ref), (xv_ref, vc_ref))):
        x = x_ref[...].astype(F32).reshape(rows, B_CMP_STRIDE * B_KV)
        first = _dot((x + pos_ref[t, 0]).astype(BF16), w1_ref[t, 0])
        second = _dot((x + pos_ref[t, 1]).astype(BF16), w1_ref[t, 1])
        hidden = first + pltpu.roll(second, rows - 1, 0)
        o_ref[...] = _dot(jax.nn.gelu(hidden).astype(BF16), w2_ref[t]).astype(o_ref.dtype)


def _compress(xk, xv, pos, w1, w2, layer, bsz):
    seq = xk.shape[0] // bsz
    rows = seq // B_CMP_STRIDE
    xspec = pl.BlockSpec((seq, B_KV), lambda b: (b, 0))
    ospec = pl.BlockSpec((None, rows, B_KV), lambda b: (b, 0, 0))
    return pl.pallas_call(
        _compress_kernel,
        name="nsa_compress",
        grid=(bsz,),
        in_specs=[xspec, xspec, _resident(pos.shape, layer), _resident(w1.shape, layer), _resident(w2.shape, layer)],
        out_specs=[ospec, ospec],
        out_shape=[jax.ShapeDtypeStruct((bsz, rows, B_KV), BF16)] * 2,
        compiler_params=_params("arbitrary"),
    )(xk, xv, pos, w1, w2)


def _group_queries_t(q, group, rep):
    tq = q.shape[0]
    zero = jnp.zeros((HEAD_DIM, tq), F32)
    cols = []
    for r in range(rep):
        head = group * rep + r
        slab = q[:, (head // 2) * LANES:(head // 2 + 1) * LANES].astype(F32).T
        part = slab[(head % 2) * HEAD_DIM:(head % 2 + 1) * HEAD_DIM]
        cols.append(jnp.concatenate([part, zero] if group == 0 else [zero, part], axis=0))
    return jnp.concatenate(cols, axis=1).astype(BF16)


def _build_near_bias(near_ref, tbl_ref, head0, groups, rep, tq, window_is_tile, minus_far):
    key = lax.broadcasted_iota(jnp.int32, (2 * tq, tq), 0)
    qry = lax.broadcasted_iota(jnp.int32, (2 * tq, tq), 1)
    dist = tq + qry - key
    visible = (dist >= 0) & (dist < tq) if window_is_tile else dist >= 0
    for g in range(groups):
        for r in range(rep):
            head = head0 + g * rep + r
            near_ref[g, :, r * tq:(r + 1) * tq] = jnp.where(visible, _bias_of_dist(dist, tbl_ref, head, minus_far), NEG)


def _swa_tile(q, k_prev, k_diag, v_prev, v_diag, bias_ref, sink_ref, qi, tq):
    cols = C_REP * tq
    prev_bias = jnp.where(qi > 0, 0.0, NEG)
    vp_t = _transpose_bf16(v_prev)
    vd_t = _transpose_bf16(v_diag)
    heads_out = []
    for g in range(C_GROUPS):
        half = slice(g * HEAD_DIM, (g + 1) * HEAD_DIM)
        qg = _group_queries_t(q, g, C_REP)
        s = jnp.concatenate([_dot(k_prev, qg) + bias_ref[g, 0:tq, :] + prev_bias,
                             _dot(k_diag, qg) + bias_ref[g, tq:2 * tq, :]], axis=0)
        m, acc = _online_update(_softmax_init(HEAD_DIM, cols), s, [vp_t[half], vd_t[half]])
        for r in range(C_REP):
            cc = slice(r * tq, (r + 1) * tq)
            sink = sink_ref[g * C_REP + r] * LOG2E
            m_all = jnp.maximum(m[:, cc], sink)
            scale = jnp.exp2(m[:, cc] - m_all)
            denom = acc[HEAD_DIM:HEAD_DIM + 1, cc] * scale + jnp.exp2(sink - m_all)
            heads_out.append(acc[0:HEAD_DIM, cc] * (scale / denom))
    return jnp.concatenate(heads_out, axis=0).T


SEL_GROUP = 4
SEL_PAD = SEL_GROUP - 1
WIN_PAD = B_WINDOW // BC_QUERY_TILE


def _nsa_kernel(tbl_ref, sink_ref, q_ref, ks_ref, vs_ref, kw_ref, vw_ref, kc_ref, vc_ref, bg_ref, ovt_ref,
                cq_ref, ckp_ref, ckd_ref, cvp_ref, cvd_ref, o_ref, oc_ref,
                kaug_ref, kwp_ref, vst_ref, vwt_ref, vct_ref, near_ref, edge_ref, band_ref, sc_ref, swab_ref,
                *, tq, seq):
    first_batch = pl.program_id(0) == 0
    qi = pl.program_id(1)
    n_cmp_pad = seq // B_CMP_STRIDE
    n_sel = seq // B_SEL_BLOCK
    top_k = min(B_SEL_TOPK, n_sel)
    nq = seq // tq
    cols = B_REP * tq
    cmp_per_tile = tq // B_CMP_STRIDE
    band = CMP_BAND_BACK + cmp_per_tile

    @pl.when(first_batch & (qi == 0))
    def _static_setup():
        _build_near_bias(near_ref, tbl_ref, A_HEADS, B_GROUPS, B_REP, tq, False, True)
        _build_near_bias(swab_ref, tbl_ref, A_HEADS + B_HEADS, C_GROUPS, C_REP, tq, True, False)
        key = lax.broadcasted_iota(jnp.int32, (tq, cols), 0)
        qry = lax.broadcasted_iota(jnp.int32, (tq, cols), 1) % tq
        edge_ref[...] = jnp.where(qry < key, 0.0, NEG)
        blk = lax.broadcasted_iota(jnp.int32, (band, tq), 0) - CMP_BAND_BACK
        dist = lax.broadcasted_iota(jnp.int32, (band, tq), 1) - (blk * B_CMP_STRIDE + B_CMP_LEN - 1)
        for g in range(B_GROUPS):
            for r in range(B_REP):
                head = A_HEADS + g * B_REP + r
                band_ref[g, :, r * tq:(r + 1) * tq] = jnp.where(dist >= 0, _bias_of_dist(dist, tbl_ref, head, True), NEG)
        sc_ref[:, 0:CMP_BAND_BACK, :] = jnp.zeros((B_GROUPS, CMP_BAND_BACK, cols), F32)
        pad = SEL_PAD * tq
        row = lax.broadcasted_iota(jnp.int32, (pad + seq, LANES), 0)
        lane = lax.broadcasted_iota(jnp.int32, (pad + seq, LANES), 1)
        blk_id = jnp.where(row < pad, LANES - 1, (row - pad) // B_SEL_BLOCK)
        kaug_ref[:, LANES:2 * LANES] = jnp.where(blk_id == lane, 1.0, 0.0).astype(BF16)
        kaug_ref[0:pad, 0:LANES] = jnp.zeros((pad, LANES), BF16)
        pad_lane = lax.broadcasted_iota(jnp.int32, (WIN_PAD * tq, 2 * LANES), 1) == LANES
        kwp_ref[0:WIN_PAD * tq, :] = jnp.where(pad_lane, 1.0, 0.0).astype(BF16)
        kwp_ref[WIN_PAD * tq:, LANES:2 * LANES] = jnp.zeros((seq, LANES), BF16)
        for c in range(SEL_PAD):
            vst_ref[c] = jnp.zeros(vst_ref.shape[1:], BF16)
        for c in range(WIN_PAD):
            vwt_ref[c] = jnp.zeros(vwt_ref.shape[1:], BF16)

    @pl.when(qi == 0)
    def _per_batch_setup():
        kaug_ref[SEL_PAD * tq:, 0:LANES] = ks_ref[...]
        kwp_ref[WIN_PAD * tq:, 0:LANES] = kw_ref[...]
        for c in range(nq):
            vst_ref[SEL_PAD + c] = _transpose_bf16(vs_ref[c * tq:(c + 1) * tq, :])
            vwt_ref[WIN_PAD + c] = _transpose_bf16(vw_ref[c * tq:(c + 1) * tq, :])
        for c in range(n_cmp_pad // LANES):
            vct_ref[:, c * LANES:(c + 1) * LANES] = _transpose_bf16(vc_ref[c * LANES:(c + 1) * LANES, :])

    q = q_ref[...]
    gates_t = jax.nn.sigmoid(bg_ref[...]).T
    first_cmp = qi * cmp_per_tile
    cmp_row = lax.broadcasted_iota(jnp.int32, (n_cmp_pad, cols), 0)
    cmp_visible = cmp_row < first_cmp + cmp_per_tile
    blk = lax.broadcasted_iota(jnp.int32, (n_sel, tq), 0)
    cur = (qi * tq + lax.broadcasted_iota(jnp.int32, (n_sel, tq), 1)) // B_SEL_BLOCK
    forced = (blk == 0) | (blk == cur) | (blk == cur - 1)
    future = blk > cur
    halves = [slice(g * HEAD_DIM, (g + 1) * HEAD_DIM) for g in range(B_GROUPS)]

    qgs, o_cs, imps = [], [], []
    for g in range(B_GROUPS):
        qg = _group_queries_t(q, g, B_REP)
        sc_ref[g, CMP_BAND_BACK:, :] = _dot(kc_ref[...], qg)
        band_rows = pl.ds(pl.multiple_of(first_cmp, SUBLANES), band)
        sc_ref[g, band_rows, :] = sc_ref[g, band_rows, :] + band_ref[g]
        s = jnp.where(cmp_visible, sc_ref[g, CMP_BAND_BACK:, :], NEG)
        top = jnp.max(s, axis=0, keepdims=True)
        e = jnp.exp2(s - top)
        inv = jnp.where(top > 0.5 * NEG, 1.0 / jnp.sum(e, axis=0, keepdims=True), 0.0)
        o_cs.append(_dot(vct_ref[halves[g], :], e.astype(BF16)) * inv)
        p_sum = e[:, 0:tq] * inv[:, 0:tq]
        for r in range(1, B_REP):
            p_sum = p_sum + e[:, r * tq:(r + 1) * tq] * inv[:, r * tq:(r + 1) * tq]
        imp = _dot(ovt_ref[...], p_sum.astype(BF16))
        imp = jnp.where(forced, B_SEL_FORCE, jnp.where(future, -B_SEL_FORCE, imp))
        qgs.append(qg)
        imps.append(imp)

    blk_f = blk.astype(F32)
    taken = [jnp.zeros((n_sel, tq), jnp.bool_) for _ in range(B_GROUPS)]
    left = list(imps)
    for _ in range(top_k):
        for g in range(B_GROUPS):
            best = jnp.max(left[g], axis=0, keepdims=True)
            first = jnp.min(jnp.where(left[g] == best, blk_f, float(n_sel)), axis=0, keepdims=True)
            pick = blk_f == first
            taken[g] = taken[g] | pick
            left[g] = jnp.where(pick, TAKEN, left[g])
    q_augs = []
    for g in range(B_GROUPS):
        sel_bias = jnp.concatenate([jnp.where(taken[g], 0.0, UNSEL),
                                    jnp.full((LANES - n_sel, tq), UNSEL, F32)], axis=0).astype(BF16)
        q_augs.append(jnp.concatenate([qgs[g], jnp.concatenate([sel_bias] * B_REP, axis=1)], axis=0))

    pair = cols
    chains = [(g, slice(0, cols)) for g in range(B_GROUPS)]
    heads_per_chain = pair // tq

    def sel_logits(first_tile):
        start = pl.multiple_of(first_tile * tq, tq)
        keys = kaug_ref[pl.ds(start, SEL_GROUP * tq), :]
        return tuple(_dot(keys, q_augs[g][:, cc]) for g, cc in chains)

    def sel_update(first_tile, logits, carries):
        return tuple(_online_update(carry, s, [vst_ref[first_tile + t, halves[g], :] for t in range(SEL_GROUP)])
                     for (g, cc), s, carry in zip(chains, logits, carries))

    start = pl.multiple_of(qi * tq, tq)
    win_keys = kwp_ref[pl.ds(start, (WIN_PAD + 1) * tq), :]
    pad_flags = jnp.where(lax.broadcasted_iota(jnp.int32, (LANES, cols), 0) == 0, UNSEL, 0.0).astype(BF16)
    partial = []
    for g, cc in chains:
        s_w = _dot(win_keys, jnp.concatenate([qgs[g], pad_flags], axis=0)[:, cc])
        slabs = []
        for t in range(WIN_PAD + 1):
            slab = s_w[t * tq:(t + 1) * tq]
            if t == 0:
                slab = slab + edge_ref[:, cc]
            if t >= WIN_PAD - 1:
                slab = slab + near_ref[g, (t - WIN_PAD + 1) * tq:(t - WIN_PAD + 2) * tq, cc]
            slabs.append(slab)
        o_w = _softmax_result(_online_update(_softmax_init(HEAD_DIM, pair), jnp.concatenate(slabs, axis=0),
                                             [vwt_ref[qi + t, halves[g], :] for t in range(WIN_PAD + 1)]))
        o_c = o_cs[g][:, cc]
        heads = []
        for rr in range(heads_per_chain):
            hc = slice(rr * tq, (rr + 1) * tq)
            c0 = (g * B_REP + rr) * 3
            heads.append(gates_t[c0:c0 + 1] * o_c[:, hc] + gates_t[c0 + 2:c0 + 3] * o_w[:, hc])
        partial.append(heads)

    oc_ref[...] = _swa_tile(cq_ref[...], ckp_ref[...], ckd_ref[...], cvp_ref[...], cvd_ref[...],
                            swab_ref, sink_ref, qi, tq).astype(oc_ref.dtype)

    def sel_far(gi, carries, count):
        tiles = [qi - SEL_GROUP * (gi + c) for c in range(count)]
        fast, safe = carries, True
        for t, logits in [(t, sel_logits(t)) for t in tiles]:
            steps = [_one_pass_update(carry, s, [vst_ref[t + k, halves[g], :] for k in range(SEL_GROUP)])
                     for (g, cc), s, carry in zip(chains, logits, fast)]
            fast = tuple(step[0] for step in steps)
            for step in steps:
                safe = safe & step[1]

        def redo():
            slow = carries
            for t in tiles:
                slow = sel_update(t, sel_logits(t), slow)
            return slow

        return lax.cond(safe, lambda: fast, redo)

    far_rows = (SEL_GROUP - 2) * tq
    nearest = tuple(jnp.concatenate([s[0:far_rows], s[far_rows:] + near_ref[g, :, cc]], axis=0)
                    for (g, cc), s in zip(chains, sel_logits(qi)))
    sel = sel_update(qi, nearest, tuple(_softmax_init(HEAD_DIM, pair) for _ in chains))
    n_far = qi // SEL_GROUP
    sel = lax.fori_loop(0, n_far // 2, lambda p, c: sel_far(1 + 2 * p, c, 2), sel)
    sel = lax.fori_loop(0, n_far % 2, lambda _, c: sel_far(n_far, c, 1), sel)

    heads_out = []
    for ci, (g, cc) in enumerate(chains):
        o_s = _softmax_result(sel[ci])
        for rr in range(heads_per_chain):
            hc = slice(rr * tq, (rr + 1) * tq)
            c1 = (g * B_REP + rr) * 3 + 1
            heads_out.append(partial[ci][rr] + gates_t[c1:c1 + 1] * o_s[:, hc])
    o_ref[...] = jnp.concatenate(heads_out, axis=0).T.astype(o_ref.dtype)


def _nsa_swa_attention(zb, kc, vc, bgate, zc, tbl, sinks, ovt, bsz, seq, tq):
    nq = seq // tq
    n_cmp_pad = seq // B_CMP_STRIDE
    cols = B_REP * tq
    kv = lambda col: pl.BlockSpec((seq, LANES), lambda b, i: (b, col))
    cmp_spec = pl.BlockSpec((None, n_cmp_pad, B_KV), lambda b, i: (b, 0, 0))
    qcols = B_WIDTH // LANES
    ccols = C_WIDTH // LANES
    wide = lambda width: pl.BlockSpec((tq, width), lambda b, i: (b * nq + i, 0))
    prev = lambda col: pl.BlockSpec((tq, LANES), lambda b, i: (b * nq + jnp.maximum(i - 1, 0), col))
    diag = lambda col: pl.BlockSpec((tq, LANES), lambda b, i: (b * nq + i, col))
    return pl.pallas_call(
        functools.partial(_nsa_kernel, tq=tq, seq=seq),
        name="nsa_attn",
        grid=(bsz, nq),
        in_specs=[_smem(), _smem(),
                  wide(B_WIDTH),
                  kv(qcols), kv(qcols + 1), kv(qcols + 2), kv(qcols + 3),
                  cmp_spec, cmp_spec,
                  wide(LANES),
                  _resident(ovt.shape),
                  wide(C_WIDTH), prev(ccols), diag(ccols), prev(ccols + 1), diag(ccols + 1)],
        out_specs=[wide(B_WIDTH), wide(C_WIDTH)],
        out_shape=[jax.ShapeDtypeStruct((bsz * seq, B_WIDTH), BF16),
                   jax.ShapeDtypeStruct((bsz * seq, C_WIDTH), BF16)],
        scratch_shapes=[pltpu.VMEM((SEL_PAD * tq + seq, 2 * LANES), BF16),
                        pltpu.VMEM((WIN_PAD * tq + seq, 2 * LANES), BF16),
                        pltpu.VMEM((SEL_PAD + nq, B_KV, tq), BF16),
                        pltpu.VMEM((WIN_PAD + nq, B_KV, tq), BF16),
                        pltpu.VMEM((B_KV, n_cmp_pad), BF16),
                        pltpu.VMEM((B_GROUPS, 2 * tq, cols), F32),
                        pltpu.VMEM((tq, cols), F32),
                        pltpu.VMEM((B_GROUPS, CMP_BAND_BACK + tq // B_CMP_STRIDE, cols), F32),
                        pltpu.VMEM((B_GROUPS, CMP_BAND_BACK + n_cmp_pad, cols), F32),
                        pltpu.VMEM((C_GROUPS, 2 * tq, C_REP * tq), F32)],
        compiler_params=_params("arbitrary", "arbitrary"),
    )(tbl, sinks, zb, zb, zb, zb, zb, kc, vc, bgate, ovt, zc, zc, zc, zc, zc)


def _prep_w_in(w_in):
    sizes = (A_WIDTH, A_WIDTH, A_WIDTH, B_WIDTH, B_KV, B_KV, B_KV, B_KV, B_KV, B_KV, 3 * B_HEADS,
             C_WIDTH, C_KV, C_KV, D_MODEL, D_MODEL, D_MODEL)
    offs = np.concatenate([[0], np.cumsum(sizes)])
    (aq, ak, av, bq, bkc, bvc, bks, bvs, bkw, bvw, bgate, cq, ck, cv, ga, gb, gc) = [
        w_in[..., offs[i]:offs[i + 1]] for i in range(len(sizes))]
    scale = HEAD_DIM ** -0.5 * LOG2E
    bgate = jnp.pad(bgate, ((0, 0), (0, 0), (0, LANES - 3 * B_HEADS)))
    cols = [aq * scale, ak, av, bq * scale, bks, bvs, bkw, bvw, bkc, bvc, cq * scale, ck, cv, ga, gb, gc, bgate]
    return jnp.concatenate([c.astype(BF16) for c in cols], axis=-1)


def _prep_compress(cmp_pos, cmp_w1, cmp_w2):
    depth = cmp_pos.shape[0]
    half = B_CMP_STRIDE
    pos = cmp_pos.reshape(depth, 2, 2, half, 1, B_HEAD_DIM)
    pos = jnp.broadcast_to(pos, (depth, 2, 2, half, B_GROUPS, B_HEAD_DIM)).reshape(depth, 2, 2, 1, half * B_KV)
    w1 = cmp_w1.reshape(depth, 2, 2, half, 1, B_HEAD_DIM, 1, B_CMP_HIDDEN)
    eye = jnp.eye(B_GROUPS, dtype=cmp_w1.dtype).reshape(1, 1, 1, 1, B_GROUPS, 1, B_GROUPS, 1)
    w1 = (w1 * eye).reshape(depth, 2, 2, half * B_KV, B_GROUPS * B_CMP_HIDDEN)
    w2 = cmp_w2.reshape(depth, 2, 1, B_CMP_HIDDEN, 1, B_HEAD_DIM)
    eye2 = jnp.eye(B_GROUPS, dtype=cmp_w2.dtype).reshape(1, 1, B_GROUPS, 1, B_GROUPS, 1)
    w2 = (w2 * eye2).reshape(depth, 2, B_GROUPS * B_CMP_HIDDEN, B_KV)
    return pos.astype(F32), w1.astype(BF16), w2.astype(BF16)


def _overlap_t(seq):
    n_cmp_pad = seq // B_CMP_STRIDE
    start = np.arange(n_cmp_pad) * B_CMP_STRIDE
    sel = np.arange(seq // B_SEL_BLOCK) * B_SEL_BLOCK
    ov = (start[None, :] < sel[:, None] + B_SEL_BLOCK) & (start[None, :] + B_CMP_LEN > sel[:, None])
    ov &= (np.arange(n_cmp_pad) < n_cmp_pad - 1)[None, :]
    return jnp.asarray(ov.astype(np.float32), dtype=BF16)


def kernel(x, p, norm_g, ffn1_wi, ffn1_wo, w_in, diff_lambda, diff_subln, nsa_cmp_pos, nsa_cmp_w1,
           nsa_cmp_w2, swa_sinks, w_branch, w_out, ffn2_wi, ffn2_wo, w_ple, w_ple_gate, rel_bias, final_norm):
    bsz, seq, _ = x.shape
    depth = norm_g.shape[0]
    tokens = bsz * seq
    tq = BC_QUERY_TILE
    assert seq % (B_CMP_STRIDE * LANES) == 0 and seq // B_SEL_BLOCK < LANES and tokens % TAIL_TOKEN_TILE == 0
    assert FAR_DIST + B_CMP_LEN - 1 <= CMP_BAND_BACK * B_CMP_STRIDE and FAR_DIST <= tq
    assert WIN_PAD * tq == B_WINDOW and tq == C_WINDOW and tq % (B_CMP_STRIDE * SUBLANES) == 0

    wi1, wo1 = ffn1_wi.astype(BF16), ffn1_wo.astype(BF16)
    wi2, wo2 = ffn2_wi.astype(BF16), ffn2_wo.astype(BF16)
    w_in_r = _prep_w_in(w_in)
    cmp_pos, cmp_w1, cmp_w2 = _prep_compress(nsa_cmp_pos, nsa_cmp_w1, nsa_cmp_w2)
    wb, wo = w_branch.astype(BF16), w_out.astype(BF16)
    wpg, wpe = w_ple_gate.astype(BF16), w_ple.astype(BF16)
    tbl = rel_bias.astype(F32)
    ovt = _overlap_t(seq)

    h = x.reshape(tokens, D_MODEL)
    p_rows = p.reshape(depth, tokens, PLE_DIM)
    for i in range(depth):
        h, za, zb, zkc, zvc, zc, gates, bgate = _head(h, norm_g[i, 0], wi1, wo1, norm_g[i, 1], w_in_r,
                                                      HEAD_TOKEN_TILE, i)
        lam_init = 0.8 - 0.6 * math.exp(-0.3 * i)
        ya = _diff_attention(za, tbl, diff_lambda[i].astype(F32), diff_subln[i], bsz, seq, lam_init, A_QUERY_TILE)
        kc, vc = _compress(zkc, zvc, cmp_pos, cmp_w1, cmp_w2, i, bsz)
        yb, yc = _nsa_swa_attention(zb, kc, vc, bgate, zc, tbl, swa_sinks[i].astype(F32), ovt, bsz, seq, tq)
        h = _tail(h, ya, yb, yc, gates, p_rows, wb, wo, norm_g[i, 2], wi2, wo2, norm_g[i, 3], wpg, wpe,
                  final_norm, TAIL_TOKEN_TILE, i, final=(i == depth - 1))
    return h.reshape(bsz, seq, D_MODEL)
```

```python
import functools
import math

import numpy as np
import jax
import jax.numpy as jnp
from jax import lax
from jax.experimental import pallas as pl
from jax.experimental.pallas import tpu as pltpu

F32 = jnp.float32
BF16 = jnp.bfloat16

D_MODEL = 1024
PLE_DIM = 256
D_FF = 2816
NORM_EPS = 1e-6
NEG = -1e30
NUM_BUCKETS = 32
MAX_DISTANCE = 128
A_HEADS = 4
A_HEAD_DIM = 64
B_HEADS = 8
B_GROUPS = 2
B_REP = B_HEADS // B_GROUPS
B_HEAD_DIM = 64
B_CMP_LEN = 32
B_CMP_STRIDE = 16
B_CMP_HIDDEN = 256
B_SEL_BLOCK = 64
B_SEL_TOPK = 16
B_WINDOW = 512
B_SEL_FORCE = 1e6
C_HEADS = 8
C_GROUPS = 2
C_REP = C_HEADS // C_GROUPS
C_HEAD_DIM = 64
C_WINDOW = 128
A_WIDTH = A_HEADS * 2 * A_HEAD_DIM
B_WIDTH = B_HEADS * B_HEAD_DIM
B_KV = B_GROUPS * B_HEAD_DIM
C_WIDTH = C_HEADS * C_HEAD_DIM
C_KV = C_GROUPS * C_HEAD_DIM

LANES = 128
SUBLANES = 8
HEAD_DIM = 64
VMEM_LIMIT = 56 * 1024 * 1024
HEAD_TOKEN_TILE = 256
TAIL_TOKEN_TILE = 512
A_QUERY_TILE = 256
BC_QUERY_TILE = 128
UNSEL = -float(2 ** 30)
TAKEN = -3e38
FFN_CHUNKS = ((0, 1536), (1536, 1280))
CMP_BAND_BACK = 16
LOG2E = math.log2(math.e)
ONES_ROWS = 16
EXP_HEADROOM = 64.0


def _bucket_thresholds():
    n = np.arange(4 * MAX_DISTANCE)
    max_exact = NUM_BUCKETS // 2
    nf = np.maximum(n, 1).astype(np.float32)
    large = max_exact + (np.log(nf / max_exact) / math.log(MAX_DISTANCE / max_exact)
                         * (NUM_BUCKETS - max_exact)).astype(np.int32)
    bucket = np.where(n < max_exact, n, np.minimum(large, NUM_BUCKETS - 1))
    out = []
    for b in range(1, NUM_BUCKETS):
        hit = np.nonzero(bucket == b)[0]
        if hit.size:
            out.append((b, int(hit[0])))
    return tuple(out)


BUCKET_LO = _bucket_thresholds()
FAR_DIST = BUCKET_LO[-1][1]


def _bias_of_dist(dist, tbl_ref, head, minus_far=False):
    shift = tbl_ref[NUM_BUCKETS - 1, head] if minus_far else 0.0
    out = jnp.full(dist.shape, (tbl_ref[0, head] - shift) * LOG2E, F32)
    for b, lo in BUCKET_LO:
        out = jnp.where(dist >= lo, (tbl_ref[b, head] - shift) * LOG2E, out)
    return out


def _rms(x, g):
    return x * lax.rsqrt(jnp.mean(x * x, axis=-1, keepdims=True) + NORM_EPS) * g


def _dot(a, b):
    return jnp.dot(a, b, preferred_element_type=F32)


def _values_and_ones(vts):
    values = jnp.concatenate(vts, axis=1)
    return jnp.concatenate([values, jnp.ones((ONES_ROWS, values.shape[1]), BF16)], axis=0)


def _online_update(carry, s, vts):
    m, acc = carry
    m_new = jnp.maximum(m, jnp.max(s, axis=0, keepdims=True))
    p = jnp.exp2(s - m_new).astype(BF16)
    return m_new, jnp.exp2(m - m_new) * acc + _dot(_values_and_ones(vts), p)


def _one_pass_update(carry, s, vts):
    m, acc = carry
    top = jnp.max(s, axis=0, keepdims=True)
    p = jnp.exp2(s - m).astype(BF16)
    m_new = jnp.maximum(m, top)
    return (m_new, jnp.exp2(m - m_new) * (acc + _dot(_values_and_ones(vts), p))), jnp.max(top - m) <= EXP_HEADROOM


def _softmax_init(width, queries):
    return jnp.full((1, queries), NEG, F32), jnp.zeros((width + ONES_ROWS, queries), F32)


def _softmax_result(carry):
    _, acc = carry
    width = acc.shape[0] - ONES_ROWS
    return acc[0:width] / acc[width:width + 1]


def _transpose_bf16(x):
    return x.astype(F32).T.astype(BF16)


def _params(*sem):
    return pltpu.CompilerParams(dimension_semantics=sem, vmem_limit_bytes=VMEM_LIMIT)


def _resident(shape, layer=None):
    if layer is None:
        zeros = (0,) * len(shape)
        return pl.BlockSpec(shape, lambda *_: zeros, pipeline_mode=pl.Buffered(1))
    index = (layer,) + (0,) * (len(shape) - 1)
    return pl.BlockSpec((None,) + tuple(shape[1:]), lambda *_: index, pipeline_mode=pl.Buffered(1))


def _smem():
    return pl.BlockSpec(memory_space=pltpu.SMEM)


def _half_ffn(x, g_ref, wi_ref, wo_ref):
    n = _rms(x, g_ref[...]).astype(BF16)
    acc = x
    for lo, width in FFN_CHUNKS:
        gate = _dot(n, wi_ref[:, lo:lo + width])
        up = _dot(n, wi_ref[:, D_FF + lo:D_FF + lo + width])
        act = (gate * jax.nn.sigmoid(gate) * up).astype(BF16)
        acc = acc + 0.5 * _dot(act, wo_ref[lo:lo + width, :])
    return acc


IN_A = 3 * A_WIDTH
IN_B = B_WIDTH + 4 * B_KV
IN_C = C_WIDTH + 2 * C_KV
IN_G = 3 * D_MODEL
IN_SECTIONS = (IN_A, IN_B, B_KV, B_KV, IN_C, IN_G, LANES)


def _head_kernel(h_ref, g0_ref, wi_ref, wo_ref, g1_ref, w_ref,
                 o_ref, za_ref, zb_ref, zkc_ref, zvc_ref, zc_ref, gates_ref, bg_ref):
    x = _half_ffn(h_ref[...], g0_ref, wi_ref, wo_ref)
    o_ref[...] = x
    n = _rms(x, g1_ref[...]).astype(BF16)
    lo = 0
    for ref, width in zip((za_ref, zb_ref, zkc_ref, zvc_ref, zc_ref, gates_ref, bg_ref), IN_SECTIONS):
        ref[...] = _dot(n, w_ref[:, lo:lo + width]).astype(ref.dtype)
        lo += width


def _head(h, g0, wi, wo, g1, w, tm, layer):
    t = h.shape[0]
    dtypes = (BF16, BF16, BF16, BF16, BF16, F32, F32)
    rows = lambda width: pl.BlockSpec((tm, width), lambda i: (i, 0))
    vec = _resident((1, D_MODEL))
    return pl.pallas_call(
        _head_kernel,
        name="head",
        grid=(t // tm,),
        in_specs=[rows(D_MODEL), vec, _resident(wi.shape, layer), _resident(wo.shape, layer), vec,
                  _resident(w.shape, layer)],
        out_specs=[rows(D_MODEL)] + [rows(width) for width in IN_SECTIONS],
        out_shape=[jax.ShapeDtypeStruct(h.shape, F32)]
                  + [jax.ShapeDtypeStruct((t, width), dt) for width, dt in zip(IN_SECTIONS, dtypes)],
        compiler_params=_params("arbitrary"),
    )(h, g0.reshape(1, D_MODEL), wi, wo, g1.reshape(1, D_MODEL), w)


def _tail_kernel(h_ref, ya_ref, yb_ref, yc_ref, gates_ref, p_ref, wb_ref, wo_ref, g2_ref, wi_ref, wo2_ref,
                 g3_ref, wg_ref, we_ref, gf_ref, o_ref, *, final):
    merged = jnp.zeros(h_ref.shape, F32)
    for m, y_ref in enumerate((ya_ref, yb_ref, yc_ref)):
        gate = jax.nn.sigmoid(gates_ref[:, m * D_MODEL:(m + 1) * D_MODEL])
        merged = merged + gate * _dot(y_ref[...], wb_ref[m])
    x = h_ref[...] + _dot(merged.astype(BF16), wo_ref[...])
    x = _half_ffn(x, g2_ref, wi_ref, wo2_ref)
    gate = jax.nn.sigmoid(_dot(_rms(x, g3_ref[...]).astype(BF16), wg_ref[...]))
    x = x + gate * _dot(p_ref[...].astype(BF16), we_ref[...])
    if final:
        x = _rms(x, gf_ref[...])
    o_ref[...] = x


def _tail(h, ya, yb, yc, gates, p, wb, wo, g2, wi, wo2, g3, wg, we, gf, tm, layer, final):
    t = h.shape[0]
    rows = lambda width: pl.BlockSpec((tm, width), lambda i: (i, 0))
    vec = _resident((1, D_MODEL))
    return pl.pallas_call(
        functools.partial(_tail_kernel, final=final),
        name="tail",
        grid=(t // tm,),
        in_specs=[rows(D_MODEL), rows(A_WIDTH), rows(B_WIDTH), rows(C_WIDTH), rows(IN_G),
                  pl.BlockSpec((None, tm, PLE_DIM), lambda i: (layer, i, 0)),
                  _resident(wb.shape, layer), _resident(wo.shape, layer), vec,
                  _resident(wi.shape, layer), _resident(wo2.shape, layer),
                  vec, _resident(wg.shape, layer), _resident(we.shape, layer), vec],
        out_specs=rows(D_MODEL),
        out_shape=jax.ShapeDtypeStruct(h.shape, F32),
        compiler_params=_params("arbitrary"),
    )(h, ya, yb, yc, gates, p, wb, wo, g2.reshape(1, D_MODEL), wi, wo2, g3.reshape(1, D_MODEL), wg, we,
      gf.reshape(1, D_MODEL))


def _diff_kernel(tbl_ref, lam_ref, subg_ref, q_ref, k_ref, v_ref, o_ref, kaug_ref, vt_ref, near_ref,
                 *, tq, lam_init):
    qi = pl.program_id(1)
    width = 2 * A_HEAD_DIM
    n_heads = A_HEADS

    @pl.when((pl.program_id(0) == 0) & (qi == 0))
    def _static_setup():
        lane = lax.broadcasted_iota(jnp.int32, (tq, 2 * LANES), 1)
        key = lax.broadcasted_iota(jnp.int32, (2 * tq, tq), 0)
        qry = lax.broadcasted_iota(jnp.int32, (2 * tq, tq), 1)
        dist = tq + qry - key
        for h in range(n_heads):
            kaug_ref[h, 0:tq, :] = jnp.where(lane == LANES, 1.0, 0.0).astype(BF16)
            kaug_ref[h, tq:, LANES:2 * LANES] = jnp.zeros((kaug_ref.shape[1] - tq, LANES), BF16)
            vt_ref[h, 0] = jnp.zeros(vt_ref.shape[2:], BF16)
            near_ref[h] = jnp.where(dist >= 0, _bias_of_dist(dist, tbl_ref, h, True), NEG)

    @pl.when(qi == 0)
    def _per_batch_setup():
        for h in range(n_heads):
            cols = slice(h * width, (h + 1) * width)
            kaug_ref[h, tq:, 0:LANES] = k_ref[:, cols]
            for c in range(vt_ref.shape[1] - 1):
                vt_ref[h, c + 1] = _transpose_bf16(v_ref[c * tq:(c + 1) * tq, cols])

    zero = jnp.zeros((A_HEAD_DIM, tq), F32)
    flag_row = lax.broadcasted_iota(jnp.int32, (LANES, 2 * tq), 0) == 0
    flags = jnp.where(flag_row, UNSEL, 0.0)
    queries = []
    for h in range(n_heads):
        qt = q_ref[:, h * width:(h + 1) * width].astype(F32).T
        both = jnp.concatenate([jnp.concatenate([qt[0:A_HEAD_DIM], zero], axis=0),
                                jnp.concatenate([zero, qt[A_HEAD_DIM:width]], axis=0)], axis=1)
        queries.append(jnp.concatenate([both, flags], axis=0).astype(BF16))

    def logits_of(first_tile):
        start = pl.multiple_of(first_tile * tq, tq)
        return tuple(_dot(kaug_ref[h, pl.ds(start, 2 * tq), :], queries[h]) for h in range(n_heads))

    def update(first_tile, logits, carries):
        return tuple(_online_update(carries[h], logits[h], [vt_ref[h, first_tile], vt_ref[h, first_tile + 1]])
                     for h in range(n_heads))

    def far(gi, carries, count):
        tiles = [qi - 2 * (gi + c) for c in range(count)]
        fast, safe = carries, True
        for t, logits in [(t, logits_of(t)) for t in tiles]:
            steps = [_one_pass_update(fast[h], logits[h], [vt_ref[h, t], vt_ref[h, t + 1]]) for h in range(n_heads)]
            fast = tuple(step[0] for step in steps)
            for step in steps:
                safe = safe & step[1]

        def redo():
            slow = carries
            for t in tiles:
                slow = update(t, logits_of(t), slow)
            return slow

        return lax.cond(safe, lambda: fast, redo)

    nearest = tuple(s + jnp.concatenate([near_ref[h], near_ref[h]], axis=1) for h, s in enumerate(logits_of(qi)))
    carries = update(qi, nearest, tuple(_softmax_init(width, 2 * tq) for _ in range(n_heads)))
    n_far = qi // 2
    carries = lax.fori_loop(0, n_far // 2, lambda p, c: far(1 + 2 * p, c, 2), carries)
    carries = lax.fori_loop(0, n_far % 2, lambda _, c: far(n_far, c, 1), carries)
    lp = lam_ref[...]
    lam = (jnp.exp(jnp.sum(lp[0:1] * lp[1:2], axis=-1, keepdims=True))
           - jnp.exp(jnp.sum(lp[2:3] * lp[3:4], axis=-1, keepdims=True)) + lam_init)
    for h in range(n_heads):
        out = _softmax_result(carries[h])
        o = out[:, 0:tq] - lam * out[:, tq:2 * tq]
        o = o * lax.rsqrt(jnp.mean(o * o, axis=0, keepdims=True) + NORM_EPS) * (1.0 - lam_init)
        o_ref[:, h * width:(h + 1) * width] = (o.T * subg_ref[...]).astype(o_ref.dtype)


def _diff_attention(za, tbl, lam_p, subg, bsz, seq, lam_init, tq):
    nq = seq // tq
    width = 2 * A_HEAD_DIM
    return pl.pallas_call(
        functools.partial(_diff_kernel, tq=tq, lam_init=lam_init),
        name="diff_attn",
        grid=(bsz, nq),
        in_specs=[_smem(), _resident(lam_p.shape), _resident((1, width)),
                  pl.BlockSpec((tq, A_WIDTH), lambda b, i: (b * nq + i, 0)),
                  pl.BlockSpec((seq, A_WIDTH), lambda b, i: (b, 1)),
                  pl.BlockSpec((seq, A_WIDTH), lambda b, i: (b, 2))],
        out_specs=pl.BlockSpec((tq, A_WIDTH), lambda b, i: (b * nq + i, 0)),
        out_shape=jax.ShapeDtypeStruct((bsz * seq, A_WIDTH), BF16),
        scratch_shapes=[pltpu.VMEM((A_HEADS, tq + seq, 2 * LANES), BF16),
                        pltpu.VMEM((A_HEADS, nq + 1, width, tq), BF16),
                        pltpu.VMEM((A_HEADS, 2 * tq, tq), F32)],
        compiler_params=_params("arbitrary", "arbitrary"),
    )(tbl, lam_p, subg.reshape(1, width), za, za, za)


def _compress_kernel(xk_ref, xv_ref, pos_ref, w1_ref, w2_ref, kc_ref, vc_ref):
    rows = xk_ref.shape[0] // B_CMP_STRIDE
    for t, (x_ref, o_ref) in enumerate(((xk_ref, kc_ref), (xv_ref, vc_ref))):
        x = x_ref[...].astype(F32).reshape(rows, B_CMP_STRIDE * B_KV)
        first = _dot((x + pos_ref[t, 0]).astype(BF16), w1_ref[t, 0])
        second = _dot((x + pos_ref[t, 1]).astype(BF16), w1_ref[t, 1])
        hidden = first + pltpu.roll(second, rows - 1, 0)
        o_ref[...] = _dot(jax.nn.gelu(hidden).astype(BF16), w2_ref[t]).astype(o_ref.dtype)


def _compress(xk, xv, pos, w1, w2, layer, bsz):
    seq = xk.shape[0] // bsz
    rows = seq // B_CMP_STRIDE
    xspec = pl.BlockSpec((seq, B_KV), lambda b: (b, 0))
    ospec = pl.BlockSpec((None, rows, B_KV), lambda b: (b, 0, 0))
    return pl.pallas_call(
        _compress_kernel,
        name="nsa_compress",
        grid=(bsz,),
        in_specs=[xspec, xspec, _resident(pos.shape, layer), _resident(w1.shape, layer), _resident(w2.shape, layer)],
        out_specs=[ospec, ospec],
        out_shape=[jax.ShapeDtypeStruct((bsz, rows, B_KV), BF16)] * 2,
        compiler_params=_params("arbitrary"),
    )(xk, xv, pos, w1, w2)


def _group_queries_t(q, group, rep):
    tq = q.shape[0]
    zero = jnp.zeros((HEAD_DIM, tq), F32)
    cols = []
    for r in range(rep):
        head = group * rep + r
        slab = q[:, (head // 2) * LANES:(head // 2 + 1) * LANES].astype(F32).T
        part = slab[(head % 2) * HEAD_DIM:(head % 2 + 1) * HEAD_DIM]
        cols.append(jnp.concatenate([part, zero] if group == 0 else [zero, part], axis=0))
    return jnp.concatenate(cols, axis=1).astype(BF16)


def _build_near_bias(near_ref, tbl_ref, head0, groups, rep, tq, window_is_tile, minus_far):
    key = lax.broadcasted_iota(jnp.int32, (2 * tq, tq), 0)
    qry = lax.broadcasted_iota(jnp.int32, (2 * tq, tq), 1)
    dist = tq + qry - key
    visible = (dist >= 0) & (dist < tq) if window_is_tile else dist >= 0
    for g in range(groups):
        for r in range(rep):
            head = head0 + g * rep + r
            near_ref[g, :, r * tq:(r + 1) * tq] = jnp.where(visible, _bias_of_dist(dist, tbl_ref, head, minus_far), NEG)


def _swa_tile(q, k_prev, k_diag, v_prev, v_diag, bias_ref, sink_ref, qi, tq):
    cols = C_REP * tq
    prev_bias = jnp.where(qi > 0, 0.0, NEG)
    vp_t = _transpose_bf16(v_prev)
    vd_t = _transpose_bf16(v_diag)
    heads_out = []
    for g in range(C_GROUPS):
        half = slice(g * HEAD_DIM, (g + 1) * HEAD_DIM)
        qg = _group_queries_t(q, g, C_REP)
        s = jnp.concatenate([_dot(k_prev, qg) + bias_ref[g, 0:tq, :] + prev_bias,
                             _dot(k_diag, qg) + bias_ref[g, tq:2 * tq, :]], axis=0)
        m, acc = _online_update(_softmax_init(HEAD_DIM, cols), s, [vp_t[half], vd_t[half]])
        for r in range(C_REP):
            cc = slice(r * tq, (r + 1) * tq)
            sink = sink_ref[g * C_REP + r] * LOG2E
            m_all = jnp.maximum(m[:, cc], sink)
            scale = jnp.exp2(m[:, cc] - m_all)
            denom = acc[HEAD_DIM:HEAD_DIM + 1, cc] * scale + jnp.exp2(sink - m_all)
            heads_out.append(acc[0:HEAD_DIM, cc] * (scale / denom))
    return jnp.concatenate(heads_out, axis=0).T


SEL_GROUP = 2
SEL_PAD = SEL_GROUP - 1
WIN_PAD = B_WINDOW // BC_QUERY_TILE


def _nsa_kernel(tbl_ref, sink_ref, q_ref, ks_ref, vs_ref, kw_ref, vw_ref, kc_ref, vc_ref, bg_ref, ovt_ref,
                cq_ref, ckp_ref, ckd_ref, cvp_ref, cvd_ref, o_ref, oc_ref,
                kaug_ref, kwp_ref, vst_ref, vwt_ref, vct_ref, near_ref, edge_ref, band_ref, sc_ref, swab_ref,
                *, tq, seq):
    first_batch = pl.program_id(0) == 0
    qi = pl.program_id(1)
    n_cmp_pad = seq // B_CMP_STRIDE
    n_sel = seq // B_SEL_BLOCK
    top_k = min(B_SEL_TOPK, n_sel)
    nq = seq // tq
    cols = B_REP * tq
    cmp_per_tile = tq // B_CMP_STRIDE
    band = CMP_BAND_BACK + cmp_per_tile

    @pl.when(first_batch & (qi == 0))
    def _static_setup():
        _build_near_bias(near_ref, tbl_ref, A_HEADS, B_GROUPS, B_REP, tq, False, True)
        _build_near_bias(swab_ref, tbl_ref, A_HEADS + B_HEADS, C_GROUPS, C_REP, tq, True, False)
        key = lax.broadcasted_iota(jnp.int32, (tq, cols), 0)
        qry = lax.broadcasted_iota(jnp.int32, (tq, cols), 1) % tq
        edge_ref[...] = jnp.where(qry < key, 0.0, NEG)
        blk = lax.broadcasted_iota(jnp.int32, (band, tq), 0) - CMP_BAND_BACK
        dist = lax.broadcasted_iota(jnp.int32, (band, tq), 1) - (blk * B_CMP_STRIDE + B_CMP_LEN - 1)
        for g in range(B_GROUPS):
            for r in range(B_REP):
                head = A_HEADS + g * B_REP + r
                band_ref[g, :, r * tq:(r + 1) * tq] = jnp.where(dist >= 0, _bias_of_dist(dist, tbl_ref, head, True), NEG)
        sc_ref[:, 0:CMP_BAND_BACK, :] = jnp.zeros((B_GROUPS, CMP_BAND_BACK, cols), F32)
        pad = SEL_PAD * tq
        row = lax.broadcasted_iota(jnp.int32, (pad + seq, LANES), 0)
        lane = lax.broadcasted_iota(jnp.int32, (pad + seq, LANES), 1)
        blk_id = jnp.where(row < pad, LANES - 1, (row - pad) // B_SEL_BLOCK)
        kaug_ref[:, LANES:2 * LANES] = jnp.where(blk_id == lane, 1.0, 0.0).astype(BF16)
        kaug_ref[0:pad, 0:LANES] = jnp.zeros((pad, LANES), BF16)
        pad_lane = lax.broadcasted_iota(jnp.int32, (WIN_PAD * tq, 2 * LANES), 1) == LANES
        kwp_ref[0:WIN_PAD * tq, :] = jnp.where(pad_lane, 1.0, 0.0).astype(BF16)
        kwp_ref[WIN_PAD * tq:, LANES:2 * LANES] = jnp.zeros((seq, LANES), BF16)
        for c in range(SEL_PAD):
            vst_ref[c] = jnp.zeros(vst_ref.shape[1:], BF16)
        for c in range(WIN_PAD):
            vwt_ref[c] = jnp.zeros(vwt_ref.shape[1:], BF16)

    @pl.when(qi == 0)
    def _per_batch_setup():
        kaug_ref[SEL_PAD * tq:, 0:LANES] = ks_ref[...]
        kwp_ref[WIN_PAD * tq:, 0:LANES] = kw_ref[...]
        for c in range(nq):
            vst_ref[SEL_PAD + c] = _transpose_bf16(vs_ref[c * tq:(c + 1) * tq, :])
            vwt_ref[WIN_PAD + c] = _transpose_bf16(vw_ref[c * tq:(c + 1) * tq, :])
        for c in range(n_cmp_pad // LANES):
            vct_ref[:, c * LANES:(c + 1) * LANES] = _transpose_bf16(vc_ref[c * LANES:(c + 1) * LANES, :])

    q = q_ref[...]
    gates_t = jax.nn.sigmoid(bg_ref[...]).T
    first_cmp = qi * cmp_per_tile
    cmp_row = lax.broadcasted_iota(jnp.int32, (n_cmp_pad, cols), 0)
    cmp_visible = cmp_row < first_cmp + cmp_per_tile
    blk = lax.broadcasted_iota(jnp.int32, (n_sel, tq), 0)
    cur = (qi * tq + lax.broadcasted_iota(jnp.int32, (n_sel, tq), 1)) // B_SEL_BLOCK
    forced = (blk == 0) | (blk == cur) | (blk == cur - 1)
    future = blk > cur
    halves = [slice(g * HEAD_DIM, (g + 1) * HEAD_DIM) for g in range(B_GROUPS)]

    qgs, o_cs, imps = [], [], []
    for g in range(B_GROUPS):
        qg = _group_queries_t(q, g, B_REP)
        sc_ref[g, CMP_BAND_BACK:, :] = _dot(kc_ref[...], qg)
        band_rows = pl.ds(pl.multiple_of(first_cmp, SUBLANES), band)
        sc_ref[g, band_rows, :] = sc_ref[g, band_rows, :] + band_ref[g]
        s = jnp.where(cmp_visible, sc_ref[g, CMP_BAND_BACK:, :], NEG)
        top = jnp.max(s, axis=0, keepdims=True)
        e = jnp.exp2(s - top)
        inv = jnp.where(top > 0.5 * NEG, 1.0 / jnp.sum(e, axis=0, keepdims=True), 0.0)
        o_cs.append(_dot(vct_ref[halves[g], :], e.astype(BF16)) * inv)
        p_sum = e[:, 0:tq] * inv[:, 0:tq]
        for r in range(1, B_REP):
            p_sum = p_sum + e[:, r * tq:(r + 1) * tq] * inv[:, r * tq:(r + 1) * tq]
        imp = _dot(ovt_ref[...], p_sum.astype(BF16))
        imp = jnp.where(forced, B_SEL_FORCE, jnp.where(future, -B_SEL_FORCE, imp))
        qgs.append(qg)
        imps.append(imp)

    blk_f = blk.astype(F32)
    taken = [jnp.zeros((n_sel, tq), jnp.bool_) for _ in range(B_GROUPS)]
    left = list(imps)
    for _ in range(top_k):
        for g in range(B_GROUPS):
            best = jnp.max(left[g], axis=0, keepdims=True)
            first = jnp.min(jnp.where(left[g] == best, blk_f, float(n_sel)), axis=0, keepdims=True)
            pick = blk_f == first
            taken[g] = taken[g] | pick
            left[g] = jnp.where(pick, TAKEN, left[g])
    q_augs = []
    for g in range(B_GROUPS):
        sel_bias = jnp.concatenate([jnp.where(taken[g], 0.0, UNSEL),
                                    jnp.full((LANES - n_sel, tq), UNSEL, F32)], axis=0).astype(BF16)
        q_augs.append(jnp.concatenate([qgs[g], jnp.concatenate([sel_bias] * B_REP, axis=1)], axis=0))

    pair = cols
    chains = [(g, slice(0, cols)) for g in range(B_GROUPS)]
    heads_per_chain = pair // tq

    def sel_logits(first_tile):
        start = pl.multiple_of(first_tile * tq, tq)
        keys = kaug_ref[pl.ds(start, SEL_GROUP * tq), :]
        return tuple(_dot(keys, q_augs[g][:, cc]) for g, cc in chains)

    def sel_update(first_tile, logits, carries):
        return tuple(_online_update(carry, s, [vst_ref[first_tile + t, halves[g], :] for t in range(SEL_GROUP)])
                     for (g, cc), s, carry in zip(chains, logits, carries))

    start = pl.multiple_of(qi * tq, tq)
    win_keys = kwp_ref[pl.ds(start, (WIN_PAD + 1) * tq), :]
    pad_flags = jnp.where(lax.broadcasted_iota(jnp.int32, (LANES, cols), 0) == 0, UNSEL, 0.0).astype(BF16)
    partial = []
    for g, cc in chains:
        s_w = _dot(win_keys, jnp.concatenate([qgs[g], pad_flags], axis=0)[:, cc])
        slabs = []
        for t in range(WIN_PAD + 1):
            slab = s_w[t * tq:(t + 1) * tq]
            if t == 0:
                slab = slab + edge_ref[:, cc]
            if t >= WIN_PAD - 1:
                slab = slab + near_ref[g, (t - WIN_PAD + 1) * tq:(t - WIN_PAD + 2) * tq, cc]
            slabs.append(slab)
        o_w = _softmax_result(_online_update(_softmax_init(HEAD_DIM, pair), jnp.concatenate(slabs, axis=0),
                                             [vwt_ref[qi + t, halves[g], :] for t in range(WIN_PAD + 1)]))
        o_c = o_cs[g][:, cc]
        heads = []
        for rr in range(heads_per_chain):
            hc = slice(rr * tq, (rr + 1) * tq)
            c0 = (g * B_REP + rr) * 3
            heads.append(gates_t[c0:c0 + 1] * o_c[:, hc] + gates_t[c0 + 2:c0 + 3] * o_w[:, hc])
        partial.append(heads)

    oc_ref[...] = _swa_tile(cq_ref[...], ckp_ref[...], ckd_ref[...], cvp_ref[...], cvd_ref[...],
                            swab_ref, sink_ref, qi, tq).astype(oc_ref.dtype)

    def sel_far(gi, carries, count):
        tiles = [qi - SEL_GROUP * (gi + c) for c in range(count)]
        fast, safe = carries, True
        for t, logits in [(t, sel_logits(t)) for t in tiles]:
            steps = [_one_pass_update(carry, s, [vst_ref[t + k, halves[g], :] for k in range(SEL_GROUP)])
                     for (g, cc), s, carry in zip(chains, logits, fast)]
            fast = tuple(step[0] for step in steps)
            for step in steps:
                safe = safe & step[1]

        def redo():
            slow = carries
            for t in tiles:
                slow = sel_update(t, sel_logits(t), slow)
            return slow

        return lax.cond(safe, lambda: fast, redo)

    nearest = tuple(s + near_ref[g, :, cc] for (g, cc), s in zip(chains, sel_logits(qi)))
    sel = sel_update(qi, nearest, tuple(_softmax_init(HEAD_DIM, pair) for _ in chains))
    n_far = qi // SEL_GROUP
    sel = lax.fori_loop(0, n_far // 4, lambda p, c: sel_far(1 + 4 * p, c, 4), sel)
    done = 4 * (n_far // 4)
    sel = lax.fori_loop(0, (n_far - done) // 2, lambda _, c: sel_far(1 + done, c, 2), sel)
    sel = lax.fori_loop(0, n_far % 2, lambda _, c: sel_far(n_far, c, 1), sel)

    heads_out = []
    for ci, (g, cc) in enumerate(chains):
        o_s = _softmax_result(sel[ci])
        for rr in range(heads_per_chain):
            hc = slice(rr * tq, (rr + 1) * tq)
            c1 = (g * B_REP + rr) * 3 + 1
            heads_out.append(partial[ci][rr] + gates_t[c1:c1 + 1] * o_s[:, hc])
    o_ref[...] = jnp.concatenate(heads_out, axis=0).T.astype(o_ref.dtype)


def _nsa_swa_attention(zb, kc, vc, bgate, zc, tbl, sinks, ovt, bsz, seq, tq):
    nq = seq // tq
    n_cmp_pad = seq // B_CMP_STRIDE
    cols = B_REP * tq
    kv = lambda col: pl.BlockSpec((seq, LANES), lambda b, i: (b, col))
    cmp_spec = pl.BlockSpec((None, n_cmp_pad, B_KV), lambda b, i: (b, 0, 0))
    qcols = B_WIDTH // LANES
    ccols = C_WIDTH // LANES
    wide = lambda width: pl.BlockSpec((tq, width), lambda b, i: (b * nq + i, 0))
    prev = lambda col: pl.BlockSpec((tq, LANES), lambda b, i: (b * nq + jnp.maximum(i - 1, 0), col))
    diag = lambda col: pl.BlockSpec((tq, LANES), lambda b, i: (b * nq + i, col))
    return pl.pallas_call(
        functools.partial(_nsa_kernel, tq=tq, seq=seq),
        name="nsa_attn",
        grid=(bsz, nq),
        in_specs=[_smem(), _smem(),
                  wide(B_WIDTH),
                  kv(qcols), kv(qcols + 1), kv(qcols + 2), kv(qcols + 3),
                  cmp_spec, cmp_spec,
                  wide(LANES),
                  _resident(ovt.shape),
                  wide(C_WIDTH), prev(ccols), diag(ccols), prev(ccols + 1), diag(ccols + 1)],
        out_specs=[wide(B_WIDTH), wide(C_WIDTH)],
        out_shape=[jax.ShapeDtypeStruct((bsz * seq, B_WIDTH), BF16),
                   jax.ShapeDtypeStruct((bsz * seq, C_WIDTH), BF16)],
        scratch_shapes=[pltpu.VMEM((SEL_PAD * tq + seq, 2 * LANES), BF16),
                        pltpu.VMEM((WIN_PAD * tq + seq, 2 * LANES), BF16),
                        pltpu.VMEM((SEL_PAD + nq, B_KV, tq), BF16),
                        pltpu.VMEM((WIN_PAD + nq, B_KV, tq), BF16),
                        pltpu.VMEM((B_KV, n_cmp_pad), BF16),
                        pltpu.VMEM((B_GROUPS, 2 * tq, cols), F32),
                        pltpu.VMEM((tq, cols), F32),
                        pltpu.VMEM((B_GROUPS, CMP_BAND_BACK + tq // B_CMP_STRIDE, cols), F32),
                        pltpu.VMEM((B_GROUPS, CMP_BAND_BACK + n_cmp_pad, cols), F32),
                        pltpu.VMEM((C_GROUPS, 2 * tq, C_REP * tq), F32)],
        compiler_params=_params("arbitrary", "arbitrary"),
    )(tbl, sinks, zb, zb, zb, zb, zb, kc, vc, bgate, ovt, zc, zc, zc, zc, zc)


def _prep_w_in(w_in):
    sizes = (A_WIDTH, A_WIDTH, A_WIDTH, B_WIDTH, B_KV, B_KV, B_KV, B_KV, B_KV, B_KV, 3 * B_HEADS,
             C_WIDTH, C_KV, C_KV, D_MODEL, D_MODEL, D_MODEL)
    offs = np.concatenate([[0], np.cumsum(sizes)])
    (aq, ak, av, bq, bkc, bvc, bks, bvs, bkw, bvw, bgate, cq, ck, cv, ga, gb, gc) = [
        w_in[..., offs[i]:offs[i + 1]] for i in range(len(sizes))]
    scale = HEAD_DIM ** -0.5 * LOG2E
    bgate = jnp.pad(bgate, ((0, 0), (0, 0), (0, LANES - 3 * B_HEADS)))
    cols = [aq * scale, ak, av, bq * scale, bks, bvs, bkw, bvw, bkc, bvc, cq * scale, ck, cv, ga, gb, gc, bgate]
    return jnp.concatenate([c.astype(BF16) for c in cols], axis=-1)


def _prep_compress(cmp_pos, cmp_w1, cmp_w2):
    depth = cmp_pos.shape[0]
    half = B_CMP_STRIDE
    pos = cmp_pos.reshape(depth, 2, 2, half, 1, B_HEAD_DIM)
    pos = jnp.broadcast_to(pos, (depth, 2, 2, half, B_GROUPS, B_HEAD_DIM)).reshape(depth, 2, 2, 1, half * B_KV)
    w1 = cmp_w1.reshape(depth, 2, 2, half, 1, B_HEAD_DIM, 1, B_CMP_HIDDEN)
    eye = jnp.eye(B_GROUPS, dtype=cmp_w1.dtype).reshape(1, 1, 1, 1, B_GROUPS, 1, B_GROUPS, 1)
    w1 = (w1 * eye).reshape(depth, 2, 2, half * B_KV, B_GROUPS * B_CMP_HIDDEN)
    w2 = cmp_w2.reshape(depth, 2, 1, B_CMP_HIDDEN, 1, B_HEAD_DIM)
    eye2 = jnp.eye(B_GROUPS, dtype=cmp_w2.dtype).reshape(1, 1, B_GROUPS, 1, B_GROUPS, 1)
    w2 = (w2 * eye2).reshape(depth, 2, B_GROUPS * B_CMP_HIDDEN, B_KV)
    return pos.astype(F32), w1.astype(BF16), w2.astype(BF16)


def _overlap_t(seq):
    n_cmp_pad = seq // B_CMP_STRIDE
    start = np.arange(n_cmp_pad) * B_CMP_STRIDE
    sel = np.arange(seq // B_SEL_BLOCK) * B_SEL_BLOCK
    ov = (start[None, :] < sel[:, None] + B_SEL_BLOCK) & (start[None, :] + B_CMP_LEN > sel[:, None])
    ov &= (np.arange(n_cmp_pad) < n_cmp_pad - 1)[None, :]
    return jnp.asarray(ov.astype(np.float32), dtype=BF16)


def kernel(x, p, norm_g, ffn1_wi, ffn1_wo, w_in, diff_lambda, diff_subln, nsa_cmp_pos, nsa_cmp_w1,
           nsa_cmp_w2, swa_sinks, w_branch, w_out, ffn2_wi, ffn2_wo, w_ple, w_ple_gate, rel_bias, final_norm):
    bsz, seq, _ = x.shape
    depth = norm_g.shape[0]
    tokens = bsz * seq
    tq = BC_QUERY_TILE
    assert seq % (B_CMP_STRIDE * LANES) == 0 and seq // B_SEL_BLOCK < LANES and tokens % TAIL_TOKEN_TILE == 0
    assert FAR_DIST + B_CMP_LEN - 1 <= CMP_BAND_BACK * B_CMP_STRIDE and FAR_DIST <= tq
    assert WIN_PAD * tq == B_WINDOW and tq == C_WINDOW and tq % (B_CMP_STRIDE * SUBLANES) == 0

    wi1, wo1 = ffn1_wi.astype(BF16), ffn1_wo.astype(BF16)
    wi2, wo2 = ffn2_wi.astype(BF16), ffn2_wo.astype(BF16)
    w_in_r = _prep_w_in(w_in)
    cmp_pos, cmp_w1, cmp_w2 = _prep_compress(nsa_cmp_pos, nsa_cmp_w1, nsa_cmp_w2)
    wb, wo = w_branch.astype(BF16), w_out.astype(BF16)
    wpg, wpe = w_ple_gate.astype(BF16), w_ple.astype(BF16)
    tbl = rel_bias.astype(F32)
    ovt = _overlap_t(seq)

    h = x.reshape(tokens, D_MODEL)
    p_rows = p.reshape(depth, tokens, PLE_DIM)
    for i in range(depth):
        h, za, zb, zkc, zvc, zc, gates, bgate = _head(h, norm_g[i, 0], wi1, wo1, norm_g[i, 1], w_in_r,
                                                      HEAD_TOKEN_TILE, i)
        lam_init = 0.8 - 0.6 * math.exp(-0.3 * i)
        ya = _diff_attention(za, tbl, diff_lambda[i].astype(F32), diff_subln[i], bsz, seq, lam_init, A_QUERY_TILE)
        kc, vc = _compress(zkc, zvc, cmp_pos, cmp_w1, cmp_w2, i, bsz)
        yb, yc = _nsa_swa_attention(zb, kc, vc, bgate, zc, tbl, swa_sinks[i].astype(F32), ovt, bsz, seq, tq)
        h = _tail(h, ya, yb, yc, gates, p_rows, wb, wo, norm_g[i, 2], wi2, wo2, norm_g[i, 3], wpg, wpe,
                  final_norm, TAIL_TOKEN_TILE, i, final=(i == depth - 1))
    return h.reshape(bsz, seq, D_MODEL)
```
